```python
import math
import jax
import jax.numpy as jnp
from jax import lax
import numpy as np

D_MODEL = 1024
BATCH = 4
SEQ = 4096
DEPTH = 2

HEAD_DIM_ATTN = 64
N_HEADS_SB = 8
N_HEADS_DIL = 8
WIDTH_SB = N_HEADS_SB * HEAD_DIM_ATTN
WIDTH_DIL = N_HEADS_DIL * HEAD_DIM_ATTN
ATTN_WIDTH = WIDTH_SB + WIDTH_DIL
DIL_CONFIGS = ((128, 1), (512, 4), (2048, 16))
Q_BLOCK = 128
NUM_BUCKETS = 32
MAX_DISTANCE = 2048
N_HEADS_MLSTM = 8
HEAD_DIM_MLSTM = D_MODEL // N_HEADS_MLSTM
MLSTM_WIDTH = N_HEADS_MLSTM * HEAD_DIM_MLSTM
MLSTM_CHUNK = 128
CONV_WIDTH = 4
FORGET_BIAS_LO = 3.0
FORGET_BIAS_HI = 6.0
D_FF = 2816
N_EVEN = (DEPTH + 1) // 2
N_ODD = DEPTH // 2
EPS = 1e-6

kernel_name = 'hybrid_stickbreak_dilated_mlstm_macaron'


def _rms_norm(x, g):
    xf = x.astype(jnp.float32)
    y = xf * lax.rsqrt(jnp.mean(xf * xf, axis=-1, keepdims=True) + EPS)
    return (y * g.astype(jnp.float32)).astype(x.dtype)


def _swiglu(x, w_gate, w_up, w_down):
    return (jax.nn.silu(x @ w_gate) * (x @ w_up)) @ w_down


def _t5_bucket(dist):
    max_exact = NUM_BUCKETS // 2
    d = jnp.maximum(dist, 1).astype(jnp.float32)
    log_b = max_exact + (jnp.log(d / max_exact) / math.log(MAX_DISTANCE / max_exact)
                         * (NUM_BUCKETS - max_exact)).astype(jnp.int32)
    log_b = jnp.minimum(log_b, NUM_BUCKETS - 1)
    return jnp.where(dist < max_exact, dist, log_b)


def _stick_breaking_attention(q, k, v):
    bsz, t, h, dh = q.shape
    nb = t // Q_BLOCK
    qb = q.reshape(bsz, nb, Q_BLOCK, h, dh).transpose(1, 0, 2, 3, 4)
    starts = jnp.arange(nb) * Q_BLOCK
    k_pos = jnp.arange(t)
    scale = 1.0 / math.sqrt(dh)

    def block(args):
        q_blk, start = args
        z = jnp.einsum('bqhd,bkhd->bhqk', q_blk, k) * scale
        causal = k_pos[None, :] < (start + jnp.arange(Q_BLOCK))[:, None]
        log_keep = jnp.where(causal, jax.nn.log_sigmoid(-z), 0.0)
        later = lax.cumsum(log_keep, axis=3, reverse=True) - log_keep
        log_w = jnp.where(causal, jax.nn.log_sigmoid(z) + later, -jnp.inf)
        return jnp.einsum('bhqk,bkhd->bqhd', jnp.exp(log_w), v)

    out = lax.map(block, (qb, starts))
    return out.transpose(1, 0, 2, 3, 4).reshape(bsz, t, h, dh)


def _dilated_branch(q, k, v, rel_bias, window, dil):
    bsz, t, h, dh = q.shape
    steps = window // dil
    seq_l = t // dil
    nb = -(-seq_l // Q_BLOCK)
    lp = nb * Q_BLOCK
    n = bsz * dil

    def to_res(a):
        return a.reshape(bsz, seq_l, dil, h, dh).transpose(0, 2, 1, 3, 4).reshape(n, seq_l, h, dh)

    qb = jnp.pad(to_res(q), ((0, 0), (0, lp - seq_l), (0, 0), (0, 0))).reshape(n, nb, Q_BLOCK, h, dh)

    def band(a):
        a = jnp.pad(to_res(a), ((0, 0), (Q_BLOCK, lp - seq_l), (0, 0), (0, 0)))
        a = a.reshape(n, nb + 1, Q_BLOCK, h, dh)
        return jnp.concatenate([a[:, :-1], a[:, 1:]], axis=2)

    kb, vb = band(k), band(v)
    qi = jnp.arange(Q_BLOCK)[:, None]
    ki = jnp.arange(2 * Q_BLOCK)[None, :]
    dist = qi + Q_BLOCK - ki
    bias = rel_bias[_t5_bucket(jnp.maximum(dist, 0) * dil)].transpose(2, 0, 1)
    key_pos = jnp.arange(nb)[:, None, None] * Q_BLOCK + ki[None] - Q_BLOCK
    valid = (dist >= 0) & (dist <= steps) & (key_pos >= 0)
    s = jnp.einsum('nbqhd,nbkhd->nbhqk', qb, kb) / math.sqrt(dh) + bias
    s = jnp.where(valid[None, :, None], s, -jnp.inf)
    m = jnp.max(s, axis=-1)
    p = jnp.exp(s - m[..., None])
    l = jnp.sum(p, axis=-1)
    num = jnp.einsum('nbhqk,nbkhd->nbqhd', p, vb)

    def back(a):
        a = a[:, :seq_l]
        rest = a.shape[2:]
        a = a.reshape((bsz, dil, seq_l) + rest)
        a = a.transpose((0, 2, 1) + tuple(range(3, a.ndim)))
        return a.reshape((bsz, t) + rest)

    num = back(num.reshape(n, lp, h, dh))
    m = back(m.transpose(0, 1, 3, 2).reshape(n, lp, h))
    l = back(l.transpose(0, 1, 3, 2).reshape(n, lp, h))
    return num, m, l


def _dilated_attention(q, k, v, rel_bias):
    outs = [_dilated_branch(q, k, v, rel_bias, w, d) for (w, d) in DIL_CONFIGS]
    m_all = jnp.stack([o[1] for o in outs])
    wts = jnp.exp(m_all - jnp.max(m_all, axis=0))
    num = sum(wts[i][..., None] * outs[i][0] for i in range(len(outs)))
    den = sum(wts[i] * outs[i][2] for i in range(len(outs)))
    return num / den[..., None]


def _attn_mixer(xn, w_in, w_out, rel_bias):
    bsz, t, _ = xn.shape
    proj = (xn @ w_in).astype(jnp.float32)
    idx = [WIDTH_SB, 2 * WIDTH_SB, 3 * WIDTH_SB, 3 * WIDTH_SB + WIDTH_DIL, 3 * WIDTH_SB + 2 * WIDTH_DIL]
    qa, ka, va, qd, kd, vd = jnp.split(proj, idx, axis=-1)
    sb = lambda a: a.reshape(bsz, t, N_HEADS_SB, HEAD_DIM_ATTN)
    dl = lambda a: a.reshape(bsz, t, N_HEADS_DIL, HEAD_DIM_ATTN)
    out_sb = _stick_breaking_attention(sb(qa), sb(ka), sb(va))
    out_dil = _dilated_attention(dl(qd), dl(kd), dl(vd), rel_bias.astype(jnp.float32))
    mixed = jnp.concatenate([out_sb.reshape(bsz, t, WIDTH_SB), out_dil.reshape(bsz, t, WIDTH_DIL)], axis=-1)
    return mixed.astype(xn.dtype) @ w_out


def _causal_depthwise_conv(x, w):
    return lax.conv_general_dilated(
        x, w[:, None, :], window_strides=(1,), padding=((CONV_WIDTH - 1, 0),),
        dimension_numbers=('NWC', 'WIO', 'NWC'), feature_group_count=x.shape[-1])


def _mlstm(q, k, v, i_pre, f_pre):
    bsz, t, h, dh = q.shape
    nc = t // MLSTM_CHUNK
    cl = MLSTM_CHUNK

    def chunks(a):
        a = a.reshape((bsz, nc, cl, h) + a.shape[3:])
        return a.transpose((0, 3, 1, 2) + tuple(range(4, a.ndim)))

    qc = chunks(q) / math.sqrt(dh)
    kc, vc = chunks(k), chunks(v)
    ic = chunks(i_pre)
    bc = jnp.cumsum(jax.nn.log_sigmoid(chunks(f_pre)), axis=-1)

    def step(carry, xs):
        c_st, n_st, m_st = carry
        k_, v_, i_, b_ = xs
        g = b_[..., -1]
        w = g[..., None] - b_ + i_
        m_new = jnp.maximum(g + m_st, jnp.max(w, axis=-1))
        decay = jnp.exp(g + m_st - m_new)
        ws = jnp.exp(w - m_new[..., None])
        c_new = decay[..., None, None] * c_st + jnp.einsum('bhs,bhsk,bhsv->bhkv', ws, k_, v_)
        n_new = decay[..., None] * n_st + jnp.einsum('bhs,bhsk->bhk', ws, k_)
        return (c_new, n_new, m_new), (c_st, n_st, m_st)

    init = (jnp.zeros((bsz, h, dh, dh), jnp.float32),
            jnp.zeros((bsz, h, dh), jnp.float32),
            jnp.zeros((bsz, h), jnp.float32))
    xs = tuple(jnp.moveaxis(a, 2, 0) for a in (kc, vc, ic, bc))
    _, (c_prev, n_prev, m_prev) = lax.scan(step, init, xs)
    c_prev = jnp.moveaxis(c_prev, 0, 2)
    n_prev = jnp.moveaxis(n_prev, 0, 2)
    m_prev = jnp.moveaxis(m_prev, 0, 2)

    a = bc + m_prev[..., None]
    tri = jnp.arange(cl)[:, None] >= jnp.arange(cl)[None, :]
    dmat = jnp.where(tri, bc[..., :, None] - bc[..., None, :] + ic[..., None, :], -jnp.inf)
    m_t = jnp.maximum(a, jnp.max(dmat, axis=-1))
    p = jnp.einsum('bhcld,bhcsd->bhcls', qc, kc) * jnp.exp(dmat - m_t[..., None])
    w_inter = jnp.exp(a - m_t)
    h_num = (jnp.einsum('bhcls,bhcsd->bhcld', p, vc)
             + w_inter[..., None] * jnp.einsum('bhcld,bhcde->bhcle', qc, c_prev))
    n_num = jnp.sum(p, axis=-1) + w_inter * jnp.einsum('bhcld,bhcd->bhcl', qc, n_prev)
    hid = h_num / jnp.maximum(jnp.abs(n_num), jnp.exp(-m_t))[..., None]
    return hid.transpose(0, 2, 3, 1, 4).reshape(bsz, t, h, dh)


def _mlstm_mixer(xn, w_in, b_gates, conv_w, head_g, w_out):
    bsz, t, _ = xn.shape
    proj = (xn @ w_in).astype(jnp.float32)
    qk, v, o, gates = jnp.split(proj, [2 * MLSTM_WIDTH, 3 * MLSTM_WIDTH, 4 * MLSTM_WIDTH], axis=-1)
    qk = jax.nn.silu(_causal_depthwise_conv(qk, conv_w.astype(jnp.float32)))
    q, k = jnp.split(qk, 2, axis=-1)
    gates = gates + b_gates.astype(jnp.float32)
    i_pre, f_pre = gates[..., :N_HEADS_MLSTM], gates[..., N_HEADS_MLSTM:]
    hd = lambda a: a.reshape(bsz, t, N_HEADS_MLSTM, HEAD_DIM_MLSTM)
    hid = _mlstm(hd(q), hd(k), hd(v), i_pre, f_pre)
    hid = hid * lax.rsqrt(jnp.mean(hid * hid, axis=-1, keepdims=True) + EPS)
    hid = hid * head_g.astype(jnp.float32).reshape(N_HEADS_MLSTM, HEAD_DIM_MLSTM)
    hid = hid.reshape(bsz, t, MLSTM_WIDTH) * jax.nn.sigmoid(o)
    return hid.astype(xn.dtype) @ w_out


def setup_inputs(seed: int = 0) -> dict:
    key = jax.random.key(seed)
    ks = jax.random.split(key, 16)
    nrm = jax.random.normal
    x = nrm(ks[0], (BATCH, SEQ, D_MODEL), jnp.float32)
    norm_g = 1.0 + 0.02 * nrm(ks[1], (DEPTH, 6, D_MODEL), jnp.float32)
    ffn_w_gate = nrm(ks[2], (DEPTH, 2, D_MODEL, D_FF), jnp.float32) * D_MODEL ** -0.5
    ffn_w_up = nrm(ks[3], (DEPTH, 2, D_MODEL, D_FF), jnp.float32) * D_MODEL ** -0.5
    ffn_w_down = nrm(ks[4], (DEPTH, 2, D_FF, D_MODEL), jnp.float32) * D_FF ** -0.5
    attn_w_in = nrm(ks[5], (N_EVEN, D_MODEL, 3 * ATTN_WIDTH), jnp.float32) * D_MODEL ** -0.5
    attn_w_out = nrm(ks[6], (N_EVEN, ATTN_WIDTH, D_MODEL), jnp.float32) * ATTN_WIDTH ** -0.5
    rel_bias = 0.5 * nrm(ks[7], (NUM_BUCKETS, N_HEADS_DIL), jnp.float32)
    mlstm_w_in = nrm(ks[8], (N_ODD, D_MODEL, 4 * MLSTM_WIDTH + 2 * N_HEADS_MLSTM), jnp.float32) * D_MODEL ** -0.5
    b_i = 0.1 * nrm(ks[9], (N_ODD, N_HEADS_MLSTM), jnp.float32)
    b_f = (jnp.linspace(FORGET_BIAS_LO, FORGET_BIAS_HI, N_HEADS_MLSTM, dtype=jnp.float32)[None]
           + 0.1 * nrm(ks[10], (N_ODD, N_HEADS_MLSTM), jnp.float32))
    mlstm_b_gates = jnp.concatenate([b_i, b_f], axis=-1)
    mlstm_conv_w = nrm(ks[11], (N_ODD, CONV_WIDTH, 2 * MLSTM_WIDTH), jnp.float32) * CONV_WIDTH ** -0.5
    mlstm_head_g = 1.0 + 0.02 * nrm(ks[12], (N_ODD, MLSTM_WIDTH), jnp.float32)
    mlstm_w_out = nrm(ks[13], (N_ODD, MLSTM_WIDTH, D_MODEL), jnp.float32) * MLSTM_WIDTH ** -0.5
    return {'x': x, 'norm_g': norm_g, 'ffn_w_gate': ffn_w_gate, 'ffn_w_up': ffn_w_up,
            'ffn_w_down': ffn_w_down, 'attn_w_in': attn_w_in, 'attn_w_out': attn_w_out,
            'rel_bias': rel_bias, 'mlstm_w_in': mlstm_w_in, 'mlstm_b_gates': mlstm_b_gates,
            'mlstm_conv_w': mlstm_conv_w, 'mlstm_head_g': mlstm_head_g, 'mlstm_w_out': mlstm_w_out}


def reference(x, norm_g, ffn_w_gate, ffn_w_up, ffn_w_down, attn_w_in, attn_w_out, rel_bias,
              mlstm_w_in, mlstm_b_gates, mlstm_conv_w, mlstm_head_g, mlstm_w_out):
    for layer in range(DEPTH):
        g = norm_g[layer]
        j = layer // 2
        hf = _swiglu(_rms_norm(x, g[0]), ffn_w_gate[layer, 0], ffn_w_up[layer, 0], ffn_w_down[layer, 0])
        x = x + 0.5 * _rms_norm(hf, g[1])
        xn = _rms_norm(x, g[2])
        if layer % 2 == 0:
            mix = _attn_mixer(xn, attn_w_in[j], attn_w_out[j], rel_bias)
        else:
            mix = _mlstm_mixer(xn, mlstm_w_in[j], mlstm_b_gates[j], mlstm_conv_w[j],
                               mlstm_head_g[j], mlstm_w_out[j])
        x = x + _rms_norm(mix, g[3])
        hf = _swiglu(_rms_norm(x, g[4]), ffn_w_gate[layer, 1], ffn_w_up[layer, 1], ffn_w_down[layer, 1])
        x = x + 0.5 * _rms_norm(hf, g[5])
    return x
```

```python
import functools
import math

import numpy as np
import jax
import jax.numpy as jnp
from jax import lax
from jax.experimental import pallas as pl
from jax.experimental.pallas import tpu as pltpu

EPS = 1e-6
HEAD_DIM_ATTN = 64
N_HEADS_SB = 8
N_HEADS_DIL = 8
DIL_CONFIGS = ((128, 1), (512, 4), (2048, 16))
DIL_BLOCK = 128
NUM_BUCKETS = 32
MAX_DISTANCE = 2048
N_HEADS_MLSTM = 8
MLSTM_CHUNK = 128
CONV_WIDTH = 4
LANES = 128
CONV_PAD = 8
MASKED = -1e30
VMEM_LIMIT = 56 * 1024 * 1024

F32 = jnp.float32
BF16 = jnp.bfloat16


def _params(sem, vmem=VMEM_LIMIT):
    return pltpu.CompilerParams(dimension_semantics=sem, vmem_limit_bytes=vmem)


def _resident(shape):
    zeros = (0,) * len(shape)
    return pl.BlockSpec(shape, lambda *_: zeros, pipeline_mode=pl.Buffered(1))


def _rms(x, g):
    return x * lax.rsqrt(jnp.mean(x * x, axis=-1, keepdims=True) + EPS) * g


def _dot(a, b):
    return jnp.dot(a, b, preferred_element_type=F32)


def _dot_nt(a, b):
    return lax.dot_general(a, b, (((1,), (1,)), ((), ())), preferred_element_type=F32)


def _dot_tn(a, b):
    return lax.dot_general(a, b, (((0,), (0,)), ((), ())), preferred_element_type=F32)


def _ffn_kernel(x_ref, gin_ref, gout_ref, wg_ref, wu_ref, wd_ref, o_ref, *, ff_chunk):
    x = x_ref[...]
    xn = _rms(x, gin_ref[...]).astype(BF16)
    d_ff = wg_ref.shape[1]
    acc = None
    for c in range(d_ff // ff_chunk):
        sl = slice(c * ff_chunk, (c + 1) * ff_chunk)
        gate = _dot(xn, wg_ref[:, sl])
        up = _dot(xn, wu_ref[:, sl])
        h = (gate * jax.nn.sigmoid(gate) * up).astype(BF16)
        part = _dot(h, wd_ref[sl, :])
        acc = part if acc is None else acc + part
    o_ref[...] = x + 0.5 * _rms(acc, gout_ref[...])


def _ffn(x, g_in, g_out, wg, wu, wd, *, tm=512, ff_chunk=256):
    m, d = x.shape
    d_ff = wg.shape[1]
    assert m % tm == 0 and d_ff % ff_chunk == 0
    row = pl.BlockSpec((tm, d), lambda i: (i, 0))
    return pl.pallas_call(
        functools.partial(_ffn_kernel, ff_chunk=ff_chunk),
        grid=(m // tm,),
        in_specs=[row, _resident((1, d)), _resident((1, d)),
                  _resident((d, d_ff)), _resident((d, d_ff)), _resident((d_ff, d))],
        out_specs=row,
        out_shape=jax.ShapeDtypeStruct((m, d), F32),
        compiler_params=_params(("parallel",)),
        name="ffn",
    )(x, g_in.reshape(1, d), g_out.reshape(1, d), wg, wu, wd)


def _attn_proj_kernel(x_ref, g_ref, w_ref, sb_ref, dil_ref, *, n_chunk, scale):
    xn = _rms(x_ref[...], g_ref[...]).astype(BF16)
    w_sb = sb_ref.shape[1]
    w_q = w_sb // 3
    for c in range(w_sb // n_chunk):
        sl = slice(c * n_chunk, (c + 1) * n_chunk)
        y = _dot(xn, w_ref[:, sl])
        if (c + 1) * n_chunk <= w_q:
            y = y * scale
        sb_ref[:, sl] = y.astype(BF16)
    for c in range(dil_ref.shape[1] // n_chunk):
        sl = slice(c * n_chunk, (c + 1) * n_chunk)
        y = _dot(xn, w_ref[:, w_sb + c * n_chunk: w_sb + (c + 1) * n_chunk])
        if (c + 1) * n_chunk <= w_q:
            y = y * scale
        dil_ref[:, sl] = y


def _attn_proj(x, g, w, *, tm=512, n_chunk=512):
    m, d = x.shape
    n = w.shape[1]
    half = n // 2
    assert m % tm == 0 and (half // 3) % n_chunk == 0
    scale = 1.0 / math.sqrt(HEAD_DIM_ATTN)
    return pl.pallas_call(
        functools.partial(_attn_proj_kernel, n_chunk=n_chunk, scale=scale),
        grid=(m // tm,),
        in_specs=[pl.BlockSpec((tm, d), lambda i: (i, 0)), _resident((1, d)), _resident((d, n))],
        out_specs=[pl.BlockSpec((tm, half), lambda i: (i, 0)),
                   pl.BlockSpec((tm, half), lambda i: (i, 0))],
        out_shape=[jax.ShapeDtypeStruct((m, half), BF16), jax.ShapeDtypeStruct((m, half), F32)],
        compiler_params=_params(("parallel",)),
        name="attn_proj",
    )(x, g.reshape(1, d), w)


def _softplus(z):
    return jnp.maximum(z, 0.0) + jnp.log(1.0 + jnp.exp(-jnp.abs(z)))


def _sb_kernel(q_ref, k_ref, v_ref, tri_ref, o_ref, *, blk):
    i = pl.program_id(2)
    q2 = q_ref[...]
    lane = lax.broadcasted_iota(jnp.int32, (1, LANES), 1)
    row = lax.broadcasted_iota(jnp.int32, (blk, blk), 0)
    col = lax.broadcasted_iota(jnp.int32, (blk, blk), 1)
    causal = col < row
    tri = tri_ref[...]

    def block(kb, qh, carry, acc, diag):
        ks = pl.multiple_of(kb * blk, blk)
        k2 = k_ref[pl.ds(ks, blk), :]
        v2 = v_ref[pl.ds(ks, blk), :]
        z = _dot_nt(qh, k2)
        log_keep = -_softplus(z)
        if diag:
            log_keep = jnp.where(causal, log_keep, 0.0)
        hi = log_keep.astype(BF16)
        lo = (log_keep - hi.astype(F32)).astype(BF16)
        later = _dot(hi, tri) + _dot(lo, tri)
        log_w = z + log_keep + later + carry
        p = jnp.exp(log_w)
        if diag:
            p = jnp.where(causal, p, 0.0)
        acc = acc + _dot(p.astype(BF16), v2)
        carry = carry + jnp.sum(log_keep, axis=-1, keepdims=True)
        return carry, acc

    out = None
    for h in range(2):
        in_head = (lane >= h * HEAD_DIM_ATTN) & (lane < (h + 1) * HEAD_DIM_ATTN)
        qh = jnp.where(in_head, q2, jnp.zeros_like(q2))
        carry = jnp.zeros((blk, 1), F32)
        acc = jnp.zeros((blk, LANES), F32)
        carry, acc = block(i, qh, carry, acc, True)

        def body(t, state, qh=qh):
            return block(i - 1 - t, qh, state[0], state[1], False)

        carry, acc = lax.fori_loop(0, i, body, (carry, acc))
        out = acc if out is None else jnp.where(in_head, acc, out)
    o_ref[...] = out.astype(o_ref.dtype)


def _sb_attention(qkv, *, blk=256):
    b, t, w3 = qkv.shape
    w = w3 // 3
    pairs = w // LANES
    blk = min(blk, t)
    assert t % blk == 0
    tri = jnp.asarray(np.tril(np.ones((blk, blk), np.float32), -1), BF16)
    return pl.pallas_call(
        functools.partial(_sb_kernel, blk=blk),
        grid=(b, pairs, t // blk),
        in_specs=[pl.BlockSpec((None, blk, LANES), lambda bi, p, i: (bi, i, p)),
                  pl.BlockSpec((None, t, LANES), lambda bi, p, i: (bi, 0, pairs + p)),
                  pl.BlockSpec((None, t, LANES), lambda bi, p, i: (bi, 0, 2 * pairs + p)),
                  _resident((blk, blk))],
        out_specs=pl.BlockSpec((None, blk, LANES), lambda bi, p, i: (bi, i, p)),
        out_shape=jax.ShapeDtypeStruct((b, t, w), BF16),
        compiler_params=_params(("parallel", "parallel", "arbitrary")),
        name="sb_attn",
    )(qkv, qkv, qkv, tri)


def _t5_bucket_np(dist):
    max_exact = NUM_BUCKETS // 2
    d = np.maximum(dist, 1).astype(np.float32)
    log_b = max_exact + (np.log(d / np.float32(max_exact)) / np.float32(math.log(MAX_DISTANCE / max_exact))
                         * np.float32(NUM_BUCKETS - max_exact)).astype(np.int32)
    log_b = np.minimum(log_b, NUM_BUCKETS - 1)
    return np.where(dist < max_exact, dist, log_b)


def _dil_tables():
    qi = np.arange(DIL_BLOCK)[:, None]
    ki = np.arange(2 * DIL_BLOCK)[None, :]
    dist = qi + DIL_BLOCK - ki
    buckets, valid = [], []
    for window, dil in DIL_CONFIGS:
        steps = window // dil
        buckets.append(_t5_bucket_np(np.maximum(dist, 0) * dil))
        valid.append((dist >= 0) & (dist <= steps))
    return np.stack(buckets).astype(np.int32), np.stack(valid).astype(np.int32)


def _dil_kernel(rb_ref, bucket_ref, valid_ref, q_ref, k_ref, v_ref, o_ref,
                bias_scr, num_scr, m_scr, l_scr, *, seq):
    p = pl.program_id(1)
    qb = DIL_BLOCK
    lane = lax.broadcasted_iota(jnp.int32, (1, LANES), 1)
    head0 = lane < HEAD_DIM_ATTN
    first_half = lax.broadcasted_iota(jnp.int32, (1, 2 * qb), 1) < qb

    for br in range(len(DIL_CONFIGS)):
        bucket = bucket_ref[br]
        valid = valid_ref[br] > 0
        for h in range(2):
            bias = jnp.zeros((qb, 2 * qb), F32)
            for b in range(NUM_BUCKETS):
                bias = jnp.where(bucket == b, rb_ref[b, 2 * p + h], bias)
            bias_scr[br, h] = jnp.where(valid, bias, MASKED)

    for br, (_, dil) in enumerate(DIL_CONFIGS):
        n_units = seq // qb

        def unit(u, _, br=br, dil=dil):
            blk = lax.shift_right_logical(u, int(math.log2(dil)))
            res = u - blk * dil
            cur = blk * (qb * dil) + res
            prev = jnp.maximum(blk - 1, 0) * (qb * dil) + res

            def rows(ref, start):
                if dil == 1:
                    return ref[pl.ds(start, qb), :]
                return ref[pl.ds(start, qb, stride=dil), :]

            q2 = rows(q_ref, cur).astype(BF16)
            k2 = jnp.concatenate([rows(k_ref, prev), rows(k_ref, cur)], axis=0).astype(BF16)
            v2 = jnp.concatenate([rows(v_ref, prev), rows(v_ref, cur)], axis=0).astype(BF16)
            no_prev = jnp.where(blk == 0, MASKED, 0.0)
            pen = jnp.where(first_half, no_prev, 0.0)
            m2 = l2 = pv2 = None
            for h in range(2):
                qh = jnp.where(head0 if h == 0 else ~head0, q2, jnp.zeros_like(q2))
                s = _dot_nt(qh, k2) + bias_scr[br, h] + pen
                m = jnp.max(s, axis=-1, keepdims=True)
                e = jnp.exp(s - m)
                l = jnp.sum(e, axis=-1, keepdims=True)
                pv = _dot(e.astype(BF16), v2)
                if h == 0:
                    m2 = jnp.broadcast_to(m, (qb, LANES))
                    l2 = jnp.broadcast_to(l, (qb, LANES))
                    pv2 = pv
                else:
                    m2 = jnp.where(head0, m2, m)
                    l2 = jnp.where(head0, l2, l)
                    pv2 = jnp.where(head0, pv2, pv)
            if br == 0:
                idx = pl.ds(cur, qb)
                num_scr[idx, :] = pv2
                m_scr[idx, :] = m2
                l_scr[idx, :] = l2
            else:
                idx = pl.ds(cur, qb, stride=dil)
                m_old = m_scr[idx, :]
                m_new = jnp.maximum(m_old, m2)
                a_old = jnp.exp(m_old - m_new)
                a_new = jnp.exp(m2 - m_new)
                num_scr[idx, :] = a_old * num_scr[idx, :] + a_new * pv2
                l_scr[idx, :] = a_old * l_scr[idx, :] + a_new * l2
                m_scr[idx, :] = m_new
            return 0

        lax.fori_loop(0, n_units, unit, 0)

    rows_out = 256

    def finish(c, _):
        idx = pl.ds(pl.multiple_of(c * rows_out, rows_out), rows_out)
        o_ref[idx, :] = (num_scr[idx, :] / l_scr[idx, :]).astype(o_ref.dtype)
        return 0

    lax.fori_loop(0, seq // rows_out, finish, 0)


def _dil_attention(qkv, rel_bias):
    b, t, w3 = qkv.shape
    w = w3 // 3
    pairs = w // LANES
    assert t % (DIL_BLOCK * max(d for _, d in DIL_CONFIGS)) == 0 and t % 256 == 0
    bucket, valid = _dil_tables()
    nbr = len(DIL_CONFIGS)
    seq_spec = lambda off: pl.BlockSpec((None, t, LANES), lambda bi, p: (bi, 0, off + p))
    return pl.pallas_call(
        functools.partial(_dil_kernel, seq=t),
        grid=(b, pairs),
        in_specs=[pl.BlockSpec(memory_space=pltpu.SMEM),
                  _resident((nbr, DIL_BLOCK, 2 * DIL_BLOCK)),
                  _resident((nbr, DIL_BLOCK, 2 * DIL_BLOCK)),
                  seq_spec(0), seq_spec(pairs), seq_spec(2 * pairs)],
        out_specs=pl.BlockSpec((None, t, LANES), lambda bi, p: (bi, 0, p)),
        out_shape=jax.ShapeDtypeStruct((b, t, w), BF16),
        scratch_shapes=[pltpu.VMEM((nbr, 2, DIL_BLOCK, 2 * DIL_BLOCK), F32),
                        pltpu.VMEM((t, LANES), F32), pltpu.VMEM((t, LANES), F32),
                        pltpu.VMEM((t, LANES), F32)],
        compiler_params=_params(("parallel", "parallel")),
        name="dil_attn",
    )(rel_bias, jnp.asarray(bucket), jnp.asarray(valid), qkv, qkv, qkv)


def _out_proj_kernel(*refs, n_parts):
    parts, (x_ref, g_ref, w_ref, o_ref) = refs[:n_parts], refs[n_parts:]
    a = jnp.concatenate([r[...] for r in parts], axis=-1) if n_parts > 1 else parts[0][...]
    y = _dot(a, w_ref[...])
    o_ref[...] = x_ref[...] + _rms(y, g_ref[...])


def _out_proj(parts, x, g, w, *, tm=512):
    m, d = x.shape
    assert m % tm == 0
    row = lambda width: pl.BlockSpec((tm, width), lambda i: (i, 0))
    return pl.pallas_call(
        functools.partial(_out_proj_kernel, n_parts=len(parts)),
        grid=(m // tm,),
        in_specs=[row(a.shape[1]) for a in parts] + [row(d), _resident((1, d)), _resident(w.shape)],
        out_specs=row(d),
        out_shape=jax.ShapeDtypeStruct((m, d), F32),
        compiler_params=_params(("parallel",)),
        name="out_proj",
    )(*parts, x, g.reshape(1, d), w)


def _mlstm_proj_kernel(x_ref, g_ref, w_ref, cw_ref, bg_ref, q_ref, k_ref, v_ref, o_ref, gate_ref,
                       hist_scr, *, tiles_per_seq, n_chunk, scale):
    i = pl.program_id(0)
    tm = x_ref.shape[0]
    width = q_ref.shape[1]
    xn = _rms(x_ref[...], g_ref[...]).astype(BF16)

    @pl.when(i % tiles_per_seq == 0)
    def _():
        hist_scr[0:CONV_PAD, :] = jnp.zeros((CONV_PAD, hist_scr.shape[1]), F32)

    for c in range(2 * width // n_chunk):
        sl = slice(c * n_chunk, (c + 1) * n_chunk)
        hist_scr[CONV_PAD:CONV_PAD + tm, sl] = _dot(xn, w_ref[:, sl])
        y = None
        for tap in range(CONV_WIDTH):
            off = CONV_PAD - (CONV_WIDTH - 1) + tap
            term = cw_ref[tap:tap + 1, sl] * hist_scr[off:off + tm, sl]
            y = term if y is None else y + term
        y = y * jax.nn.sigmoid(y)
        hist_scr[0:CONV_PAD, sl] = hist_scr[tm:tm + CONV_PAD, sl]
        if c * n_chunk < width:
            q_ref[:, sl] = (y * scale).astype(BF16)
        else:
            k_ref[:, c * n_chunk - width:(c + 1) * n_chunk - width] = y.astype(BF16)
    for c in range(width // n_chunk):
        sl = slice(c * n_chunk, (c + 1) * n_chunk)
        v_ref[:, sl] = _dot(xn, w_ref[:, 2 * width + c * n_chunk:2 * width + (c + 1) * n_chunk]).astype(BF16)
        o_ref[:, sl] = _dot(xn, w_ref[:, 3 * width + c * n_chunk:3 * width + (c + 1) * n_chunk])
    gate_ref[...] = _dot(xn, w_ref[:, 4 * width:]) + bg_ref[...]


def _mlstm_proj(x, g, w_pad, conv_w, b_gates_pad, *, seq, tm=512, n_chunk=512):
    m, d = x.shape
    width = (w_pad.shape[1] - LANES) // 4
    assert m % tm == 0 and seq % tm == 0 and width % n_chunk == 0
    row = lambda wd: pl.BlockSpec((tm, wd), lambda i: (i, 0))
    scale = 1.0 / math.sqrt(width // N_HEADS_MLSTM)
    return pl.pallas_call(
        functools.partial(_mlstm_proj_kernel, tiles_per_seq=seq // tm, n_chunk=n_chunk, scale=scale),
        grid=(m // tm,),
        in_specs=[row(d), _resident((1, d)), _resident(w_pad.shape), _resident(conv_w.shape),
                  _resident((1, LANES))],
        out_specs=[row(width), row(width), row(width), row(width), row(LANES)],
        out_shape=[jax.ShapeDtypeStruct((m, width), BF16), jax.ShapeDtypeStruct((m, width), BF16),
                   jax.ShapeDtypeStruct((m, width), BF16), jax.ShapeDtypeStruct((m, width), F32),
                   jax.ShapeDtypeStruct((m, LANES), F32)],
        scratch_shapes=[pltpu.VMEM((tm + CONV_PAD, 2 * width), F32)],
        compiler_params=_params(("arbitrary",)),
        name="mlstm_proj",
    )(x, g.reshape(1, d), w_pad, conv_w, b_gates_pad)


def _mlstm_kernel(q_ref, k_ref, v_ref, o_ref, gate_ref, hg_ref, out_ref, c_scr, n_scr, m_scr, *, heads):
    cl = q_ref.shape[0]
    dh = q_ref.shape[1] // heads
    hi = lax.Precision.HIGHEST

    @pl.when(pl.program_id(1) == 0)
    def _():
        c_scr[...] = jnp.zeros_like(c_scr)
        n_scr[...] = jnp.zeros_like(n_scr)
        m_scr[...] = jnp.zeros_like(m_scr)

    row = lax.broadcasted_iota(jnp.int32, (cl, cl), 0)
    col = lax.broadcasted_iota(jnp.int32, (cl, cl), 1)
    lower = row >= col
    incl = jnp.where(lower, 1.0, 0.0)
    eye = jnp.where(row == col, 1.0, 0.0)

    gates = gate_ref[...]
    log_f = jax.nn.log_sigmoid(gates)
    b_all = jnp.dot(incl, log_f, precision=hi, preferred_element_type=F32)
    gates_t = lax.dot_general(gates, eye, (((0,), (0,)), ((), ())), precision=hi,
                              preferred_element_type=F32)
    b_all_t = lax.dot_general(b_all, eye, (((0,), (0,)), ((), ())), precision=hi,
                              preferred_element_type=F32)

    for h in range(heads):
        sl = slice(h * dh, (h + 1) * dh)
        q = q_ref[:, sl]
        k = k_ref[:, sl]
        v = v_ref[:, sl]
        i_row = gates_t[h:h + 1, :]
        i_col = gates[:, h:h + 1]
        b_row = b_all_t[heads + h:heads + h + 1, :]
        b_col = b_all[:, heads + h:heads + h + 1]
        g_tot = b_col[cl - 1:cl, :]
        c_prev = c_scr[h]
        n_prev = n_scr[h:h + 1, :]
        m_prev = m_scr[h:h + 1, 0:1]

        a = b_col + m_prev
        dmat = jnp.where(lower, b_col - b_row + i_row, -jnp.inf)
        m_t = jnp.maximum(a, jnp.max(dmat, axis=-1, keepdims=True))
        p = _dot_nt(q, k) * jnp.exp(dmat - m_t)
        w_inter = jnp.exp(a - m_t)
        h_num = _dot(p.astype(BF16), v) + w_inter * _dot(q, c_prev.astype(BF16))
        qn = jnp.sum(q.astype(F32) * n_prev, axis=-1, keepdims=True)
        n_num = jnp.sum(p, axis=-1, keepdims=True) + w_inter * qn
        hid = h_num / jnp.maximum(jnp.abs(n_num), jnp.exp(-m_t))
        hid = hid * lax.rsqrt(jnp.mean(hid * hid, axis=-1, keepdims=True) + EPS) * hg_ref[:, sl]
        out_ref[:, sl] = (hid * jax.nn.sigmoid(o_ref[:, sl])).astype(out_ref.dtype)

        w_row = g_tot - b_row + i_row
        w_col = g_tot - b_col + i_col
        m_new = jnp.maximum(g_tot + m_prev, jnp.max(w_row, axis=-1, keepdims=True))
        decay = jnp.exp(g_tot + m_prev - m_new)
        ws_col = jnp.exp(w_col - m_new)
        c_scr[h] = decay * c_prev + _dot_tn(k, (ws_col * v.astype(F32)).astype(BF16))
        n_scr[h:h + 1, :] = decay * n_prev + jnp.sum(ws_col * k.astype(F32), axis=0, keepdims=True)
        m_scr[h:h + 1, :] = jnp.broadcast_to(m_new, (1, m_scr.shape[1]))


def _mlstm(q, k, v, o, gates, head_g, *, heads=N_HEADS_MLSTM):
    b, t, w = q.shape
    cl = MLSTM_CHUNK
    dh = w // heads
    assert t % cl == 0 and dh == LANES
    blk = lambda wd: pl.BlockSpec((None, cl, wd), lambda bi, c: (bi, c, 0))
    return pl.pallas_call(
        functools.partial(_mlstm_kernel, heads=heads),
        grid=(b, t // cl),
        in_specs=[blk(w), blk(w), blk(w), blk(w), blk(LANES), _resident((1, w))],
        out_specs=blk(w),
        out_shape=jax.ShapeDtypeStruct((b, t, w), BF16),
        scratch_shapes=[pltpu.VMEM((heads, dh, dh), F32), pltpu.VMEM((heads, dh), F32),
                        pltpu.VMEM((heads, LANES), F32)],
        compiler_params=_params(("parallel", "arbitrary")),
        name="mlstm",
    )(q, k, v, o, gates, head_g.reshape(1, w))


def kernel(x, norm_g, ffn_w_gate, ffn_w_up, ffn_w_down, attn_w_in, attn_w_out, rel_bias,
           mlstm_w_in, mlstm_b_gates, mlstm_conv_w, mlstm_head_g, mlstm_w_out):
    bsz, t, d = x.shape
    depth = norm_g.shape[0]
    h = x.reshape(bsz * t, d)
    bf = lambda a: a.astype(BF16)

    def ffn(h, layer, half):
        g = norm_g[layer]
        return _ffn(h, g[2 * half * 2], g[2 * half * 2 + 1], bf(ffn_w_gate[layer, half]),
                    bf(ffn_w_up[layer, half]), bf(ffn_w_down[layer, half]))

    for layer in range(depth):
        g = norm_g[layer]
        j = layer // 2
        h = ffn(h, layer, 0)
        if layer % 2 == 0:
            sb, dil = _attn_proj(h, g[2], bf(attn_w_in[j]))
            out_sb = _sb_attention(sb.reshape(bsz, t, -1))
            out_dil = _dil_attention(dil.reshape(bsz, t, -1), rel_bias.astype(F32))
            parts = [out_sb.reshape(bsz * t, -1), out_dil.reshape(bsz * t, -1)]
            h = _out_proj(parts, h, g[3], bf(attn_w_out[j]))
        else:
            width = mlstm_w_out.shape[1]
            n_gate = mlstm_b_gates.shape[1]
            w_pad = jnp.pad(bf(mlstm_w_in[j]), ((0, 0), (0, LANES - n_gate)))
            b_pad = jnp.pad(mlstm_b_gates[j].astype(F32), (0, LANES - n_gate)).reshape(1, LANES)
            q, k, v, o, gates = _mlstm_proj(h, g[2], w_pad, mlstm_conv_w[j].astype(F32), b_pad, seq=t)
            r3 = lambda a: a.reshape(bsz, t, -1)
            hid = _mlstm(r3(q), r3(k), r3(v), r3(o), r3(gates), mlstm_head_g[j].astype(F32))
            h = _out_proj([hid.reshape(bsz * t, width)], h, g[3], bf(mlstm_w_out[j]))
        h = ffn(h, layer, 1)
    return h.reshape(bsz, t, d)
```

```python
import functools
import math

import numpy as np
import jax
import jax.numpy as jnp
from jax import lax
from jax.experimental import pallas as pl
from jax.experimental.pallas import tpu as pltpu

EPS = 1e-6
HEAD_DIM_ATTN = 64
N_HEADS_SB = 8
N_HEADS_DIL = 8
DIL_CONFIGS = ((128, 1), (512, 4), (2048, 16))
DIL_BLOCK = 128
NUM_BUCKETS = 32
MAX_DISTANCE = 2048
N_HEADS_MLSTM = 8
MLSTM_CHUNK = 128
CONV_WIDTH = 4
LANES = 128
CONV_PAD = 8
MASKED = -1e30
VMEM_LIMIT = 56 * 1024 * 1024

F32 = jnp.float32
BF16 = jnp.bfloat16


def _params(sem, vmem=VMEM_LIMIT):
    return pltpu.CompilerParams(dimension_semantics=sem, vmem_limit_bytes=vmem)


def _resident(shape):
    zeros = (0,) * len(shape)
    return pl.BlockSpec(shape, lambda *_: zeros, pipeline_mode=pl.Buffered(1))


def _rms(x, g):
    return x * lax.rsqrt(jnp.mean(x * x, axis=-1, keepdims=True) + EPS) * g


def _dot(a, b):
    return jnp.dot(a, b, preferred_element_type=F32)


def _dot_nt(a, b):
    return lax.dot_general(a, b, (((1,), (1,)), ((), ())), preferred_element_type=F32)


def _dot_tn(a, b):
    return lax.dot_general(a, b, (((0,), (0,)), ((), ())), preferred_element_type=F32)


def _ffn_kernel(x_ref, gin_ref, gout_ref, wg_ref, wu_ref, wd_ref, o_ref, *, ff_chunk):
    x = x_ref[...]
    xn = _rms(x, gin_ref[...]).astype(BF16)
    d_ff = wg_ref.shape[1]
    acc = None
    for c in range(d_ff // ff_chunk):
        sl = slice(c * ff_chunk, (c + 1) * ff_chunk)
        gate = _dot(xn, wg_ref[:, sl])
        up = _dot(xn, wu_ref[:, sl])
        h = (gate * jax.nn.sigmoid(gate) * up).astype(BF16)
        part = _dot(h, wd_ref[sl, :])
        acc = part if acc is None else acc + part
    o_ref[...] = x + 0.5 * _rms(acc, gout_ref[...])


def _ffn(x, g_in, g_out, wg, wu, wd, *, tm=512, ff_chunk=256):
    m, d = x.shape
    d_ff = wg.shape[1]
    assert m % tm == 0 and d_ff % ff_chunk == 0
    row = pl.BlockSpec((tm, d), lambda i: (i, 0))
    return pl.pallas_call(
        functools.partial(_ffn_kernel, ff_chunk=ff_chunk),
        grid=(m // tm,),
        in_specs=[row, _resident((1, d)), _resident((1, d)),
                  _resident((d, d_ff)), _resident((d, d_ff)), _resident((d_ff, d))],
        out_specs=row,
        out_shape=jax.ShapeDtypeStruct((m, d), F32),
        compiler_params=_params(("parallel",)),
        name="ffn",
    )(x, g_in.reshape(1, d), g_out.reshape(1, d), wg, wu, wd)


def _attn_proj_kernel(x_ref, g_ref, w_ref, sb_ref, dil_ref, *, n_chunk, scale):
    xn = _rms(x_ref[...], g_ref[...]).astype(BF16)
    w_sb = sb_ref.shape[1]
    w_q = w_sb // 3
    for c in range(w_sb // n_chunk):
        sl = slice(c * n_chunk, (c + 1) * n_chunk)
        y = _dot(xn, w_ref[:, sl])
        if (c + 1) * n_chunk <= w_q:
            y = y * scale
        sb_ref[:, sl] = y.astype(BF16)
    for c in range(dil_ref.shape[1] // n_chunk):
        sl = slice(c * n_chunk, (c + 1) * n_chunk)
        y = _dot(xn, w_ref[:, w_sb + c * n_chunk: w_sb + (c + 1) * n_chunk])
        if (c + 1) * n_chunk <= w_q:
            y = y * scale
        dil_ref[:, sl] = y


def _attn_proj(x, g, w, *, tm=512, n_chunk=512):
    m, d = x.shape
    n = w.shape[1]
    half = n // 2
    assert m % tm == 0 and (half // 3) % n_chunk == 0
    scale = 1.0 / math.sqrt(HEAD_DIM_ATTN)
    return pl.pallas_call(
        functools.partial(_attn_proj_kernel, n_chunk=n_chunk, scale=scale),
        grid=(m // tm,),
        in_specs=[pl.BlockSpec((tm, d), lambda i: (i, 0)), _resident((1, d)), _resident((d, n))],
        out_specs=[pl.BlockSpec((tm, half), lambda i: (i, 0)),
                   pl.BlockSpec((tm, half), lambda i: (i, 0))],
        out_shape=[jax.ShapeDtypeStruct((m, half), BF16), jax.ShapeDtypeStruct((m, half), F32)],
        compiler_params=_params(("parallel",)),
        name="attn_proj",
    )(x, g.reshape(1, d), w)


LOG2E = math.log2(math.e)
SB_DEAD_LOG2 = -160.0


def _sb_kernel(q_ref, k_ref, v_ref, tri_ref, o_ref, *, blk):
    i = pl.program_id(2)
    q2 = q_ref[...]
    lane = lax.broadcasted_iota(jnp.int32, (1, LANES), 1)
    row = lax.broadcasted_iota(jnp.int32, (blk, blk), 0)
    col = lax.broadcasted_iota(jnp.int32, (blk, blk), 1)
    causal = col < row
    tri = tri_ref[...]

    head0 = lane < HEAD_DIM_ATTN
    q_heads = (jnp.where(head0, q2, jnp.zeros_like(q2)), jnp.where(head0, jnp.zeros_like(q2), q2))

    def block(kb, state, diag):
        ks = pl.multiple_of(kb * blk, blk)
        k2 = k_ref[pl.ds(ks, blk), :]
        v2 = v_ref[pl.ds(ks, blk), :]
        new_state = []
        for qh, (carry, acc) in zip(q_heads, state):
            z = _dot_nt(qh, k2) * LOG2E
            log_keep = -(jnp.maximum(z, 0.0) + jnp.log2(1.0 + jnp.exp2(-jnp.abs(z))))
            if diag:
                log_keep = jnp.where(causal, log_keep, 0.0)
            hi = log_keep.astype(BF16)
            lo = (log_keep - hi.astype(F32)).astype(BF16)
            later = _dot(hi, tri) + _dot(lo, tri)
            p = jnp.exp2(z + log_keep + later + carry)
            if diag:
                p = jnp.where(causal, p, 0.0)
            acc = acc + _dot(p.astype(BF16), v2)
            carry = carry + jnp.sum(log_keep, axis=-1, keepdims=True)
            new_state.append((carry, acc))
        return tuple(new_state)

    def alive(state):
        top = jnp.maximum(jnp.max(state[0][0]), jnp.max(state[1][0]))
        return (top > SB_DEAD_LOG2).astype(jnp.int32)

    zero = (jnp.zeros((blk, 1), F32), jnp.zeros((blk, LANES), F32))
    state = block(i, (zero, zero), True)

    def cond(loop):
        t, live, _ = loop
        return (t < i) & (live > 0)

    def body(loop):
        t, _, state = loop
        state = block(i - 1 - t, state, False)
        return t + 1, alive(state), state

    _, _, state = lax.while_loop(cond, body, (jnp.int32(0), alive(state), state))
    o_ref[...] = jnp.where(head0, state[0][1], state[1][1]).astype(o_ref.dtype)


def _sb_attention(qkv, *, blk=256):
    b, t, w3 = qkv.shape
    w = w3 // 3
    pairs = w // LANES
    blk = min(blk, t)
    assert t % blk == 0
    tri = jnp.asarray(np.tril(np.ones((blk, blk), np.float32), -1), BF16)
    return pl.pallas_call(
        functools.partial(_sb_kernel, blk=blk),
        grid=(b, pairs, t // blk),
        in_specs=[pl.BlockSpec((None, blk, LANES), lambda bi, p, i: (bi, i, p)),
                  pl.BlockSpec((None, t, LANES), lambda bi, p, i: (bi, 0, pairs + p)),
                  pl.BlockSpec((None, t, LANES), lambda bi, p, i: (bi, 0, 2 * pairs + p)),
                  _resident((blk, blk))],
        out_specs=pl.BlockSpec((None, blk, LANES), lambda bi, p, i: (bi, i, p)),
        out_shape=jax.ShapeDtypeStruct((b, t, w), BF16),
        compiler_params=_params(("parallel", "parallel", "arbitrary")),
        name="sb_attn",
    )(qkv, qkv, qkv, tri)


def _t5_bucket_np(dist):
    max_exact = NUM_BUCKETS // 2
    d = np.maximum(dist, 1).astype(np.float32)
    log_b = max_exact + (np.log(d / np.float32(max_exact)) / np.float32(math.log(MAX_DISTANCE / max_exact))
                         * np.float32(NUM_BUCKETS - max_exact)).astype(np.int32)
    log_b = np.minimum(log_b, NUM_BUCKETS - 1)
    return np.where(dist < max_exact, dist, log_b)


def _dil_tables():
    qi = np.arange(DIL_BLOCK)[:, None]
    ki = np.arange(2 * DIL_BLOCK)[None, :]
    dist = qi + DIL_BLOCK - ki
    buckets, valid = [], []
    for window, dil in DIL_CONFIGS:
        steps = window // dil
        buckets.append(_t5_bucket_np(np.maximum(dist, 0) * dil))
        valid.append((dist >= 0) & (dist <= steps))
    return np.stack(buckets).astype(np.int32), np.stack(valid).astype(np.int32)


def _dil_kernel(rb_ref, bucket_ref, valid_ref, q_ref, k_ref, v_ref, o_ref,
                bias_scr, num_scr, m_scr, l_scr, *, seq):
    p = pl.program_id(1)
    qb = DIL_BLOCK
    lane = lax.broadcasted_iota(jnp.int32, (1, LANES), 1)
    head0 = lane < HEAD_DIM_ATTN
    first_half = lax.broadcasted_iota(jnp.int32, (1, 2 * qb), 1) < qb

    for br in range(len(DIL_CONFIGS)):
        bucket = bucket_ref[br]
        valid = valid_ref[br] > 0
        for h in range(2):
            bias = jnp.zeros((qb, 2 * qb), F32)
            for b in range(NUM_BUCKETS):
                bias = jnp.where(bucket == b, rb_ref[b, 2 * p + h], bias)
            bias_scr[br, h] = jnp.where(valid, bias, MASKED)

    for br, (_, dil) in enumerate(DIL_CONFIGS):
        n_units = seq // qb

        def unit(u, _, br=br, dil=dil):
            blk = lax.shift_right_logical(u, int(math.log2(dil)))
            res = u - blk * dil
            cur = blk * (qb * dil) + res
            prev = jnp.maximum(blk - 1, 0) * (qb * dil) + res

            def rows(ref, start):
                if dil == 1:
                    return ref[pl.ds(start, qb), :]
                return ref[pl.ds(start, qb, stride=dil), :]

            q2 = rows(q_ref, cur).astype(BF16)
            k2 = jnp.concatenate([rows(k_ref, prev), rows(k_ref, cur)], axis=0).astype(BF16)
            v2 = jnp.concatenate([rows(v_ref, prev), rows(v_ref, cur)], axis=0).astype(BF16)
            no_prev = jnp.where(blk == 0, MASKED, 0.0)
            pen = jnp.where(first_half, no_prev, 0.0)
            m2 = l2 = pv2 = None
            for h in range(2):
                qh = jnp.where(head0 if h == 0 else ~head0, q2, jnp.zeros_like(q2))
                s = _dot_nt(qh, k2) + bias_scr[br, h] + pen
                m = jnp.max(s, axis=-1, keepdims=True)
                e = jnp.exp(s - m)
                l = jnp.sum(e, axis=-1, keepdims=True)
                pv = _dot(e.astype(BF16), v2)
                if h == 0:
                    m2 = jnp.broadcast_to(m, (qb, LANES))
                    l2 = jnp.broadcast_to(l, (qb, LANES))
                    pv2 = pv
                else:
                    m2 = jnp.where(head0, m2, m)
                    l2 = jnp.where(head0, l2, l)
                    pv2 = jnp.where(head0, pv2, pv)
            if br == 0:
                idx = pl.ds(cur, qb)
                num_scr[idx, :] = pv2
                m_scr[idx, :] = m2
                l_scr[idx, :] = l2
            else:
                idx = pl.ds(cur, qb, stride=dil)
                m_old = m_scr[idx, :]
                m_new = jnp.maximum(m_old, m2)
                a_old = jnp.exp(m_old - m_new)
                a_new = jnp.exp(m2 - m_new)
                num_scr[idx, :] = a_old * num_scr[idx, :] + a_new * pv2
                l_scr[idx, :] = a_old * l_scr[idx, :] + a_new * l2
                m_scr[idx, :] = m_new
            return 0

        lax.fori_loop(0, n_units, unit, 0)

    rows_out = 256

    def finish(c, _):
        idx = pl.ds(pl.multiple_of(c * rows_out, rows_out), rows_out)
        o_ref[idx, :] = (num_scr[idx, :] / l_scr[idx, :]).astype(o_ref.dtype)
        return 0

    lax.fori_loop(0, seq // rows_out, finish, 0)


def _dil_attention(qkv, rel_bias):
    b, t, w3 = qkv.shape
    w = w3 // 3
    pairs = w // LANES
    assert t % (DIL_BLOCK * max(d for _, d in DIL_CONFIGS)) == 0 and t % 256 == 0
    bucket, valid = _dil_tables()
    nbr = len(DIL_CONFIGS)
    seq_spec = lambda off: pl.BlockSpec((None, t, LANES), lambda bi, p: (bi, 0, off + p))
    return pl.pallas_call(
        functools.partial(_dil_kernel, seq=t),
        grid=(b, pairs),
        in_specs=[pl.BlockSpec(memory_space=pltpu.SMEM),
                  _resident((nbr, DIL_BLOCK, 2 * DIL_BLOCK)),
                  _resident((nbr, DIL_BLOCK, 2 * DIL_BLOCK)),
                  seq_spec(0), seq_spec(pairs), seq_spec(2 * pairs)],
        out_specs=pl.BlockSpec((None, t, LANES), lambda bi, p: (bi, 0, p)),
        out_shape=jax.ShapeDtypeStruct((b, t, w), BF16),
        scratch_shapes=[pltpu.VMEM((nbr, 2, DIL_BLOCK, 2 * DIL_BLOCK), F32),
                        pltpu.VMEM((t, LANES), F32), pltpu.VMEM((t, LANES), F32),
                        pltpu.VMEM((t, LANES), F32)],
        compiler_params=_params(("parallel", "parallel")),
        name="dil_attn",
    )(rel_bias, jnp.asarray(bucket), jnp.asarray(valid), qkv, qkv, qkv)


def _out_proj_kernel(*refs, n_parts):
    parts, (x_ref, g_ref, w_ref, o_ref) = refs[:n_parts], refs[n_parts:]
    a = jnp.concatenate([r[...] for r in parts], axis=-1) if n_parts > 1 else parts[0][...]
    y = _dot(a, w_ref[...])
    o_ref[...] = x_ref[...] + _rms(y, g_ref[...])


def _out_proj(parts, x, g, w, *, tm=512):
    m, d = x.shape
    assert m % tm == 0
    row = lambda width: pl.BlockSpec((tm, width), lambda i: (i, 0))
    return pl.pallas_call(
        functools.partial(_out_proj_kernel, n_parts=len(parts)),
        grid=(m // tm,),
        in_specs=[row(a.shape[1]) for a in parts] + [row(d), _resident((1, d)), _resident(w.shape)],
        out_specs=row(d),
        out_shape=jax.ShapeDtypeStruct((m, d), F32),
        compiler_params=_params(("parallel",)),
        name="out_proj",
    )(*parts, x, g.reshape(1, d), w)


def _mlstm_proj_kernel(x_ref, g_ref, w_ref, cw_ref, bg_ref, q_ref, k_ref, v_ref, o_ref, gate_ref,
                       hist_scr, *, tiles_per_seq, n_chunk, scale):
    i = pl.program_id(0)
    tm = x_ref.shape[0]
    width = q_ref.shape[1]
    xn = _rms(x_ref[...], g_ref[...]).astype(BF16)

    @pl.when(i % tiles_per_seq == 0)
    def _():
        hist_scr[0:CONV_PAD, :] = jnp.zeros((CONV_PAD, hist_scr.shape[1]), F32)

    for c in range(2 * width // n_chunk):
        sl = slice(c * n_chunk, (c + 1) * n_chunk)
        hist_scr[CONV_PAD:CONV_PAD + tm, sl] = _dot(xn, w_ref[:, sl])
        y = None
        for tap in range(CONV_WIDTH):
            off = CONV_PAD - (CONV_WIDTH - 1) + tap
            term = cw_ref[tap:tap + 1, sl] * hist_scr[off:off + tm, sl]
            y = term if y is None else y + term
        y = y * jax.nn.sigmoid(y)
        hist_scr[0:CONV_PAD, sl] = hist_scr[tm:tm + CONV_PAD, sl]
        if c * n_chunk < width:
            q_ref[:, sl] = (y * scale).astype(BF16)
        else:
            k_ref[:, c * n_chunk - width:(c + 1) * n_chunk - width] = y.astype(BF16)
    for c in range(width // n_chunk):
        sl = slice(c * n_chunk, (c + 1) * n_chunk)
        v_ref[:, sl] = _dot(xn, w_ref[:, 2 * width + c * n_chunk:2 * width + (c + 1) * n_chunk]).astype(BF16)
        o_ref[:, sl] = _dot(xn, w_ref[:, 3 * width + c * n_chunk:3 * width + (c + 1) * n_chunk])
    gate_ref[...] = _dot(xn, w_ref[:, 4 * width:]) + bg_ref[...]


def _mlstm_proj(x, g, w_pad, conv_w, b_gates_pad, *, seq, tm=512, n_chunk=512):
    m, d = x.shape
    width = (w_pad.shape[1] - LANES) // 4
    assert m % tm == 0 and seq % tm == 0 and width % n_chunk == 0
    row = lambda wd: pl.BlockSpec((tm, wd), lambda i: (i, 0))
    scale = 1.0 / math.sqrt(width // N_HEADS_MLSTM)
    return pl.pallas_call(
        functools.partial(_mlstm_proj_kernel, tiles_per_seq=seq // tm, n_chunk=n_chunk, scale=scale),
        grid=(m // tm,),
        in_specs=[row(d), _resident((1, d)), _resident(w_pad.shape), _resident(conv_w.shape),
                  _resident((1, LANES))],
        out_specs=[row(width), row(width), row(width), row(width), row(LANES)],
        out_shape=[jax.ShapeDtypeStruct((m, width), BF16), jax.ShapeDtypeStruct((m, width), BF16),
                   jax.ShapeDtypeStruct((m, width), BF16), jax.ShapeDtypeStruct((m, width), F32),
                   jax.ShapeDtypeStruct((m, LANES), F32)],
        scratch_shapes=[pltpu.VMEM((tm + CONV_PAD, 2 * width), F32)],
        compiler_params=_params(("arbitrary",)),
        name="mlstm_proj",
    )(x, g.reshape(1, d), w_pad, conv_w, b_gates_pad)


def _mlstm_kernel(q_ref, k_ref, v_ref, o_ref, gate_ref, hg_ref, out_ref, c_scr, n_scr, m_scr, *, heads):
    cl = q_ref.shape[0]
    dh = q_ref.shape[1] // heads
    hi = lax.Precision.HIGHEST

    @pl.when(pl.program_id(1) == 0)
    def _():
        c_scr[...] = jnp.zeros_like(c_scr)
        n_scr[...] = jnp.zeros_like(n_scr)
        m_scr[...] = jnp.zeros_like(m_scr)

    row = lax.broadcasted_iota(jnp.int32, (cl, cl), 0)
    col = lax.broadcasted_iota(jnp.int32, (cl, cl), 1)
    lower = row >= col
    incl = jnp.where(lower, 1.0, 0.0)
    eye = jnp.where(row == col, 1.0, 0.0)

    gates = gate_ref[...]
    log_f = jax.nn.log_sigmoid(gates)
    b_all = jnp.dot(incl, log_f, precision=hi, preferred_element_type=F32)
    gates_t = lax.dot_general(gates, eye, (((0,), (0,)), ((), ())), precision=hi,
                              preferred_element_type=F32)
    b_all_t = lax.dot_general(b_all, eye, (((0,), (0,)), ((), ())), precision=hi,
                              preferred_element_type=F32)

    for h in range(heads):
        sl = slice(h * dh, (h + 1) * dh)
        q = q_ref[:, sl]
        k = k_ref[:, sl]
        v = v_ref[:, sl]
        i_row = gates_t[h:h + 1, :]
        i_col = gates[:, h:h + 1]
        b_row = b_all_t[heads + h:heads + h + 1, :]
        b_col = b_all[:, heads + h:heads + h + 1]
        g_tot = b_col[cl - 1:cl, :]
        c_prev = c_scr[h]
        n_prev = n_scr[h:h + 1, :]
        m_prev = m_scr[h:h + 1, 0:1]

        a = b_col + m_prev
        dmat = jnp.where(lower, b_col - b_row + i_row, -jnp.inf)
        m_t = jnp.maximum(a, jnp.max(dmat, axis=-1, keepdims=True))
        p = _dot_nt(q, k) * jnp.exp(dmat - m_t)
        w_inter = jnp.exp(a - m_t)
        h_num = _dot(p.astype(BF16), v) + w_inter * _dot(q, c_prev.astype(BF16))
        qn = jnp.sum(q.astype(F32) * n_prev, axis=-1, keepdims=True)
        n_num = jnp.sum(p, axis=-1, keepdims=True) + w_inter * qn
        hid = h_num / jnp.maximum(jnp.abs(n_num), jnp.exp(-m_t))
        hid = hid * lax.rsqrt(jnp.mean(hid * hid, axis=-1, keepdims=True) + EPS) * hg_ref[:, sl]
        out_ref[:, sl] = (hid * jax.nn.sigmoid(o_ref[:, sl])).astype(out_ref.dtype)

        w_row = g_tot - b_row + i_row
        w_col = g_tot - b_col + i_col
        m_new = jnp.maximum(g_tot + m_prev, jnp.max(w_row, axis=-1, keepdims=True))
        decay = jnp.exp(g_tot + m_prev - m_new)
        ws_col = jnp.exp(w_col - m_new)
        c_scr[h] = decay * c_prev + _dot_tn(k, (ws_col * v.astype(F32)).astype(BF16))
        n_scr[h:h + 1, :] = decay * n_prev + jnp.sum(ws_col * k.astype(F32), axis=0, keepdims=True)
        m_scr[h:h + 1, :] = jnp.broadcast_to(m_new, (1, m_scr.shape[1]))


def _mlstm(q, k, v, o, gates, head_g, *, heads=N_HEADS_MLSTM):
    b, t, w = q.shape
    cl = MLSTM_CHUNK
    dh = w // heads
    assert t % cl == 0 and dh == LANES
    blk = lambda wd: pl.BlockSpec((None, cl, wd), lambda bi, c: (bi, c, 0))
    return pl.pallas_call(
        functools.partial(_mlstm_kernel, heads=heads),
        grid=(b, t // cl),
        in_specs=[blk(w), blk(w), blk(w), blk(w), blk(LANES), _resident((1, w))],
        out_specs=blk(w),
        out_shape=jax.ShapeDtypeStruct((b, t, w), BF16),
        scratch_shapes=[pltpu.VMEM((heads, dh, dh), F32), pltpu.VMEM((heads, dh), F32),
                        pltpu.VMEM((heads, LANES), F32)],
        compiler_params=_params(("parallel", "arbitrary")),
        name="mlstm",
    )(q, k, v, o, gates, head_g.reshape(1, w))


def kernel(x, norm_g, ffn_w_gate, ffn_w_up, ffn_w_down, attn_w_in, attn_w_out, rel_bias,
           mlstm_w_in, mlstm_b_gates, mlstm_conv_w, mlstm_head_g, mlstm_w_out):
    bsz, t, d = x.shape
    depth = norm_g.shape[0]
    h = x.reshape(bsz * t, d)
    bf = lambda a: a.astype(BF16)

    def ffn(h, layer, half):
        g = norm_g[layer]
        return _ffn(h, g[2 * half * 2], g[2 * half * 2 + 1], bf(ffn_w_gate[layer, half]),
                    bf(ffn_w_up[layer, half]), bf(ffn_w_down[layer, half]))

    for layer in range(depth):
        g = norm_g[layer]
        j = layer // 2
        h = ffn(h, layer, 0)
        if layer % 2 == 0:
            sb, dil = _attn_proj(h, g[2], bf(attn_w_in[j]))
            out_sb = _sb_attention(sb.reshape(bsz, t, -1))
            out_dil = _dil_attention(dil.reshape(bsz, t, -1), rel_bias.astype(F32))
            parts = [out_sb.reshape(bsz * t, -1), out_dil.reshape(bsz * t, -1)]
            h = _out_proj(parts, h, g[3], bf(attn_w_out[j]))
        else:
            width = mlstm_w_out.shape[1]
            n_gate = mlstm_b_gates.shape[1]
            w_pad = jnp.pad(bf(mlstm_w_in[j]), ((0, 0), (0, LANES - n_gate)))
            b_pad = jnp.pad(mlstm_b_gates[j].astype(F32), (0, LANES - n_gate)).reshape(1, LANES)
            q, k, v, o, gates = _mlstm_proj(h, g[2], w_pad, mlstm_conv_w[j].astype(F32), b_pad, seq=t)
            r3 = lambda a: a.reshape(bsz, t, -1)
            hid = _mlstm(r3(q), r3(k), r3(v), r3(o), r3(gates), mlstm_head_g[j].astype(F32))
            h = _out_proj([hid.reshape(bsz * t, width)], h, g[3], bf(mlstm_w_out[j]))
        h = ffn(h, layer, 1)
    return h.reshape(bsz, t, d)
```

```python
import functools
import math

import numpy as np
import jax
import jax.numpy as jnp
from jax import lax
from jax.experimental import pallas as pl
from jax.experimental.pallas import tpu as pltpu

EPS = 1e-6
HEAD_DIM_ATTN = 64
N_HEADS_SB = 8
N_HEADS_DIL = 8
DIL_CONFIGS = ((128, 1), (512, 4), (2048, 16))
DIL_BLOCK = 128
DIL_UNROLL = 8
NUM_BUCKETS = 32
MAX_DISTANCE = 2048
N_HEADS_MLSTM = 8
MLSTM_CHUNK = 128
CONV_WIDTH = 4
LANES = 128
CONV_PAD = 8
MASKED = -1e30
VMEM_LIMIT = 56 * 1024 * 1024

F32 = jnp.float32
BF16 = jnp.bfloat16


def _params(sem, vmem=VMEM_LIMIT):
    return pltpu.CompilerParams(dimension_semantics=sem, vmem_limit_bytes=vmem)


def _resident(shape):
    zeros = (0,) * len(shape)
    return pl.BlockSpec(shape, lambda *_: zeros, pipeline_mode=pl.Buffered(1))


def _rms(x, g):
    return x * lax.rsqrt(jnp.mean(x * x, axis=-1, keepdims=True) + EPS) * g


def _dot(a, b):
    return jnp.dot(a, b, preferred_element_type=F32)


def _dot_nt(a, b):
    return lax.dot_general(a, b, (((1,), (1,)), ((), ())), preferred_element_type=F32)


def _dot_tn(a, b):
    return lax.dot_general(a, b, (((0,), (0,)), ((), ())), preferred_element_type=F32)


def _ffn_kernel(x_ref, gin_ref, gout_ref, wg_ref, wu_ref, wd_ref, o_ref, *, ff_chunk):
    x = x_ref[...]
    xn = _rms(x, gin_ref[...]).astype(BF16)
    d_ff = wg_ref.shape[1]
    acc = None
    for c in range(d_ff // ff_chunk):
        sl = slice(c * ff_chunk, (c + 1) * ff_chunk)
        gate = _dot(xn, wg_ref[:, sl])
        up = _dot(xn, wu_ref[:, sl])
        h = (gate * jax.nn.sigmoid(gate) * up).astype(BF16)
        part = _dot(h, wd_ref[sl, :])
        acc = part if acc is None else acc + part
    o_ref[...] = x + 0.5 * _rms(acc, gout_ref[...])


def _ffn(x, g_in, g_out, wg, wu, wd, *, tm=512, ff_chunk=256):
    m, d = x.shape
    d_ff = wg.shape[1]
    assert m % tm == 0 and d_ff % ff_chunk == 0
    row = pl.BlockSpec((tm, d), lambda i: (i, 0))
    return pl.pallas_call(
        functools.partial(_ffn_kernel, ff_chunk=ff_chunk),
        grid=(m // tm,),
        in_specs=[row, _resident((1, d)), _resident((1, d)),
                  _resident((d, d_ff)), _resident((d, d_ff)), _resident((d_ff, d))],
        out_specs=row,
        out_shape=jax.ShapeDtypeStruct((m, d), F32),
        compiler_params=_params(("parallel",)),
        name="ffn",
    )(x, g_in.reshape(1, d), g_out.reshape(1, d), wg, wu, wd)


def _attn_proj_kernel(x_ref, g_ref, w_ref, sb_ref, dil_ref, *, n_chunk, scale):
    xn = _rms(x_ref[...], g_ref[...]).astype(BF16)
    w_sb = sb_ref.shape[1]
    w_q = w_sb // 3
    for c in range(w_sb // n_chunk):
        sl = slice(c * n_chunk, (c + 1) * n_chunk)
        y = _dot(xn, w_ref[:, sl])
        if (c + 1) * n_chunk <= w_q:
            y = y * scale
        sb_ref[:, sl] = y.astype(BF16)
    for c in range(dil_ref.shape[1] // n_chunk):
        sl = slice(c * n_chunk, (c + 1) * n_chunk)
        y = _dot(xn, w_ref[:, w_sb + c * n_chunk: w_sb + (c + 1) * n_chunk])
        if (c + 1) * n_chunk <= w_q:
            y = y * scale
        dil_ref[:, sl] = y


def _attn_proj(x, g, w, *, tm=512, n_chunk=512):
    m, d = x.shape
    n = w.shape[1]
    half = n // 2
    assert m % tm == 0 and (half // 3) % n_chunk == 0
    scale = 1.0 / math.sqrt(HEAD_DIM_ATTN)
    return pl.pallas_call(
        functools.partial(_attn_proj_kernel, n_chunk=n_chunk, scale=scale),
        grid=(m // tm,),
        in_specs=[pl.BlockSpec((tm, d), lambda i: (i, 0)), _resident((1, d)), _resident((d, n))],
        out_specs=[pl.BlockSpec((tm, half), lambda i: (i, 0)),
                   pl.BlockSpec((tm, half), lambda i: (i, 0))],
        out_shape=[jax.ShapeDtypeStruct((m, half), BF16), jax.ShapeDtypeStruct((m, half), F32)],
        compiler_params=_params(("parallel",)),
        name="attn_proj",
    )(x, g.reshape(1, d), w)


LOG2E = math.log2(math.e)
SB_DEAD_LOG2 = -160.0


def _sb_kernel(q_ref, k_ref, v_ref, tri_ref, o_ref, *, blk):
    i = pl.program_id(2)
    q2 = q_ref[...]
    lane = lax.broadcasted_iota(jnp.int32, (1, LANES), 1)
    row = lax.broadcasted_iota(jnp.int32, (blk, blk), 0)
    col = lax.broadcasted_iota(jnp.int32, (blk, blk), 1)
    causal = col < row
    tri = tri_ref[...]

    head0 = lane < HEAD_DIM_ATTN
    q_heads = (jnp.where(head0, q2, jnp.zeros_like(q2)), jnp.where(head0, jnp.zeros_like(q2), q2))

    def block(kb, state, diag):
        ks = pl.multiple_of(kb * blk, blk)
        k2 = k_ref[pl.ds(ks, blk), :]
        v2 = v_ref[pl.ds(ks, blk), :]
        new_state = []
        for qh, (carry, acc) in zip(q_heads, state):
            z = _dot_nt(qh, k2) * LOG2E
            log_keep = -(jnp.maximum(z, 0.0) + jnp.log2(1.0 + jnp.exp2(-jnp.abs(z))))
            if diag:
                log_keep = jnp.where(causal, log_keep, 0.0)
            hi = log_keep.astype(BF16)
            lo = (log_keep - hi.astype(F32)).astype(BF16)
            later = _dot(hi, tri) + _dot(lo, tri)
            p = jnp.exp2(z + log_keep + later + carry)
            if diag:
                p = jnp.where(causal, p, 0.0)
            acc = acc + _dot(p.astype(BF16), v2)
            carry = carry + jnp.sum(log_keep, axis=-1, keepdims=True)
            new_state.append((carry, acc))
        return tuple(new_state)

    def alive(state):
        top = jnp.maximum(jnp.max(state[0][0]), jnp.max(state[1][0]))
        return (top > SB_DEAD_LOG2).astype(jnp.int32)

    zero = (jnp.zeros((blk, 1), F32), jnp.zeros((blk, LANES), F32))
    state = block(i, (zero, zero), True)

    def cond(loop):
        t, live, _ = loop
        return (t < i) & (live > 0)

    def body(loop):
        t, _, state = loop
        state = block(i - 1 - t, state, False)
        return t + 1, alive(state), state

    _, _, state = lax.while_loop(cond, body, (jnp.int32(0), alive(state), state))
    o_ref[...] = jnp.where(head0, state[0][1], state[1][1]).astype(o_ref.dtype)


def _sb_attention(qkv, *, blk=256):
    b, t, w3 = qkv.shape
    w = w3 // 3
    pairs = w // LANES
    blk = min(blk, t)
    assert t % blk == 0
    tri = jnp.asarray(np.tril(np.ones((blk, blk), np.float32), -1), BF16)
    return pl.pallas_call(
        functools.partial(_sb_kernel, blk=blk),
        grid=(b, pairs, t // blk),
        in_specs=[pl.BlockSpec((None, blk, LANES), lambda bi, p, i: (bi, i, p)),
                  pl.BlockSpec((None, t, LANES), lambda bi, p, i: (bi, 0, pairs + p)),
                  pl.BlockSpec((None, t, LANES), lambda bi, p, i: (bi, 0, 2 * pairs + p)),
                  _resident((blk, blk))],
        out_specs=pl.BlockSpec((None, blk, LANES), lambda bi, p, i: (bi, i, p)),
        out_shape=jax.ShapeDtypeStruct((b, t, w), BF16),
        compiler_params=_params(("parallel", "parallel", "arbitrary")),
        name="sb_attn",
    )(qkv, qkv, qkv, tri)


def _t5_bucket_np(dist):
    max_exact = NUM_BUCKETS // 2
    d = np.maximum(dist, 1).astype(np.float32)
    log_b = max_exact + (np.log(d / np.float32(max_exact)) / np.float32(math.log(MAX_DISTANCE / max_exact))
                         * np.float32(NUM_BUCKETS - max_exact)).astype(np.int32)
    log_b = np.minimum(log_b, NUM_BUCKETS - 1)
    return np.where(dist < max_exact, dist, log_b)


def _dil_tables():
    qi = np.arange(DIL_BLOCK)[:, None]
    ki = np.arange(2 * DIL_BLOCK)[None, :]
    dist = qi + DIL_BLOCK - ki
    buckets, valid = [], []
    for window, dil in DIL_CONFIGS:
        steps = window // dil
        buckets.append(_t5_bucket_np(np.maximum(dist, 0) * dil))
        valid.append((dist >= 0) & (dist <= steps))
    return np.stack(buckets).astype(np.int32), np.stack(valid).astype(np.int32)


def _dil_kernel(rb_ref, bucket_ref, valid_ref, q_ref, k_ref, v_ref, o_ref,
                bias_scr, num_scr, m_scr, l_scr, *, seq):
    p = pl.program_id(0)
    qb = DIL_BLOCK
    lane = lax.broadcasted_iota(jnp.int32, (1, LANES), 1)
    head0 = lane < HEAD_DIM_ATTN
    first_half = lax.broadcasted_iota(jnp.int32, (1, 2 * qb), 1) < qb

    @pl.when(pl.program_id(1) == 0)
    def _():
        for br in range(len(DIL_CONFIGS)):
            bucket = bucket_ref[br]
            valid = valid_ref[br] > 0
            for h in range(2):
                bias = jnp.zeros((qb, 2 * qb), F32)
                for b in range(NUM_BUCKETS):
                    bias = jnp.where(bucket == b, rb_ref[b, 2 * p + h], bias)
                bias_scr[br, h * qb:(h + 1) * qb, :] = jnp.where(valid, bias, MASKED)

    for br, (_, dil) in enumerate(DIL_CONFIGS):
        n_units = seq // qb

        n_blocks = seq // (qb * dil)
        run = min(DIL_UNROLL, n_blocks)
        runs = DIL_UNROLL // run
        assert DIL_UNROLL % run == 0 and n_blocks % run == 0 and (dil == 1 or n_blocks == run)

        def rows(ref, start, dil=dil):
            if dil == 1:
                return ref[pl.ds(start, qb), :].astype(BF16)
            return ref[pl.ds(start, qb, stride=dil), :].astype(BF16)

        def load_run(rho, dil=dil, run=run, whole=(n_blocks == run)):
            res, base = (rho, 0) if whole else (0, rho * run)
            starts = [(base + i) * (qb * dil) + res for i in range(run)]
            kb = [rows(k_ref, st) for st in starts]
            vb = [rows(v_ref, st) for st in starts]
            if whole:
                k_prev, v_prev = kb[0], vb[0]
                pen = jnp.where(first_half, MASKED, 0.0)
            else:
                st = jnp.maximum(base - 1, 0) * (qb * dil) + res
                k_prev, v_prev = rows(k_ref, st), rows(v_ref, st)
                pen = jnp.where(first_half, jnp.where(base == 0, MASKED, 0.0), 0.0)
            units = []
            for i, st in enumerate(starts):
                q2 = rows(q_ref, st)
                zeros = jnp.zeros_like(q2)
                qq = jnp.concatenate([jnp.where(head0, q2, zeros), jnp.where(head0, zeros, q2)], axis=0)
                k2 = jnp.concatenate([kb[i - 1] if i else k_prev, kb[i]], axis=0)
                v2 = jnp.concatenate([vb[i - 1] if i else v_prev, vb[i]], axis=0)
                units.append((st, pen if i == 0 else None, qq, k2, v2))
            return units

        def softmax_parts(s):
            m = jnp.max(s, axis=-1, keepdims=True)
            e = jnp.exp(s - m)
            return m, e, jnp.sum(e, axis=-1, keepdims=True)

        def group(g, _, br=br, dil=dil, runs=runs):
            units = [u for r in range(runs) for u in load_run(g * runs + r)]
            scores, parts = {}, {}
            for j in range(DIL_UNROLL + 2):
                if j < DIL_UNROLL:
                    _, pen, qq, k2, _ = units[j]
                    scores[j] = _dot_nt(qq, k2) + bias_scr[br]
                    if pen is not None:
                        scores[j] = scores[j] + pen
                if 0 <= j - 1 < DIL_UNROLL:
                    parts[j - 1] = softmax_parts(scores.pop(j - 1))
                if 0 <= j - 2 < DIL_UNROLL:
                    st, _, _, _, v2 = units[j - 2]
                    m, e, l = parts.pop(j - 2)
                    pv = _dot(e.astype(BF16), v2)
                    idx = pl.ds(st, qb) if dil == 1 else pl.ds(st, qb, stride=dil)
                    num_scr[br, idx, :] = jnp.where(head0, pv[:qb], pv[qb:])
                    m_scr[br, idx, :] = jnp.where(head0, m[:qb], m[qb:])
                    l_scr[br, idx, :] = jnp.where(head0, l[:qb], l[qb:])
            return 0

        lax.fori_loop(0, n_units // DIL_UNROLL, group, 0)

    rows_out = 256

    def finish(c, _):
        idx = pl.ds(pl.multiple_of(c * rows_out, rows_out), rows_out)
        m_all = [m_scr[br, idx, :] for br in range(len(DIL_CONFIGS))]
        m_max = functools.reduce(jnp.maximum, m_all)
        num = den = None
        for br, m_br in enumerate(m_all):
            wt = jnp.exp(m_br - m_max)
            n_br = wt * num_scr[br, idx, :]
            d_br = wt * l_scr[br, idx, :]
            num = n_br if num is None else num + n_br
            den = d_br if den is None else den + d_br
        o_ref[idx, :] = (num / den).astype(o_ref.dtype)
        return 0

    lax.fori_loop(0, seq // rows_out, finish, 0)


def _dil_attention(qkv, rel_bias):
    b, t, w3 = qkv.shape
    w = w3 // 3
    pairs = w // LANES
    assert t % (DIL_BLOCK * max(d for _, d in DIL_CONFIGS)) == 0 and t % 256 == 0
    bucket, valid = _dil_tables()
    nbr = len(DIL_CONFIGS)
    seq_spec = lambda off: pl.BlockSpec((None, t, LANES), lambda p, bi: (bi, 0, off + p))
    return pl.pallas_call(
        functools.partial(_dil_kernel, seq=t),
        grid=(pairs, b),
        in_specs=[pl.BlockSpec(memory_space=pltpu.SMEM),
                  _resident((nbr, DIL_BLOCK, 2 * DIL_BLOCK)),
                  _resident((nbr, DIL_BLOCK, 2 * DIL_BLOCK)),
                  seq_spec(0), seq_spec(pairs), seq_spec(2 * pairs)],
        out_specs=pl.BlockSpec((None, t, LANES), lambda p, bi: (bi, 0, p)),
        out_shape=jax.ShapeDtypeStruct((b, t, w), BF16),
        scratch_shapes=[pltpu.VMEM((nbr, 2 * DIL_BLOCK, 2 * DIL_BLOCK), F32),
                        pltpu.VMEM((nbr, t, LANES), F32), pltpu.VMEM((nbr, t, LANES), F32),
                        pltpu.VMEM((nbr, t, LANES), F32)],
        compiler_params=_params(("arbitrary", "arbitrary")),
        name="dil_attn",
    )(rel_bias, jnp.asarray(bucket), jnp.asarray(valid), qkv, qkv, qkv)


def _out_proj_kernel(*refs, n_parts):
    parts, (x_ref, g_ref, w_ref, o_ref) = refs[:n_parts], refs[n_parts:]
    a = jnp.concatenate([r[...] for r in parts], axis=-1) if n_parts > 1 else parts[0][...]
    y = _dot(a, w_ref[...])
    o_ref[...] = x_ref[...] + _rms(y, g_ref[...])


def _out_proj(parts, x, g, w, *, tm=512):
    m, d = x.shape
    assert m % tm == 0
    row = lambda width: pl.BlockSpec((tm, width), lambda i: (i, 0))
    return pl.pallas_call(
        functools.partial(_out_proj_kernel, n_parts=len(parts)),
        grid=(m // tm,),
        in_specs=[row(a.shape[1]) for a in parts] + [row(d), _resident((1, d)), _resident(w.shape)],
        out_specs=row(d),
        out_shape=jax.ShapeDtypeStruct((m, d), F32),
        compiler_params=_params(("parallel",)),
        name="out_proj",
    )(*parts, x, g.reshape(1, d), w)


def _mlstm_proj_kernel(x_ref, g_ref, w_ref, cw_ref, bg_ref, q_ref, k_ref, v_ref, o_ref, gate_ref,
                       hist_scr, *, tiles_per_seq, n_chunk, scale):
    i = pl.program_id(0)
    tm = x_ref.shape[0]
    width = q_ref.shape[1]
    xn = _rms(x_ref[...], g_ref[...]).astype(BF16)

    @pl.when(i % tiles_per_seq == 0)
    def _():
        hist_scr[0:CONV_PAD, :] = jnp.zeros((CONV_PAD, hist_scr.shape[1]), F32)

    for c in range(2 * width // n_chunk):
        sl = slice(c * n_chunk, (c + 1) * n_chunk)
        hist_scr[CONV_PAD:CONV_PAD + tm, sl] = _dot(xn, w_ref[:, sl])
        y = None
        for tap in range(CONV_WIDTH):
            off = CONV_PAD - (CONV_WIDTH - 1) + tap
            term = cw_ref[tap:tap + 1, sl] * hist_scr[off:off + tm, sl]
            y = term if y is None else y + term
        y = y * jax.nn.sigmoid(y)
        hist_scr[0:CONV_PAD, sl] = hist_scr[tm:tm + CONV_PAD, sl]
        if c * n_chunk < width:
            q_ref[:, sl] = (y * scale).astype(BF16)
        else:
            k_ref[:, c * n_chunk - width:(c + 1) * n_chunk - width] = y.astype(BF16)
    for c in range(width // n_chunk):
        sl = slice(c * n_chunk, (c + 1) * n_chunk)
        v_ref[:, sl] = _dot(xn, w_ref[:, 2 * width + c * n_chunk:2 * width + (c + 1) * n_chunk]).astype(BF16)
        o_ref[:, sl] = _dot(xn, w_ref[:, 3 * width + c * n_chunk:3 * width + (c + 1) * n_chunk])
    gate_ref[...] = _dot(xn, w_ref[:, 4 * width:]) + bg_ref[...]


def _mlstm_proj(x, g, w_pad, conv_w, b_gates_pad, *, seq, tm=512, n_chunk=512):
    m, d = x.shape
    width = (w_pad.shape[1] - LANES) // 4
    assert m % tm == 0 and seq % tm == 0 and width % n_chunk == 0
    row = lambda wd: pl.BlockSpec((tm, wd), lambda i: (i, 0))
    scale = 1.0 / math.sqrt(width // N_HEADS_MLSTM)
    return pl.pallas_call(
        functools.partial(_mlstm_proj_kernel, tiles_per_seq=seq // tm, n_chunk=n_chunk, scale=scale),
        grid=(m // tm,),
        in_specs=[row(d), _resident((1, d)), _resident(w_pad.shape), _resident(conv_w.shape),
                  _resident((1, LANES))],
        out_specs=[row(width), row(width), row(width), row(width), row(LANES)],
        out_shape=[jax.ShapeDtypeStruct((m, width), BF16), jax.ShapeDtypeStruct((m, width), BF16),
                   jax.ShapeDtypeStruct((m, width), BF16), jax.ShapeDtypeStruct((m, width), F32),
                   jax.ShapeDtypeStruct((m, LANES), F32)],
        scratch_shapes=[pltpu.VMEM((tm + CONV_PAD, 2 * width), F32)],
        compiler_params=_params(("arbitrary",)),
        name="mlstm_proj",
    )(x, g.reshape(1, d), w_pad, conv_w, b_gates_pad)


def _mlstm_kernel(q_ref, k_ref, v_ref, o_ref, gate_ref, hg_ref, out_ref, c_scr, n_scr, m_scr, *, heads):
    cl = q_ref.shape[0]
    dh = q_ref.shape[1] // heads
    hi = lax.Precision.HIGHEST

    @pl.when(pl.program_id(1) == 0)
    def _():
        c_scr[...] = jnp.zeros_like(c_scr)
        n_scr[...] = jnp.zeros_like(n_scr)
        m_scr[...] = jnp.zeros_like(m_scr)

    row = lax.broadcasted_iota(jnp.int32, (cl, cl), 0)
    col = lax.broadcasted_iota(jnp.int32, (cl, cl), 1)
    lower = row >= col
    incl = jnp.where(lower, 1.0, 0.0)
    eye = jnp.where(row == col, 1.0, 0.0)

    gates = gate_ref[...]
    log_f = jax.nn.log_sigmoid(gates)
    b_all = jnp.dot(incl, log_f, precision=hi, preferred_element_type=F32)
    gates_t = lax.dot_general(gates, eye, (((0,), (0,)), ((), ())), precision=hi,
                              preferred_element_type=F32)
    b_all_t = lax.dot_general(b_all, eye, (((0,), (0,)), ((), ())), precision=hi,
                              preferred_element_type=F32)

    for h in range(heads):
        sl = slice(h * dh, (h + 1) * dh)
        q = q_ref[:, sl]
        k = k_ref[:, sl]
        v = v_ref[:, sl]
        i_row = gates_t[h:h + 1, :]
        i_col = gates[:, h:h + 1]
        b_row = b_all_t[heads + h:heads + h + 1, :]
        b_col = b_all[:, heads + h:heads + h + 1]
        g_tot = b_col[cl - 1:cl, :]
        c_prev = c_scr[h]
        n_prev = n_scr[h:h + 1, :]
        m_prev = m_scr[h:h + 1, 0:1]

        a = b_col + m_prev
        dmat = jnp.where(lower, b_col - b_row + i_row, -jnp.inf)
        m_t = jnp.maximum(a, jnp.max(dmat, axis=-1, keepdims=True))
        p = _dot_nt(q, k) * jnp.exp(dmat - m_t)
        w_inter = jnp.exp(a - m_t)
        h_num = _dot(p.astype(BF16), v) + w_inter * _dot(q, c_prev.astype(BF16))
        qn = jnp.sum(q.astype(F32) * n_prev, axis=-1, keepdims=True)
        n_num = jnp.sum(p, axis=-1, keepdims=True) + w_inter * qn
        hid = h_num / jnp.maximum(jnp.abs(n_num), jnp.exp(-m_t))
        hid = hid * lax.rsqrt(jnp.mean(hid * hid, axis=-1, keepdims=True) + EPS) * hg_ref[:, sl]
        out_ref[:, sl] = (hid * jax.nn.sigmoid(o_ref[:, sl])).astype(out_ref.dtype)

        w_row = g_tot - b_row + i_row
        w_col = g_tot - b_col + i_col
        m_new = jnp.maximum(g_tot + m_prev, jnp.max(w_row, axis=-1, keepdims=True))
        decay = jnp.exp(g_tot + m_prev - m_new)
        ws_col = jnp.exp(w_col - m_new)
        c_scr[h] = decay * c_prev + _dot_tn(k, (ws_col * v.astype(F32)).astype(BF16))
        n_scr[h:h + 1, :] = decay * n_prev + jnp.sum(ws_col * k.astype(F32), axis=0, keepdims=True)
        m_scr[h:h + 1, :] = jnp.broadcast_to(m_new, (1, m_scr.shape[1]))


def _mlstm(q, k, v, o, gates, head_g, *, heads=N_HEADS_MLSTM):
    b, t, w = q.shape
    cl = MLSTM_CHUNK
    dh = w // heads
    assert t % cl == 0 and dh == LANES
    blk = lambda wd: pl.BlockSpec((None, cl, wd), lambda bi, c: (bi, c, 0))
    return pl.pallas_call(
        functools.partial(_mlstm_kernel, heads=heads),
        grid=(b, t // cl),
        in_specs=[blk(w), blk(w), blk(w), blk(w), blk(LANES), _resident((1, w))],
        out_specs=blk(w),
        out_shape=jax.ShapeDtypeStruct((b, t, w), BF16),
        scratch_shapes=[pltpu.VMEM((heads, dh, dh), F32), pltpu.VMEM((heads, dh), F32),
                        pltpu.VMEM((heads, LANES), F32)],
        compiler_params=_params(("parallel", "arbitrary")),
        name="mlstm",
    )(q, k, v, o, gates, head_g.reshape(1, w))


def kernel(x, norm_g, ffn_w_gate, ffn_w_up, ffn_w_down, attn_w_in, attn_w_out, rel_bias,
           mlstm_w_in, mlstm_b_gates, mlstm_conv_w, mlstm_head_g, mlstm_w_out):
    bsz, t, d = x.shape
    depth = norm_g.shape[0]
    h = x.reshape(bsz * t, d)
    bf = lambda a: a.astype(BF16)

    def ffn(h, layer, half):
        g = norm_g[layer]
        return _ffn(h, g[2 * half * 2], g[2 * half * 2 + 1], bf(ffn_w_gate[layer, half]),
                    bf(ffn_w_up[layer, half]), bf(ffn_w_down[layer, half]))

    for layer in range(depth):
        g = norm_g[layer]
        j = layer // 2
        h = ffn(h, layer, 0)
        if layer % 2 == 0:
            sb, dil = _attn_proj(h, g[2], bf(attn_w_in[j]))
            out_sb = _sb_attention(sb.reshape(bsz, t, -1))
            out_dil = _dil_attention(dil.reshape(bsz, t, -1), rel_bias.astype(F32))
            parts = [out_sb.reshape(bsz * t, -1), out_dil.reshape(bsz * t, -1)]
            h = _out_proj(parts, h, g[3], bf(attn_w_out[j]))
        else:
            width = mlstm_w_out.shape[1]
            n_gate = mlstm_b_gates.shape[1]
            w_pad = jnp.pad(bf(mlstm_w_in[j]), ((0, 0), (0, LANES - n_gate)))
            b_pad = jnp.pad(mlstm_b_gates[j].astype(F32), (0, LANES - n_gate)).reshape(1, LANES)
            q, k, v, o, gates = _mlstm_proj(h, g[2], w_pad, mlstm_conv_w[j].astype(F32), b_pad, seq=t)
            r3 = lambda a: a.reshape(bsz, t, -1)
            hid = _mlstm(r3(q), r3(k), r3(v), r3(o), r3(gates), mlstm_head_g[j].astype(F32))
            h = _out_proj([hid.reshape(bsz * t, width)], h, g[3], bf(mlstm_w_out[j]))
        h = ffn(h, layer, 1)
    return h.reshape(bsz, t, d)
```

```python
import functools
import math

import numpy as np
import jax
import jax.numpy as jnp
from jax import lax
from jax.experimental import pallas as pl
from jax.experimental.pallas import tpu as pltpu

EPS = 1e-6
HEAD_DIM_ATTN = 64
N_HEADS_SB = 8
N_HEADS_DIL = 8
DIL_CONFIGS = ((128, 1), (512, 4), (2048, 16))
DIL_BLOCK = 128
DIL_UNROLL = 8
NUM_BUCKETS = 32
MAX_DISTANCE = 2048
N_HEADS_MLSTM = 8
MLSTM_CHUNK = 128
CONV_WIDTH = 4
LANES = 128
CONV_PAD = 8
MASKED = -1e30
VMEM_LIMIT = 56 * 1024 * 1024

F32 = jnp.float32
BF16 = jnp.bfloat16


def _params(sem, vmem=VMEM_LIMIT):
    return pltpu.CompilerParams(dimension_semantics=sem, vmem_limit_bytes=vmem)


def _resident(shape):
    zeros = (0,) * len(shape)
    return pl.BlockSpec(shape, lambda *_: zeros, pipeline_mode=pl.Buffered(1))


def _rms(x, g):
    return x * lax.rsqrt(jnp.mean(x * x, axis=-1, keepdims=True) + EPS) * g


def _dot(a, b):
    return jnp.dot(a, b, preferred_element_type=F32)


def _dot_nt(a, b):
    return lax.dot_general(a, b, (((1,), (1,)), ((), ())), preferred_element_type=F32)


def _dot_tn(a, b):
    return lax.dot_general(a, b, (((0,), (0,)), ((), ())), preferred_element_type=F32)


def _ffn_kernel(x_ref, gin_ref, gout_ref, wg_ref, wu_ref, wd_ref, o_ref, *, ff_chunk):
    x = x_ref[...]
    xn = _rms(x, gin_ref[...]).astype(BF16)
    d_ff = wg_ref.shape[1]
    acc = None
    for c in range(d_ff // ff_chunk):
        sl = slice(c * ff_chunk, (c + 1) * ff_chunk)
        gate = _dot(xn, wg_ref[:, sl])
        up = _dot(xn, wu_ref[:, sl])
        h = (gate * jax.nn.sigmoid(gate) * up).astype(BF16)
        part = _dot(h, wd_ref[sl, :])
        acc = part if acc is None else acc + part
    o_ref[...] = x + 0.5 * _rms(acc, gout_ref[...])


def _ffn(x, g_in, g_out, wg, wu, wd, *, tm=512, ff_chunk=256):
    m, d = x.shape
    d_ff = wg.shape[1]
    assert m % tm == 0 and d_ff % ff_chunk == 0
    row = pl.BlockSpec((tm, d), lambda i: (i, 0))
    return pl.pallas_call(
        functools.partial(_ffn_kernel, ff_chunk=ff_chunk),
        grid=(m // tm,),
        in_specs=[row, _resident((1, d)), _resident((1, d)),
                  _resident((d, d_ff)), _resident((d, d_ff)), _resident((d_ff, d))],
        out_specs=row,
        out_shape=jax.ShapeDtypeStruct((m, d), F32),
        compiler_params=_params(("parallel",)),
        name="ffn",
    )(x, g_in.reshape(1, d), g_out.reshape(1, d), wg, wu, wd)


def _attn_proj_kernel(x_ref, g_ref, w_ref, sb_ref, dil_ref, *, n_chunk, scale):
    xn = _rms(x_ref[...], g_ref[...]).astype(BF16)
    w_sb = sb_ref.shape[1]
    w_q = w_sb // 3
    for c in range(w_sb // n_chunk):
        sl = slice(c * n_chunk, (c + 1) * n_chunk)
        y = _dot(xn, w_ref[:, sl])
        if (c + 1) * n_chunk <= w_q:
            y = y * scale
        sb_ref[:, sl] = y.astype(BF16)
    for c in range(dil_ref.shape[1] // n_chunk):
        sl = slice(c * n_chunk, (c + 1) * n_chunk)
        y = _dot(xn, w_ref[:, w_sb + c * n_chunk: w_sb + (c + 1) * n_chunk])
        if (c + 1) * n_chunk <= w_q:
            y = y * scale
        dil_ref[:, sl] = y


def _attn_proj(x, g, w, *, tm=512, n_chunk=512):
    m, d = x.shape
    n = w.shape[1]
    half = n // 2
    assert m % tm == 0 and (half // 3) % n_chunk == 0
    scale = 1.0 / math.sqrt(HEAD_DIM_ATTN)
    return pl.pallas_call(
        functools.partial(_attn_proj_kernel, n_chunk=n_chunk, scale=scale),
        grid=(m // tm,),
        in_specs=[pl.BlockSpec((tm, d), lambda i: (i, 0)), _resident((1, d)), _resident((d, n))],
        out_specs=[pl.BlockSpec((tm, half), lambda i: (i, 0)),
                   pl.BlockSpec((tm, half), lambda i: (i, 0))],
        out_shape=[jax.ShapeDtypeStruct((m, half), BF16), jax.ShapeDtypeStruct((m, half), F32)],
        compiler_params=_params(("parallel",)),
        name="attn_proj",
    )(x, g.reshape(1, d), w)


LOG2E = math.log2(math.e)
SB_DEAD_LOG2 = -160.0


def _sb_kernel(q_ref, k_ref, v_ref, tri_ref, o_ref, *, blk):
    i = pl.program_id(2)
    q2 = q_ref[...]
    lane = lax.broadcasted_iota(jnp.int32, (1, LANES), 1)
    row = lax.broadcasted_iota(jnp.int32, (blk, blk), 0)
    col = lax.broadcasted_iota(jnp.int32, (blk, blk), 1)
    causal = col < row
    tri = tri_ref[...]

    head0 = lane < HEAD_DIM_ATTN
    q_heads = (jnp.where(head0, q2, jnp.zeros_like(q2)), jnp.where(head0, jnp.zeros_like(q2), q2))

    def block(kb, state, diag):
        ks = pl.multiple_of(kb * blk, blk)
        k2 = k_ref[pl.ds(ks, blk), :]
        v2 = v_ref[pl.ds(ks, blk), :]
        new_state = []
        for qh, (carry, acc) in zip(q_heads, state):
            z = _dot_nt(qh, k2) * LOG2E
            log_keep = -(jnp.maximum(z, 0.0) + jnp.log2(1.0 + jnp.exp2(-jnp.abs(z))))
            if diag:
                log_keep = jnp.where(causal, log_keep, 0.0)
            hi = log_keep.astype(BF16)
            lo = (log_keep - hi.astype(F32)).astype(BF16)
            later = _dot(hi, tri) + _dot(lo, tri)
            p = jnp.exp2(z + log_keep + later + carry)
            if diag:
                p = jnp.where(causal, p, 0.0)
            acc = acc + _dot(p.astype(BF16), v2)
            carry = carry + jnp.sum(log_keep, axis=-1, keepdims=True)
            new_state.append((carry, acc))
        return tuple(new_state)

    def alive(state):
        top = jnp.maximum(jnp.max(state[0][0]), jnp.max(state[1][0]))
        return (top > SB_DEAD_LOG2).astype(jnp.int32)

    zero = (jnp.zeros((blk, 1), F32), jnp.zeros((blk, LANES), F32))
    state = block(i, (zero, zero), True)

    def cond(loop):
        t, live, _ = loop
        return (t < i) & (live > 0)

    def body(loop):
        t, _, state = loop
        state = block(i - 1 - t, state, False)
        return t + 1, alive(state), state

    _, _, state = lax.while_loop(cond, body, (jnp.int32(0), alive(state), state))
    o_ref[...] = jnp.where(head0, state[0][1], state[1][1]).astype(o_ref.dtype)


def _sb_attention(qkv, *, blk=256):
    b, t, w3 = qkv.shape
    w = w3 // 3
    pairs = w // LANES
    blk = min(blk, t)
    assert t % blk == 0
    tri = jnp.asarray(np.tril(np.ones((blk, blk), np.float32), -1), BF16)
    return pl.pallas_call(
        functools.partial(_sb_kernel, blk=blk),
        grid=(b, pairs, t // blk),
        in_specs=[pl.BlockSpec((None, blk, LANES), lambda bi, p, i: (bi, i, p)),
                  pl.BlockSpec((None, t, LANES), lambda bi, p, i: (bi, 0, pairs + p)),
                  pl.BlockSpec((None, t, LANES), lambda bi, p, i: (bi, 0, 2 * pairs + p)),
                  _resident((blk, blk))],
        out_specs=pl.BlockSpec((None, blk, LANES), lambda bi, p, i: (bi, i, p)),
        out_shape=jax.ShapeDtypeStruct((b, t, w), BF16),
        compiler_params=_params(("parallel", "parallel", "arbitrary")),
        name="sb_attn",
    )(qkv, qkv, qkv, tri)


def _t5_bucket_np(dist):
    max_exact = NUM_BUCKETS // 2
    d = np.maximum(dist, 1).astype(np.float32)
    log_b = max_exact + (np.log(d / np.float32(max_exact)) / np.float32(math.log(MAX_DISTANCE / max_exact))
                         * np.float32(NUM_BUCKETS - max_exact)).astype(np.int32)
    log_b = np.minimum(log_b, NUM_BUCKETS - 1)
    return np.where(dist < max_exact, dist, log_b)


def _dil_tables():
    qi = np.arange(DIL_BLOCK)[:, None]
    ki = np.arange(2 * DIL_BLOCK)[None, :]
    dist = qi + DIL_BLOCK - ki
    buckets, valid = [], []
    for window, dil in DIL_CONFIGS:
        steps = window // dil
        buckets.append(_t5_bucket_np(np.maximum(dist, 0) * dil))
        valid.append((dist >= 0) & (dist <= steps))
    return np.stack(buckets).astype(np.int32), np.stack(valid).astype(np.int32)


def _dil_kernel(rb_ref, bucket_ref, valid_ref, q_ref, k_ref, v_ref, o_ref,
                bias_scr, num_scr, m_scr, l_scr, *, seq):
    p = pl.program_id(0)
    qb = DIL_BLOCK
    lane = lax.broadcasted_iota(jnp.int32, (1, LANES), 1)
    head0 = lane < HEAD_DIM_ATTN
    first_half = lax.broadcasted_iota(jnp.int32, (1, 2 * qb), 1) < qb

    @pl.when(pl.program_id(1) == 0)
    def _():
        for br in range(len(DIL_CONFIGS)):
            bucket = bucket_ref[br]
            valid = valid_ref[br] > 0
            for h in range(2):
                bias = jnp.zeros((qb, 2 * qb), F32)
                for b in range(NUM_BUCKETS):
                    bias = jnp.where(bucket == b, rb_ref[b, 2 * p + h], bias)
                bias_scr[br, h * qb:(h + 1) * qb, :] = jnp.where(valid, bias, MASKED)

    for br, (_, dil) in enumerate(DIL_CONFIGS):
        n_units = seq // qb

        n_blocks = seq // (qb * dil)
        run = min(DIL_UNROLL, n_blocks)
        runs = DIL_UNROLL // run
        assert DIL_UNROLL % run == 0 and n_blocks % run == 0 and (dil == 1 or n_blocks == run)

        def rows(ref, start, dil=dil):
            if dil == 1:
                return ref[pl.ds(start, qb), :].astype(BF16)
            return ref[pl.ds(start, qb, stride=dil), :].astype(BF16)

        def load_run(rho, dil=dil, run=run, whole=(n_blocks == run)):
            res, base = (rho, 0) if whole else (0, rho * run)
            starts = [(base + i) * (qb * dil) + res for i in range(run)]
            kb = [rows(k_ref, st) for st in starts]
            vb = [rows(v_ref, st) for st in starts]
            if whole:
                k_prev, v_prev = kb[0], vb[0]
                pen = jnp.where(first_half, MASKED, 0.0)
            else:
                st = jnp.maximum(base - 1, 0) * (qb * dil) + res
                k_prev, v_prev = rows(k_ref, st), rows(v_ref, st)
                pen = jnp.where(first_half, jnp.where(base == 0, MASKED, 0.0), 0.0)
            units = []
            for i, st in enumerate(starts):
                q2 = rows(q_ref, st)
                zeros = jnp.zeros_like(q2)
                qq = jnp.concatenate([jnp.where(head0, q2, zeros), jnp.where(head0, zeros, q2)], axis=0)
                k2 = jnp.concatenate([kb[i - 1] if i else k_prev, kb[i]], axis=0)
                v2 = jnp.concatenate([vb[i - 1] if i else v_prev, vb[i]], axis=0)
                units.append((st, pen if i == 0 else None, qq, k2, v2))
            return units

        def softmax_parts(s):
            m = jnp.max(s, axis=-1, keepdims=True)
            e = jnp.exp(s - m)
            return m, e, jnp.sum(e, axis=-1, keepdims=True)

        def group(g, _, br=br, dil=dil, runs=runs):
            units = [u for r in range(runs) for u in load_run(g * runs + r)]
            scores, parts = {}, {}
            for j in range(DIL_UNROLL + 2):
                if j < DIL_UNROLL:
                    _, pen, qq, k2, _ = units[j]
                    scores[j] = _dot_nt(qq, k2) + bias_scr[br]
                    if pen is not None:
                        scores[j] = scores[j] + pen
                if 0 <= j - 1 < DIL_UNROLL:
                    parts[j - 1] = softmax_parts(scores.pop(j - 1))
                if 0 <= j - 2 < DIL_UNROLL:
                    st, _, _, _, v2 = units[j - 2]
                    m, e, l = parts.pop(j - 2)
                    pv = _dot(e.astype(BF16), v2)
                    idx = pl.ds(st, qb) if dil == 1 else pl.ds(st, qb, stride=dil)
                    num_scr[br, idx, :] = jnp.where(head0, pv[:qb], pv[qb:])
                    m_scr[br, idx, :] = jnp.where(head0, m[:qb], m[qb:])
                    l_scr[br, idx, :] = jnp.where(head0, l[:qb], l[qb:])
            return 0

        lax.fori_loop(0, n_units // DIL_UNROLL, group, 0)

    rows_out = 256

    def finish(c, _):
        idx = pl.ds(pl.multiple_of(c * rows_out, rows_out), rows_out)
        m_all = [m_scr[br, idx, :] for br in range(len(DIL_CONFIGS))]
        m_max = functools.reduce(jnp.maximum, m_all)
        num = den = None
        for br, m_br in enumerate(m_all):
            wt = jnp.exp(m_br - m_max)
            n_br = wt * num_scr[br, idx, :]
            d_br = wt * l_scr[br, idx, :]
            num = n_br if num is None else num + n_br
            den = d_br if den is None else den + d_br
        o_ref[idx, :] = (num / den).astype(o_ref.dtype)
        return 0

    lax.fori_loop(0, seq // rows_out, finish, 0)


def _dil_attention(qkv, rel_bias):
    b, t, w3 = qkv.shape
    w = w3 // 3
    pairs = w // LANES
    assert t % (DIL_BLOCK * max(d for _, d in DIL_CONFIGS)) == 0 and t % 256 == 0
    bucket, valid = _dil_tables()
    nbr = len(DIL_CONFIGS)
    seq_spec = lambda off: pl.BlockSpec((None, t, LANES), lambda p, bi: (bi, 0, off + p))
    return pl.pallas_call(
        functools.partial(_dil_kernel, seq=t),
        grid=(pairs, b),
        in_specs=[pl.BlockSpec(memory_space=pltpu.SMEM),
                  _resident((nbr, DIL_BLOCK, 2 * DIL_BLOCK)),
                  _resident((nbr, DIL_BLOCK, 2 * DIL_BLOCK)),
                  seq_spec(0), seq_spec(pairs), seq_spec(2 * pairs)],
        out_specs=pl.BlockSpec((None, t, LANES), lambda p, bi: (bi, 0, p)),
        out_shape=jax.ShapeDtypeStruct((b, t, w), BF16),
        scratch_shapes=[pltpu.VMEM((nbr, 2 * DIL_BLOCK, 2 * DIL_BLOCK), F32),
                        pltpu.VMEM((nbr, t, LANES), F32), pltpu.VMEM((nbr, t, LANES), F32),
                        pltpu.VMEM((nbr, t, LANES), F32)],
        compiler_params=_params(("arbitrary", "arbitrary")),
        name="dil_attn",
    )(rel_bias, jnp.asarray(bucket), jnp.asarray(valid), qkv, qkv, qkv)


def _out_proj_kernel(*refs, n_parts):
    parts, (x_ref, g_ref, w_ref, o_ref) = refs[:n_parts], refs[n_parts:]
    a = jnp.concatenate([r[...] for r in parts], axis=-1) if n_parts > 1 else parts[0][...]
    y = _dot(a, w_ref[...])
    o_ref[...] = x_ref[...] + _rms(y, g_ref[...])


def _out_proj(parts, x, g, w, *, tm=512):
    m, d = x.shape
    assert m % tm == 0
    row = lambda width: pl.BlockSpec((tm, width), lambda i: (i, 0))
    return pl.pallas_call(
        functools.partial(_out_proj_kernel, n_parts=len(parts)),
        grid=(m // tm,),
        in_specs=[row(a.shape[1]) for a in parts] + [row(d), _resident((1, d)), _resident(w.shape)],
        out_specs=row(d),
        out_shape=jax.ShapeDtypeStruct((m, d), F32),
        compiler_params=_params(("parallel",)),
        name="out_proj",
    )(*parts, x, g.reshape(1, d), w)


def _mlstm_proj_kernel(x_ref, g_ref, w_ref, cw_ref, bg_ref, q_ref, kt_ref, v_ref, o_ref, gate_ref,
                       hist_scr, *, tiles_per_seq, n_chunk, scale):
    i = pl.program_id(0)
    tm = x_ref.shape[0]
    width = q_ref.shape[1]
    xn = _rms(x_ref[...], g_ref[...]).astype(BF16)

    @pl.when(i % tiles_per_seq == 0)
    def _():
        hist_scr[0:CONV_PAD, :] = jnp.zeros((CONV_PAD, hist_scr.shape[1]), F32)

    for c in range(2 * width // n_chunk):
        sl = slice(c * n_chunk, (c + 1) * n_chunk)
        hist_scr[CONV_PAD:CONV_PAD + tm, sl] = _dot(xn, w_ref[:, sl])
        y = None
        for tap in range(CONV_WIDTH):
            off = CONV_PAD - (CONV_WIDTH - 1) + tap
            term = cw_ref[tap:tap + 1, sl] * hist_scr[off:off + tm, sl]
            y = term if y is None else y + term
        y = y * jax.nn.sigmoid(y)
        hist_scr[0:CONV_PAD, sl] = hist_scr[tm:tm + CONV_PAD, sl]
        if c * n_chunk < width:
            q_ref[:, sl] = (y * scale).astype(BF16)
        else:
            kt_ref[c * n_chunk - width:(c + 1) * n_chunk - width, :] = y.T.astype(BF16)
    for c in range(width // n_chunk):
        sl = slice(c * n_chunk, (c + 1) * n_chunk)
        v_ref[:, sl] = _dot(xn, w_ref[:, 2 * width + c * n_chunk:2 * width + (c + 1) * n_chunk]).astype(BF16)
        o_ref[:, sl] = _dot(xn, w_ref[:, 3 * width + c * n_chunk:3 * width + (c + 1) * n_chunk])
    gate_ref[...] = _dot(xn, w_ref[:, 4 * width:]) + bg_ref[...]


def _mlstm_proj(x, g, w_pad, conv_w, b_gates_pad, *, seq, tm=512, n_chunk=512):
    m, d = x.shape
    width = (w_pad.shape[1] - 2 * LANES) // 4
    assert m % tm == 0 and seq % tm == 0 and width % n_chunk == 0
    row = lambda wd: pl.BlockSpec((tm, wd), lambda i: (i, 0))
    scale = 1.0 / math.sqrt(width // N_HEADS_MLSTM)
    return pl.pallas_call(
        functools.partial(_mlstm_proj_kernel, tiles_per_seq=seq // tm, n_chunk=n_chunk, scale=scale),
        grid=(m // tm,),
        in_specs=[row(d), _resident((1, d)), _resident(w_pad.shape), _resident(conv_w.shape),
                  _resident((1, 2 * LANES))],
        out_specs=[row(width), pl.BlockSpec((width, tm), lambda i: (0, i)), row(width), row(width),
                   row(2 * LANES)],
        out_shape=[jax.ShapeDtypeStruct((m, width), BF16), jax.ShapeDtypeStruct((width, m), BF16),
                   jax.ShapeDtypeStruct((m, width), BF16), jax.ShapeDtypeStruct((m, width), F32),
                   jax.ShapeDtypeStruct((m, 2 * LANES), F32)],
        scratch_shapes=[pltpu.VMEM((tm + CONV_PAD, 2 * width), F32)],
        compiler_params=_params(("arbitrary",)),
        name="mlstm_proj",
    )(x, g.reshape(1, d), w_pad, conv_w, b_gates_pad)


def _split3(x):
    hi = x.astype(BF16)
    r = x - hi.astype(F32)
    mid = r.astype(BF16)
    return hi, mid, (r - mid.astype(F32)).astype(BF16)


def _gate_prep_kernel(g_ref, cols_ref, rows_ref, *, heads, cl):
    tm = g_ref.shape[0]
    row = lax.broadcasted_iota(jnp.int32, (cl, cl), 0)
    col = lax.broadcasted_iota(jnp.int32, (cl, cl), 1)
    incl = jnp.where(row >= col, 1.0, 0.0).astype(BF16)
    lane = lax.broadcasted_iota(jnp.int32, (1, cl), 1)
    pad = jnp.zeros((cl - heads, cl), F32)
    for c in range(tm // cl):
        r = slice(c * cl, (c + 1) * cl)
        b = None
        for part in _split3(jax.nn.log_sigmoid(g_ref[r, LANES:])):
            term = _dot(incl, part)
            b = term if b is None else b + term
        u = g_ref[r, :LANES] - b
        u_rows = u.T[0:heads, :]
        b_rows = b.T[0:heads, :]
        cmax = u_rows
        shift = 1
        while shift < cl:
            cmax = jnp.where(lane >= shift, jnp.maximum(cmax, pltpu.roll(cmax, shift, axis=1)), cmax)
            shift *= 2
        u_max = jnp.broadcast_to(jnp.max(u_rows, axis=1, keepdims=True), (heads, cl))
        b_last = jnp.broadcast_to(jnp.min(b_rows, axis=1, keepdims=True), (heads, cl))
        cols_ref[r, :] = jnp.concatenate([u, b, jnp.concatenate([cmax, pad], axis=0).T], axis=1)
        rows_ref[:, r] = jnp.concatenate([u_rows, u_max, b_last], axis=0)


def _gate_prep(gates, *, heads, cl, tm=1024):
    m = gates.shape[0]
    tm = min(tm, m)
    assert m % tm == 0 and tm % cl == 0 and cl == LANES and heads == 8
    return pl.pallas_call(
        functools.partial(_gate_prep_kernel, heads=heads, cl=cl),
        grid=(m // tm,),
        in_specs=[pl.BlockSpec((tm, 2 * LANES), lambda i: (i, 0))],
        out_specs=[pl.BlockSpec((tm, 3 * LANES), lambda i: (i, 0)),
                   pl.BlockSpec((3 * heads, tm), lambda i: (0, i))],
        out_shape=[jax.ShapeDtypeStruct((m, 3 * LANES), F32),
                   jax.ShapeDtypeStruct((3 * heads, m), F32)],
        compiler_params=_params(("parallel",)),
        name="gate_prep",
    )(gates)


def _mlstm_kernel(q_ref, kt_ref, v_ref, o_ref, gcol_ref, grow_ref, hg_ref, out_ref,
                  s_scr, m_scr, ml_scr, *, heads):
    cl = q_ref.shape[0]
    dh = q_ref.shape[1] // heads

    @pl.when(pl.program_id(1) == 0)
    def _():
        s_scr[...] = jnp.zeros_like(s_scr)
        m_scr[...] = jnp.zeros_like(m_scr)
        ml_scr[...] = jnp.zeros_like(ml_scr)

    row = lax.broadcasted_iota(jnp.int32, (cl, cl), 0)
    col = lax.broadcasted_iota(jnp.int32, (cl, cl), 1)
    lower = row >= col
    ones = jnp.ones((cl, dh), BF16)

    u_cols, b_cols, cmax_cols = (gcol_ref[:, t * LANES:(t + 1) * LANES] for t in range(3))
    u_rows, u_max_rows, b_last_rows = (grow_ref[t * heads:(t + 1) * heads, :] for t in range(3))
    m_prev_rows = m_scr[...]
    m_prev_lane = ml_scr[0:1, :]
    mm_last_rows = jnp.maximum(m_prev_rows, u_max_rows)
    decay_rows = jnp.exp(m_prev_rows - mm_last_rows)
    m_scr[...] = b_last_rows + mm_last_rows
    ws_rows = jnp.exp(u_rows - mm_last_rows)
    mm_cols = jnp.maximum(cmax_cols, m_prev_lane)
    floor_cols = jnp.exp(-(b_cols + mm_cols))
    ml_scr[...] = jnp.broadcast_to(b_cols[cl - 1:cl, :] + mm_cols[cl - 1:cl, :], ml_scr.shape)

    def lanes_of(mat, c):
        return jnp.broadcast_to(mat[:, c:c + 1], (cl, LANES))

    def stage_a(h):
        sl = slice(h * dh, (h + 1) * dh)
        q, kt = q_ref[:, sl], kt_ref[sl, :]
        v1 = jnp.concatenate([v_ref[:, sl], ones], axis=1)
        s_prev = s_scr[h]
        qk = _dot(q, kt)
        qs = _dot(q, s_prev.astype(BF16))
        kw = (kt.astype(F32) * ws_rows[h:h + 1, :]).astype(BF16)
        decay = jnp.broadcast_to(decay_rows[h:h + 1, :], (dh, LANES))
        s_scr[h] = jnp.concatenate([decay, decay], axis=1) * s_prev + _dot(kw, v1)
        return h, sl, v1, qk, qs

    def stage_b(h, sl, v1, qk, qs):
        mm = lanes_of(mm_cols, h)
        weight = jnp.where(lower, jnp.exp(u_rows[h:h + 1, :] - mm), 0.0)
        w_inter = jnp.exp(m_prev_rows[h:h + 1, :] - mm)
        return sl, v1, qs, (qk * weight).astype(BF16), w_inter, lanes_of(floor_cols, h)

    def stage_c(sl, v1, qs, p, w_inter, floor):
        pv = _dot(p, v1)
        both = pv + jnp.concatenate([w_inter, w_inter], axis=1) * qs
        hid = both[:, :dh] / jnp.maximum(jnp.abs(both[:, dh:]), floor)
        hid = hid * lax.rsqrt(jnp.mean(hid * hid, axis=-1, keepdims=True) + EPS) * hg_ref[:, sl]
        out_ref[:, sl] = (hid * jax.nn.sigmoid(o_ref[:, sl])).astype(out_ref.dtype)

    after_a, after_b = {}, {}
    for j in range(heads + 2):
        if j < heads:
            after_a[j] = stage_a(j)
        if 0 <= j - 1 < heads:
            after_b[j - 1] = stage_b(*after_a.pop(j - 1))
        if 0 <= j - 2 < heads:
            stage_c(*after_b.pop(j - 2))


def _mlstm(q, kt, v, o, gates, head_g, *, heads=N_HEADS_MLSTM):
    b, t, w = q.shape
    cl = MLSTM_CHUNK
    dh = w // heads
    nc = t // cl
    assert t % cl == 0 and dh == LANES and cl == LANES
    gcols, grows = _gate_prep(gates, heads=heads, cl=cl)
    blk = lambda wd: pl.BlockSpec((None, cl, wd), lambda bi, c: (bi, c, 0))
    return pl.pallas_call(
        functools.partial(_mlstm_kernel, heads=heads),
        grid=(b, nc),
        in_specs=[blk(w), pl.BlockSpec((w, cl), lambda bi, c: (0, bi * nc + c)), blk(w), blk(w),
                  pl.BlockSpec((cl, 3 * LANES), lambda bi, c: (bi * nc + c, 0)),
                  pl.BlockSpec((3 * heads, cl), lambda bi, c: (0, bi * nc + c)),
                  _resident((1, w))],
        out_specs=blk(w),
        out_shape=jax.ShapeDtypeStruct((b, t, w), BF16),
        scratch_shapes=[pltpu.VMEM((heads, dh, 2 * dh), F32), pltpu.VMEM((heads, LANES), F32),
                        pltpu.VMEM((heads, LANES), F32)],
        compiler_params=_params(("parallel", "arbitrary")),
        name="mlstm",
    )(q, kt, v, o, gcols, grows, head_g.reshape(1, w))


def kernel(x, norm_g, ffn_w_gate, ffn_w_up, ffn_w_down, attn_w_in, attn_w_out, rel_bias,
           mlstm_w_in, mlstm_b_gates, mlstm_conv_w, mlstm_head_g, mlstm_w_out):
    bsz, t, d = x.shape
    depth = norm_g.shape[0]
    h = x.reshape(bsz * t, d)
    bf = lambda a: a.astype(BF16)

    def ffn(h, layer, half):
        g = norm_g[layer]
        return _ffn(h, g[2 * half * 2], g[2 * half * 2 + 1], bf(ffn_w_gate[layer, half]),
                    bf(ffn_w_up[layer, half]), bf(ffn_w_down[layer, half]))

    for layer in range(depth):
        g = norm_g[layer]
        j = layer // 2
        h = ffn(h, layer, 0)
        if layer % 2 == 0:
            sb, dil = _attn_proj(h, g[2], bf(attn_w_in[j]))
            out_sb = _sb_attention(sb.reshape(bsz, t, -1))
            out_dil = _dil_attention(dil.reshape(bsz, t, -1), rel_bias.astype(F32))
            parts = [out_sb.reshape(bsz * t, -1), out_dil.reshape(bsz * t, -1)]
            h = _out_proj(parts, h, g[3], bf(attn_w_out[j]))
        else:
            width = mlstm_w_out.shape[1]
            n_head = mlstm_b_gates.shape[1] // 2
            tiles = lambda a: jnp.concatenate(
                [jnp.pad(part, ((0, 0), (0, LANES - n_head)))
                 for part in (a[:, :n_head], a[:, n_head:])], axis=1)
            w_in = bf(mlstm_w_in[j])
            w_pad = jnp.concatenate([w_in[:, :4 * width], tiles(w_in[:, 4 * width:])], axis=1)
            b_pad = tiles(mlstm_b_gates[j].astype(F32).reshape(1, -1))
            q, kt, v, o, gates = _mlstm_proj(h, g[2], w_pad, mlstm_conv_w[j].astype(F32), b_pad, seq=t)
            r3 = lambda a: a.reshape(bsz, t, -1)
            hid = _mlstm(r3(q), kt, r3(v), r3(o), gates, mlstm_head_g[j].astype(F32))
            h = _out_proj([hid.reshape(bsz * t, width)], h, g[3], bf(mlstm_w_out[j]))
        h = ffn(h, layer, 1)
    return h.reshape(bsz, t, d)
```

```python
import functools
import math

import numpy as np
import jax
import jax.numpy as jnp
from jax import lax
from jax.experimental import pallas as pl
from jax.experimental.pallas import tpu as pltpu

EPS = 1e-6
HEAD_DIM_ATTN = 64
N_HEADS_SB = 8
N_HEADS_DIL = 8
DIL_CONFIGS = ((128, 1), (512, 4), (2048, 16))
DIL_BLOCK = 128
DIL_UNROLL = 8
NUM_BUCKETS = 32
MAX_DISTANCE = 2048
N_HEADS_MLSTM = 8
MLSTM_CHUNK = 128
CONV_WIDTH = 4
LANES = 128
CONV_PAD = 8
MASKED = -1e30
VMEM_LIMIT = 56 * 1024 * 1024

F32 = jnp.float32
BF16 = jnp.bfloat16


def _params(sem, vmem=VMEM_LIMIT):
    return pltpu.CompilerParams(dimension_semantics=sem, vmem_limit_bytes=vmem)


def _resident(shape):
    zeros = (0,) * len(shape)
    return pl.BlockSpec(shape, lambda *_: zeros, pipeline_mode=pl.Buffered(1))


def _rms(x, g):
    return x * lax.rsqrt(jnp.mean(x * x, axis=-1, keepdims=True) + EPS) * g


def _dot(a, b):
    return jnp.dot(a, b, preferred_element_type=F32)


def _dot_nt(a, b):
    return lax.dot_general(a, b, (((1,), (1,)), ((), ())), preferred_element_type=F32)


def _dot_tn(a, b):
    return lax.dot_general(a, b, (((0,), (0,)), ((), ())), preferred_element_type=F32)


def _ffn_kernel(x_ref, gin_ref, gout_ref, wg_ref, wu_ref, wd_ref, o_ref, *, ff_chunk):
    x = x_ref[...]
    xn = _rms(x, gin_ref[...]).astype(BF16)
    d_ff = wg_ref.shape[1]
    acc = None
    for c in range(d_ff // ff_chunk):
        sl = slice(c * ff_chunk, (c + 1) * ff_chunk)
        gate = _dot(xn, wg_ref[:, sl])
        up = _dot(xn, wu_ref[:, sl])
        h = (gate * jax.nn.sigmoid(gate) * up).astype(BF16)
        part = _dot(h, wd_ref[sl, :])
        acc = part if acc is None else acc + part
    o_ref[...] = x + 0.5 * _rms(acc, gout_ref[...])


def _ffn(x, g_in, g_out, wg, wu, wd, *, tm=512, ff_chunk=256):
    m, d = x.shape
    d_ff = wg.shape[1]
    assert m % tm == 0 and d_ff % ff_chunk == 0
    row = pl.BlockSpec((tm, d), lambda i: (i, 0))
    return pl.pallas_call(
        functools.partial(_ffn_kernel, ff_chunk=ff_chunk),
        grid=(m // tm,),
        in_specs=[row, _resident((1, d)), _resident((1, d)),
                  _resident((d, d_ff)), _resident((d, d_ff)), _resident((d_ff, d))],
        out_specs=row,
        out_shape=jax.ShapeDtypeStruct((m, d), F32),
        compiler_params=_params(("parallel",)),
        name="ffn",
    )(x, g_in.reshape(1, d), g_out.reshape(1, d), wg, wu, wd)


def _attn_proj_kernel(x_ref, g_ref, w_ref, sb_ref, dil_ref, *, n_chunk, scale):
    xn = _rms(x_ref[...], g_ref[...]).astype(BF16)
    w_sb = sb_ref.shape[1]
    w_q = w_sb // 3
    for c in range(w_sb // n_chunk):
        sl = slice(c * n_chunk, (c + 1) * n_chunk)
        y = _dot(xn, w_ref[:, sl])
        if (c + 1) * n_chunk <= w_q:
            y = y * scale
        sb_ref[:, sl] = y.astype(BF16)
    for c in range(dil_ref.shape[1] // n_chunk):
        sl = slice(c * n_chunk, (c + 1) * n_chunk)
        y = _dot(xn, w_ref[:, w_sb + c * n_chunk: w_sb + (c + 1) * n_chunk])
        if (c + 1) * n_chunk <= w_q:
            y = y * scale
        dil_ref[:, sl] = y


def _attn_proj(x, g, w, *, tm=512, n_chunk=512):
    m, d = x.shape
    n = w.shape[1]
    half = n // 2
    assert m % tm == 0 and (half // 3) % n_chunk == 0
    scale = 1.0 / math.sqrt(HEAD_DIM_ATTN)
    return pl.pallas_call(
        functools.partial(_attn_proj_kernel, n_chunk=n_chunk, scale=scale),
        grid=(m // tm,),
        in_specs=[pl.BlockSpec((tm, d), lambda i: (i, 0)), _resident((1, d)), _resident((d, n))],
        out_specs=[pl.BlockSpec((tm, half), lambda i: (i, 0)),
                   pl.BlockSpec((tm, half), lambda i: (i, 0))],
        out_shape=[jax.ShapeDtypeStruct((m, half), BF16), jax.ShapeDtypeStruct((m, half), F32)],
        compiler_params=_params(("parallel",)),
        name="attn_proj",
    )(x, g.reshape(1, d), w)


LOG2E = math.log2(math.e)
SB_DEAD_LOG2 = -160.0


def _sb_kernel(q_ref, k_ref, v_ref, tri_ref, o_ref, *, blk):
    i = pl.program_id(2)
    q2 = q_ref[...]
    lane = lax.broadcasted_iota(jnp.int32, (1, LANES), 1)
    row = lax.broadcasted_iota(jnp.int32, (blk, blk), 0)
    col = lax.broadcasted_iota(jnp.int32, (blk, blk), 1)
    causal = col < row
    tri = tri_ref[...]

    head0 = lane < HEAD_DIM_ATTN
    zeros = jnp.zeros_like(q2)
    qq = jnp.concatenate([jnp.where(head0, q2, zeros), jnp.where(head0, zeros, q2)], axis=0)
    causal2 = jnp.concatenate([causal, causal], axis=0)

    def pair(kb, state, diag):
        carry, acc = state
        has_second = kb >= 1
        starts = [pl.multiple_of(kb * blk, blk), pl.multiple_of(jnp.maximum(kb - 1, 0) * blk, blk)]
        masks = [causal2 if diag else None, has_second]
        zs = [_dot_nt(qq, k_ref[pl.ds(ks, blk), :]) * LOG2E for ks in starts]
        keeps, sums, laters = [], [], []
        for z, mask in zip(zs, masks):
            log_keep = -(jnp.maximum(z, 0.0) + jnp.log2(1.0 + jnp.exp2(-jnp.abs(z))))
            if mask is not None:
                log_keep = jnp.where(mask, log_keep, 0.0)
            keeps.append(log_keep)
            sums.append(jnp.sum(log_keep, axis=-1, keepdims=True))
        for log_keep in keeps:
            hi = log_keep.astype(BF16)
            lo = (log_keep - hi.astype(F32)).astype(BF16)
            laters.append(_dot(hi, tri) + _dot(lo, tri))
        for n, ks in enumerate(starts):
            p = jnp.exp2(zs[n] + keeps[n] + laters[n] + carry)
            if masks[n] is not None:
                p = jnp.where(masks[n], p, 0.0)
            acc = acc + _dot(p.astype(BF16), v_ref[pl.ds(ks, blk), :])
            carry = carry + sums[n]
        return carry, acc

    def alive(state):
        return (jnp.max(state[0]) > SB_DEAD_LOG2).astype(jnp.int32)

    zero = (jnp.zeros((2 * blk, 1), F32), jnp.zeros((2 * blk, LANES), F32))
    state = pair(i, zero, True)
    n_pairs = lax.shift_right_logical(i, 1)

    def cond(loop):
        t, live, _ = loop
        return (t < n_pairs) & (live > 0)

    def body(loop):
        t, _, state = loop
        state = pair(i - 2 - 2 * t, state, False)
        return t + 1, alive(state), state

    _, _, state = lax.while_loop(cond, body, (jnp.int32(0), alive(state), state))
    o_ref[...] = jnp.where(head0, state[1][:blk], state[1][blk:]).astype(o_ref.dtype)


def _sb_attention(qkv, *, blk=256):
    b, t, w3 = qkv.shape
    w = w3 // 3
    pairs = w // LANES
    blk = min(blk, t)
    assert t % blk == 0
    tri = jnp.asarray(np.tril(np.ones((blk, blk), np.float32), -1), BF16)
    return pl.pallas_call(
        functools.partial(_sb_kernel, blk=blk),
        grid=(b, pairs, t // blk),
        in_specs=[pl.BlockSpec((None, blk, LANES), lambda bi, p, i: (bi, i, p)),
                  pl.BlockSpec((None, t, LANES), lambda bi, p, i: (bi, 0, pairs + p)),
                  pl.BlockSpec((None, t, LANES), lambda bi, p, i: (bi, 0, 2 * pairs + p)),
                  _resident((blk, blk))],
        out_specs=pl.BlockSpec((None, blk, LANES), lambda bi, p, i: (bi, i, p)),
        out_shape=jax.ShapeDtypeStruct((b, t, w), BF16),
        compiler_params=_params(("parallel", "parallel", "arbitrary")),
        name="sb_attn",
    )(qkv, qkv, qkv, tri)


def _t5_bucket_np(dist):
    max_exact = NUM_BUCKETS // 2
    d = np.maximum(dist, 1).astype(np.float32)
    log_b = max_exact + (np.log(d / np.float32(max_exact)) / np.float32(math.log(MAX_DISTANCE / max_exact))
                         * np.float32(NUM_BUCKETS - max_exact)).astype(np.int32)
    log_b = np.minimum(log_b, NUM_BUCKETS - 1)
    return np.where(dist < max_exact, dist, log_b)


def _dil_tables():
    qi = np.arange(DIL_BLOCK)[:, None]
    ki = np.arange(2 * DIL_BLOCK)[None, :]
    dist = qi + DIL_BLOCK - ki
    buckets, valid = [], []
    for window, dil in DIL_CONFIGS:
        steps = window // dil
        buckets.append(_t5_bucket_np(np.maximum(dist, 0) * dil))
        valid.append((dist >= 0) & (dist <= steps))
    return np.stack(buckets).astype(np.int32), np.stack(valid).astype(np.int32)


def _dil_kernel(rb_ref, bucket_ref, valid_ref, q_ref, k_ref, v_ref, o_ref,
                bias_scr, num_scr, m_scr, l_scr, *, seq):
    p = pl.program_id(0)
    qb = DIL_BLOCK
    lane = lax.broadcasted_iota(jnp.int32, (1, LANES), 1)
    head0 = lane < HEAD_DIM_ATTN
    first_half = lax.broadcasted_iota(jnp.int32, (1, 2 * qb), 1) < qb

    @pl.when(pl.program_id(1) == 0)
    def _():
        for br in range(len(DIL_CONFIGS)):
            bucket = bucket_ref[br]
            valid = valid_ref[br] > 0
            for h in range(2):
                bias = jnp.zeros((qb, 2 * qb), F32)
                for b in range(NUM_BUCKETS):
                    bias = jnp.where(bucket == b, rb_ref[b, 2 * p + h], bias)
                bias_scr[br, h * qb:(h + 1) * qb, :] = jnp.where(valid, bias, MASKED)

    for br, (_, dil) in enumerate(DIL_CONFIGS):
        n_units = seq // qb

        n_blocks = seq // (qb * dil)
        run = min(DIL_UNROLL, n_blocks)
        runs = DIL_UNROLL // run
        assert DIL_UNROLL % run == 0 and n_blocks % run == 0 and (dil == 1 or n_blocks == run)

        def rows(ref, start, dil=dil):
            if dil == 1:
                return ref[pl.ds(start, qb), :].astype(BF16)
            return ref[pl.ds(start, qb, stride=dil), :].astype(BF16)

        def load_run(rho, dil=dil, run=run, whole=(n_blocks == run)):
            res, base = (rho, 0) if whole else (0, rho * run)
            starts = [(base + i) * (qb * dil) + res for i in range(run)]
            kb = [rows(k_ref, st) for st in starts]
            vb = [rows(v_ref, st) for st in starts]
            if whole:
                k_prev, v_prev = kb[0], vb[0]
                pen = jnp.where(first_half, MASKED, 0.0)
            else:
                st = jnp.maximum(base - 1, 0) * (qb * dil) + res
                k_prev, v_prev = rows(k_ref, st), rows(v_ref, st)
                pen = jnp.where(first_half, jnp.where(base == 0, MASKED, 0.0), 0.0)
            units = []
            for i, st in enumerate(starts):
                q2 = rows(q_ref, st)
                zeros = jnp.zeros_like(q2)
                qq = jnp.concatenate([jnp.where(head0, q2, zeros), jnp.where(head0, zeros, q2)], axis=0)
                k2 = jnp.concatenate([kb[i - 1] if i else k_prev, kb[i]], axis=0)
                v2 = jnp.concatenate([vb[i - 1] if i else v_prev, vb[i]], axis=0)
                units.append((st, pen if i == 0 else None, qq, k2, v2))
            return units

        def softmax_parts(s):
            m = jnp.max(s, axis=-1, keepdims=True)
            e = jnp.exp(s - m)
            return m, e, jnp.sum(e, axis=-1, keepdims=True)

        def group(g, _, br=br, dil=dil, runs=runs):
            units = [u for r in range(runs) for u in load_run(g * runs + r)]
            scores, parts = {}, {}
            for j in range(DIL_UNROLL + 2):
                if j < DIL_UNROLL:
                    _, pen, qq, k2, _ = units[j]
                    scores[j] = _dot_nt(qq, k2) + bias_scr[br]
                    if pen is not None:
                        scores[j] = scores[j] + pen
                if 0 <= j - 1 < DIL_UNROLL:
                    parts[j - 1] = softmax_parts(scores.pop(j - 1))
                if 0 <= j - 2 < DIL_UNROLL:
                    st, _, _, _, v2 = units[j - 2]
                    m, e, l = parts.pop(j - 2)
                    pv = _dot(e.astype(BF16), v2)
                    idx = pl.ds(st, qb) if dil == 1 else pl.ds(st, qb, stride=dil)
                    num_scr[br, idx, :] = jnp.where(head0, pv[:qb], pv[qb:])
                    m_scr[br, idx, :] = jnp.where(head0, m[:qb], m[qb:])
                    l_scr[br, idx, :] = jnp.where(head0, l[:qb], l[qb:])
            return 0

        lax.fori_loop(0, n_units // DIL_UNROLL, group, 0)

    rows_out = 256

    def finish(c, _):
        idx = pl.ds(pl.multiple_of(c * rows_out, rows_out), rows_out)
        m_all = [m_scr[br, idx, :] for br in range(len(DIL_CONFIGS))]
        m_max = functools.reduce(jnp.maximum, m_all)
        num = den = None
        for br, m_br in enumerate(m_all):
            wt = jnp.exp(m_br - m_max)
            n_br = wt * num_scr[br, idx, :]
            d_br = wt * l_scr[br, idx, :]
            num = n_br if num is None else num + n_br
            den = d_br if den is None else den + d_br
        o_ref[idx, :] = (num / den).astype(o_ref.dtype)
        return 0

    lax.fori_loop(0, seq // rows_out, finish, 0)


def _dil_attention(qkv, rel_bias):
    b, t, w3 = qkv.shape
    w = w3 // 3
    pairs = w // LANES
    assert t % (DIL_BLOCK * max(d for _, d in DIL_CONFIGS)) == 0 and t % 256 == 0
    bucket, valid = _dil_tables()
    nbr = len(DIL_CONFIGS)
    seq_spec = lambda off: pl.BlockSpec((None, t, LANES), lambda p, bi: (bi, 0, off + p))
    return pl.pallas_call(
        functools.partial(_dil_kernel, seq=t),
        grid=(pairs, b),
        in_specs=[pl.BlockSpec(memory_space=pltpu.SMEM),
                  _resident((nbr, DIL_BLOCK, 2 * DIL_BLOCK)),
                  _resident((nbr, DIL_BLOCK, 2 * DIL_BLOCK)),
                  seq_spec(0), seq_spec(pairs), seq_spec(2 * pairs)],
        out_specs=pl.BlockSpec((None, t, LANES), lambda p, bi: (bi, 0, p)),
        out_shape=jax.ShapeDtypeStruct((b, t, w), BF16),
        scratch_shapes=[pltpu.VMEM((nbr, 2 * DIL_BLOCK, 2 * DIL_BLOCK), F32),
                        pltpu.VMEM((nbr, t, LANES), F32), pltpu.VMEM((nbr, t, LANES), F32),
                        pltpu.VMEM((nbr, t, LANES), F32)],
        compiler_params=_params(("arbitrary", "arbitrary")),
        name="dil_attn",
    )(rel_bias, jnp.asarray(bucket), jnp.asarray(valid), qkv, qkv, qkv)


def _out_proj_kernel(*refs, n_parts):
    parts, (x_ref, g_ref, w_ref, o_ref) = refs[:n_parts], refs[n_parts:]
    a = jnp.concatenate([r[...] for r in parts], axis=-1) if n_parts > 1 else parts[0][...]
    y = _dot(a, w_ref[...])
    o_ref[...] = x_ref[...] + _rms(y, g_ref[...])


def _out_proj(parts, x, g, w, *, tm=512):
    m, d = x.shape
    assert m % tm == 0
    row = lambda width: pl.BlockSpec((tm, width), lambda i: (i, 0))
    return pl.pallas_call(
        functools.partial(_out_proj_kernel, n_parts=len(parts)),
        grid=(m // tm,),
        in_specs=[row(a.shape[1]) for a in parts] + [row(d), _resident((1, d)), _resident(w.shape)],
        out_specs=row(d),
        out_shape=jax.ShapeDtypeStruct((m, d), F32),
        compiler_params=_params(("parallel",)),
        name="out_proj",
    )(*parts, x, g.reshape(1, d), w)


def _mlstm_proj_kernel(x_ref, g_ref, w_ref, cw_ref, bg_ref, q_ref, kt_ref, v_ref, o_ref, gate_ref,
                       hist_scr, *, tiles_per_seq, n_chunk, scale):
    i = pl.program_id(0)
    tm = x_ref.shape[0]
    width = q_ref.shape[1]
    xn = _rms(x_ref[...], g_ref[...]).astype(BF16)

    @pl.when(i % tiles_per_seq == 0)
    def _():
        hist_scr[0:CONV_PAD, :] = jnp.zeros((CONV_PAD, hist_scr.shape[1]), F32)

    for c in range(2 * width // n_chunk):
        sl = slice(c * n_chunk, (c + 1) * n_chunk)
        hist_scr[CONV_PAD:CONV_PAD + tm, sl] = _dot(xn, w_ref[:, sl])
        y = None
        for tap in range(CONV_WIDTH):
            off = CONV_PAD - (CONV_WIDTH - 1) + tap
            term = cw_ref[tap:tap + 1, sl] * hist_scr[off:off + tm, sl]
            y = term if y is None else y + term
        y = y * jax.nn.sigmoid(y)
        hist_scr[0:CONV_PAD, sl] = hist_scr[tm:tm + CONV_PAD, sl]
        if c * n_chunk < width:
            q_ref[:, sl] = (y * scale).astype(BF16)
        else:
            kt_ref[c * n_chunk - width:(c + 1) * n_chunk - width, :] = y.T.astype(BF16)
    for c in range(width // n_chunk):
        sl = slice(c * n_chunk, (c + 1) * n_chunk)
        v_ref[:, sl] = _dot(xn, w_ref[:, 2 * width + c * n_chunk:2 * width + (c + 1) * n_chunk]).astype(BF16)
        o_ref[:, sl] = _dot(xn, w_ref[:, 3 * width + c * n_chunk:3 * width + (c + 1) * n_chunk])
    gate_ref[...] = _dot(xn, w_ref[:, 4 * width:]) + bg_ref[...]


def _mlstm_proj(x, g, w_pad, conv_w, b_gates_pad, *, seq, tm=512, n_chunk=512):
    m, d = x.shape
    width = (w_pad.shape[1] - 2 * LANES) // 4
    assert m % tm == 0 and seq % tm == 0 and width % n_chunk == 0
    row = lambda wd: pl.BlockSpec((tm, wd), lambda i: (i, 0))
    scale = 1.0 / math.sqrt(width // N_HEADS_MLSTM)
    return pl.pallas_call(
        functools.partial(_mlstm_proj_kernel, tiles_per_seq=seq // tm, n_chunk=n_chunk, scale=scale),
        grid=(m // tm,),
        in_specs=[row(d), _resident((1, d)), _resident(w_pad.shape), _resident(conv_w.shape),
                  _resident((1, 2 * LANES))],
        out_specs=[row(width), pl.BlockSpec((width, tm), lambda i: (0, i)), row(width), row(width),
                   row(2 * LANES)],
        out_shape=[jax.ShapeDtypeStruct((m, width), BF16), jax.ShapeDtypeStruct((width, m), BF16),
                   jax.ShapeDtypeStruct((m, width), BF16), jax.ShapeDtypeStruct((m, width), F32),
                   jax.ShapeDtypeStruct((m, 2 * LANES), F32)],
        scratch_shapes=[pltpu.VMEM((tm + CONV_PAD, 2 * width), F32)],
        compiler_params=_params(("arbitrary",)),
        name="mlstm_proj",
    )(x, g.reshape(1, d), w_pad, conv_w, b_gates_pad)


def _split3(x):
    hi = x.astype(BF16)
    r = x - hi.astype(F32)
    mid = r.astype(BF16)
    return hi, mid, (r - mid.astype(F32)).astype(BF16)


def _gate_prep_kernel(g_ref, cols_ref, rows_ref, *, heads, cl):
    tm = g_ref.shape[0]
    row = lax.broadcasted_iota(jnp.int32, (cl, cl), 0)
    col = lax.broadcasted_iota(jnp.int32, (cl, cl), 1)
    incl = jnp.where(row >= col, 1.0, 0.0).astype(BF16)
    lane = lax.broadcasted_iota(jnp.int32, (1, cl), 1)
    pad = jnp.zeros((cl - heads, cl), F32)
    for c in range(tm // cl):
        r = slice(c * cl, (c + 1) * cl)
        b = None
        for part in _split3(jax.nn.log_sigmoid(g_ref[r, LANES:])):
            term = _dot(incl, part)
            b = term if b is None else b + term
        u = g_ref[r, :LANES] - b
        u_rows = u.T[0:heads, :]
        b_rows = b.T[0:heads, :]
        cmax = u_rows
        shift = 1
        while shift < cl:
            cmax = jnp.where(lane >= shift, jnp.maximum(cmax, pltpu.roll(cmax, shift, axis=1)), cmax)
            shift *= 2
        u_max = jnp.broadcast_to(jnp.max(u_rows, axis=1, keepdims=True), (heads, cl))
        b_last = jnp.broadcast_to(jnp.min(b_rows, axis=1, keepdims=True), (heads, cl))
        cols_ref[r, :] = jnp.concatenate([u, b, jnp.concatenate([cmax, pad], axis=0).T], axis=1)
        rows_ref[:, r] = jnp.concatenate([u_rows, u_max, b_last], axis=0)


def _gate_prep(gates, *, heads, cl, tm=1024):
    m = gates.shape[0]
    tm = min(tm, m)
    assert m % tm == 0 and tm % cl == 0 and cl == LANES and heads == 8
    return pl.pallas_call(
        functools.partial(_gate_prep_kernel, heads=heads, cl=cl),
        grid=(m // tm,),
        in_specs=[pl.BlockSpec((tm, 2 * LANES), lambda i: (i, 0))],
        out_specs=[pl.BlockSpec((tm, 3 * LANES), lambda i: (i, 0)),
                   pl.BlockSpec((3 * heads, tm), lambda i: (0, i))],
        out_shape=[jax.ShapeDtypeStruct((m, 3 * LANES), F32),
                   jax.ShapeDtypeStruct((3 * heads, m), F32)],
        compiler_params=_params(("parallel",)),
        name="gate_prep",
    )(gates)


def _mlstm_kernel(q_ref, kt_ref, v_ref, o_ref, gcol_ref, grow_ref, hg_ref, out_ref,
                  s_scr, m_scr, ml_scr, *, heads):
    cl = q_ref.shape[0]
    dh = q_ref.shape[1] // heads

    @pl.when(pl.program_id(1) == 0)
    def _():
        s_scr[...] = jnp.zeros_like(s_scr)
        m_scr[...] = jnp.zeros_like(m_scr)
        ml_scr[...] = jnp.zeros_like(ml_scr)

    row = lax.broadcasted_iota(jnp.int32, (cl, cl), 0)
    col = lax.broadcasted_iota(jnp.int32, (cl, cl), 1)
    lower = row >= col
    ones = jnp.ones((cl, dh), BF16)

    u_cols, b_cols, cmax_cols = (gcol_ref[:, t * LANES:(t + 1) * LANES] for t in range(3))
    u_rows, u_max_rows, b_last_rows = (grow_ref[t * heads:(t + 1) * heads, :] for t in range(3))
    m_prev_rows = m_scr[...]
    m_prev_lane = ml_scr[0:1, :]
    mm_last_rows = jnp.maximum(m_prev_rows, u_max_rows)
    decay_rows = jnp.exp(m_prev_rows - mm_last_rows)
    m_scr[...] = b_last_rows + mm_last_rows
    ws_rows = jnp.exp(u_rows - mm_last_rows)
    mm_cols = jnp.maximum(cmax_cols, m_prev_lane)
    floor_cols = jnp.exp(-(b_cols + mm_cols))
    ml_scr[...] = jnp.broadcast_to(b_cols[cl - 1:cl, :] + mm_cols[cl - 1:cl, :], ml_scr.shape)

    def lanes_of(mat, c):
        return jnp.broadcast_to(mat[:, c:c + 1], (cl, LANES))

    def stage_a(h):
        sl = slice(h * dh, (h + 1) * dh)
        q, kt = q_ref[:, sl], kt_ref[sl, :]
        v1 = jnp.concatenate([v_ref[:, sl], ones], axis=1)
        s_prev = s_scr[h]
        qk = _dot(q, kt)
        qs = _dot(q, s_prev.astype(BF16))
        kw = (kt.astype(F32) * ws_rows[h:h + 1, :]).astype(BF16)
        decay = jnp.broadcast_to(decay_rows[h:h + 1, :], (dh, LANES))
        s_scr[h] = jnp.concatenate([decay, decay], axis=1) * s_prev + _dot(kw, v1)
        return h, sl, v1, qk, qs

    def stage_b(h, sl, v1, qk, qs):
        mm = lanes_of(mm_cols, h)
        weight = jnp.where(lower, jnp.exp(u_rows[h:h + 1, :] - mm), 0.0)
        w_inter = jnp.exp(m_prev_rows[h:h + 1, :] - mm)
        return sl, v1, qs, (qk * weight).astype(BF16), w_inter, lanes_of(floor_cols, h)

    def stage_c(sl, v1, qs, p, w_inter, floor):
        pv = _dot(p, v1)
        both = pv + jnp.concatenate([w_inter, w_inter], axis=1) * qs
        hid = both[:, :dh] / jnp.maximum(jnp.abs(both[:, dh:]), floor)
        hid = hid * lax.rsqrt(jnp.mean(hid * hid, axis=-1, keepdims=True) + EPS) * hg_ref[:, sl]
        out_ref[:, sl] = (hid * jax.nn.sigmoid(o_ref[:, sl])).astype(out_ref.dtype)

    after_a, after_b = {}, {}
    for j in range(heads + 2):
        if j < heads:
            after_a[j] = stage_a(j)
        if 0 <= j - 1 < heads:
            after_b[j - 1] = stage_b(*after_a.pop(j - 1))
        if 0 <= j - 2 < heads:
            stage_c(*after_b.pop(j - 2))


def _mlstm(q, kt, v, o, gates, head_g, *, heads=N_HEADS_MLSTM):
    b, t, w = q.shape
    cl = MLSTM_CHUNK
    dh = w // heads
    nc = t // cl
    assert t % cl == 0 and dh == LANES and cl == LANES
    gcols, grows = _gate_prep(gates, heads=heads, cl=cl)
    blk = lambda wd: pl.BlockSpec((None, cl, wd), lambda bi, c: (bi, c, 0))
    return pl.pallas_call(
        functools.partial(_mlstm_kernel, heads=heads),
        grid=(b, nc),
        in_specs=[blk(w), pl.BlockSpec((w, cl), lambda bi, c: (0, bi * nc + c)), blk(w), blk(w),
                  pl.BlockSpec((cl, 3 * LANES), lambda bi, c: (bi * nc + c, 0)),
                  pl.BlockSpec((3 * heads, cl), lambda bi, c: (0, bi * nc + c)),
                  _resident((1, w))],
        out_specs=blk(w),
        out_shape=jax.ShapeDtypeStruct((b, t, w), BF16),
        scratch_shapes=[pltpu.VMEM((heads, dh, 2 * dh), F32), pltpu.VMEM((heads, LANES), F32),
                        pltpu.VMEM((heads, LANES), F32)],
        compiler_params=_params(("parallel", "arbitrary")),
        name="mlstm",
    )(q, kt, v, o, gcols, grows, head_g.reshape(1, w))


def kernel(x, norm_g, ffn_w_gate, ffn_w_up, ffn_w_down, attn_w_in, attn_w_out, rel_bias,
           mlstm_w_in, mlstm_b_gates, mlstm_conv_w, mlstm_head_g, mlstm_w_out):
    bsz, t, d = x.shape
    depth = norm_g.shape[0]
    h = x.reshape(bsz * t, d)
    bf = lambda a: a.astype(BF16)

    def ffn(h, layer, half):
        g = norm_g[layer]
        return _ffn(h, g[2 * half * 2], g[2 * half * 2 + 1], bf(ffn_w_gate[layer, half]),
                    bf(ffn_w_up[layer, half]), bf(ffn_w_down[layer, half]))

    for layer in range(depth):
        g = norm_g[layer]
        j = layer // 2
        h = ffn(h, layer, 0)
        if layer % 2 == 0:
            sb, dil = _attn_proj(h, g[2], bf(attn_w_in[j]))
            out_sb = _sb_attention(sb.reshape(bsz, t, -1))
            out_dil = _dil_attention(dil.reshape(bsz, t, -1), rel_bias.astype(F32))
            parts = [out_sb.reshape(bsz * t, -1), out_dil.reshape(bsz * t, -1)]
            h = _out_proj(parts, h, g[3], bf(attn_w_out[j]))
        else:
            width = mlstm_w_out.shape[1]
            n_head = mlstm_b_gates.shape[1] // 2
            tiles = lambda a: jnp.concatenate(
                [jnp.pad(part, ((0, 0), (0, LANES - n_head)))
                 for part in (a[:, :n_head], a[:, n_head:])], axis=1)
            w_in = bf(mlstm_w_in[j])
            w_pad = jnp.concatenate([w_in[:, :4 * width], tiles(w_in[:, 4 * width:])], axis=1)
            b_pad = tiles(mlstm_b_gates[j].astype(F32).reshape(1, -1))
            q, kt, v, o, gates = _mlstm_proj(h, g[2], w_pad, mlstm_conv_w[j].astype(F32), b_pad, seq=t)
            r3 = lambda a: a.reshape(bsz, t, -1)
            hid = _mlstm(r3(q), kt, r3(v), r3(o), gates, mlstm_head_g[j].astype(F32))
            h = _out_proj([hid.reshape(bsz * t, width)], h, g[3], bf(mlstm_w_out[j]))
        h = ffn(h, layer, 1)
    return h.reshape(bsz, t, d)
```

```python
import functools
import math

import numpy as np
import jax
import jax.numpy as jnp
from jax import lax
from jax.experimental import pallas as pl
from jax.experimental.pallas import tpu as pltpu

EPS = 1e-6
HEAD_DIM_ATTN = 64
N_HEADS_SB = 8
N_HEADS_DIL = 8
DIL_CONFIGS = ((128, 1), (512, 4), (2048, 16))
DIL_BLOCK = 128
DIL_UNROLL = 8
NUM_BUCKETS = 32
MAX_DISTANCE = 2048
N_HEADS_MLSTM = 8
MLSTM_CHUNK = 128
CONV_WIDTH = 4
LANES = 128
CONV_PAD = 8
MASKED = -1e30
VMEM_LIMIT = 56 * 1024 * 1024

F32 = jnp.float32
BF16 = jnp.bfloat16


def _params(sem, vmem=VMEM_LIMIT):
    return pltpu.CompilerParams(dimension_semantics=sem, vmem_limit_bytes=vmem)


def _resident(shape):
    zeros = (0,) * len(shape)
    return pl.BlockSpec(shape, lambda *_: zeros, pipeline_mode=pl.Buffered(1))


def _rms(x, g):
    return x * lax.rsqrt(jnp.mean(x * x, axis=-1, keepdims=True) + EPS) * g


def _dot(a, b):
    return jnp.dot(a, b, preferred_element_type=F32)


def _dot_nt(a, b):
    return lax.dot_general(a, b, (((1,), (1,)), ((), ())), preferred_element_type=F32)


def _dot_tn(a, b):
    return lax.dot_general(a, b, (((0,), (0,)), ((), ())), preferred_element_type=F32)


def _ffn_kernel(x_ref, gin_ref, gout_ref, wg_ref, wu_ref, wd_ref, o_ref, *, ff_chunk):
    x = x_ref[...]
    xn = _rms(x, gin_ref[...]).astype(BF16)
    d_ff = wg_ref.shape[1]
    acc = None
    for c in range(d_ff // ff_chunk):
        sl = slice(c * ff_chunk, (c + 1) * ff_chunk)
        gate = _dot(xn, wg_ref[:, sl])
        up = _dot(xn, wu_ref[:, sl])
        h = (gate * jax.nn.sigmoid(gate) * up).astype(BF16)
        part = _dot(h, wd_ref[sl, :])
        acc = part if acc is None else acc + part
    o_ref[...] = x + 0.5 * _rms(acc, gout_ref[...])


def _ffn(x, g_in, g_out, wg, wu, wd, *, tm=512, ff_chunk=256):
    m, d = x.shape
    d_ff = wg.shape[1]
    assert m % tm == 0 and d_ff % ff_chunk == 0
    row = pl.BlockSpec((tm, d), lambda i: (i, 0))
    return pl.pallas_call(
        functools.partial(_ffn_kernel, ff_chunk=ff_chunk),
        grid=(m // tm,),
        in_specs=[row, _resident((1, d)), _resident((1, d)),
                  _resident((d, d_ff)), _resident((d, d_ff)), _resident((d_ff, d))],
        out_specs=row,
        out_shape=jax.ShapeDtypeStruct((m, d), F32),
        compiler_params=_params(("parallel",)),
        name="ffn",
    )(x, g_in.reshape(1, d), g_out.reshape(1, d), wg, wu, wd)


def _attn_proj_kernel(x_ref, g_ref, w_ref, sb_ref, dil_ref, *, n_chunk, scale):
    xn = _rms(x_ref[...], g_ref[...]).astype(BF16)
    w_sb = sb_ref.shape[1]
    w_q = w_sb // 3
    for c in range(w_sb // n_chunk):
        sl = slice(c * n_chunk, (c + 1) * n_chunk)
        y = _dot(xn, w_ref[:, sl])
        if (c + 1) * n_chunk <= w_q:
            y = y * scale
        sb_ref[:, sl] = y.astype(BF16)
    for c in range(dil_ref.shape[1] // n_chunk):
        sl = slice(c * n_chunk, (c + 1) * n_chunk)
        y = _dot(xn, w_ref[:, w_sb + c * n_chunk: w_sb + (c + 1) * n_chunk])
        if (c + 1) * n_chunk <= w_q:
            y = y * scale
        dil_ref[:, sl] = y


def _attn_proj(x, g, w, *, tm=512, n_chunk=512):
    m, d = x.shape
    n = w.shape[1]
    half = n // 2
    assert m % tm == 0 and (half // 3) % n_chunk == 0
    scale = 1.0 / math.sqrt(HEAD_DIM_ATTN)
    return pl.pallas_call(
        functools.partial(_attn_proj_kernel, n_chunk=n_chunk, scale=scale),
        grid=(m // tm,),
        in_specs=[pl.BlockSpec((tm, d), lambda i: (i, 0)), _resident((1, d)), _resident((d, n))],
        out_specs=[pl.BlockSpec((tm, half), lambda i: (i, 0)),
                   pl.BlockSpec((tm, half), lambda i: (i, 0))],
        out_shape=[jax.ShapeDtypeStruct((m, half), BF16), jax.ShapeDtypeStruct((m, half), F32)],
        compiler_params=_params(("parallel",)),
        name="attn_proj",
    )(x, g.reshape(1, d), w)


LOG2E = math.log2(math.e)
SB_DEAD_LOG2 = -160.0


def _sb_kernel(q_ref, k_ref, v_ref, tri_ref, o_ref, *, blk):
    i = pl.program_id(2)
    q2 = q_ref[...]
    lane = lax.broadcasted_iota(jnp.int32, (1, LANES), 1)
    row = lax.broadcasted_iota(jnp.int32, (blk, blk), 0)
    col = lax.broadcasted_iota(jnp.int32, (blk, blk), 1)
    causal = col < row
    tri = tri_ref[...]

    head0 = lane < HEAD_DIM_ATTN
    zeros = jnp.zeros_like(q2)
    qq = jnp.concatenate([jnp.where(head0, q2, zeros), jnp.where(head0, zeros, q2)], axis=0)
    causal2 = jnp.concatenate([causal, causal], axis=0)

    def pair(kb, state, diag):
        carry, acc = state
        has_second = kb >= 1
        starts = [pl.multiple_of(kb * blk, blk), pl.multiple_of(jnp.maximum(kb - 1, 0) * blk, blk)]
        masks = [causal2 if diag else None, has_second]
        zs = [_dot_nt(qq, k_ref[pl.ds(ks, blk), :]) * LOG2E for ks in starts]
        keeps, sums, laters = [], [], []
        for z, mask in zip(zs, masks):
            log_keep = -(jnp.maximum(z, 0.0) + jnp.log2(1.0 + jnp.exp2(-jnp.abs(z))))
            if mask is not None:
                log_keep = jnp.where(mask, log_keep, 0.0)
            keeps.append(log_keep)
            sums.append(jnp.sum(log_keep, axis=-1, keepdims=True))
        for log_keep in keeps:
            hi = log_keep.astype(BF16)
            lo = (log_keep - hi.astype(F32)).astype(BF16)
            laters.append(_dot(hi, tri) + _dot(lo, tri))
        for n, ks in enumerate(starts):
            p = jnp.exp2(zs[n] + keeps[n] + laters[n] + carry)
            if masks[n] is not None:
                p = jnp.where(masks[n], p, 0.0)
            acc = acc + _dot(p.astype(BF16), v_ref[pl.ds(ks, blk), :])
            carry = carry + sums[n]
        return carry, acc

    def alive(state):
        return (jnp.max(state[0]) > SB_DEAD_LOG2).astype(jnp.int32)

    zero = (jnp.zeros((2 * blk, 1), F32), jnp.zeros((2 * blk, LANES), F32))
    state = pair(i, zero, True)
    n_pairs = lax.shift_right_logical(i, 1)

    def cond(loop):
        t, live, _ = loop
        return (t < n_pairs) & (live > 0)

    def body(loop):
        t, _, state = loop
        state = pair(i - 2 - 2 * t, state, False)
        return t + 1, alive(state), state

    _, _, state = lax.while_loop(cond, body, (jnp.int32(0), alive(state), state))
    o_ref[...] = jnp.where(head0, state[1][:blk], state[1][blk:]).astype(o_ref.dtype)


def _sb_attention(qkv, *, blk=256):
    b, t, w3 = qkv.shape
    w = w3 // 3
    pairs = w // LANES
    blk = min(blk, t)
    assert t % blk == 0
    tri = jnp.asarray(np.tril(np.ones((blk, blk), np.float32), -1), BF16)
    return pl.pallas_call(
        functools.partial(_sb_kernel, blk=blk),
        grid=(b, pairs, t // blk),
        in_specs=[pl.BlockSpec((None, blk, LANES), lambda bi, p, i: (bi, i, p)),
                  pl.BlockSpec((None, t, LANES), lambda bi, p, i: (bi, 0, pairs + p)),
                  pl.BlockSpec((None, t, LANES), lambda bi, p, i: (bi, 0, 2 * pairs + p)),
                  _resident((blk, blk))],
        out_specs=pl.BlockSpec((None, blk, LANES), lambda bi, p, i: (bi, i, p)),
        out_shape=jax.ShapeDtypeStruct((b, t, w), BF16),
        compiler_params=_params(("parallel", "parallel", "arbitrary")),
        name="sb_attn",
    )(qkv, qkv, qkv, tri)


def _t5_bucket_np(dist):
    max_exact = NUM_BUCKETS // 2
    d = np.maximum(dist, 1).astype(np.float32)
    log_b = max_exact + (np.log(d / np.float32(max_exact)) / np.float32(math.log(MAX_DISTANCE / max_exact))
                         * np.float32(NUM_BUCKETS - max_exact)).astype(np.int32)
    log_b = np.minimum(log_b, NUM_BUCKETS - 1)
    return np.where(dist < max_exact, dist, log_b)


def _dil_tables():
    qi = np.arange(DIL_BLOCK)[:, None]
    ki = np.arange(2 * DIL_BLOCK)[None, :]
    dist = qi + DIL_BLOCK - ki
    buckets, valid = [], []
    for window, dil in DIL_CONFIGS:
        steps = window // dil
        buckets.append(_t5_bucket_np(np.maximum(dist, 0) * dil))
        valid.append((dist >= 0) & (dist <= steps))
    return np.stack(buckets).astype(np.int32), np.stack(valid).astype(np.int32)


def _dil_kernel(rb_ref, bucket_ref, valid_ref, q_ref, k_ref, v_ref, o_ref,
                bias_scr, num_scr, m_scr, l_scr, *, seq):
    p = pl.program_id(0)
    qb = DIL_BLOCK
    lane = lax.broadcasted_iota(jnp.int32, (1, LANES), 1)
    head0 = lane < HEAD_DIM_ATTN
    first_half = lax.broadcasted_iota(jnp.int32, (1, 2 * qb), 1) < qb

    @pl.when(pl.program_id(1) == 0)
    def _():
        for br in range(len(DIL_CONFIGS)):
            bucket = bucket_ref[br]
            valid = valid_ref[br] > 0
            for h in range(2):
                bias = jnp.zeros((qb, 2 * qb), F32)
                for b in range(NUM_BUCKETS):
                    bias = jnp.where(bucket == b, rb_ref[b, 2 * p + h], bias)
                bias_scr[br, h * qb:(h + 1) * qb, :] = jnp.where(valid, bias, MASKED)

    for br, (_, dil) in enumerate(DIL_CONFIGS):
        n_units = seq // qb

        n_blocks = seq // (qb * dil)
        run = min(DIL_UNROLL, n_blocks)
        runs = DIL_UNROLL // run
        assert DIL_UNROLL % run == 0 and n_blocks % run == 0 and (dil == 1 or n_blocks == run)

        def rows(ref, start, dil=dil):
            if dil == 1:
                return ref[pl.ds(start, qb), :].astype(BF16)
            return ref[pl.ds(start, qb, stride=dil), :].astype(BF16)

        def load_run(rho, dil=dil, run=run, whole=(n_blocks == run)):
            res, base = (rho, 0) if whole else (0, rho * run)
            starts = [(base + i) * (qb * dil) + res for i in range(run)]
            kb = [rows(k_ref, st) for st in starts]
            vb = [rows(v_ref, st) for st in starts]
            if whole:
                k_prev, v_prev = kb[0], vb[0]
                pen = jnp.where(first_half, MASKED, 0.0)
            else:
                st = jnp.maximum(base - 1, 0) * (qb * dil) + res
                k_prev, v_prev = rows(k_ref, st), rows(v_ref, st)
                pen = jnp.where(first_half, jnp.where(base == 0, MASKED, 0.0), 0.0)
            units = []
            for i, st in enumerate(starts):
                q2 = rows(q_ref, st)
                zeros = jnp.zeros_like(q2)
                qq = jnp.concatenate([jnp.where(head0, q2, zeros), jnp.where(head0, zeros, q2)], axis=0)
                k2 = jnp.concatenate([kb[i - 1] if i else k_prev, kb[i]], axis=0)
                v2 = jnp.concatenate([vb[i - 1] if i else v_prev, vb[i]], axis=0)
                units.append((st, pen if i == 0 else None, qq, k2, v2))
            return units

        def softmax_parts(s):
            m = jnp.max(s, axis=-1, keepdims=True)
            e = jnp.exp(s - m)
            return m, e, jnp.sum(e, axis=-1, keepdims=True)

        def group(g, _, br=br, dil=dil, runs=runs):
            units = [u for r in range(runs) for u in load_run(g * runs + r)]
            scores, parts = {}, {}
            for j in range(DIL_UNROLL + 2):
                if j < DIL_UNROLL:
                    _, pen, qq, k2, _ = units[j]
                    scores[j] = _dot_nt(qq, k2) + bias_scr[br]
                    if pen is not None:
                        scores[j] = scores[j] + pen
                if 0 <= j - 1 < DIL_UNROLL:
                    parts[j - 1] = softmax_parts(scores.pop(j - 1))
                if 0 <= j - 2 < DIL_UNROLL:
                    st, _, _, _, v2 = units[j - 2]
                    m, e, l = parts.pop(j - 2)
                    pv = _dot(e.astype(BF16), v2)
                    idx = pl.ds(st, qb) if dil == 1 else pl.ds(st, qb, stride=dil)
                    num_scr[br, idx, :] = jnp.where(head0, pv[:qb], pv[qb:])
                    m_scr[br, idx, :] = jnp.where(head0, m[:qb], m[qb:])
                    l_scr[br, idx, :] = jnp.where(head0, l[:qb], l[qb:])
            return 0

        lax.fori_loop(0, n_units // DIL_UNROLL, group, 0)

    rows_out = 256

    def finish(c, _):
        idx = pl.ds(pl.multiple_of(c * rows_out, rows_out), rows_out)
        m_all = [m_scr[br, idx, :] for br in range(len(DIL_CONFIGS))]
        m_max = functools.reduce(jnp.maximum, m_all)
        num = den = None
        for br, m_br in enumerate(m_all):
            wt = jnp.exp(m_br - m_max)
            n_br = wt * num_scr[br, idx, :]
            d_br = wt * l_scr[br, idx, :]
            num = n_br if num is None else num + n_br
            den = d_br if den is None else den + d_br
        o_ref[idx, :] = (num / den).astype(o_ref.dtype)
        return 0

    lax.fori_loop(0, seq // rows_out, finish, 0)


def _dil_attention(qkv, rel_bias):
    b, t, w3 = qkv.shape
    w = w3 // 3
    pairs = w // LANES
    assert t % (DIL_BLOCK * max(d for _, d in DIL_CONFIGS)) == 0 and t % 256 == 0
    bucket, valid = _dil_tables()
    nbr = len(DIL_CONFIGS)
    seq_spec = lambda off: pl.BlockSpec((None, t, LANES), lambda p, bi: (bi, 0, off + p))
    return pl.pallas_call(
        functools.partial(_dil_kernel, seq=t),
        grid=(pairs, b),
        in_specs=[pl.BlockSpec(memory_space=pltpu.SMEM),
                  _resident((nbr, DIL_BLOCK, 2 * DIL_BLOCK)),
                  _resident((nbr, DIL_BLOCK, 2 * DIL_BLOCK)),
                  seq_spec(0), seq_spec(pairs), seq_spec(2 * pairs)],
        out_specs=pl.BlockSpec((None, t, LANES), lambda p, bi: (bi, 0, p)),
        out_shape=jax.ShapeDtypeStruct((b, t, w), BF16),
        scratch_shapes=[pltpu.VMEM((nbr, 2 * DIL_BLOCK, 2 * DIL_BLOCK), F32),
                        pltpu.VMEM((nbr, t, LANES), F32), pltpu.VMEM((nbr, t, LANES), F32),
                        pltpu.VMEM((nbr, t, LANES), F32)],
        compiler_params=_params(("arbitrary", "arbitrary")),
        name="dil_attn",
    )(rel_bias, jnp.asarray(bucket), jnp.asarray(valid), qkv, qkv, qkv)


def _out_proj_kernel(*refs, n_parts):
    parts, (x_ref, g_ref, w_ref, o_ref) = refs[:n_parts], refs[n_parts:]
    a = jnp.concatenate([r[...] for r in parts], axis=-1) if n_parts > 1 else parts[0][...]
    y = _dot(a, w_ref[...])
    o_ref[...] = x_ref[...] + _rms(y, g_ref[...])


def _out_proj(parts, x, g, w, *, tm=512):
    m, d = x.shape
    assert m % tm == 0
    row = lambda width: pl.BlockSpec((tm, width), lambda i: (i, 0))
    return pl.pallas_call(
        functools.partial(_out_proj_kernel, n_parts=len(parts)),
        grid=(m // tm,),
        in_specs=[row(a.shape[1]) for a in parts] + [row(d), _resident((1, d)), _resident(w.shape)],
        out_specs=row(d),
        out_shape=jax.ShapeDtypeStruct((m, d), F32),
        compiler_params=_params(("parallel",)),
        name="out_proj",
    )(*parts, x, g.reshape(1, d), w)


def _mlstm_proj_kernel(x_ref, g_ref, w_ref, cw_ref, bg_ref, q_ref, kt_ref, v_ref, o_ref, gate_ref,
                       *hist_scrs, tiles_per_seq, n_chunk, scale):
    i = pl.program_id(0)
    tm = x_ref.shape[0]
    width = q_ref.shape[1]
    xn = _rms(x_ref[...], g_ref[...]).astype(BF16)

    def even_rows(first, n):
        return pl.ds(2 * first, n, stride=2)

    slabs = n_chunk // LANES
    n_conv = 2 * width // n_chunk
    assert len(hist_scrs) == n_conv

    @pl.when(i % tiles_per_seq == 0)
    def _():
        for hist in hist_scrs:
            hist[:, 0:2 * CONV_PAD, :] = jnp.zeros((slabs, 2 * CONV_PAD, LANES), F32)

    def project(c):
        pre = _dot(xn, w_ref[:, c * n_chunk:(c + 1) * n_chunk])
        for s in range(slabs):
            hist_scrs[c][s, even_rows(CONV_PAD, tm), :] = pre[:, s * LANES:(s + 1) * LANES]

    def conv(c):
        sl = slice(c * n_chunk, (c + 1) * n_chunk)
        hist = hist_scrs[c]
        parts = []
        for s in range(slabs):
            y = None
            for tap in range(CONV_WIDTH):
                off = CONV_PAD - (CONV_WIDTH - 1) + tap
                w_tap = cw_ref[tap:tap + 1, (c * slabs + s) * LANES:(c * slabs + s + 1) * LANES]
                term = w_tap * hist[s, even_rows(off, tm), :]
                y = term if y is None else y + term
            parts.append(y)
            hist[s, even_rows(0, CONV_PAD), :] = hist[s, even_rows(tm, CONV_PAD), :]
        y = jnp.concatenate(parts, axis=1)
        y = y * jax.nn.sigmoid(y)
        if c * n_chunk < width:
            q_ref[:, sl] = (y * scale).astype(BF16)
        else:
            kt_ref[c * n_chunk - width:(c + 1) * n_chunk - width, :] = y.T.astype(BF16)

    for c in range(n_conv + 1):
        if c < n_conv:
            project(c)
        if c >= 1:
            conv(c - 1)
    for c in range(width // n_chunk):
        sl = slice(c * n_chunk, (c + 1) * n_chunk)
        v_ref[:, sl] = _dot(xn, w_ref[:, 2 * width + c * n_chunk:2 * width + (c + 1) * n_chunk]).astype(BF16)
        o_ref[:, sl] = _dot(xn, w_ref[:, 3 * width + c * n_chunk:3 * width + (c + 1) * n_chunk])
    gate_ref[...] = _dot(xn, w_ref[:, 4 * width:]) + bg_ref[...]


def _mlstm_proj(x, g, w_pad, conv_w, b_gates_pad, *, seq, tm=512, n_chunk=512):
    m, d = x.shape
    width = (w_pad.shape[1] - 2 * LANES) // 4
    assert m % tm == 0 and seq % tm == 0 and width % n_chunk == 0
    row = lambda wd: pl.BlockSpec((tm, wd), lambda i: (i, 0))
    scale = 1.0 / math.sqrt(width // N_HEADS_MLSTM)
    return pl.pallas_call(
        functools.partial(_mlstm_proj_kernel, tiles_per_seq=seq // tm, n_chunk=n_chunk, scale=scale),
        grid=(m // tm,),
        in_specs=[row(d), _resident((1, d)), _resident(w_pad.shape), _resident(conv_w.shape),
                  _resident((1, 2 * LANES))],
        out_specs=[row(width), pl.BlockSpec((width, tm), lambda i: (0, i)), row(width), row(width),
                   row(2 * LANES)],
        out_shape=[jax.ShapeDtypeStruct((m, width), BF16), jax.ShapeDtypeStruct((width, m), BF16),
                   jax.ShapeDtypeStruct((m, width), BF16), jax.ShapeDtypeStruct((m, width), F32),
                   jax.ShapeDtypeStruct((m, 2 * LANES), F32)],
        scratch_shapes=[pltpu.VMEM((n_chunk // LANES, 2 * (tm + CONV_PAD), LANES), F32)
                        for _ in range(2 * width // n_chunk)],
        compiler_params=_params(("arbitrary",)),
        name="mlstm_proj",
    )(x, g.reshape(1, d), w_pad, conv_w, b_gates_pad)


def _split3(x):
    hi = x.astype(BF16)
    r = x - hi.astype(F32)
    mid = r.astype(BF16)
    return hi, mid, (r - mid.astype(F32)).astype(BF16)


def _gate_prep_kernel(g_ref, cols_ref, rows_ref, *, heads, cl):
    tm = g_ref.shape[0]
    row = lax.broadcasted_iota(jnp.int32, (cl, cl), 0)
    col = lax.broadcasted_iota(jnp.int32, (cl, cl), 1)
    incl = jnp.where(row >= col, 1.0, 0.0).astype(BF16)
    lane = lax.broadcasted_iota(jnp.int32, (1, cl), 1)
    pad = jnp.zeros((cl - heads, cl), F32)
    for c in range(tm // cl):
        r = slice(c * cl, (c + 1) * cl)
        b = None
        for part in _split3(jax.nn.log_sigmoid(g_ref[r, LANES:])):
            term = _dot(incl, part)
            b = term if b is None else b + term
        u = g_ref[r, :LANES] - b
        u_rows = u.T[0:heads, :]
        b_rows = b.T[0:heads, :]
        cmax = u_rows
        shift = 1
        while shift < cl:
            cmax = jnp.where(lane >= shift, jnp.maximum(cmax, pltpu.roll(cmax, shift, axis=1)), cmax)
            shift *= 2
        u_max = jnp.broadcast_to(jnp.max(u_rows, axis=1, keepdims=True), (heads, cl))
        b_last = jnp.broadcast_to(jnp.min(b_rows, axis=1, keepdims=True), (heads, cl))
        cols_ref[r, :] = jnp.concatenate([u, b, jnp.concatenate([cmax, pad], axis=0).T], axis=1)
        rows_ref[:, r] = jnp.concatenate([u_rows, u_max, b_last], axis=0)


def _gate_prep(gates, *, heads, cl, tm=1024):
    m = gates.shape[0]
    tm = min(tm, m)
    assert m % tm == 0 and tm % cl == 0 and cl == LANES and heads == 8
    return pl.pallas_call(
        functools.partial(_gate_prep_kernel, heads=heads, cl=cl),
        grid=(m // tm,),
        in_specs=[pl.BlockSpec((tm, 2 * LANES), lambda i: (i, 0))],
        out_specs=[pl.BlockSpec((tm, 3 * LANES), lambda i: (i, 0)),
                   pl.BlockSpec((3 * heads, tm), lambda i: (0, i))],
        out_shape=[jax.ShapeDtypeStruct((m, 3 * LANES), F32),
                   jax.ShapeDtypeStruct((3 * heads, m), F32)],
        compiler_params=_params(("parallel",)),
        name="gate_prep",
    )(gates)


def _mlstm_kernel(q_ref, kt_ref, v_ref, o_ref, gcol_ref, grow_ref, hg_ref, out_ref,
                  s_scr, m_scr, ml_scr, *, heads):
    cl = q_ref.shape[0]
    dh = q_ref.shape[1] // heads

    @pl.when(pl.program_id(1) == 0)
    def _():
        s_scr[...] = jnp.zeros_like(s_scr)
        m_scr[...] = jnp.zeros_like(m_scr)
        ml_scr[...] = jnp.zeros_like(ml_scr)

    row = lax.broadcasted_iota(jnp.int32, (cl, cl), 0)
    col = lax.broadcasted_iota(jnp.int32, (cl, cl), 1)
    lower = row >= col
    ones = jnp.ones((cl, dh), BF16)

    u_cols, b_cols, cmax_cols = (gcol_ref[:, t * LANES:(t + 1) * LANES] for t in range(3))
    u_rows, u_max_rows, b_last_rows = (grow_ref[t * heads:(t + 1) * heads, :] for t in range(3))
    m_prev_rows = m_scr[...]
    m_prev_lane = ml_scr[0:1, :]
    mm_last_rows = jnp.maximum(m_prev_rows, u_max_rows)
    decay_rows = jnp.exp(m_prev_rows - mm_last_rows)
    m_scr[...] = b_last_rows + mm_last_rows
    ws_rows = jnp.exp(u_rows - mm_last_rows)
    mm_cols = jnp.maximum(cmax_cols, m_prev_lane)
    floor_cols = jnp.exp(-(b_cols + mm_cols))
    ml_scr[...] = jnp.broadcast_to(b_cols[cl - 1:cl, :] + mm_cols[cl - 1:cl, :], ml_scr.shape)

    def lanes_of(mat, c):
        return jnp.broadcast_to(mat[:, c:c + 1], (cl, LANES))

    def stage_a(h):
        sl = slice(h * dh, (h + 1) * dh)
        q, kt = q_ref[:, sl], kt_ref[sl, :]
        v1 = jnp.concatenate([v_ref[:, sl], ones], axis=1)
        s_prev = s_scr[h]
        qk = _dot(q, kt)
        qs = _dot(q, s_prev.astype(BF16))
        kw = (kt.astype(F32) * ws_rows[h:h + 1, :]).astype(BF16)
        decay = jnp.broadcast_to(decay_rows[h:h + 1, :], (dh, LANES))
        s_scr[h] = jnp.concatenate([decay, decay], axis=1) * s_prev + _dot(kw, v1)
        return h, sl, v1, qk, qs

    def stage_b(h, sl, v1, qk, qs):
        mm = lanes_of(mm_cols, h)
        weight = jnp.where(lower, jnp.exp(u_rows[h:h + 1, :] - mm), 0.0)
        w_inter = jnp.exp(m_prev_rows[h:h + 1, :] - mm)
        return sl, v1, qs, (qk * weight).astype(BF16), w_inter, lanes_of(floor_cols, h)

    def stage_c(sl, v1, qs, p, w_inter, floor):
        pv = _dot(p, v1)
        both = pv + jnp.concatenate([w_inter, w_inter], axis=1) * qs
        hid = both[:, :dh] / jnp.maximum(jnp.abs(both[:, dh:]), floor)
        hid = hid * lax.rsqrt(jnp.mean(hid * hid, axis=-1, keepdims=True) + EPS) * hg_ref[:, sl]
        out_ref[:, sl] = (hid * jax.nn.sigmoid(o_ref[:, sl])).astype(out_ref.dtype)

    after_a, after_b = {}, {}
    for j in range(heads + 2):
        if j < heads:
            after_a[j] = stage_a(j)
        if 0 <= j - 1 < heads:
            after_b[j - 1] = stage_b(*after_a.pop(j - 1))
        if 0 <= j - 2 < heads:
            stage_c(*after_b.pop(j - 2))


def _mlstm(q, kt, v, o, gates, head_g, *, heads=N_HEADS_MLSTM):
    b, t, w = q.shape
    cl = MLSTM_CHUNK
    dh = w // heads
    nc = t // cl
    assert t % cl == 0 and dh == LANES and cl == LANES
    gcols, grows = _gate_prep(gates, heads=heads, cl=cl)
    blk = lambda wd: pl.BlockSpec((None, cl, wd), lambda bi, c: (bi, c, 0))
    return pl.pallas_call(
        functools.partial(_mlstm_kernel, heads=heads),
        grid=(b, nc),
        in_specs=[blk(w), pl.BlockSpec((w, cl), lambda bi, c: (0, bi * nc + c)), blk(w), blk(w),
                  pl.BlockSpec((cl, 3 * LANES), lambda bi, c: (bi * nc + c, 0)),
                  pl.BlockSpec((3 * heads, cl), lambda bi, c: (0, bi * nc + c)),
                  _resident((1, w))],
        out_specs=blk(w),
        out_shape=jax.ShapeDtypeStruct((b, t, w), BF16),
        scratch_shapes=[pltpu.VMEM((heads, dh, 2 * dh), F32), pltpu.VMEM((heads, LANES), F32),
                        pltpu.VMEM((heads, LANES), F32)],
        compiler_params=_params(("parallel", "arbitrary")),
        name="mlstm",
    )(q, kt, v, o, gcols, grows, head_g.reshape(1, w))


def kernel(x, norm_g, ffn_w_gate, ffn_w_up, ffn_w_down, attn_w_in, attn_w_out, rel_bias,
           mlstm_w_in, mlstm_b_gates, mlstm_conv_w, mlstm_head_g, mlstm_w_out):
    bsz, t, d = x.shape
    depth = norm_g.shape[0]
    h = x.reshape(bsz * t, d)
    bf = lambda a: a.astype(BF16)

    def ffn(h, layer, half):
        g = norm_g[layer]
        return _ffn(h, g[2 * half * 2], g[2 * half * 2 + 1], bf(ffn_w_gate[layer, half]),
                    bf(ffn_w_up[layer, half]), bf(ffn_w_down[layer, half]))

    for layer in range(depth):
        g = norm_g[layer]
        j = layer // 2
        h = ffn(h, layer, 0)
        if layer % 2 == 0:
            sb, dil = _attn_proj(h, g[2], bf(attn_w_in[j]))
            out_sb = _sb_attention(sb.reshape(bsz, t, -1))
            out_dil = _dil_attention(dil.reshape(bsz, t, -1), rel_bias.astype(F32))
            parts = [out_sb.reshape(bsz * t, -1), out_dil.reshape(bsz * t, -1)]
            h = _out_proj(parts, h, g[3], bf(attn_w_out[j]))
        else:
            width = mlstm_w_out.shape[1]
            n_head = mlstm_b_gates.shape[1] // 2
            tiles = lambda a: jnp.concatenate(
                [jnp.pad(part, ((0, 0), (0, LANES - n_head)))
                 for part in (a[:, :n_head], a[:, n_head:])], axis=1)
            w_in = bf(mlstm_w_in[j])
            w_pad = jnp.concatenate([w_in[:, :4 * width], tiles(w_in[:, 4 * width:])], axis=1)
            b_pad = tiles(mlstm_b_gates[j].astype(F32).reshape(1, -1))
            q, kt, v, o, gates = _mlstm_proj(h, g[2], w_pad, mlstm_conv_w[j].astype(F32), b_pad, seq=t)
            r3 = lambda a: a.reshape(bsz, t, -1)
            hid = _mlstm(r3(q), kt, r3(v), r3(o), gates, mlstm_head_g[j].astype(F32))
            h = _out_proj([hid.reshape(bsz * t, width)], h, g[3], bf(mlstm_w_out[j]))
        h = ffn(h, layer, 1)
    return h.reshape(bsz, t, d)
```

```python
import functools
import math

import numpy as np
import jax
import jax.numpy as jnp
from jax import lax
from jax.experimental import pallas as pl
from jax.experimental.pallas import tpu as pltpu

EPS = 1e-6
HEAD_DIM_ATTN = 64
N_HEADS_SB = 8
N_HEADS_DIL = 8
DIL_CONFIGS = ((128, 1), (512, 4), (2048, 16))
DIL_BLOCK = 128
DIL_UNROLL = 8
NUM_BUCKETS = 32
MAX_DISTANCE = 2048
N_HEADS_MLSTM = 8
MLSTM_CHUNK = 128
CONV_WIDTH = 4
LANES = 128
CONV_PAD = 8
MASKED = -1e30
VMEM_LIMIT = 56 * 1024 * 1024

F32 = jnp.float32
BF16 = jnp.bfloat16


def _params(sem, vmem=VMEM_LIMIT):
    return pltpu.CompilerParams(dimension_semantics=sem, vmem_limit_bytes=vmem)


def _resident(shape):
    zeros = (0,) * len(shape)
    return pl.BlockSpec(shape, lambda *_: zeros, pipeline_mode=pl.Buffered(1))


def _rms(x, g):
    return x * lax.rsqrt(jnp.mean(x * x, axis=-1, keepdims=True) + EPS) * g


def _dot(a, b):
    return jnp.dot(a, b, preferred_element_type=F32)


def _dot_nt(a, b):
    return lax.dot_general(a, b, (((1,), (1,)), ((), ())), preferred_element_type=F32)


def _dot_tn(a, b):
    return lax.dot_general(a, b, (((0,), (0,)), ((), ())), preferred_element_type=F32)


def _ffn_kernel(x_ref, gin_ref, gout_ref, wg_ref, wu_ref, wd_ref, o_ref, *, ff_chunk):
    x = x_ref[...]
    xn = _rms(x, gin_ref[...]).astype(BF16)
    d_ff = wg_ref.shape[1]
    acc = None
    for c in range(d_ff // ff_chunk):
        sl = slice(c * ff_chunk, (c + 1) * ff_chunk)
        gate = _dot(xn, wg_ref[:, sl])
        up = _dot(xn, wu_ref[:, sl])
        h = (gate * jax.nn.sigmoid(gate) * up).astype(BF16)
        part = _dot(h, wd_ref[sl, :])
        acc = part if acc is None else acc + part
    o_ref[...] = x + 0.5 * _rms(acc, gout_ref[...])


def _ffn(x, g_in, g_out, wg, wu, wd, which=(), *, tm=512, ff_chunk=256):
    m, d = x.shape
    d_ff = wg.shape[-1]
    assert m % tm == 0 and d_ff % ff_chunk == 0 and len(which) == wg.ndim - 2
    row = pl.BlockSpec((tm, d), lambda i: (i, 0))
    lead = (None,) * len(which)
    weight = lambda shape: pl.BlockSpec(lead + shape, lambda i: tuple(which) + (0, 0),
                                        pipeline_mode=pl.Buffered(1))
    return pl.pallas_call(
        functools.partial(_ffn_kernel, ff_chunk=ff_chunk),
        grid=(m // tm,),
        in_specs=[row, _resident((1, d)), _resident((1, d)),
                  weight((d, d_ff)), weight((d, d_ff)), weight((d_ff, d))],
        out_specs=row,
        out_shape=jax.ShapeDtypeStruct((m, d), F32),
        compiler_params=_params(("parallel",)),
        name="ffn",
    )(x, g_in.reshape(1, d), g_out.reshape(1, d), wg, wu, wd)


def _attn_proj_kernel(x_ref, g_ref, w_ref, sb_ref, dil_ref, *, n_chunk, scale):
    xn = _rms(x_ref[...], g_ref[...]).astype(BF16)
    w_sb = sb_ref.shape[1]
    w_q = w_sb // 3
    for c in range(w_sb // n_chunk):
        sl = slice(c * n_chunk, (c + 1) * n_chunk)
        y = _dot(xn, w_ref[:, sl])
        if (c + 1) * n_chunk <= w_q:
            y = y * scale
        sb_ref[:, sl] = y.astype(BF16)
    for c in range(dil_ref.shape[1] // n_chunk):
        sl = slice(c * n_chunk, (c + 1) * n_chunk)
        y = _dot(xn, w_ref[:, w_sb + c * n_chunk: w_sb + (c + 1) * n_chunk])
        if (c + 1) * n_chunk <= w_q:
            y = y * scale
        dil_ref[:, sl] = y


def _attn_proj(x, g, w, *, tm=512, n_chunk=512):
    m, d = x.shape
    n = w.shape[1]
    half = n // 2
    assert m % tm == 0 and (half // 3) % n_chunk == 0
    scale = 1.0 / math.sqrt(HEAD_DIM_ATTN)
    return pl.pallas_call(
        functools.partial(_attn_proj_kernel, n_chunk=n_chunk, scale=scale),
        grid=(m // tm,),
        in_specs=[pl.BlockSpec((tm, d), lambda i: (i, 0)), _resident((1, d)), _resident((d, n))],
        out_specs=[pl.BlockSpec((tm, half), lambda i: (i, 0)),
                   pl.BlockSpec((tm, half), lambda i: (i, 0))],
        out_shape=[jax.ShapeDtypeStruct((m, half), BF16), jax.ShapeDtypeStruct((m, half), F32)],
        compiler_params=_params(("parallel",)),
        name="attn_proj",
    )(x, g.reshape(1, d), w)


LOG2E = math.log2(math.e)
SB_DEAD_LOG2 = -160.0


def _sb_kernel(q_ref, k_ref, v_ref, tri_ref, o_ref, *, blk):
    i = pl.program_id(2)
    q2 = q_ref[...]
    lane = lax.broadcasted_iota(jnp.int32, (1, LANES), 1)
    row = lax.broadcasted_iota(jnp.int32, (blk, blk), 0)
    col = lax.broadcasted_iota(jnp.int32, (blk, blk), 1)
    causal = col < row
    tri = tri_ref[...]

    head0 = lane < HEAD_DIM_ATTN
    zeros = jnp.zeros_like(q2)
    qq = jnp.concatenate([jnp.where(head0, q2, zeros), jnp.where(head0, zeros, q2)], axis=0)
    causal2 = jnp.concatenate([causal, causal], axis=0)

    def pair(kb, state, diag):
        carry, acc = state
        has_second = kb >= 1
        starts = [pl.multiple_of(kb * blk, blk), pl.multiple_of(jnp.maximum(kb - 1, 0) * blk, blk)]
        masks = [causal2 if diag else None, has_second]
        zs = [_dot_nt(qq, k_ref[pl.ds(ks, blk), :]) * LOG2E for ks in starts]
        keeps, sums, laters = [], [], []
        for z, mask in zip(zs, masks):
            log_keep = -(jnp.maximum(z, 0.0) + jnp.log2(1.0 + jnp.exp2(-jnp.abs(z))))
            if mask is not None:
                log_keep = jnp.where(mask, log_keep, 0.0)
            keeps.append(log_keep)
            sums.append(jnp.sum(log_keep, axis=-1, keepdims=True))
        for log_keep in keeps:
            hi = log_keep.astype(BF16)
            lo = (log_keep - hi.astype(F32)).astype(BF16)
            laters.append(_dot(hi, tri) + _dot(lo, tri))
        for n, ks in enumerate(starts):
            p = jnp.exp2(zs[n] + keeps[n] + laters[n] + carry)
            if masks[n] is not None:
                p = jnp.where(masks[n], p, 0.0)
            acc = acc + _dot(p.astype(BF16), v_ref[pl.ds(ks, blk), :])
            carry = carry + sums[n]
        return carry, acc

    def alive(state):
        return (jnp.max(state[0]) > SB_DEAD_LOG2).astype(jnp.int32)

    zero = (jnp.zeros((2 * blk, 1), F32), jnp.zeros((2 * blk, LANES), F32))
    state = pair(i, zero, True)
    n_pairs = lax.shift_right_logical(i, 1)

    def cond(loop):
        t, live, _ = loop
        return (t < n_pairs) & (live > 0)

    def body(loop):
        t, _, state = loop
        state = pair(i - 2 - 2 * t, state, False)
        return t + 1, alive(state), state

    _, _, state = lax.while_loop(cond, body, (jnp.int32(0), alive(state), state))
    o_ref[...] = jnp.where(head0, state[1][:blk], state[1][blk:]).astype(o_ref.dtype)


def _sb_attention(qkv, *, blk=256):
    b, t, w3 = qkv.shape
    w = w3 // 3
    pairs = w // LANES
    blk = min(blk, t)
    assert t % blk == 0
    tri = jnp.asarray(np.tril(np.ones((blk, blk), np.float32), -1), BF16)
    return pl.pallas_call(
        functools.partial(_sb_kernel, blk=blk),
        grid=(b, pairs, t // blk),
        in_specs=[pl.BlockSpec((None, blk, LANES), lambda bi, p, i: (bi, i, p)),
                  pl.BlockSpec((None, t, LANES), lambda bi, p, i: (bi, 0, pairs + p)),
                  pl.BlockSpec((None, t, LANES), lambda bi, p, i: (bi, 0, 2 * pairs + p)),
                  _resident((blk, blk))],
        out_specs=pl.BlockSpec((None, blk, LANES), lambda bi, p, i: (bi, i, p)),
        out_shape=jax.ShapeDtypeStruct((b, t, w), BF16),
        compiler_params=_params(("parallel", "parallel", "arbitrary")),
        name="sb_attn",
    )(qkv, qkv, qkv, tri)


def _t5_bucket_np(dist):
    max_exact = NUM_BUCKETS // 2
    d = np.maximum(dist, 1).astype(np.float32)
    log_b = max_exact + (np.log(d / np.float32(max_exact)) / np.float32(math.log(MAX_DISTANCE / max_exact))
                         * np.float32(NUM_BUCKETS - max_exact)).astype(np.int32)
    log_b = np.minimum(log_b, NUM_BUCKETS - 1)
    return np.where(dist < max_exact, dist, log_b)


def _dil_tables():
    qi = np.arange(DIL_BLOCK)[:, None]
    ki = np.arange(2 * DIL_BLOCK)[None, :]
    dist = qi + DIL_BLOCK - ki
    buckets, valid = [], []
    for window, dil in DIL_CONFIGS:
        steps = window // dil
        buckets.append(_t5_bucket_np(np.maximum(dist, 0) * dil))
        valid.append((dist >= 0) & (dist <= steps))
    return np.stack(buckets).astype(np.int32), np.stack(valid).astype(np.int32)


def _dil_kernel(rb_ref, bucket_ref, valid_ref, q_ref, k_ref, v_ref, o_ref,
                bias_scr, num_scr, m_scr, l_scr, *, seq):
    p = pl.program_id(0)
    qb = DIL_BLOCK
    lane = lax.broadcasted_iota(jnp.int32, (1, LANES), 1)
    head0 = lane < HEAD_DIM_ATTN
    first_half = lax.broadcasted_iota(jnp.int32, (1, 2 * qb), 1) < qb

    @pl.when(pl.program_id(1) == 0)
    def _():
        for br in range(len(DIL_CONFIGS)):
            bucket = bucket_ref[br]
            valid = valid_ref[br] > 0
            for h in range(2):
                bias = jnp.zeros((qb, 2 * qb), F32)
                for b in range(NUM_BUCKETS):
                    bias = jnp.where(bucket == b, rb_ref[b, 2 * p + h], bias)
                bias_scr[br, h * qb:(h + 1) * qb, :] = jnp.where(valid, bias, MASKED)

    for br, (_, dil) in enumerate(DIL_CONFIGS):
        n_units = seq // qb

        n_blocks = seq // (qb * dil)
        run = min(DIL_UNROLL, n_blocks)
        runs = DIL_UNROLL // run
        assert DIL_UNROLL % run == 0 and n_blocks % run == 0 and (dil == 1 or n_blocks == run)

        def rows(ref, start, dil=dil):
            if dil == 1:
                return ref[pl.ds(start, qb), :].astype(BF16)
            return ref[pl.ds(start, qb, stride=dil), :].astype(BF16)

        def load_run(rho, dil=dil, run=run, whole=(n_blocks == run)):
            res, base = (rho, 0) if whole else (0, rho * run)
            starts = [(base + i) * (qb * dil) + res for i in range(run)]
            kb = [rows(k_ref, st) for st in starts]
            vb = [rows(v_ref, st) for st in starts]
            if whole:
                k_prev, v_prev = kb[0], vb[0]
                pen = jnp.where(first_half, MASKED, 0.0)
            else:
                st = jnp.maximum(base - 1, 0) * (qb * dil) + res
                k_prev, v_prev = rows(k_ref, st), rows(v_ref, st)
                pen = jnp.where(first_half, jnp.where(base == 0, MASKED, 0.0), 0.0)
            units = []
            for i, st in enumerate(starts):
                q2 = rows(q_ref, st)
                zeros = jnp.zeros_like(q2)
                qq = jnp.concatenate([jnp.where(head0, q2, zeros), jnp.where(head0, zeros, q2)], axis=0)
                k2 = jnp.concatenate([kb[i - 1] if i else k_prev, kb[i]], axis=0)
                v2 = jnp.concatenate([vb[i - 1] if i else v_prev, vb[i]], axis=0)
                units.append((st, pen if i == 0 else None, qq, k2, v2))
            return units

        def softmax_parts(s):
            m = jnp.max(s, axis=-1, keepdims=True)
            e = jnp.exp(s - m)
            return m, e, jnp.sum(e, axis=-1, keepdims=True)

        def group(g, _, br=br, dil=dil, runs=runs):
            units = [u for r in range(runs) for u in load_run(g * runs + r)]
            scores, parts = {}, {}
            for j in range(DIL_UNROLL + 2):
                if j < DIL_UNROLL:
                    _, pen, qq, k2, _ = units[j]
                    scores[j] = _dot_nt(qq, k2) + bias_scr[br]
                    if pen is not None:
                        scores[j] = scores[j] + pen
                if 0 <= j - 1 < DIL_UNROLL:
                    parts[j - 1] = softmax_parts(scores.pop(j - 1))
                if 0 <= j - 2 < DIL_UNROLL:
                    st, _, _, _, v2 = units[j - 2]
                    m, e, l = parts.pop(j - 2)
                    pv = _dot(e.astype(BF16), v2)
                    idx = pl.ds(st, qb) if dil == 1 else pl.ds(st, qb, stride=dil)
                    num_scr[br, idx, :] = jnp.where(head0, pv[:qb], pv[qb:])
                    m_scr[br, idx, :] = jnp.where(head0, m[:qb], m[qb:])
                    l_scr[br, idx, :] = jnp.where(head0, l[:qb], l[qb:])
            return 0

        lax.fori_loop(0, n_units // DIL_UNROLL, group, 0)

    rows_out = 256

    def finish(c, _):
        idx = pl.ds(pl.multiple_of(c * rows_out, rows_out), rows_out)
        m_all = [m_scr[br, idx, :] for br in range(len(DIL_CONFIGS))]
        m_max = functools.reduce(jnp.maximum, m_all)
        num = den = None
        for br, m_br in enumerate(m_all):
            wt = jnp.exp(m_br - m_max)
            n_br = wt * num_scr[br, idx, :]
            d_br = wt * l_scr[br, idx, :]
            num = n_br if num is None else num + n_br
            den = d_br if den is None else den + d_br
        o_ref[idx, :] = (num / den).astype(o_ref.dtype)
        return 0

    lax.fori_loop(0, seq // rows_out, finish, 0)


def _dil_attention(qkv, rel_bias):
    b, t, w3 = qkv.shape
    w = w3 // 3
    pairs = w // LANES
    assert t % (DIL_BLOCK * max(d for _, d in DIL_CONFIGS)) == 0 and t % 256 == 0
    bucket, valid = _dil_tables()
    nbr = len(DIL_CONFIGS)
    seq_spec = lambda off: pl.BlockSpec((None, t, LANES), lambda p, bi: (bi, 0, off + p))
    return pl.pallas_call(
        functools.partial(_dil_kernel, seq=t),
        grid=(pairs, b),
        in_specs=[pl.BlockSpec(memory_space=pltpu.SMEM),
                  _resident((nbr, DIL_BLOCK, 2 * DIL_BLOCK)),
                  _resident((nbr, DIL_BLOCK, 2 * DIL_BLOCK)),
                  seq_spec(0), seq_spec(pairs), seq_spec(2 * pairs)],
        out_specs=pl.BlockSpec((None, t, LANES), lambda p, bi: (bi, 0, p)),
        out_shape=jax.ShapeDtypeStruct((b, t, w), BF16),
        scratch_shapes=[pltpu.VMEM((nbr, 2 * DIL_BLOCK, 2 * DIL_BLOCK), F32),
                        pltpu.VMEM((nbr, t, LANES), F32), pltpu.VMEM((nbr, t, LANES), F32),
                        pltpu.VMEM((nbr, t, LANES), F32)],
        compiler_params=_params(("arbitrary", "arbitrary")),
        name="dil_attn",
    )(rel_bias, jnp.asarray(bucket), jnp.asarray(valid), qkv, qkv, qkv)


def _out_proj_kernel(*refs, n_parts):
    parts, (x_ref, g_ref, w_ref, o_ref) = refs[:n_parts], refs[n_parts:]
    a = jnp.concatenate([r[...] for r in parts], axis=-1) if n_parts > 1 else parts[0][...]
    y = _dot(a, w_ref[...])
    o_ref[...] = x_ref[...] + _rms(y, g_ref[...])


def _out_proj(parts, x, g, w, *, tm=512):
    m, d = x.shape
    assert m % tm == 0
    row = lambda width: pl.BlockSpec((tm, width), lambda i: (i, 0))
    return pl.pallas_call(
        functools.partial(_out_proj_kernel, n_parts=len(parts)),
        grid=(m // tm,),
        in_specs=[row(a.shape[1]) for a in parts] + [row(d), _resident((1, d)), _resident(w.shape)],
        out_specs=row(d),
        out_shape=jax.ShapeDtypeStruct((m, d), F32),
        compiler_params=_params(("parallel",)),
        name="out_proj",
    )(*parts, x, g.reshape(1, d), w)


def _mlstm_proj_kernel(x_ref, g_ref, w_ref, wg_ref, cw_ref, bg_ref, q_ref, kt_ref, v_ref, o_ref,
                       gate_ref, *hist_scrs, tiles_per_seq, n_chunk, scale):
    i = pl.program_id(0)
    tm = x_ref.shape[0]
    width = q_ref.shape[1]
    xn = _rms(x_ref[...], g_ref[...]).astype(BF16)

    def even_rows(first, n):
        return pl.ds(2 * first, n, stride=2)

    slabs = n_chunk // LANES
    n_conv = 2 * width // n_chunk
    assert len(hist_scrs) == n_conv

    @pl.when(i % tiles_per_seq == 0)
    def _():
        for hist in hist_scrs:
            hist[:, 0:2 * CONV_PAD, :] = jnp.zeros((slabs, 2 * CONV_PAD, LANES), F32)

    def project(c):
        pre = _dot(xn, w_ref[:, c * n_chunk:(c + 1) * n_chunk])
        for s in range(slabs):
            hist_scrs[c][s, even_rows(CONV_PAD, tm), :] = pre[:, s * LANES:(s + 1) * LANES]

    def conv(c):
        sl = slice(c * n_chunk, (c + 1) * n_chunk)
        hist = hist_scrs[c]
        parts = []
        for s in range(slabs):
            y = None
            for tap in range(CONV_WIDTH):
                off = CONV_PAD - (CONV_WIDTH - 1) + tap
                w_tap = cw_ref[tap:tap + 1, (c * slabs + s) * LANES:(c * slabs + s + 1) * LANES]
                term = w_tap * hist[s, even_rows(off, tm), :]
                y = term if y is None else y + term
            parts.append(y)
            hist[s, even_rows(0, CONV_PAD), :] = hist[s, even_rows(tm, CONV_PAD), :]
        y = jnp.concatenate(parts, axis=1)
        y = y * jax.nn.sigmoid(y)
        if c * n_chunk < width:
            q_ref[:, sl] = (y * scale).astype(BF16)
        else:
            kt_ref[c * n_chunk - width:(c + 1) * n_chunk - width, :] = y.T.astype(BF16)

    for c in range(n_conv + 1):
        if c < n_conv:
            project(c)
        if c >= 1:
            conv(c - 1)
    for c in range(width // n_chunk):
        sl = slice(c * n_chunk, (c + 1) * n_chunk)
        v_ref[:, sl] = _dot(xn, w_ref[:, 2 * width + c * n_chunk:2 * width + (c + 1) * n_chunk]).astype(BF16)
        o_ref[:, sl] = _dot(xn, w_ref[:, 3 * width + c * n_chunk:3 * width + (c + 1) * n_chunk])
    gate_ref[...] = _dot(xn, wg_ref[...]) + bg_ref[...]


def _mlstm_proj(x, g, w_in, w_gates, conv_w, b_gates, *, seq, tm=512, n_chunk=512):
    m, d = x.shape
    width = conv_w.shape[1] // 2
    assert m % tm == 0 and seq % tm == 0 and width % n_chunk == 0 and w_in.shape[1] >= 4 * width
    row = lambda wd: pl.BlockSpec((tm, wd), lambda i: (i, 0))
    scale = 1.0 / math.sqrt(width // N_HEADS_MLSTM)
    return pl.pallas_call(
        functools.partial(_mlstm_proj_kernel, tiles_per_seq=seq // tm, n_chunk=n_chunk, scale=scale),
        grid=(m // tm,),
        in_specs=[row(d), _resident((1, d)), _resident(w_in.shape), _resident((d, 2 * LANES)),
                  _resident(conv_w.shape), _resident((1, 2 * LANES))],
        out_specs=[row(width), pl.BlockSpec((width, tm), lambda i: (0, i)), row(width), row(width),
                   row(2 * LANES)],
        out_shape=[jax.ShapeDtypeStruct((m, width), BF16), jax.ShapeDtypeStruct((width, m), BF16),
                   jax.ShapeDtypeStruct((m, width), BF16), jax.ShapeDtypeStruct((m, width), F32),
                   jax.ShapeDtypeStruct((m, 2 * LANES), F32)],
        scratch_shapes=[pltpu.VMEM((n_chunk // LANES, 2 * (tm + CONV_PAD), LANES), F32)
                        for _ in range(2 * width // n_chunk)],
        compiler_params=_params(("arbitrary",)),
        name="mlstm_proj",
    )(x, g.reshape(1, d), w_in, w_gates, conv_w, b_gates)


def _split3(x):
    hi = x.astype(BF16)
    r = x - hi.astype(F32)
    mid = r.astype(BF16)
    return hi, mid, (r - mid.astype(F32)).astype(BF16)


def _gate_prep_kernel(g_ref, cols_ref, rows_ref, *, heads, cl):
    tm = g_ref.shape[0]
    row = lax.broadcasted_iota(jnp.int32, (cl, cl), 0)
    col = lax.broadcasted_iota(jnp.int32, (cl, cl), 1)
    incl = jnp.where(row >= col, 1.0, 0.0).astype(BF16)
    lane = lax.broadcasted_iota(jnp.int32, (1, cl), 1)
    pad = jnp.zeros((cl - heads, cl), F32)
    for c in range(tm // cl):
        r = slice(c * cl, (c + 1) * cl)
        b = None
        for part in _split3(jax.nn.log_sigmoid(g_ref[r, LANES:])):
            term = _dot(incl, part)
            b = term if b is None else b + term
        u = g_ref[r, :LANES] - b
        u_rows = u.T[0:heads, :]
        b_rows = b.T[0:heads, :]
        cmax = u_rows
        shift = 1
        while shift < cl:
            cmax = jnp.where(lane >= shift, jnp.maximum(cmax, pltpu.roll(cmax, shift, axis=1)), cmax)
            shift *= 2
        u_max = jnp.broadcast_to(jnp.max(u_rows, axis=1, keepdims=True), (heads, cl))
        b_last = jnp.broadcast_to(jnp.min(b_rows, axis=1, keepdims=True), (heads, cl))
        cols_ref[r, :] = jnp.concatenate([u, b, jnp.concatenate([cmax, pad], axis=0).T], axis=1)
        rows_ref[:, r] = jnp.concatenate([u_rows, u_max, b_last], axis=0)


def _gate_prep(gates, *, heads, cl, tm=1024):
    m = gates.shape[0]
    tm = min(tm, m)
    assert m % tm == 0 and tm % cl == 0 and cl == LANES and heads == 8
    return pl.pallas_call(
        functools.partial(_gate_prep_kernel, heads=heads, cl=cl),
        grid=(m // tm,),
        in_specs=[pl.BlockSpec((tm, 2 * LANES), lambda i: (i, 0))],
        out_specs=[pl.BlockSpec((tm, 3 * LANES), lambda i: (i, 0)),
                   pl.BlockSpec((3 * heads, tm), lambda i: (0, i))],
        out_shape=[jax.ShapeDtypeStruct((m, 3 * LANES), F32),
                   jax.ShapeDtypeStruct((3 * heads, m), F32)],
        compiler_params=_params(("parallel",)),
        name="gate_prep",
    )(gates)


def _mlstm_kernel(q_ref, kt_ref, v_ref, o_ref, gcol_ref, grow_ref, hg_ref, out_ref,
                  s_scr, m_scr, ml_scr, *, heads):
    cl = q_ref.shape[0]
    dh = q_ref.shape[1] // heads

    @pl.when(pl.program_id(1) == 0)
    def _():
        s_scr[...] = jnp.zeros_like(s_scr)
        m_scr[...] = jnp.zeros_like(m_scr)
        ml_scr[...] = jnp.zeros_like(ml_scr)

    row = lax.broadcasted_iota(jnp.int32, (cl, cl), 0)
    col = lax.broadcasted_iota(jnp.int32, (cl, cl), 1)
    lower = row >= col
    ones = jnp.ones((cl, dh), BF16)

    u_cols, b_cols, cmax_cols = (gcol_ref[:, t * LANES:(t + 1) * LANES] for t in range(3))
    u_rows, u_max_rows, b_last_rows = (grow_ref[t * heads:(t + 1) * heads, :] for t in range(3))
    m_prev_rows = m_scr[...]
    m_prev_lane = ml_scr[0:1, :]
    mm_last_rows = jnp.maximum(m_prev_rows, u_max_rows)
    decay_rows = jnp.exp(m_prev_rows - mm_last_rows)
    m_scr[...] = b_last_rows + mm_last_rows
    ws_rows = jnp.exp(u_rows - mm_last_rows)
    mm_cols = jnp.maximum(cmax_cols, m_prev_lane)
    floor_cols = jnp.exp(-(b_cols + mm_cols))
    ml_scr[...] = jnp.broadcast_to(b_cols[cl - 1:cl, :] + mm_cols[cl - 1:cl, :], ml_scr.shape)

    def lanes_of(mat, c):
        return jnp.broadcast_to(mat[:, c:c + 1], (cl, LANES))

    def stage_a(h):
        sl = slice(h * dh, (h + 1) * dh)
        q, kt = q_ref[:, sl], kt_ref[sl, :]
        v1 = jnp.concatenate([v_ref[:, sl], ones], axis=1)
        s_prev = s_scr[h]
        qk = _dot(q, kt)
        qs = _dot(q, s_prev.astype(BF16))
        kw = (kt.astype(F32) * ws_rows[h:h + 1, :]).astype(BF16)
        decay = jnp.broadcast_to(decay_rows[h:h + 1, :], (dh, LANES))
        s_scr[h] = jnp.concatenate([decay, decay], axis=1) * s_prev + _dot(kw, v1)
        return h, sl, v1, qk, qs

    def stage_b(h, sl, v1, qk, qs):
        mm = lanes_of(mm_cols, h)
        weight = jnp.where(lower, jnp.exp(u_rows[h:h + 1, :] - mm), 0.0)
        w_inter = jnp.exp(m_prev_rows[h:h + 1, :] - mm)
        return sl, v1, qs, (qk * weight).astype(BF16), w_inter, lanes_of(floor_cols, h)

    def stage_c(sl, v1, qs, p, w_inter, floor):
        pv = _dot(p, v1)
        both = pv + jnp.concatenate([w_inter, w_inter], axis=1) * qs
        hid = both[:, :dh] / jnp.maximum(jnp.abs(both[:, dh:]), floor)
        hid = hid * lax.rsqrt(jnp.mean(hid * hid, axis=-1, keepdims=True) + EPS) * hg_ref[:, sl]
        out_ref[:, sl] = (hid * jax.nn.sigmoid(o_ref[:, sl])).astype(out_ref.dtype)

    after_a, after_b = {}, {}
    for j in range(heads + 2):
        if j < heads:
            after_a[j] = stage_a(j)
        if 0 <= j - 1 < heads:
            after_b[j - 1] = stage_b(*after_a.pop(j - 1))
        if 0 <= j - 2 < heads:
            stage_c(*after_b.pop(j - 2))


def _mlstm(q, kt, v, o, gates, head_g, *, heads=N_HEADS_MLSTM):
    b, t, w = q.shape
    cl = MLSTM_CHUNK
    dh = w // heads
    nc = t // cl
    assert t % cl == 0 and dh == LANES and cl == LANES
    gcols, grows = _gate_prep(gates, heads=heads, cl=cl)
    blk = lambda wd: pl.BlockSpec((None, cl, wd), lambda bi, c: (bi, c, 0))
    return pl.pallas_call(
        functools.partial(_mlstm_kernel, heads=heads),
        grid=(b, nc),
        in_specs=[blk(w), pl.BlockSpec((w, cl), lambda bi, c: (0, bi * nc + c)), blk(w), blk(w),
                  pl.BlockSpec((cl, 3 * LANES), lambda bi, c: (bi * nc + c, 0)),
                  pl.BlockSpec((3 * heads, cl), lambda bi, c: (0, bi * nc + c)),
                  _resident((1, w))],
        out_specs=blk(w),
        out_shape=jax.ShapeDtypeStruct((b, t, w), BF16),
        scratch_shapes=[pltpu.VMEM((heads, dh, 2 * dh), F32), pltpu.VMEM((heads, LANES), F32),
                        pltpu.VMEM((heads, LANES), F32)],
        compiler_params=_params(("parallel", "arbitrary")),
        name="mlstm",
    )(q, kt, v, o, gcols, grows, head_g.reshape(1, w))


def kernel(x, norm_g, ffn_w_gate, ffn_w_up, ffn_w_down, attn_w_in, attn_w_out, rel_bias,
           mlstm_w_in, mlstm_b_gates, mlstm_conv_w, mlstm_head_g, mlstm_w_out):
    bsz, t, d = x.shape
    depth = norm_g.shape[0]
    h = x.reshape(bsz * t, d)
    bf = lambda a: a.astype(BF16)

    w_gate, w_up, w_down = bf(ffn_w_gate), bf(ffn_w_up), bf(ffn_w_down)

    def ffn(h, layer, half):
        g = norm_g[layer]
        return _ffn(h, g[2 * half * 2], g[2 * half * 2 + 1], w_gate, w_up, w_down, (layer, half))

    for layer in range(depth):
        g = norm_g[layer]
        j = layer // 2
        h = ffn(h, layer, 0)
        if layer % 2 == 0:
            sb, dil = _attn_proj(h, g[2], bf(attn_w_in[j]))
            out_sb = _sb_attention(sb.reshape(bsz, t, -1))
            out_dil = _dil_attention(dil.reshape(bsz, t, -1), rel_bias.astype(F32))
            parts = [out_sb.reshape(bsz * t, -1), out_dil.reshape(bsz * t, -1)]
            h = _out_proj(parts, h, g[3], bf(attn_w_out[j]))
        else:
            width = mlstm_w_out.shape[1]
            n_head = mlstm_b_gates.shape[1] // 2
            tiles = lambda a: jnp.concatenate(
                [jnp.pad(part, ((0, 0), (0, LANES - n_head)))
                 for part in (a[:, :n_head], a[:, n_head:])], axis=1)
            w_in = bf(mlstm_w_in[j])
            w_gates = tiles(w_in[:, 4 * width:])
            b_gates = tiles(mlstm_b_gates[j].astype(F32).reshape(1, -1))
            q, kt, v, o, gates = _mlstm_proj(h, g[2], w_in, w_gates, mlstm_conv_w[j].astype(F32),
                                             b_gates, seq=t)
            r3 = lambda a: a.reshape(bsz, t, -1)
            hid = _mlstm(r3(q), kt, r3(v), r3(o), gates, mlstm_head_g[j].astype(F32))
            h = _out_proj([hid.reshape(bsz * t, width)], h, g[3], bf(mlstm_w_out[j]))
        h = ffn(h, layer, 1)
    return h.reshape(bsz, t, d)
```

```python
import functools
import math

import numpy as np
import jax
import jax.numpy as jnp
from jax import lax
from jax.experimental import pallas as pl
from jax.experimental.pallas import tpu as pltpu

EPS = 1e-6
HEAD_DIM_ATTN = 64
N_HEADS_SB = 8
N_HEADS_DIL = 8
DIL_CONFIGS = ((128, 1), (512, 4), (2048, 16))
DIL_BLOCK = 128
DIL_UNROLL = 8
NUM_BUCKETS = 32
MAX_DISTANCE = 2048
N_HEADS_MLSTM = 8
MLSTM_CHUNK = 128
CONV_WIDTH = 4
LANES = 128
CONV_PAD = 8
MASKED = -1e30
VMEM_LIMIT = 56 * 1024 * 1024

F32 = jnp.float32
BF16 = jnp.bfloat16


def _params(sem, vmem=VMEM_LIMIT):
    return pltpu.CompilerParams(dimension_semantics=sem, vmem_limit_bytes=vmem)


def _resident(shape):
    zeros = (0,) * len(shape)
    return pl.BlockSpec(shape, lambda *_: zeros, pipeline_mode=pl.Buffered(1))


def _rms(x, g):
    return x * lax.rsqrt(jnp.mean(x * x, axis=-1, keepdims=True) + EPS) * g


def _dot(a, b):
    return jnp.dot(a, b, preferred_element_type=F32)


def _dot_nt(a, b):
    return lax.dot_general(a, b, (((1,), (1,)), ((), ())), preferred_element_type=F32)


def _dot_tn(a, b):
    return lax.dot_general(a, b, (((0,), (0,)), ((), ())), preferred_element_type=F32)


def _ffn_kernel(*refs, ff_chunk, n_mix, groups):
    mix_refs, refs = refs[:n_mix], refs[n_mix:]
    if n_mix:
        (wmix_ref, gmix_ref), refs = refs[:2], refs[2:]
    x_ref, gin_ref, gout_ref, wg_ref, wu_ref, wd_ref, o_ref = refs
    d_ff = wg_ref.shape[1]
    rows = x_ref.shape[0] // groups
    spans = [slice(r * rows, (r + 1) * rows) for r in range(groups)]

    xs = []
    for rs in spans:
        x = x_ref[rs, :]
        if n_mix:
            mixed = jnp.concatenate([r[rs, :] for r in mix_refs], axis=-1)
            x = x + _rms(_dot(mixed, wmix_ref[...]), gmix_ref[...])
        xs.append(x)
    xns = [_rms(x, gin_ref[...]).astype(BF16) for x in xs]
    accs = []
    for xn in xns:
        acc = None
        for c in range(d_ff // ff_chunk):
            sl = slice(c * ff_chunk, (c + 1) * ff_chunk)
            gate = _dot(xn, wg_ref[:, sl])
            up = _dot(xn, wu_ref[:, sl])
            h = (gate * jax.nn.sigmoid(gate) * up).astype(BF16)
            part = _dot(h, wd_ref[sl, :])
            acc = part if acc is None else acc + part
        accs.append(acc)
    for rs, x, acc in zip(spans, xs, accs):
        o_ref[rs, :] = x + 0.5 * _rms(acc, gout_ref[...])


def _ffn(x, g_in, g_out, wg, wu, wd, which=(), mix=None, *, tm=1024, groups=2, ff_chunk=256):
    m, d = x.shape
    d_ff = wg.shape[-1]
    assert m % tm == 0 and tm % groups == 0 and d_ff % ff_chunk == 0 and len(which) == wg.ndim - 2
    row = lambda width: pl.BlockSpec((tm, width), lambda i: (i, 0))
    lead = (None,) * len(which)
    weight = lambda shape: pl.BlockSpec(lead + shape, lambda i: tuple(which) + (0, 0),
                                        pipeline_mode=pl.Buffered(1))
    parts, w_mix, g_mix = mix if mix else ((), None, None)
    mix_specs = [row(a.shape[1]) for a in parts] + ([_resident(w_mix.shape), _resident((1, d))] if mix else [])
    mix_args = list(parts) + ([w_mix, g_mix.reshape(1, d)] if mix else [])
    return pl.pallas_call(
        functools.partial(_ffn_kernel, ff_chunk=ff_chunk, n_mix=len(parts), groups=groups),
        grid=(m // tm,),
        in_specs=mix_specs + [row(d), _resident((1, d)), _resident((1, d)),
                              weight((d, d_ff)), weight((d, d_ff)), weight((d_ff, d))],
        out_specs=row(d),
        out_shape=jax.ShapeDtypeStruct((m, d), F32),
        compiler_params=_params(("parallel",)),
        name="ffn",
    )(*mix_args, x, g_in.reshape(1, d), g_out.reshape(1, d), wg, wu, wd)


def _attn_proj_kernel(x_ref, g_ref, w_ref, sb_ref, dil_ref, *, n_chunk, scale):
    xn = _rms(x_ref[...], g_ref[...]).astype(BF16)
    w_sb = sb_ref.shape[1]
    w_q = w_sb // 3
    for c in range(w_sb // n_chunk):
        sl = slice(c * n_chunk, (c + 1) * n_chunk)
        y = _dot(xn, w_ref[:, sl])
        if (c + 1) * n_chunk <= w_q:
            y = y * scale
        sb_ref[:, sl] = y.astype(BF16)
    for c in range(dil_ref.shape[1] // n_chunk):
        sl = slice(c * n_chunk, (c + 1) * n_chunk)
        y = _dot(xn, w_ref[:, w_sb + c * n_chunk: w_sb + (c + 1) * n_chunk])
        if (c + 1) * n_chunk <= w_q:
            y = y * scale
        dil_ref[:, sl] = y


def _attn_proj(x, g, w, *, tm=512, n_chunk=512):
    m, d = x.shape
    n = w.shape[1]
    half = n // 2
    assert m % tm == 0 and (half // 3) % n_chunk == 0
    scale = 1.0 / math.sqrt(HEAD_DIM_ATTN)
    return pl.pallas_call(
        functools.partial(_attn_proj_kernel, n_chunk=n_chunk, scale=scale),
        grid=(m // tm,),
        in_specs=[pl.BlockSpec((tm, d), lambda i: (i, 0)), _resident((1, d)), _resident((d, n))],
        out_specs=[pl.BlockSpec((tm, half), lambda i: (i, 0)),
                   pl.BlockSpec((tm, half), lambda i: (i, 0))],
        out_shape=[jax.ShapeDtypeStruct((m, half), BF16), jax.ShapeDtypeStruct((m, half), F32)],
        compiler_params=_params(("parallel",)),
        name="attn_proj",
    )(x, g.reshape(1, d), w)


LOG2E = math.log2(math.e)
SB_DEAD_LOG2 = -160.0


def _sb_kernel(q_ref, k_ref, v_ref, tri_ref, o_ref, *, blk):
    i = pl.program_id(2)
    q2 = q_ref[...]
    lane = lax.broadcasted_iota(jnp.int32, (1, LANES), 1)
    row = lax.broadcasted_iota(jnp.int32, (blk, blk), 0)
    col = lax.broadcasted_iota(jnp.int32, (blk, blk), 1)
    causal = col < row
    tri = tri_ref[...]

    head0 = lane < HEAD_DIM_ATTN
    zeros = jnp.zeros_like(q2)
    qq = jnp.concatenate([jnp.where(head0, q2, zeros), jnp.where(head0, zeros, q2)], axis=0)
    causal2 = jnp.concatenate([causal, causal], axis=0)

    def pair(kb, state, diag):
        carry, acc = state
        has_second = kb >= 1
        starts = [pl.multiple_of(kb * blk, blk), pl.multiple_of(jnp.maximum(kb - 1, 0) * blk, blk)]
        masks = [causal2 if diag else None, has_second]
        zs = [_dot_nt(qq, k_ref[pl.ds(ks, blk), :]) * LOG2E for ks in starts]
        keeps, sums, laters = [], [], []
        for z, mask in zip(zs, masks):
            log_keep = -(jnp.maximum(z, 0.0) + jnp.log2(1.0 + jnp.exp2(-jnp.abs(z))))
            if mask is not None:
                log_keep = jnp.where(mask, log_keep, 0.0)
            keeps.append(log_keep)
            sums.append(jnp.sum(log_keep, axis=-1, keepdims=True))
        for log_keep in keeps:
            hi = log_keep.astype(BF16)
            lo = (log_keep - hi.astype(F32)).astype(BF16)
            laters.append(_dot(hi, tri) + _dot(lo, tri))
        for n, ks in enumerate(starts):
            p = jnp.exp2(zs[n] + keeps[n] + laters[n] + carry)
            if masks[n] is not None:
                p = jnp.where(masks[n], p, 0.0)
            acc = acc + _dot(p.astype(BF16), v_ref[pl.ds(ks, blk), :])
            carry = carry + sums[n]
        return carry, acc

    def alive(state):
        return (jnp.max(state[0]) > SB_DEAD_LOG2).astype(jnp.int32)

    zero = (jnp.zeros((2 * blk, 1), F32), jnp.zeros((2 * blk, LANES), F32))
    state = pair(i, zero, True)
    n_pairs = lax.shift_right_logical(i, 1)

    def cond(loop):
        t, live, _ = loop
        return (t < n_pairs) & (live > 0)

    def body(loop):
        t, _, state = loop
        state = pair(i - 2 - 2 * t, state, False)
        return t + 1, alive(state), state

    _, _, state = lax.while_loop(cond, body, (jnp.int32(0), alive(state), state))
    o_ref[...] = jnp.where(head0, state[1][:blk], state[1][blk:]).astype(o_ref.dtype)


def _sb_attention(qkv, *, blk=256):
    b, t, w3 = qkv.shape
    w = w3 // 3
    pairs = w // LANES
    blk = min(blk, t)
    assert t % blk == 0
    tri = jnp.asarray(np.tril(np.ones((blk, blk), np.float32), -1), BF16)
    return pl.pallas_call(
        functools.partial(_sb_kernel, blk=blk),
        grid=(b, pairs, t // blk),
        in_specs=[pl.BlockSpec((None, blk, LANES), lambda bi, p, i: (bi, i, p)),
                  pl.BlockSpec((None, t, LANES), lambda bi, p, i: (bi, 0, pairs + p)),
                  pl.BlockSpec((None, t, LANES), lambda bi, p, i: (bi, 0, 2 * pairs + p)),
                  _resident((blk, blk))],
        out_specs=pl.BlockSpec((None, blk, LANES), lambda bi, p, i: (bi, i, p)),
        out_shape=jax.ShapeDtypeStruct((b, t, w), BF16),
        compiler_params=_params(("parallel", "parallel", "arbitrary")),
        name="sb_attn",
    )(qkv, qkv, qkv, tri)


def _t5_bucket_np(dist):
    max_exact = NUM_BUCKETS // 2
    d = np.maximum(dist, 1).astype(np.float32)
    log_b = max_exact + (np.log(d / np.float32(max_exact)) / np.float32(math.log(MAX_DISTANCE / max_exact))
                         * np.float32(NUM_BUCKETS - max_exact)).astype(np.int32)
    log_b = np.minimum(log_b, NUM_BUCKETS - 1)
    return np.where(dist < max_exact, dist, log_b)


def _dil_tables():
    qi = np.arange(DIL_BLOCK)[:, None]
    ki = np.arange(2 * DIL_BLOCK)[None, :]
    dist = qi + DIL_BLOCK - ki
    buckets, valid = [], []
    for window, dil in DIL_CONFIGS:
        steps = window // dil
        buckets.append(_t5_bucket_np(np.maximum(dist, 0) * dil))
        valid.append((dist >= 0) & (dist <= steps))
    return np.stack(buckets).astype(np.int32), np.stack(valid).astype(np.int32)


def _dil_kernel(rb_ref, bucket_ref, valid_ref, q_ref, k_ref, v_ref, o_ref,
                bias_scr, num_scr, m_scr, l_scr, *, seq):
    p = pl.program_id(0)
    qb = DIL_BLOCK
    lane = lax.broadcasted_iota(jnp.int32, (1, LANES), 1)
    head0 = lane < HEAD_DIM_ATTN
    first_half = lax.broadcasted_iota(jnp.int32, (1, 2 * qb), 1) < qb

    @pl.when(pl.program_id(1) == 0)
    def _():
        for br in range(len(DIL_CONFIGS)):
            bucket = bucket_ref[br]
            valid = valid_ref[br] > 0
            for h in range(2):
                bias = jnp.zeros((qb, 2 * qb), F32)
                for b in range(NUM_BUCKETS):
                    bias = jnp.where(bucket == b, rb_ref[b, 2 * p + h], bias)
                bias_scr[br, h * qb:(h + 1) * qb, :] = jnp.where(valid, bias, MASKED)

    for br, (_, dil) in enumerate(DIL_CONFIGS):
        n_units = seq // qb

        n_blocks = seq // (qb * dil)
        run = min(DIL_UNROLL, n_blocks)
        runs = DIL_UNROLL // run
        assert DIL_UNROLL % run == 0 and n_blocks % run == 0 and (dil == 1 or n_blocks == run)

        def rows(ref, start, dil=dil):
            if dil == 1:
                return ref[pl.ds(start, qb), :].astype(BF16)
            return ref[pl.ds(start, qb, stride=dil), :].astype(BF16)

        def load_run(rho, dil=dil, run=run, whole=(n_blocks == run)):
            res, base = (rho, 0) if whole else (0, rho * run)
            starts = [(base + i) * (qb * dil) + res for i in range(run)]
            kb = [rows(k_ref, st) for st in starts]
            vb = [rows(v_ref, st) for st in starts]
            if whole:
                k_prev, v_prev = kb[0], vb[0]
                pen = jnp.where(first_half, MASKED, 0.0)
            else:
                st = jnp.maximum(base - 1, 0) * (qb * dil) + res
                k_prev, v_prev = rows(k_ref, st), rows(v_ref, st)
                pen = jnp.where(first_half, jnp.where(base == 0, MASKED, 0.0), 0.0)
            units = []
            for i, st in enumerate(starts):
                q2 = rows(q_ref, st)
                zeros = jnp.zeros_like(q2)
                qq = jnp.concatenate([jnp.where(head0, q2, zeros), jnp.where(head0, zeros, q2)], axis=0)
                k2 = jnp.concatenate([kb[i - 1] if i else k_prev, kb[i]], axis=0)
                v2 = jnp.concatenate([vb[i - 1] if i else v_prev, vb[i]], axis=0)
                units.append((st, pen if i == 0 else None, qq, k2, v2))
            return units

        def softmax_parts(s):
            m = jnp.max(s, axis=-1, keepdims=True)
            e = jnp.exp(s - m)
            return m, e, jnp.sum(e, axis=-1, keepdims=True)

        def group(g, _, br=br, dil=dil, runs=runs):
            units = [u for r in range(runs) for u in load_run(g * runs + r)]
            scores, parts = {}, {}
            for j in range(DIL_UNROLL + 2):
                if j < DIL_UNROLL:
                    _, pen, qq, k2, _ = units[j]
                    scores[j] = _dot_nt(qq, k2) + bias_scr[br]
                    if pen is not None:
                        scores[j] = scores[j] + pen
                if 0 <= j - 1 < DIL_UNROLL:
                    parts[j - 1] = softmax_parts(scores.pop(j - 1))
                if 0 <= j - 2 < DIL_UNROLL:
                    st, _, _, _, v2 = units[j - 2]
                    m, e, l = parts.pop(j - 2)
                    pv = _dot(e.astype(BF16), v2)
                    idx = pl.ds(st, qb) if dil == 1 else pl.ds(st, qb, stride=dil)
                    num_scr[br, idx, :] = jnp.where(head0, pv[:qb], pv[qb:])
                    m_scr[br, idx, :] = jnp.where(head0, m[:qb], m[qb:])
                    l_scr[br, idx, :] = jnp.where(head0, l[:qb], l[qb:])
            return 0

        lax.fori_loop(0, n_units // DIL_UNROLL, group, 0)

    rows_out = 256

    def finish(c, _):
        idx = pl.ds(pl.multiple_of(c * rows_out, rows_out), rows_out)
        m_all = [m_scr[br, idx, :] for br in range(len(DIL_CONFIGS))]
        m_max = functools.reduce(jnp.maximum, m_all)
        num = den = None
        for br, m_br in enumerate(m_all):
            wt = jnp.exp(m_br - m_max)
            n_br = wt * num_scr[br, idx, :]
            d_br = wt * l_scr[br, idx, :]
            num = n_br if num is None else num + n_br
            den = d_br if den is None else den + d_br
        o_ref[idx, :] = (num / den).astype(o_ref.dtype)
        return 0

    lax.fori_loop(0, seq // rows_out, finish, 0)


def _dil_attention(qkv, rel_bias):
    b, t, w3 = qkv.shape
    w = w3 // 3
    pairs = w // LANES
    assert t % (DIL_BLOCK * max(d for _, d in DIL_CONFIGS)) == 0 and t % 256 == 0
    bucket, valid = _dil_tables()
    nbr = len(DIL_CONFIGS)
    seq_spec = lambda off: pl.BlockSpec((None, t, LANES), lambda p, bi: (bi, 0, off + p))
    return pl.pallas_call(
        functools.partial(_dil_kernel, seq=t),
        grid=(pairs, b),
        in_specs=[pl.BlockSpec(memory_space=pltpu.SMEM),
                  _resident((nbr, DIL_BLOCK, 2 * DIL_BLOCK)),
                  _resident((nbr, DIL_BLOCK, 2 * DIL_BLOCK)),
                  seq_spec(0), seq_spec(pairs), seq_spec(2 * pairs)],
        out_specs=pl.BlockSpec((None, t, LANES), lambda p, bi: (bi, 0, p)),
        out_shape=jax.ShapeDtypeStruct((b, t, w), BF16),
        scratch_shapes=[pltpu.VMEM((nbr, 2 * DIL_BLOCK, 2 * DIL_BLOCK), F32),
                        pltpu.VMEM((nbr, t, LANES), F32), pltpu.VMEM((nbr, t, LANES), F32),
                        pltpu.VMEM((nbr, t, LANES), F32)],
        compiler_params=_params(("arbitrary", "arbitrary")),
        name="dil_attn",
    )(rel_bias, jnp.asarray(bucket), jnp.asarray(valid), qkv, qkv, qkv)


def _mlstm_proj_kernel(x_ref, g_ref, w_ref, wg_ref, cw_ref, bg_ref, q_ref, kt_ref, v_ref, o_ref,
                       gate_ref, *hist_scrs, tiles_per_seq, n_chunk, scale):
    i = pl.program_id(0)
    tm = x_ref.shape[0]
    width = q_ref.shape[1]
    xn = _rms(x_ref[...], g_ref[...]).astype(BF16)

    def even_rows(first, n):
        return pl.ds(2 * first, n, stride=2)

    slabs = n_chunk // LANES
    n_conv = 2 * width // n_chunk
    assert len(hist_scrs) == n_conv

    @pl.when(i % tiles_per_seq == 0)
    def _():
        for hist in hist_scrs:
            hist[:, 0:2 * CONV_PAD, :] = jnp.zeros((slabs, 2 * CONV_PAD, LANES), F32)

    def project(c):
        pre = _dot(xn, w_ref[:, c * n_chunk:(c + 1) * n_chunk])
        for s in range(slabs):
            hist_scrs[c][s, even_rows(CONV_PAD, tm), :] = pre[:, s * LANES:(s + 1) * LANES]

    def conv(c):
        sl = slice(c * n_chunk, (c + 1) * n_chunk)
        hist = hist_scrs[c]
        parts = []
        for s in range(slabs):
            y = None
            for tap in range(CONV_WIDTH):
                off = CONV_PAD - (CONV_WIDTH - 1) + tap
                w_tap = cw_ref[tap:tap + 1, (c * slabs + s) * LANES:(c * slabs + s + 1) * LANES]
                term = w_tap * hist[s, even_rows(off, tm), :]
                y = term if y is None else y + term
            parts.append(y)
            hist[s, even_rows(0, CONV_PAD), :] = hist[s, even_rows(tm, CONV_PAD), :]
        y = jnp.concatenate(parts, axis=1)
        y = y * jax.nn.sigmoid(y)
        if c * n_chunk < width:
            q_ref[:, sl] = (y * scale).astype(BF16)
        else:
            kt_ref[c * n_chunk - width:(c + 1) * n_chunk - width, :] = y.T.astype(BF16)

    for c in range(n_conv + 1):
        if c < n_conv:
            project(c)
        if c >= 1:
            conv(c - 1)
    for c in range(width // n_chunk):
        sl = slice(c * n_chunk, (c + 1) * n_chunk)
        v_ref[:, sl] = _dot(xn, w_ref[:, 2 * width + c * n_chunk:2 * width + (c + 1) * n_chunk]).astype(BF16)
        o_ref[:, sl] = _dot(xn, w_ref[:, 3 * width + c * n_chunk:3 * width + (c + 1) * n_chunk])
    gate_ref[...] = _dot(xn, wg_ref[...]) + bg_ref[...]


def _mlstm_proj(x, g, w_in, w_gates, conv_w, b_gates, *, seq, tm=512, n_chunk=512):
    m, d = x.shape
    width = conv_w.shape[1] // 2
    assert m % tm == 0 and seq % tm == 0 and width % n_chunk == 0 and w_in.shape[1] >= 4 * width
    row = lambda wd: pl.BlockSpec((tm, wd), lambda i: (i, 0))
    scale = 1.0 / math.sqrt(width // N_HEADS_MLSTM)
    return pl.pallas_call(
        functools.partial(_mlstm_proj_kernel, tiles_per_seq=seq // tm, n_chunk=n_chunk, scale=scale),
        grid=(m // tm,),
        in_specs=[row(d), _resident((1, d)), _resident(w_in.shape), _resident((d, 2 * LANES)),
                  _resident(conv_w.shape), _resident((1, 2 * LANES))],
        out_specs=[row(width), pl.BlockSpec((width, tm), lambda i: (0, i)), row(width), row(width),
                   row(2 * LANES)],
        out_shape=[jax.ShapeDtypeStruct((m, width), BF16), jax.ShapeDtypeStruct((width, m), BF16),
                   jax.ShapeDtypeStruct((m, width), BF16), jax.ShapeDtypeStruct((m, width), F32),
                   jax.ShapeDtypeStruct((m, 2 * LANES), F32)],
        scratch_shapes=[pltpu.VMEM((n_chunk // LANES, 2 * (tm + CONV_PAD), LANES), F32)
                        for _ in range(2 * width // n_chunk)],
        compiler_params=_params(("arbitrary",)),
        name="mlstm_proj",
    )(x, g.reshape(1, d), w_in, w_gates, conv_w, b_gates)


def _split3(x):
    hi = x.astype(BF16)
    r = x - hi.astype(F32)
    mid = r.astype(BF16)
    return hi, mid, (r - mid.astype(F32)).astype(BF16)


def _gate_prep_kernel(g_ref, cols_ref, rows_ref, *, heads, cl):
    tm = g_ref.shape[0]
    row = lax.broadcasted_iota(jnp.int32, (cl, cl), 0)
    col = lax.broadcasted_iota(jnp.int32, (cl, cl), 1)
    incl = jnp.where(row >= col, 1.0, 0.0).astype(BF16)
    lane = lax.broadcasted_iota(jnp.int32, (1, cl), 1)
    pad = jnp.zeros((cl - heads, cl), F32)
    for c in range(tm // cl):
        r = slice(c * cl, (c + 1) * cl)
        b = None
        for part in _split3(jax.nn.log_sigmoid(g_ref[r, LANES:])):
            term = _dot(incl, part)
            b = term if b is None else b + term
        u = g_ref[r, :LANES] - b
        u_rows = u.T[0:heads, :]
        b_rows = b.T[0:heads, :]
        cmax = u_rows
        shift = 1
        while shift < cl:
            cmax = jnp.where(lane >= shift, jnp.maximum(cmax, pltpu.roll(cmax, shift, axis=1)), cmax)
            shift *= 2
        u_max = jnp.broadcast_to(jnp.max(u_rows, axis=1, keepdims=True), (heads, cl))
        b_last = jnp.broadcast_to(jnp.min(b_rows, axis=1, keepdims=True), (heads, cl))
        cols_ref[r, :] = jnp.concatenate([u, b, jnp.concatenate([cmax, pad], axis=0).T], axis=1)
        rows_ref[:, r] = jnp.concatenate([u_rows, u_max, b_last], axis=0)


def _gate_prep(gates, *, heads, cl, tm=1024):
    m = gates.shape[0]
    tm = min(tm, m)
    assert m % tm == 0 and tm % cl == 0 and cl == LANES and heads == 8
    return pl.pallas_call(
        functools.partial(_gate_prep_kernel, heads=heads, cl=cl),
        grid=(m // tm,),
        in_specs=[pl.BlockSpec((tm, 2 * LANES), lambda i: (i, 0))],
        out_specs=[pl.BlockSpec((tm, 3 * LANES), lambda i: (i, 0)),
                   pl.BlockSpec((3 * heads, tm), lambda i: (0, i))],
        out_shape=[jax.ShapeDtypeStruct((m, 3 * LANES), F32),
                   jax.ShapeDtypeStruct((3 * heads, m), F32)],
        compiler_params=_params(("parallel",)),
        name="gate_prep",
    )(gates)


def _mlstm_kernel(q_ref, kt_ref, v_ref, o_ref, gcol_ref, grow_ref, hg_ref, out_ref,
                  s_scr, m_scr, ml_scr, *, heads):
    cl = q_ref.shape[0]
    dh = q_ref.shape[1] // heads

    @pl.when(pl.program_id(1) == 0)
    def _():
        s_scr[...] = jnp.zeros_like(s_scr)
        m_scr[...] = jnp.zeros_like(m_scr)
        ml_scr[...] = jnp.zeros_like(ml_scr)

    row = lax.broadcasted_iota(jnp.int32, (cl, cl), 0)
    col = lax.broadcasted_iota(jnp.int32, (cl, cl), 1)
    lower = row >= col
    ones = jnp.ones((cl, dh), BF16)

    u_cols, b_cols, cmax_cols = (gcol_ref[:, t * LANES:(t + 1) * LANES] for t in range(3))
    u_rows, u_max_rows, b_last_rows = (grow_ref[t * heads:(t + 1) * heads, :] for t in range(3))
    m_prev_rows = m_scr[...]
    m_prev_lane = ml_scr[0:1, :]
    mm_last_rows = jnp.maximum(m_prev_rows, u_max_rows)
    decay_rows = jnp.exp(m_prev_rows - mm_last_rows)
    m_scr[...] = b_last_rows + mm_last_rows
    ws_rows = jnp.exp(u_rows - mm_last_rows)
    mm_cols = jnp.maximum(cmax_cols, m_prev_lane)
    floor_cols = jnp.exp(-(b_cols + mm_cols))
    ml_scr[...] = jnp.broadcast_to(b_cols[cl - 1:cl, :] + mm_cols[cl - 1:cl, :], ml_scr.shape)

    def lanes_of(mat, c):
        return jnp.broadcast_to(mat[:, c:c + 1], (cl, LANES))

    def stage_a(h):
        sl = slice(h * dh, (h + 1) * dh)
        q, kt = q_ref[:, sl], kt_ref[sl, :]
        v1 = jnp.concatenate([v_ref[:, sl], ones], axis=1)
        s_prev = s_scr[h]
        qk = _dot(q, kt)
        qs = _dot(q, s_prev.astype(BF16))
        kw = (kt.astype(F32) * ws_rows[h:h + 1, :]).astype(BF16)
        decay = jnp.broadcast_to(decay_rows[h:h + 1, :], (dh, LANES))
        s_scr[h] = jnp.concatenate([decay, decay], axis=1) * s_prev + _dot(kw, v1)
        return h, sl, v1, qk, qs

    def stage_b(h, sl, v1, qk, qs):
        mm = lanes_of(mm_cols, h)
        weight = jnp.where(lower, jnp.exp(u_rows[h:h + 1, :] - mm), 0.0)
        w_inter = jnp.exp(m_prev_rows[h:h + 1, :] - mm)
        return sl, v1, qs, (qk * weight).astype(BF16), w_inter, lanes_of(floor_cols, h)

    def stage_c(sl, v1, qs, p, w_inter, floor):
        pv = _dot(p, v1)
        both = pv + jnp.concatenate([w_inter, w_inter], axis=1) * qs
        hid = both[:, :dh] / jnp.maximum(jnp.abs(both[:, dh:]), floor)
        hid = hid * lax.rsqrt(jnp.mean(hid * hid, axis=-1, keepdims=True) + EPS) * hg_ref[:, sl]
        out_ref[:, sl] = (hid * jax.nn.sigmoid(o_ref[:, sl])).astype(out_ref.dtype)

    after_a, after_b = {}, {}
    for j in range(heads + 2):
        if j < heads:
            after_a[j] = stage_a(j)
        if 0 <= j - 1 < heads:
            after_b[j - 1] = stage_b(*after_a.pop(j - 1))
        if 0 <= j - 2 < heads:
            stage_c(*after_b.pop(j - 2))


def _mlstm(q, kt, v, o, gates, head_g, *, heads=N_HEADS_MLSTM):
    b, t, w = q.shape
    cl = MLSTM_CHUNK
    dh = w // heads
    nc = t // cl
    assert t % cl == 0 and dh == LANES and cl == LANES
    gcols, grows = _gate_prep(gates, heads=heads, cl=cl)
    blk = lambda wd: pl.BlockSpec((None, cl, wd), lambda bi, c: (bi, c, 0))
    return pl.pallas_call(
        functools.partial(_mlstm_kernel, heads=heads),
        grid=(b, nc),
        in_specs=[blk(w), pl.BlockSpec((w, cl), lambda bi, c: (0, bi * nc + c)), blk(w), blk(w),
                  pl.BlockSpec((cl, 3 * LANES), lambda bi, c: (bi * nc + c, 0)),
                  pl.BlockSpec((3 * heads, cl), lambda bi, c: (0, bi * nc + c)),
                  _resident((1, w))],
        out_specs=blk(w),
        out_shape=jax.ShapeDtypeStruct((b, t, w), BF16),
        scratch_shapes=[pltpu.VMEM((heads, dh, 2 * dh), F32), pltpu.VMEM((heads, LANES), F32),
                        pltpu.VMEM((heads, LANES), F32)],
        compiler_params=_params(("parallel", "arbitrary")),
        name="mlstm",
    )(q, kt, v, o, gcols, grows, head_g.reshape(1, w))


def kernel(x, norm_g, ffn_w_gate, ffn_w_up, ffn_w_down, attn_w_in, attn_w_out, rel_bias,
           mlstm_w_in, mlstm_b_gates, mlstm_conv_w, mlstm_head_g, mlstm_w_out):
    bsz, t, d = x.shape
    depth = norm_g.shape[0]
    h = x.reshape(bsz * t, d)
    bf = lambda a: a.astype(BF16)

    w_gate, w_up, w_down = bf(ffn_w_gate), bf(ffn_w_up), bf(ffn_w_down)

    def ffn(h, layer, half, mix=None):
        g = norm_g[layer]
        return _ffn(h, g[2 * half * 2], g[2 * half * 2 + 1], w_gate, w_up, w_down, (layer, half), mix)

    for layer in range(depth):
        g = norm_g[layer]
        j = layer // 2
        h = ffn(h, layer, 0)
        if layer % 2 == 0:
            sb, dil = _attn_proj(h, g[2], bf(attn_w_in[j]))
            out_sb = _sb_attention(sb.reshape(bsz, t, -1))
            out_dil = _dil_attention(dil.reshape(bsz, t, -1), rel_bias.astype(F32))
            mix = ([out_sb.reshape(bsz * t, -1), out_dil.reshape(bsz * t, -1)], bf(attn_w_out[j]), g[3])
        else:
            width = mlstm_w_out.shape[1]
            n_head = mlstm_b_gates.shape[1] // 2
            tiles = lambda a: jnp.concatenate(
                [jnp.pad(part, ((0, 0), (0, LANES - n_head)))
                 for part in (a[:, :n_head], a[:, n_head:])], axis=1)
            w_in = bf(mlstm_w_in[j])
            w_gates = tiles(w_in[:, 4 * width:])
            b_gates = tiles(mlstm_b_gates[j].astype(F32).reshape(1, -1))
            q, kt, v, o, gates = _mlstm_proj(h, g[2], w_in, w_gates, mlstm_conv_w[j].astype(F32),
                                             b_gates, seq=t)
            r3 = lambda a: a.reshape(bsz, t, -1)
            hid = _mlstm(r3(q), kt, r3(v), r3(o), gates, mlstm_head_g[j].astype(F32))
            mix = ([hid.reshape(bsz * t, width)], bf(mlstm_w_out[j]), g[3])
        h = ffn(h, layer, 1, mix)
    return h.reshape(bsz, t, d)
```

```python
import functools
import math

import numpy as np
import jax
import jax.numpy as jnp
from jax import lax
from jax.experimental import pallas as pl
from jax.experimental.pallas import tpu as pltpu

EPS = 1e-6
HEAD_DIM_ATTN = 64
N_HEADS_SB = 8
N_HEADS_DIL = 8
DIL_CONFIGS = ((128, 1), (512, 4), (2048, 16))
DIL_BLOCK = 128
DIL_UNROLL = 8
NUM_BUCKETS = 32
MAX_DISTANCE = 2048
N_HEADS_MLSTM = 8
MLSTM_CHUNK = 128
CONV_WIDTH = 4
LANES = 128
CONV_PAD = 8
MASKED = -1e30
VMEM_LIMIT = 56 * 1024 * 1024

F32 = jnp.float32
BF16 = jnp.bfloat16


def _params(sem, vmem=VMEM_LIMIT):
    return pltpu.CompilerParams(dimension_semantics=sem, vmem_limit_bytes=vmem)


def _resident(shape):
    zeros = (0,) * len(shape)
    return pl.BlockSpec(shape, lambda *_: zeros, pipeline_mode=pl.Buffered(1))


def _rms(x, g):
    return x * lax.rsqrt(jnp.mean(x * x, axis=-1, keepdims=True) + EPS) * g


def _dot(a, b):
    return jnp.dot(a, b, preferred_element_type=F32)


def _dot_nt(a, b):
    return lax.dot_general(a, b, (((1,), (1,)), ((), ())), preferred_element_type=F32)


def _dot_tn(a, b):
    return lax.dot_general(a, b, (((0,), (0,)), ((), ())), preferred_element_type=F32)


def _ffn_kernel(*refs, ff_chunk, n_mix, groups):
    mix_refs, refs = refs[:n_mix], refs[n_mix:]
    if n_mix:
        (wmix_ref, gmix_ref), refs = refs[:2], refs[2:]
    x_ref, gin_ref, gout_ref, wg_ref, wu_ref, wd_ref, o_ref = refs
    d_ff = wg_ref.shape[1]
    rows = x_ref.shape[0] // groups
    spans = [slice(r * rows, (r + 1) * rows) for r in range(groups)]

    xs = []
    for rs in spans:
        x = x_ref[rs, :]
        if n_mix:
            mixed = jnp.concatenate([r[rs, :] for r in mix_refs], axis=-1)
            x = x + _rms(_dot(mixed, wmix_ref[...]), gmix_ref[...])
        xs.append(x)
    xns = [_rms(x, gin_ref[...]).astype(BF16) for x in xs]
    accs = []
    for xn in xns:
        acc = None
        for c in range(d_ff // ff_chunk):
            sl = slice(c * ff_chunk, (c + 1) * ff_chunk)
            gate = _dot(xn, wg_ref[:, sl])
            up = _dot(xn, wu_ref[:, sl])
            h = (gate * jax.nn.sigmoid(gate) * up).astype(BF16)
            part = _dot(h, wd_ref[sl, :])
            acc = part if acc is None else acc + part
        accs.append(acc)
    for rs, x, acc in zip(spans, xs, accs):
        o_ref[rs, :] = x + 0.5 * _rms(acc, gout_ref[...])


def _ffn(x, g_in, g_out, wg, wu, wd, which=(), mix=None, *, tm=1024, groups=2, ff_chunk=256):
    m, d = x.shape
    d_ff = wg.shape[-1]
    assert m % tm == 0 and tm % groups == 0 and d_ff % ff_chunk == 0 and len(which) == wg.ndim - 2
    row = lambda width: pl.BlockSpec((tm, width), lambda i: (i, 0))
    lead = (None,) * len(which)
    weight = lambda shape: pl.BlockSpec(lead + shape, lambda i: tuple(which) + (0, 0),
                                        pipeline_mode=pl.Buffered(1))
    parts, w_mix, g_mix = mix if mix else ((), None, None)
    mix_specs = [row(a.shape[1]) for a in parts] + ([_resident(w_mix.shape), _resident((1, d))] if mix else [])
    mix_args = list(parts) + ([w_mix, g_mix.reshape(1, d)] if mix else [])
    return pl.pallas_call(
        functools.partial(_ffn_kernel, ff_chunk=ff_chunk, n_mix=len(parts), groups=groups),
        grid=(m // tm,),
        in_specs=mix_specs + [row(d), _resident((1, d)), _resident((1, d)),
                              weight((d, d_ff)), weight((d, d_ff)), weight((d_ff, d))],
        out_specs=row(d),
        out_shape=jax.ShapeDtypeStruct((m, d), F32),
        compiler_params=_params(("parallel",)),
        name="ffn",
    )(*mix_args, x, g_in.reshape(1, d), g_out.reshape(1, d), wg, wu, wd)


def _attn_proj_kernel(x_ref, g_ref, w_ref, sb_ref, dil_ref, *, n_chunk, scale):
    xn = _rms(x_ref[...], g_ref[...]).astype(BF16)
    w_sb = sb_ref.shape[1]
    w_q = w_sb // 3
    for c in range(w_sb // n_chunk):
        sl = slice(c * n_chunk, (c + 1) * n_chunk)
        y = _dot(xn, w_ref[:, sl])
        if (c + 1) * n_chunk <= w_q:
            y = y * scale
        sb_ref[:, sl] = y.astype(BF16)
    for c in range(dil_ref.shape[1] // n_chunk):
        sl = slice(c * n_chunk, (c + 1) * n_chunk)
        y = _dot(xn, w_ref[:, w_sb + c * n_chunk: w_sb + (c + 1) * n_chunk])
        if (c + 1) * n_chunk <= w_q:
            y = y * scale
        dil_ref[:, sl] = y


def _attn_proj(x, g, w, *, tm=512, n_chunk=512):
    m, d = x.shape
    n = w.shape[1]
    half = n // 2
    assert m % tm == 0 and (half // 3) % n_chunk == 0
    scale = 1.0 / math.sqrt(HEAD_DIM_ATTN)
    return pl.pallas_call(
        functools.partial(_attn_proj_kernel, n_chunk=n_chunk, scale=scale),
        grid=(m // tm,),
        in_specs=[pl.BlockSpec((tm, d), lambda i: (i, 0)), _resident((1, d)), _resident((d, n))],
        out_specs=[pl.BlockSpec((tm, half), lambda i: (i, 0)),
                   pl.BlockSpec((tm, half), lambda i: (i, 0))],
        out_shape=[jax.ShapeDtypeStruct((m, half), BF16), jax.ShapeDtypeStruct((m, half), F32)],
        compiler_params=_params(("parallel",)),
        name="attn_proj",
    )(x, g.reshape(1, d), w)


LOG2E = math.log2(math.e)
SB_DEAD_LOG2 = -160.0


def _sb_kernel(q_ref, k_ref, v_ref, tri_ref, o_ref, *, blk, q_blocks):
    first_q = pl.program_id(2) * q_blocks
    lane = lax.broadcasted_iota(jnp.int32, (1, LANES), 1)
    row = lax.broadcasted_iota(jnp.int32, (blk, blk), 0)
    col = lax.broadcasted_iota(jnp.int32, (blk, blk), 1)
    causal = col < row
    causal2 = jnp.concatenate([causal, causal], axis=0)
    tri = tri_ref[...]
    head0 = lane < HEAD_DIM_ATTN

    def stacked(q2):
        zeros = jnp.zeros_like(q2)
        return jnp.concatenate([jnp.where(head0, q2, zeros), jnp.where(head0, zeros, q2)], axis=0)

    def pairs(jobs):
        chains = []
        for j, (qq, kb, diag, _) in enumerate(jobs):
            chains.append((j, pl.multiple_of(kb * blk, blk), causal2 if diag else None))
            chains.append((j, pl.multiple_of(jnp.maximum(kb - 1, 0) * blk, blk), kb >= 1))
        zs = [_dot_nt(jobs[j][0], k_ref[pl.ds(ks, blk), :]) * LOG2E for j, ks, _ in chains]
        keeps, sums, laters = [], [], []
        for z, (_, _, mask) in zip(zs, chains):
            log_keep = -(jnp.maximum(z, 0.0) + jnp.log2(1.0 + jnp.exp2(-jnp.abs(z))))
            if mask is not None:
                log_keep = jnp.where(mask, log_keep, 0.0)
            keeps.append(log_keep)
            sums.append(jnp.sum(log_keep, axis=-1, keepdims=True))
        for log_keep in keeps:
            hi = log_keep.astype(BF16)
            lo = (log_keep - hi.astype(F32)).astype(BF16)
            laters.append(_dot(hi, tri) + _dot(lo, tri))
        states = [job[3] for job in jobs]
        for n, (j, ks, mask) in enumerate(chains):
            carry, acc = states[j]
            p = jnp.exp2(zs[n] + keeps[n] + laters[n] + carry)
            if mask is not None:
                p = jnp.where(mask, p, 0.0)
            states[j] = (carry + sums[n], acc + _dot(p.astype(BF16), v_ref[pl.ds(ks, blk), :]))
        return states

    def alive(state):
        return (jnp.max(state[0]) > SB_DEAD_LOG2).astype(jnp.int32)

    zero = (jnp.zeros((2 * blk, 1), F32), jnp.zeros((2 * blk, LANES), F32))
    qqs = [stacked(q_ref[g * blk:(g + 1) * blk, :]) for g in range(q_blocks)]
    states = pairs([(qq, first_q + g, True, zero) for g, qq in enumerate(qqs)])

    for g, (qq, state) in enumerate(zip(qqs, states)):
        i = first_q + g
        n_pairs = lax.shift_right_logical(i, 1)

        def cond(loop, n_pairs=n_pairs):
            t, live, _ = loop
            return (t < n_pairs) & (live > 0)

        def body(loop, qq=qq, i=i):
            t, _, state = loop
            state, = pairs([(qq, i - 2 - 2 * t, False, state)])
            return t + 1, alive(state), state

        _, _, state = lax.while_loop(cond, body, (jnp.int32(0), alive(state), state))
        o_ref[g * blk:(g + 1) * blk, :] = jnp.where(head0, state[1][:blk], state[1][blk:]).astype(o_ref.dtype)


def _sb_attention(qkv, *, blk=256, q_blocks=2):
    b, t, w3 = qkv.shape
    w = w3 // 3
    pairs = w // LANES
    blk = min(blk, t)
    rows = blk * q_blocks
    assert t % rows == 0
    tri = jnp.asarray(np.tril(np.ones((blk, blk), np.float32), -1), BF16)
    return pl.pallas_call(
        functools.partial(_sb_kernel, blk=blk, q_blocks=q_blocks),
        grid=(b, pairs, t // rows),
        in_specs=[pl.BlockSpec((None, rows, LANES), lambda bi, p, i: (bi, i, p)),
                  pl.BlockSpec((None, t, LANES), lambda bi, p, i: (bi, 0, pairs + p)),
                  pl.BlockSpec((None, t, LANES), lambda bi, p, i: (bi, 0, 2 * pairs + p)),
                  _resident((blk, blk))],
        out_specs=pl.BlockSpec((None, rows, LANES), lambda bi, p, i: (bi, i, p)),
        out_shape=jax.ShapeDtypeStruct((b, t, w), BF16),
        compiler_params=_params(("parallel", "parallel", "arbitrary")),
        name="sb_attn",
    )(qkv, qkv, qkv, tri)


def _t5_bucket_np(dist):
    max_exact = NUM_BUCKETS // 2
    d = np.maximum(dist, 1).astype(np.float32)
    log_b = max_exact + (np.log(d / np.float32(max_exact)) / np.float32(math.log(MAX_DISTANCE / max_exact))
                         * np.float32(NUM_BUCKETS - max_exact)).astype(np.int32)
    log_b = np.minimum(log_b, NUM_BUCKETS - 1)
    return np.where(dist < max_exact, dist, log_b)


def _dil_tables():
    qi = np.arange(DIL_BLOCK)[:, None]
    ki = np.arange(2 * DIL_BLOCK)[None, :]
    dist = qi + DIL_BLOCK - ki
    buckets, valid = [], []
    for window, dil in DIL_CONFIGS:
        steps = window // dil
        buckets.append(_t5_bucket_np(np.maximum(dist, 0) * dil))
        valid.append((dist >= 0) & (dist <= steps))
    return np.stack(buckets).astype(np.int32), np.stack(valid).astype(np.int32)


def _dil_kernel(rb_ref, bucket_ref, valid_ref, q_ref, k_ref, v_ref, o_ref,
                bias_scr, num_scr, m_scr, l_scr, *, seq):
    p = pl.program_id(0)
    qb = DIL_BLOCK
    lane = lax.broadcasted_iota(jnp.int32, (1, LANES), 1)
    head0 = lane < HEAD_DIM_ATTN
    first_half = lax.broadcasted_iota(jnp.int32, (1, 2 * qb), 1) < qb

    @pl.when(pl.program_id(1) == 0)
    def _():
        for br in range(len(DIL_CONFIGS)):
            bucket = bucket_ref[br]
            valid = valid_ref[br] > 0
            for h in range(2):
                bias = jnp.zeros((qb, 2 * qb), F32)
                for b in range(NUM_BUCKETS):
                    bias = jnp.where(bucket == b, rb_ref[b, 2 * p + h], bias)
                bias_scr[br, h * qb:(h + 1) * qb, :] = jnp.where(valid, bias, MASKED)

    for br, (_, dil) in enumerate(DIL_CONFIGS):
        n_units = seq // qb

        n_blocks = seq // (qb * dil)
        run = min(DIL_UNROLL, n_blocks)
        runs = DIL_UNROLL // run
        assert DIL_UNROLL % run == 0 and n_blocks % run == 0 and (dil == 1 or n_blocks == run)

        def rows(ref, start, dil=dil):
            if dil == 1:
                return ref[pl.ds(start, qb), :].astype(BF16)
            return ref[pl.ds(start, qb, stride=dil), :].astype(BF16)

        def load_run(rho, dil=dil, run=run, whole=(n_blocks == run)):
            res, base = (rho, 0) if whole else (0, rho * run)
            starts = [(base + i) * (qb * dil) + res for i in range(run)]
            kb = [rows(k_ref, st) for st in starts]
            vb = [rows(v_ref, st) for st in starts]
            if whole:
                k_prev, v_prev = kb[0], vb[0]
                pen = jnp.where(first_half, MASKED, 0.0)
            else:
                st = jnp.maximum(base - 1, 0) * (qb * dil) + res
                k_prev, v_prev = rows(k_ref, st), rows(v_ref, st)
                pen = jnp.where(first_half, jnp.where(base == 0, MASKED, 0.0), 0.0)
            units = []
            for i, st in enumerate(starts):
                q2 = rows(q_ref, st)
                zeros = jnp.zeros_like(q2)
                qq = jnp.concatenate([jnp.where(head0, q2, zeros), jnp.where(head0, zeros, q2)], axis=0)
                k2 = jnp.concatenate([kb[i - 1] if i else k_prev, kb[i]], axis=0)
                v2 = jnp.concatenate([vb[i - 1] if i else v_prev, vb[i]], axis=0)
                units.append((st, pen if i == 0 else None, qq, k2, v2))
            return units

        def softmax_parts(s):
            m = jnp.max(s, axis=-1, keepdims=True)
            e = jnp.exp(s - m)
            return m, e, jnp.sum(e, axis=-1, keepdims=True)

        def group(g, _, br=br, dil=dil, runs=runs):
            units = [u for r in range(runs) for u in load_run(g * runs + r)]
            scores, parts = {}, {}
            for j in range(DIL_UNROLL + 2):
                if j < DIL_UNROLL:
                    _, pen, qq, k2, _ = units[j]
                    scores[j] = _dot_nt(qq, k2) + bias_scr[br]
                    if pen is not None:
                        scores[j] = scores[j] + pen
                if 0 <= j - 1 < DIL_UNROLL:
                    parts[j - 1] = softmax_parts(scores.pop(j - 1))
                if 0 <= j - 2 < DIL_UNROLL:
                    st, _, _, _, v2 = units[j - 2]
                    m, e, l = parts.pop(j - 2)
                    pv = _dot(e.astype(BF16), v2)
                    idx = pl.ds(st, qb) if dil == 1 else pl.ds(st, qb, stride=dil)
                    num_scr[br, idx, :] = jnp.where(head0, pv[:qb], pv[qb:])
                    m_scr[br, idx, :] = jnp.where(head0, m[:qb], m[qb:])
                    l_scr[br, idx, :] = jnp.where(head0, l[:qb], l[qb:])
            return 0

        lax.fori_loop(0, n_units // DIL_UNROLL, group, 0)

    rows_out = 256

    def finish(c, _):
        idx = pl.ds(pl.multiple_of(c * rows_out, rows_out), rows_out)
        m_all = [m_scr[br, idx, :] for br in range(len(DIL_CONFIGS))]
        m_max = functools.reduce(jnp.maximum, m_all)
        num = den = None
        for br, m_br in enumerate(m_all):
            wt = jnp.exp(m_br - m_max)
            n_br = wt * num_scr[br, idx, :]
            d_br = wt * l_scr[br, idx, :]
            num = n_br if num is None else num + n_br
            den = d_br if den is None else den + d_br
        o_ref[idx, :] = (num / den).astype(o_ref.dtype)
        return 0

    lax.fori_loop(0, seq // rows_out, finish, 0)


def _dil_attention(qkv, rel_bias):
    b, t, w3 = qkv.shape
    w = w3 // 3
    pairs = w // LANES
    assert t % (DIL_BLOCK * max(d for _, d in DIL_CONFIGS)) == 0 and t % 256 == 0
    bucket, valid = _dil_tables()
    nbr = len(DIL_CONFIGS)
    seq_spec = lambda off: pl.BlockSpec((None, t, LANES), lambda p, bi: (bi, 0, off + p))
    return pl.pallas_call(
        functools.partial(_dil_kernel, seq=t),
        grid=(pairs, b),
        in_specs=[pl.BlockSpec(memory_space=pltpu.SMEM),
                  _resident((nbr, DIL_BLOCK, 2 * DIL_BLOCK)),
                  _resident((nbr, DIL_BLOCK, 2 * DIL_BLOCK)),
                  seq_spec(0), seq_spec(pairs), seq_spec(2 * pairs)],
        out_specs=pl.BlockSpec((None, t, LANES), lambda p, bi: (bi, 0, p)),
        out_shape=jax.ShapeDtypeStruct((b, t, w), BF16),
        scratch_shapes=[pltpu.VMEM((nbr, 2 * DIL_BLOCK, 2 * DIL_BLOCK), F32),
                        pltpu.VMEM((nbr, t, LANES), F32), pltpu.VMEM((nbr, t, LANES), F32),
                        pltpu.VMEM((nbr, t, LANES), F32)],
        compiler_params=_params(("arbitrary", "arbitrary")),
        name="dil_attn",
    )(rel_bias, jnp.asarray(bucket), jnp.asarray(valid), qkv, qkv, qkv)


def _mlstm_proj_kernel(x_ref, g_ref, w_ref, wg_ref, cw_ref, bg_ref, q_ref, kt_ref, v_ref, o_ref,
                       gate_ref, *hist_scrs, tiles_per_seq, n_chunk, scale):
    i = pl.program_id(0)
    tm = x_ref.shape[0]
    width = q_ref.shape[1]
    xn = _rms(x_ref[...], g_ref[...]).astype(BF16)

    def even_rows(first, n):
        return pl.ds(2 * first, n, stride=2)

    slabs = n_chunk // LANES
    n_conv = 2 * width // n_chunk
    assert len(hist_scrs) == n_conv

    @pl.when(i % tiles_per_seq == 0)
    def _():
        for hist in hist_scrs:
            hist[:, 0:2 * CONV_PAD, :] = jnp.zeros((slabs, 2 * CONV_PAD, LANES), F32)

    def project(c):
        pre = _dot(xn, w_ref[:, c * n_chunk:(c + 1) * n_chunk])
        for s in range(slabs):
            hist_scrs[c][s, even_rows(CONV_PAD, tm), :] = pre[:, s * LANES:(s + 1) * LANES]

    def conv(c):
        sl = slice(c * n_chunk, (c + 1) * n_chunk)
        hist = hist_scrs[c]
        parts = []
        for s in range(slabs):
            y = None
            for tap in range(CONV_WIDTH):
                off = CONV_PAD - (CONV_WIDTH - 1) + tap
                w_tap = cw_ref[tap:tap + 1, (c * slabs + s) * LANES:(c * slabs + s + 1) * LANES]
                term = w_tap * hist[s, even_rows(off, tm), :]
                y = term if y is None else y + term
            parts.append(y)
            hist[s, even_rows(0, CONV_PAD), :] = hist[s, even_rows(tm, CONV_PAD), :]
        y = jnp.concatenate(parts, axis=1)
        y = y * jax.nn.sigmoid(y)
        if c * n_chunk < width:
            q_ref[:, sl] = (y * scale).astype(BF16)
        else:
            kt_ref[c * n_chunk - width:(c + 1) * n_chunk - width, :] = y.T.astype(BF16)

    for c in range(n_conv + 1):
        if c < n_conv:
            project(c)
        if c >= 1:
            conv(c - 1)
    for c in range(width // n_chunk):
        sl = slice(c * n_chunk, (c + 1) * n_chunk)
        v_ref[:, sl] = _dot(xn, w_ref[:, 2 * width + c * n_chunk:2 * width + (c + 1) * n_chunk]).astype(BF16)
        o_ref[:, sl] = _dot(xn, w_ref[:, 3 * width + c * n_chunk:3 * width + (c + 1) * n_chunk])
    gate_ref[...] = _dot(xn, wg_ref[...]) + bg_ref[...]


def _mlstm_proj(x, g, w_in, w_gates, conv_w, b_gates, *, seq, tm=512, n_chunk=512):
    m, d = x.shape
    width = conv_w.shape[1] // 2
    assert m % tm == 0 and seq % tm == 0 and width % n_chunk == 0 and w_in.shape[1] >= 4 * width
    row = lambda wd: pl.BlockSpec((tm, wd), lambda i: (i, 0))
    scale = 1.0 / math.sqrt(width // N_HEADS_MLSTM)
    return pl.pallas_call(
        functools.partial(_mlstm_proj_kernel, tiles_per_seq=seq // tm, n_chunk=n_chunk, scale=scale),
        grid=(m // tm,),
        in_specs=[row(d), _resident((1, d)), _resident(w_in.shape), _resident((d, 2 * LANES)),
                  _resident(conv_w.shape), _resident((1, 2 * LANES))],
        out_specs=[row(width), pl.BlockSpec((width, tm), lambda i: (0, i)), row(width), row(width),
                   row(2 * LANES)],
        out_shape=[jax.ShapeDtypeStruct((m, width), BF16), jax.ShapeDtypeStruct((width, m), BF16),
                   jax.ShapeDtypeStruct((m, width), BF16), jax.ShapeDtypeStruct((m, width), F32),
                   jax.ShapeDtypeStruct((m, 2 * LANES), F32)],
        scratch_shapes=[pltpu.VMEM((n_chunk // LANES, 2 * (tm + CONV_PAD), LANES), F32)
                        for _ in range(2 * width // n_chunk)],
        compiler_params=_params(("arbitrary",)),
        name="mlstm_proj",
    )(x, g.reshape(1, d), w_in, w_gates, conv_w, b_gates)


def _split3(x):
    hi = x.astype(BF16)
    r = x - hi.astype(F32)
    mid = r.astype(BF16)
    return hi, mid, (r - mid.astype(F32)).astype(BF16)


def _gate_prep_kernel(g_ref, cols_ref, rows_ref, *, heads, cl):
    tm = g_ref.shape[0]
    row = lax.broadcasted_iota(jnp.int32, (cl, cl), 0)
    col = lax.broadcasted_iota(jnp.int32, (cl, cl), 1)
    incl = jnp.where(row >= col, 1.0, 0.0).astype(BF16)
    lane = lax.broadcasted_iota(jnp.int32, (1, cl), 1)
    pad = jnp.zeros((cl - heads, cl), F32)
    for c in range(tm // cl):
        r = slice(c * cl, (c + 1) * cl)
        b = None
        for part in _split3(jax.nn.log_sigmoid(g_ref[r, LANES:])):
            term = _dot(incl, part)
            b = term if b is None else b + term
        u = g_ref[r, :LANES] - b
        u_rows = u.T[0:heads, :]
        b_rows = b.T[0:heads, :]
        cmax = u_rows
        shift = 1
        while shift < cl:
            cmax = jnp.where(lane >= shift, jnp.maximum(cmax, pltpu.roll(cmax, shift, axis=1)), cmax)
            shift *= 2
        u_max = jnp.broadcast_to(jnp.max(u_rows, axis=1, keepdims=True), (heads, cl))
        b_last = jnp.broadcast_to(jnp.min(b_rows, axis=1, keepdims=True), (heads, cl))
        cols_ref[r, :] = jnp.concatenate([u, b, jnp.concatenate([cmax, pad], axis=0).T], axis=1)
        rows_ref[:, r] = jnp.concatenate([u_rows, u_max, b_last], axis=0)


def _gate_prep(gates, *, heads, cl, tm=1024):
    m = gates.shape[0]
    tm = min(tm, m)
    assert m % tm == 0 and tm % cl == 0 and cl == LANES and heads == 8
    return pl.pallas_call(
        functools.partial(_gate_prep_kernel, heads=heads, cl=cl),
        grid=(m // tm,),
        in_specs=[pl.BlockSpec((tm, 2 * LANES), lambda i: (i, 0))],
        out_specs=[pl.BlockSpec((tm, 3 * LANES), lambda i: (i, 0)),
                   pl.BlockSpec((3 * heads, tm), lambda i: (0, i))],
        out_shape=[jax.ShapeDtypeStruct((m, 3 * LANES), F32),
                   jax.ShapeDtypeStruct((3 * heads, m), F32)],
        compiler_params=_params(("parallel",)),
        name="gate_prep",
    )(gates)


def _mlstm_kernel(q_ref, kt_ref, v_ref, o_ref, gcol_ref, grow_ref, hg_ref, out_ref,
                  s_scr, m_scr, ml_scr, *, heads):
    cl = q_ref.shape[0]
    dh = q_ref.shape[1] // heads

    @pl.when(pl.program_id(1) == 0)
    def _():
        s_scr[...] = jnp.zeros_like(s_scr)
        m_scr[...] = jnp.zeros_like(m_scr)
        ml_scr[...] = jnp.zeros_like(ml_scr)

    row = lax.broadcasted_iota(jnp.int32, (cl, cl), 0)
    col = lax.broadcasted_iota(jnp.int32, (cl, cl), 1)
    lower = row >= col
    ones = jnp.ones((cl, dh), BF16)

    u_cols, b_cols, cmax_cols = (gcol_ref[:, t * LANES:(t + 1) * LANES] for t in range(3))
    u_rows, u_max_rows, b_last_rows = (grow_ref[t * heads:(t + 1) * heads, :] for t in range(3))
    m_prev_rows = m_scr[...]
    m_prev_lane = ml_scr[0:1, :]
    mm_last_rows = jnp.maximum(m_prev_rows, u_max_rows)
    decay_rows = jnp.exp(m_prev_rows - mm_last_rows)
    m_scr[...] = b_last_rows + mm_last_rows
    ws_rows = jnp.exp(u_rows - mm_last_rows)
    mm_cols = jnp.maximum(cmax_cols, m_prev_lane)
    floor_cols = jnp.exp(-(b_cols + mm_cols))
    ml_scr[...] = jnp.broadcast_to(b_cols[cl - 1:cl, :] + mm_cols[cl - 1:cl, :], ml_scr.shape)

    def lanes_of(mat, c):
        return jnp.broadcast_to(mat[:, c:c + 1], (cl, LANES))

    def stage_a(h):
        sl = slice(h * dh, (h + 1) * dh)
        q, kt = q_ref[:, sl], kt_ref[sl, :]
        v1 = jnp.concatenate([v_ref[:, sl], ones], axis=1)
        s_prev = s_scr[h]
        qk = _dot(q, kt)
        qs = _dot(q, s_prev.astype(BF16))
        kw = (kt.astype(F32) * ws_rows[h:h + 1, :]).astype(BF16)
        decay = jnp.broadcast_to(decay_rows[h:h + 1, :], (dh, LANES))
        s_scr[h] = jnp.concatenate([decay, decay], axis=1) * s_prev + _dot(kw, v1)
        return h, sl, v1, qk, qs

    def stage_b(h, sl, v1, qk, qs):
        mm = lanes_of(mm_cols, h)
        weight = jnp.where(lower, jnp.exp(u_rows[h:h + 1, :] - mm), 0.0)
        w_inter = jnp.exp(m_prev_rows[h:h + 1, :] - mm)
        return sl, v1, qs, (qk * weight).astype(BF16), w_inter, lanes_of(floor_cols, h)

    def stage_c(sl, v1, qs, p, w_inter, floor):
        pv = _dot(p, v1)
        both = pv + jnp.concatenate([w_inter, w_inter], axis=1) * qs
        hid = both[:, :dh] / jnp.maximum(jnp.abs(both[:, dh:]), floor)
        hid = hid * lax.rsqrt(jnp.mean(hid * hid, axis=-1, keepdims=True) + EPS) * hg_ref[:, sl]
        out_ref[:, sl] = (hid * jax.nn.sigmoid(o_ref[:, sl])).astype(out_ref.dtype)

    after_a, after_b = {}, {}
    for j in range(heads + 2):
        if j < heads:
            after_a[j] = stage_a(j)
        if 0 <= j - 1 < heads:
            after_b[j - 1] = stage_b(*after_a.pop(j - 1))
        if 0 <= j - 2 < heads:
            stage_c(*after_b.pop(j - 2))


def _mlstm(q, kt, v, o, gates, head_g, *, heads=N_HEADS_MLSTM):
    b, t, w = q.shape
    cl = MLSTM_CHUNK
    dh = w // heads
    nc = t // cl
    assert t % cl == 0 and dh == LANES and cl == LANES
    gcols, grows = _gate_prep(gates, heads=heads, cl=cl)
    blk = lambda wd: pl.BlockSpec((None, cl, wd), lambda bi, c: (bi, c, 0))
    return pl.pallas_call(
        functools.partial(_mlstm_kernel, heads=heads),
        grid=(b, nc),
        in_specs=[blk(w), pl.BlockSpec((w, cl), lambda bi, c: (0, bi * nc + c)), blk(w), blk(w),
                  pl.BlockSpec((cl, 3 * LANES), lambda bi, c: (bi * nc + c, 0)),
                  pl.BlockSpec((3 * heads, cl), lambda bi, c: (0, bi * nc + c)),
                  _resident((1, w))],
        out_specs=blk(w),
        out_shape=jax.ShapeDtypeStruct((b, t, w), BF16),
        scratch_shapes=[pltpu.VMEM((heads, dh, 2 * dh), F32), pltpu.VMEM((heads, LANES), F32),
                        pltpu.VMEM((heads, LANES), F32)],
        compiler_params=_params(("parallel", "arbitrary")),
        name="mlstm",
    )(q, kt, v, o, gcols, grows, head_g.reshape(1, w))


def kernel(x, norm_g, ffn_w_gate, ffn_w_up, ffn_w_down, attn_w_in, attn_w_out, rel_bias,
           mlstm_w_in, mlstm_b_gates, mlstm_conv_w, mlstm_head_g, mlstm_w_out):
    bsz, t, d = x.shape
    depth = norm_g.shape[0]
    h = x.reshape(bsz * t, d)
    bf = lambda a: a.astype(BF16)

    w_gate, w_up, w_down = bf(ffn_w_gate), bf(ffn_w_up), bf(ffn_w_down)

    def ffn(h, layer, half, mix=None):
        g = norm_g[layer]
        return _ffn(h, g[2 * half * 2], g[2 * half * 2 + 1], w_gate, w_up, w_down, (layer, half), mix)

    for layer in range(depth):
        g = norm_g[layer]
        j = layer // 2
        h = ffn(h, layer, 0)
        if layer % 2 == 0:
            sb, dil = _attn_proj(h, g[2], bf(attn_w_in[j]))
            out_sb = _sb_attention(sb.reshape(bsz, t, -1))
            out_dil = _dil_attention(dil.reshape(bsz, t, -1), rel_bias.astype(F32))
            mix = ([out_sb.reshape(bsz * t, -1), out_dil.reshape(bsz * t, -1)], bf(attn_w_out[j]), g[3])
        else:
            width = mlstm_w_out.shape[1]
            n_head = mlstm_b_gates.shape[1] // 2
            tiles = lambda a: jnp.concatenate(
                [jnp.pad(part, ((0, 0), (0, LANES - n_head)))
                 for part in (a[:, :n_head], a[:, n_head:])], axis=1)
            w_in = bf(mlstm_w_in[j])
            w_gates = tiles(w_in[:, 4 * width:])
            b_gates = tiles(mlstm_b_gates[j].astype(F32).reshape(1, -1))
            q, kt, v, o, gates = _mlstm_proj(h, g[2], w_in, w_gates, mlstm_conv_w[j].astype(F32),
                                             b_gates, seq=t)
            r3 = lambda a: a.reshape(bsz, t, -1)
            hid = _mlstm(r3(q), kt, r3(v), r3(o), gates, mlstm_head_g[j].astype(F32))
            mix = ([hid.reshape(bsz * t, width)], bf(mlstm_w_out[j]), g[3])
        h = ffn(h, layer, 1, mix)
    return h.reshape(bsz, t, d)
```

```python
import functools
import math

import numpy as np
import jax
import jax.numpy as jnp
from jax import lax
from jax.experimental import pallas as pl
from jax.experimental.pallas import tpu as pltpu

EPS = 1e-6
HEAD_DIM_ATTN = 64
N_HEADS_SB = 8
N_HEADS_DIL = 8
DIL_CONFIGS = ((128, 1), (512, 4), (2048, 16))
DIL_BLOCK = 128
DIL_UNROLL = 8
NUM_BUCKETS = 32
MAX_DISTANCE = 2048
N_HEADS_MLSTM = 8
MLSTM_CHUNK = 128
CONV_WIDTH = 4
LANES = 128
CONV_PAD = 8
MASKED = -1e30
VMEM_LIMIT = 56 * 1024 * 1024

F32 = jnp.float32
BF16 = jnp.bfloat16


def _params(sem, vmem=VMEM_LIMIT):
    return pltpu.CompilerParams(dimension_semantics=sem, vmem_limit_bytes=vmem)


def _resident(shape):
    zeros = (0,) * len(shape)
    return pl.BlockSpec(shape, lambda *_: zeros, pipeline_mode=pl.Buffered(1))


def _rms(x, g):
    return x * lax.rsqrt(jnp.mean(x * x, axis=-1, keepdims=True) + EPS) * g


def _dot(a, b):
    return jnp.dot(a, b, preferred_element_type=F32)


def _dot_nt(a, b):
    return lax.dot_general(a, b, (((1,), (1,)), ((), ())), preferred_element_type=F32)


def _dot_tn(a, b):
    return lax.dot_general(a, b, (((0,), (0,)), ((), ())), preferred_element_type=F32)


def _ffn_kernel(*refs, ff_chunk, n_mix, groups):
    mix_refs, refs = refs[:n_mix], refs[n_mix:]
    if n_mix:
        (wmix_ref, gmix_ref), refs = refs[:2], refs[2:]
    x_ref, gin_ref, gout_ref, wg_ref, wu_ref, wd_ref, o_ref = refs
    d_ff = wg_ref.shape[1]
    rows = x_ref.shape[0] // groups
    spans = [slice(r * rows, (r + 1) * rows) for r in range(groups)]

    xs = []
    for rs in spans:
        x = x_ref[rs, :]
        if n_mix:
            mixed = jnp.concatenate([r[rs, :] for r in mix_refs], axis=-1)
            x = x + _rms(_dot(mixed, wmix_ref[...]), gmix_ref[...])
        xs.append(x)
    xns = [_rms(x, gin_ref[...]).astype(BF16) for x in xs]
    accs = []
    for xn in xns:
        acc = None
        for c in range(d_ff // ff_chunk):
            sl = slice(c * ff_chunk, (c + 1) * ff_chunk)
            gate = _dot(xn, wg_ref[:, sl])
            up = _dot(xn, wu_ref[:, sl])
            h = (gate * jax.nn.sigmoid(gate) * up).astype(BF16)
            part = _dot(h, wd_ref[sl, :])
            acc = part if acc is None else acc + part
        accs.append(acc)
    for rs, x, acc in zip(spans, xs, accs):
        o_ref[rs, :] = x + 0.5 * _rms(acc, gout_ref[...])


def _ffn(x, g_in, g_out, wg, wu, wd, which=(), mix=None, *, tm=1024, groups=2, ff_chunk=256):
    m, d = x.shape
    d_ff = wg.shape[-1]
    assert m % tm == 0 and tm % groups == 0 and d_ff % ff_chunk == 0 and len(which) == wg.ndim - 2
    row = lambda width: pl.BlockSpec((tm, width), lambda i: (i, 0))
    lead = (None,) * len(which)
    weight = lambda shape: pl.BlockSpec(lead + shape, lambda i: tuple(which) + (0, 0),
                                        pipeline_mode=pl.Buffered(1))
    parts, w_mix, g_mix = mix if mix else ((), None, None)
    mix_specs = [row(a.shape[1]) for a in parts] + ([_resident(w_mix.shape), _resident((1, d))] if mix else [])
    mix_args = list(parts) + ([w_mix, g_mix.reshape(1, d)] if mix else [])
    return pl.pallas_call(
        functools.partial(_ffn_kernel, ff_chunk=ff_chunk, n_mix=len(parts), groups=groups),
        grid=(m // tm,),
        in_specs=mix_specs + [row(d), _resident((1, d)), _resident((1, d)),
                              weight((d, d_ff)), weight((d, d_ff)), weight((d_ff, d))],
        out_specs=row(d),
        out_shape=jax.ShapeDtypeStruct((m, d), F32),
        compiler_params=_params(("parallel",)),
        name="ffn",
    )(*mix_args, x, g_in.reshape(1, d), g_out.reshape(1, d), wg, wu, wd)


def _attn_proj_kernel(x_ref, g_ref, w_ref, sb_ref, dil_ref, *, n_chunk, scale):
    xn = _rms(x_ref[...], g_ref[...]).astype(BF16)
    w_sb = sb_ref.shape[1]
    w_q = w_sb // 3
    for c in range(w_sb // n_chunk):
        sl = slice(c * n_chunk, (c + 1) * n_chunk)
        y = _dot(xn, w_ref[:, sl])
        if (c + 1) * n_chunk <= w_q:
            y = y * scale
        sb_ref[:, sl] = y.astype(BF16)
    for c in range(dil_ref.shape[1] // n_chunk):
        sl = slice(c * n_chunk, (c + 1) * n_chunk)
        y = _dot(xn, w_ref[:, w_sb + c * n_chunk: w_sb + (c + 1) * n_chunk])
        if (c + 1) * n_chunk <= w_q:
            y = y * scale
        dil_ref[:, sl] = y


def _attn_proj(x, g, w, *, tm=512, n_chunk=512):
    m, d = x.shape
    n = w.shape[1]
    half = n // 2
    assert m % tm == 0 and (half // 3) % n_chunk == 0
    scale = 1.0 / math.sqrt(HEAD_DIM_ATTN)
    return pl.pallas_call(
        functools.partial(_attn_proj_kernel, n_chunk=n_chunk, scale=scale),
        grid=(m // tm,),
        in_specs=[pl.BlockSpec((tm, d), lambda i: (i, 0)), _resident((1, d)), _resident((d, n))],
        out_specs=[pl.BlockSpec((tm, half), lambda i: (i, 0)),
                   pl.BlockSpec((tm, half), lambda i: (i, 0))],
        out_shape=[jax.ShapeDtypeStruct((m, half), BF16), jax.ShapeDtypeStruct((m, half), F32)],
        compiler_params=_params(("parallel",)),
        name="attn_proj",
    )(x, g.reshape(1, d), w)


LOG2E = math.log2(math.e)
SB_DEAD_LOG2 = -160.0


def _sb_kernel(q_ref, k_ref, v_ref, tri_ref, o_ref, *, blk, q_blocks):
    first_q = pl.program_id(2) * q_blocks
    lane = lax.broadcasted_iota(jnp.int32, (1, LANES), 1)
    row = lax.broadcasted_iota(jnp.int32, (blk, blk), 0)
    col = lax.broadcasted_iota(jnp.int32, (blk, blk), 1)
    causal = col < row
    causal2 = jnp.concatenate([causal, causal], axis=0)
    tri = tri_ref[...]
    head0 = lane < HEAD_DIM_ATTN

    def stacked(q2):
        zeros = jnp.zeros_like(q2)
        return jnp.concatenate([jnp.where(head0, q2, zeros), jnp.where(head0, zeros, q2)], axis=0)

    def pairs(jobs):
        chains = []
        for j, (qq, kb, diag, _) in enumerate(jobs):
            chains.append((j, pl.multiple_of(kb * blk, blk), causal2 if diag else None))
            chains.append((j, pl.multiple_of(jnp.maximum(kb - 1, 0) * blk, blk), kb >= 1))
        zs = [_dot_nt(jobs[j][0], k_ref[pl.ds(ks, blk), :]) * LOG2E for j, ks, _ in chains]
        splits, sums, bases, laters = [], [], [], []
        for z, (_, _, mask) in zip(zs, chains):
            log_keep = -(jnp.maximum(z, 0.0) + jnp.log2(1.0 + jnp.exp2(-jnp.abs(z))))
            log_beta = z + log_keep
            if mask is not None:
                log_keep = jnp.where(mask, log_keep, 0.0)
                log_beta = jnp.where(mask, log_beta, MASKED)
            hi = log_keep.astype(BF16)
            splits.append((hi, (log_keep - hi.astype(F32)).astype(BF16)))
            sums.append(jnp.sum(log_keep, axis=-1, keepdims=True))
            bases.append(log_beta)
        for hi, lo in splits:
            laters.append(_dot(hi, tri) + _dot(lo, tri))
        states = [job[3] for job in jobs]
        for n, (j, ks, _) in enumerate(chains):
            carry, acc = states[j]
            p = jnp.exp2(bases[n] + laters[n] + carry)
            states[j] = (carry + sums[n], acc + _dot(p.astype(BF16), v_ref[pl.ds(ks, blk), :]))
        return states

    def alive(state):
        return (jnp.max(state[0]) > SB_DEAD_LOG2).astype(jnp.int32)

    zero = (jnp.zeros((2 * blk, 1), F32), jnp.zeros((2 * blk, LANES), F32))
    qqs = [stacked(q_ref[g * blk:(g + 1) * blk, :]) for g in range(q_blocks)]
    states = pairs([(qq, first_q + g, True, zero) for g, qq in enumerate(qqs)])

    for g, (qq, state) in enumerate(zip(qqs, states)):
        i = first_q + g
        n_pairs = lax.shift_right_logical(i, 1)

        def cond(loop, n_pairs=n_pairs):
            t, live, _ = loop
            return (t < n_pairs) & (live > 0)

        def body(loop, qq=qq, i=i):
            t, _, state = loop
            state, = pairs([(qq, i - 2 - 2 * t, False, state)])
            return t + 1, alive(state), state

        _, _, state = lax.while_loop(cond, body, (jnp.int32(0), alive(state), state))
        o_ref[g * blk:(g + 1) * blk, :] = jnp.where(head0, state[1][:blk], state[1][blk:]).astype(o_ref.dtype)


def _sb_attention(qkv, *, blk=256, q_blocks=4):
    b, t, w3 = qkv.shape
    w = w3 // 3
    pairs = w // LANES
    blk = min(blk, t)
    rows = blk * q_blocks
    assert t % rows == 0
    tri = jnp.asarray(np.tril(np.ones((blk, blk), np.float32), -1), BF16)
    return pl.pallas_call(
        functools.partial(_sb_kernel, blk=blk, q_blocks=q_blocks),
        grid=(b, pairs, t // rows),
        in_specs=[pl.BlockSpec((None, rows, LANES), lambda bi, p, i: (bi, i, p)),
                  pl.BlockSpec((None, t, LANES), lambda bi, p, i: (bi, 0, pairs + p)),
                  pl.BlockSpec((None, t, LANES), lambda bi, p, i: (bi, 0, 2 * pairs + p)),
                  _resident((blk, blk))],
        out_specs=pl.BlockSpec((None, rows, LANES), lambda bi, p, i: (bi, i, p)),
        out_shape=jax.ShapeDtypeStruct((b, t, w), BF16),
        compiler_params=_params(("parallel", "parallel", "arbitrary")),
        name="sb_attn",
    )(qkv, qkv, qkv, tri)


def _t5_bucket_np(dist):
    max_exact = NUM_BUCKETS // 2
    d = np.maximum(dist, 1).astype(np.float32)
    log_b = max_exact + (np.log(d / np.float32(max_exact)) / np.float32(math.log(MAX_DISTANCE / max_exact))
                         * np.float32(NUM_BUCKETS - max_exact)).astype(np.int32)
    log_b = np.minimum(log_b, NUM_BUCKETS - 1)
    return np.where(dist < max_exact, dist, log_b)


def _dil_tables():
    qi = np.arange(DIL_BLOCK)[:, None]
    ki = np.arange(2 * DIL_BLOCK)[None, :]
    dist = qi + DIL_BLOCK - ki
    buckets, valid = [], []
    for window, dil in DIL_CONFIGS:
        steps = window // dil
        buckets.append(_t5_bucket_np(np.maximum(dist, 0) * dil))
        valid.append((dist >= 0) & (dist <= steps))
    return np.stack(buckets).astype(np.int32), np.stack(valid).astype(np.int32)


def _dil_kernel(rb_ref, bucket_ref, valid_ref, q_ref, k_ref, v_ref, o_ref,
                bias_scr, num_scr, m_scr, l_scr, *, seq):
    p = pl.program_id(0)
    qb = DIL_BLOCK
    lane = lax.broadcasted_iota(jnp.int32, (1, LANES), 1)
    head0 = lane < HEAD_DIM_ATTN
    first_half = lax.broadcasted_iota(jnp.int32, (1, 2 * qb), 1) < qb

    @pl.when(pl.program_id(1) == 0)
    def _():
        for br in range(len(DIL_CONFIGS)):
            bucket = bucket_ref[br]
            valid = valid_ref[br] > 0
            for h in range(2):
                bias = jnp.zeros((qb, 2 * qb), F32)
                for b in range(NUM_BUCKETS):
                    bias = jnp.where(bucket == b, rb_ref[b, 2 * p + h], bias)
                bias_scr[br, h * qb:(h + 1) * qb, :] = jnp.where(valid, bias, MASKED)

    for br, (_, dil) in enumerate(DIL_CONFIGS):
        n_units = seq // qb

        n_blocks = seq // (qb * dil)
        run = min(DIL_UNROLL, n_blocks)
        runs = DIL_UNROLL // run
        assert DIL_UNROLL % run == 0 and n_blocks % run == 0 and (dil == 1 or n_blocks == run)

        def rows(ref, start, dil=dil):
            if dil == 1:
                return ref[pl.ds(start, qb), :].astype(BF16)
            return ref[pl.ds(start, qb, stride=dil), :].astype(BF16)

        def load_run(rho, dil=dil, run=run, whole=(n_blocks == run)):
            res, base = (rho, 0) if whole else (0, rho * run)
            starts = [(base + i) * (qb * dil) + res for i in range(run)]
            kb = [rows(k_ref, st) for st in starts]
            vb = [rows(v_ref, st) for st in starts]
            if whole:
                k_prev, v_prev = kb[0], vb[0]
                pen = jnp.where(first_half, MASKED, 0.0)
            else:
                st = jnp.maximum(base - 1, 0) * (qb * dil) + res
                k_prev, v_prev = rows(k_ref, st), rows(v_ref, st)
                pen = jnp.where(first_half, jnp.where(base == 0, MASKED, 0.0), 0.0)
            units = []
            for i, st in enumerate(starts):
                q2 = rows(q_ref, st)
                zeros = jnp.zeros_like(q2)
                qq = jnp.concatenate([jnp.where(head0, q2, zeros), jnp.where(head0, zeros, q2)], axis=0)
                k2 = jnp.concatenate([kb[i - 1] if i else k_prev, kb[i]], axis=0)
                v2 = jnp.concatenate([vb[i - 1] if i else v_prev, vb[i]], axis=0)
                units.append((st, pen if i == 0 else None, qq, k2, v2))
            return units

        def softmax_parts(s):
            m = jnp.max(s, axis=-1, keepdims=True)
            e = jnp.exp(s - m)
            return m, e, jnp.sum(e, axis=-1, keepdims=True)

        def group(g, _, br=br, dil=dil, runs=runs):
            units = [u for r in range(runs) for u in load_run(g * runs + r)]
            scores, parts = {}, {}
            for j in range(DIL_UNROLL + 2):
                if j < DIL_UNROLL:
                    _, pen, qq, k2, _ = units[j]
                    scores[j] = _dot_nt(qq, k2) + bias_scr[br]
                    if pen is not None:
                        scores[j] = scores[j] + pen
                if 0 <= j - 1 < DIL_UNROLL:
                    parts[j - 1] = softmax_parts(scores.pop(j - 1))
                if 0 <= j - 2 < DIL_UNROLL:
                    st, _, _, _, v2 = units[j - 2]
                    m, e, l = parts.pop(j - 2)
                    pv = _dot(e.astype(BF16), v2)
                    idx = pl.ds(st, qb) if dil == 1 else pl.ds(st, qb, stride=dil)
                    num_scr[br, idx, :] = jnp.where(head0, pv[:qb], pv[qb:])
                    m_scr[br, idx, :] = jnp.where(head0, m[:qb], m[qb:])
                    l_scr[br, idx, :] = jnp.where(head0, l[:qb], l[qb:])
            return 0

        lax.fori_loop(0, n_units // DIL_UNROLL, group, 0)

    rows_out = 256

    def finish(c, _):
        idx = pl.ds(pl.multiple_of(c * rows_out, rows_out), rows_out)
        m_all = [m_scr[br, idx, :] for br in range(len(DIL_CONFIGS))]
        m_max = functools.reduce(jnp.maximum, m_all)
        num = den = None
        for br, m_br in enumerate(m_all):
            wt = jnp.exp(m_br - m_max)
            n_br = wt * num_scr[br, idx, :]
            d_br = wt * l_scr[br, idx, :]
            num = n_br if num is None else num + n_br
            den = d_br if den is None else den + d_br
        o_ref[idx, :] = (num / den).astype(o_ref.dtype)
        return 0

    lax.fori_loop(0, seq // rows_out, finish, 0)


def _dil_attention(qkv, rel_bias):
    b, t, w3 = qkv.shape
    w = w3 // 3
    pairs = w // LANES
    assert t % (DIL_BLOCK * max(d for _, d in DIL_CONFIGS)) == 0 and t % 256 == 0
    bucket, valid = _dil_tables()
    nbr = len(DIL_CONFIGS)
    seq_spec = lambda off: pl.BlockSpec((None, t, LANES), lambda p, bi: (bi, 0, off + p))
    return pl.pallas_call(
        functools.partial(_dil_kernel, seq=t),
        grid=(pairs, b),
        in_specs=[pl.BlockSpec(memory_space=pltpu.SMEM),
                  _resident((nbr, DIL_BLOCK, 2 * DIL_BLOCK)),
                  _resident((nbr, DIL_BLOCK, 2 * DIL_BLOCK)),
                  seq_spec(0), seq_spec(pairs), seq_spec(2 * pairs)],
        out_specs=pl.BlockSpec((None, t, LANES), lambda p, bi: (bi, 0, p)),
        out_shape=jax.ShapeDtypeStruct((b, t, w), BF16),
        scratch_shapes=[pltpu.VMEM((nbr, 2 * DIL_BLOCK, 2 * DIL_BLOCK), F32),
                        pltpu.VMEM((nbr, t, LANES), F32), pltpu.VMEM((nbr, t, LANES), F32),
                        pltpu.VMEM((nbr, t, LANES), F32)],
        compiler_params=_params(("arbitrary", "arbitrary")),
        name="dil_attn",
    )(rel_bias, jnp.asarray(bucket), jnp.asarray(valid), qkv, qkv, qkv)


def _mlstm_proj_kernel(x_ref, g_ref, w_ref, wg_ref, cw_ref, bg_ref, q_ref, kt_ref, v_ref, o_ref,
                       gate_ref, *hist_scrs, tiles_per_seq, n_chunk, scale):
    i = pl.program_id(0)
    tm = x_ref.shape[0]
    width = q_ref.shape[1]
    xn = _rms(x_ref[...], g_ref[...]).astype(BF16)

    def even_rows(first, n):
        return pl.ds(2 * first, n, stride=2)

    slabs = n_chunk // LANES
    n_conv = 2 * width // n_chunk
    assert len(hist_scrs) == n_conv

    @pl.when(i % tiles_per_seq == 0)
    def _():
        for hist in hist_scrs:
            hist[:, 0:2 * CONV_PAD, :] = jnp.zeros((slabs, 2 * CONV_PAD, LANES), F32)

    def project(c):
        pre = _dot(xn, w_ref[:, c * n_chunk:(c + 1) * n_chunk])
        for s in range(slabs):
            hist_scrs[c][s, even_rows(CONV_PAD, tm), :] = pre[:, s * LANES:(s + 1) * LANES]

    def conv(c):
        sl = slice(c * n_chunk, (c + 1) * n_chunk)
        hist = hist_scrs[c]
        parts = []
        for s in range(slabs):
            y = None
            for tap in range(CONV_WIDTH):
                off = CONV_PAD - (CONV_WIDTH - 1) + tap
                w_tap = cw_ref[tap:tap + 1, (c * slabs + s) * LANES:(c * slabs + s + 1) * LANES]
                term = w_tap * hist[s, even_rows(off, tm), :]
                y = term if y is None else y + term
            parts.append(y)
            hist[s, even_rows(0, CONV_PAD), :] = hist[s, even_rows(tm, CONV_PAD), :]
        y = jnp.concatenate(parts, axis=1)
        y = y * jax.nn.sigmoid(y)
        if c * n_chunk < width:
            q_ref[:, sl] = (y * scale).astype(BF16)
        else:
            kt_ref[c * n_chunk - width:(c + 1) * n_chunk - width, :] = y.T.astype(BF16)

    for c in range(n_conv + 1):
        if c < n_conv:
            project(c)
        if c >= 1:
            conv(c - 1)
    for c in range(width // n_chunk):
        sl = slice(c * n_chunk, (c + 1) * n_chunk)
        v_ref[:, sl] = _dot(xn, w_ref[:, 2 * width + c * n_chunk:2 * width + (c + 1) * n_chunk]).astype(BF16)
        o_ref[:, sl] = _dot(xn, w_ref[:, 3 * width + c * n_chunk:3 * width + (c + 1) * n_chunk])
    gate_ref[...] = _dot(xn, wg_ref[...]) + bg_ref[...]


def _mlstm_proj(x, g, w_in, w_gates, conv_w, b_gates, *, seq, tm=512, n_chunk=512):
    m, d = x.shape
    width = conv_w.shape[1] // 2
    assert m % tm == 0 and seq % tm == 0 and width % n_chunk == 0 and w_in.shape[1] >= 4 * width
    row = lambda wd: pl.BlockSpec((tm, wd), lambda i: (i, 0))
    scale = 1.0 / math.sqrt(width // N_HEADS_MLSTM)
    return pl.pallas_call(
        functools.partial(_mlstm_proj_kernel, tiles_per_seq=seq // tm, n_chunk=n_chunk, scale=scale),
        grid=(m // tm,),
        in_specs=[row(d), _resident((1, d)), _resident(w_in.shape), _resident((d, 2 * LANES)),
                  _resident(conv_w.shape), _resident((1, 2 * LANES))],
        out_specs=[row(width), pl.BlockSpec((width, tm), lambda i: (0, i)), row(width), row(width),
                   row(2 * LANES)],
        out_shape=[jax.ShapeDtypeStruct((m, width), BF16), jax.ShapeDtypeStruct((width, m), BF16),
                   jax.ShapeDtypeStruct((m, width), BF16), jax.ShapeDtypeStruct((m, width), F32),
                   jax.ShapeDtypeStruct((m, 2 * LANES), F32)],
        scratch_shapes=[pltpu.VMEM((n_chunk // LANES, 2 * (tm + CONV_PAD), LANES), F32)
                        for _ in range(2 * width // n_chunk)],
        compiler_params=_params(("arbitrary",)),
        name="mlstm_proj",
    )(x, g.reshape(1, d), w_in, w_gates, conv_w, b_gates)


def _split3(x):
    hi = x.astype(BF16)
    r = x - hi.astype(F32)
    mid = r.astype(BF16)
    return hi, mid, (r - mid.astype(F32)).astype(BF16)


def _gate_prep_kernel(g_ref, cols_ref, rows_ref, *, heads, cl):
    tm = g_ref.shape[0]
    row = lax.broadcasted_iota(jnp.int32, (cl, cl), 0)
    col = lax.broadcasted_iota(jnp.int32, (cl, cl), 1)
    incl = jnp.where(row >= col, 1.0, 0.0).astype(BF16)
    lane = lax.broadcasted_iota(jnp.int32, (1, cl), 1)
    pad = jnp.zeros((cl - heads, cl), F32)
    for c in range(tm // cl):
        r = slice(c * cl, (c + 1) * cl)
        b = None
        for part in _split3(jax.nn.log_sigmoid(g_ref[r, LANES:])):
            term = _dot(incl, part)
            b = term if b is None else b + term
        u = g_ref[r, :LANES] - b
        u_rows = u.T[0:heads, :]
        b_rows = b.T[0:heads, :]
        cmax = u_rows
        shift = 1
        while shift < cl:
            cmax = jnp.where(lane >= shift, jnp.maximum(cmax, pltpu.roll(cmax, shift, axis=1)), cmax)
            shift *= 2
        u_max = jnp.broadcast_to(jnp.max(u_rows, axis=1, keepdims=True), (heads, cl))
        b_last = jnp.broadcast_to(jnp.min(b_rows, axis=1, keepdims=True), (heads, cl))
        cols_ref[r, :] = jnp.concatenate([u, b, jnp.concatenate([cmax, pad], axis=0).T], axis=1)
        rows_ref[:, r] = jnp.concatenate([u_rows, u_max, b_last], axis=0)


def _gate_prep(gates, *, heads, cl, tm=1024):
    m = gates.shape[0]
    tm = min(tm, m)
    assert m % tm == 0 and tm % cl == 0 and cl == LANES and heads == 8
    return pl.pallas_call(
        functools.partial(_gate_prep_kernel, heads=heads, cl=cl),
        grid=(m // tm,),
        in_specs=[pl.BlockSpec((tm, 2 * LANES), lambda i: (i, 0))],
        out_specs=[pl.BlockSpec((tm, 3 * LANES), lambda i: (i, 0)),
                   pl.BlockSpec((3 * heads, tm), lambda i: (0, i))],
        out_shape=[jax.ShapeDtypeStruct((m, 3 * LANES), F32),
                   jax.ShapeDtypeStruct((3 * heads, m), F32)],
        compiler_params=_params(("parallel",)),
        name="gate_prep",
    )(gates)


def _mlstm_kernel(q_ref, kt_ref, v_ref, o_ref, gcol_ref, grow_ref, hg_ref, out_ref,
                  s_scr, m_scr, ml_scr, *, heads):
    cl = MLSTM_CHUNK
    n_chunks = q_ref.shape[0] // cl
    dh = q_ref.shape[1] // heads

    @pl.when(pl.program_id(1) == 0)
    def _():
        s_scr[...] = jnp.zeros_like(s_scr)
        m_scr[...] = jnp.zeros_like(m_scr)
        ml_scr[...] = jnp.zeros_like(ml_scr)

    row = lax.broadcasted_iota(jnp.int32, (cl, cl), 0)
    col = lax.broadcasted_iota(jnp.int32, (cl, cl), 1)
    lower = row >= col
    ones = jnp.ones((cl, dh), BF16)

    m_rows = m_scr[...]
    m_lane = ml_scr[0:1, :]
    tables = []
    for c in range(n_chunks):
        span = slice(c * cl, (c + 1) * cl)
        u_cols, b_cols, cmax_cols = (gcol_ref[span, t * LANES:(t + 1) * LANES] for t in range(3))
        u_rows, u_max_rows, b_last_rows = (grow_ref[t * heads:(t + 1) * heads, span] for t in range(3))
        mm_last_rows = jnp.maximum(m_rows, u_max_rows)
        mm_cols = jnp.maximum(cmax_cols, m_lane)
        tables.append(dict(
            span=span, u_rows=u_rows, m_prev_rows=m_rows, mm_cols=mm_cols,
            decay_rows=jnp.exp(m_rows - mm_last_rows),
            ws_rows=jnp.exp(u_rows - mm_last_rows),
            floor_cols=jnp.exp(-(b_cols + mm_cols))))
        m_rows = b_last_rows + mm_last_rows
        m_lane = b_cols[cl - 1:cl, :] + mm_cols[cl - 1:cl, :]
    m_scr[...] = m_rows
    ml_scr[...] = jnp.broadcast_to(m_lane, ml_scr.shape)

    def lanes_of(mat, c):
        return jnp.broadcast_to(mat[:, c:c + 1], (cl, LANES))

    def stage_a(tab, h):
        sl = slice(h * dh, (h + 1) * dh)
        q, kt = q_ref[tab["span"], sl], kt_ref[sl, tab["span"]]
        v1 = jnp.concatenate([v_ref[tab["span"], sl], ones], axis=1)
        s_prev = s_scr[h]
        qk = _dot(q, kt)
        qs = _dot(q, s_prev.astype(BF16))
        kw = (kt.astype(F32) * tab["ws_rows"][h:h + 1, :]).astype(BF16)
        decay = jnp.broadcast_to(tab["decay_rows"][h:h + 1, :], (dh, LANES))
        s_scr[h] = jnp.concatenate([decay, decay], axis=1) * s_prev + _dot(kw, v1)
        return tab, h, sl, v1, qk, qs

    def stage_b(tab, h, sl, v1, qk, qs):
        mm = lanes_of(tab["mm_cols"], h)
        weight = jnp.where(lower, jnp.exp(tab["u_rows"][h:h + 1, :] - mm), 0.0)
        w_inter = jnp.exp(tab["m_prev_rows"][h:h + 1, :] - mm)
        return tab, sl, v1, qs, (qk * weight).astype(BF16), w_inter, lanes_of(tab["floor_cols"], h)

    def stage_c(tab, sl, v1, qs, p, w_inter, floor):
        pv = _dot(p, v1)
        both = pv + jnp.concatenate([w_inter, w_inter], axis=1) * qs
        hid = both[:, :dh] / jnp.maximum(jnp.abs(both[:, dh:]), floor)
        hid = hid * lax.rsqrt(jnp.mean(hid * hid, axis=-1, keepdims=True) + EPS) * hg_ref[:, sl]
        out_ref[tab["span"], sl] = (hid * jax.nn.sigmoid(o_ref[tab["span"], sl])).astype(out_ref.dtype)

    jobs = [(tab, h) for tab in tables for h in range(heads)]
    after_a, after_b = {}, {}
    for j in range(len(jobs) + 2):
        if j < len(jobs):
            after_a[j] = stage_a(*jobs[j])
        if 0 <= j - 1 < len(jobs):
            after_b[j - 1] = stage_b(*after_a.pop(j - 1))
        if 0 <= j - 2 < len(jobs):
            stage_c(*after_b.pop(j - 2))


def _mlstm(q, kt, v, o, gates, head_g, *, heads=N_HEADS_MLSTM, chunks=4):
    b, t, w = q.shape
    cl = MLSTM_CHUNK
    dh = w // heads
    rows = chunks * cl
    nc = t // rows
    assert t % rows == 0 and dh == LANES and cl == LANES
    gcols, grows = _gate_prep(gates, heads=heads, cl=cl)
    blk = lambda wd: pl.BlockSpec((None, rows, wd), lambda bi, c: (bi, c, 0))
    return pl.pallas_call(
        functools.partial(_mlstm_kernel, heads=heads),
        grid=(b, nc),
        in_specs=[blk(w), pl.BlockSpec((w, rows), lambda bi, c: (0, bi * nc + c)), blk(w), blk(w),
                  pl.BlockSpec((rows, 3 * LANES), lambda bi, c: (bi * nc + c, 0)),
                  pl.BlockSpec((3 * heads, rows), lambda bi, c: (0, bi * nc + c)),
                  _resident((1, w))],
        out_specs=blk(w),
        out_shape=jax.ShapeDtypeStruct((b, t, w), BF16),
        scratch_shapes=[pltpu.VMEM((heads, dh, 2 * dh), F32), pltpu.VMEM((heads, LANES), F32),
                        pltpu.VMEM((heads, LANES), F32)],
        compiler_params=_params(("parallel", "arbitrary")),
        name="mlstm",
    )(q, kt, v, o, gcols, grows, head_g.reshape(1, w))


def kernel(x, norm_g, ffn_w_gate, ffn_w_up, ffn_w_down, attn_w_in, attn_w_out, rel_bias,
           mlstm_w_in, mlstm_b_gates, mlstm_conv_w, mlstm_head_g, mlstm_w_out):
    bsz, t, d = x.shape
    depth = norm_g.shape[0]
    h = x.reshape(bsz * t, d)
    bf = lambda a: a.astype(BF16)

    w_gate, w_up, w_down = bf(ffn_w_gate), bf(ffn_w_up), bf(ffn_w_down)

    def ffn(h, layer, half, mix=None):
        g = norm_g[layer]
        return _ffn(h, g[2 * half * 2], g[2 * half * 2 + 1], w_gate, w_up, w_down, (layer, half), mix)

    for layer in range(depth):
        g = norm_g[layer]
        j = layer // 2
        h = ffn(h, layer, 0)
        if layer % 2 == 0:
            sb, dil = _attn_proj(h, g[2], bf(attn_w_in[j]))
            out_sb = _sb_attention(sb.reshape(bsz, t, -1))
            out_dil = _dil_attention(dil.reshape(bsz, t, -1), rel_bias.astype(F32))
            mix = ([out_sb.reshape(bsz * t, -1), out_dil.reshape(bsz * t, -1)], bf(attn_w_out[j]), g[3])
        else:
            width = mlstm_w_out.shape[1]
            n_head = mlstm_b_gates.shape[1] // 2
            tiles = lambda a: jnp.concatenate(
                [jnp.pad(part, ((0, 0), (0, LANES - n_head)))
                 for part in (a[:, :n_head], a[:, n_head:])], axis=1)
            w_in = bf(mlstm_w_in[j])
            w_gates = tiles(w_in[:, 4 * width:])
            b_gates = tiles(mlstm_b_gates[j].astype(F32).reshape(1, -1))
            q, kt, v, o, gates = _mlstm_proj(h, g[2], w_in, w_gates, mlstm_conv_w[j].astype(F32),
                                             b_gates, seq=t)
            r3 = lambda a: a.reshape(bsz, t, -1)
            hid = _mlstm(r3(q), kt, r3(v), r3(o), gates, mlstm_head_g[j].astype(F32))
            mix = ([hid.reshape(bsz * t, width)], bf(mlstm_w_out[j]), g[3])
        h = ffn(h, layer, 1, mix)
    return h.reshape(bsz, t, d)
```

```python
import functools
import math

import numpy as np
import jax
import jax.numpy as jnp
from jax import lax
from jax.experimental import pallas as pl
from jax.experimental.pallas import tpu as pltpu

EPS = 1e-6
HEAD_DIM_ATTN = 64
DIL_CONFIGS = ((128, 1), (512, 4), (2048, 16))
DIL_BLOCK = 128
DIL_UNROLL = 16
NUM_BUCKETS = 32
MAX_DISTANCE = 2048
N_HEADS_MLSTM = 8
MLSTM_CHUNK = 128
CONV_WIDTH = 4
LANES = 128
CONV_PAD = 8
MASKED = -1e30
V7X_VMEM_BYTES = 64 * 1024 * 1024
VMEM_LIMIT = V7X_VMEM_BYTES * 7 // 8

F32 = jnp.float32
BF16 = jnp.bfloat16


def _params(sem, vmem=VMEM_LIMIT):
    return pltpu.CompilerParams(dimension_semantics=sem, vmem_limit_bytes=vmem)


def _resident(shape):
    zeros = (0,) * len(shape)
    return pl.BlockSpec(shape, lambda *_: zeros, pipeline_mode=pl.Buffered(1))


def _rms(x, g):
    return x * lax.rsqrt(jnp.mean(x * x, axis=-1, keepdims=True) + EPS) * g


def _dot(a, b):
    return jnp.dot(a, b, preferred_element_type=F32)


def _dot_nt(a, b):
    return lax.dot_general(a, b, (((1,), (1,)), ((), ())), preferred_element_type=F32)


def _ffn_kernel(*refs, ff_chunk, n_mix, groups):
    mix_refs, refs = refs[:n_mix], refs[n_mix:]
    if n_mix:
        (wmix_ref, gmix_ref), refs = refs[:2], refs[2:]
    x_ref, gin_ref, gout_ref, wg_ref, wu_ref, wd_ref, o_ref = refs
    d_ff = wg_ref.shape[1]
    rows = x_ref.shape[0] // groups
    spans = [slice(r * rows, (r + 1) * rows) for r in range(groups)]

    xs = []
    for rs in spans:
        x = x_ref[rs, :]
        if n_mix:
            mixed = jnp.concatenate([r[rs, :] for r in mix_refs], axis=-1)
            x = x + _rms(_dot(mixed, wmix_ref[...]), gmix_ref[...])
        xs.append(x)
    xns = [_rms(x, gin_ref[...]).astype(BF16) for x in xs]
    accs = []
    for xn in xns:
        acc = None
        for c in range(d_ff // ff_chunk):
            sl = slice(c * ff_chunk, (c + 1) * ff_chunk)
            gate = _dot(xn, wg_ref[:, sl])
            up = _dot(xn, wu_ref[:, sl])
            h = (gate * jax.nn.sigmoid(gate) * up).astype(BF16)
            part = _dot(h, wd_ref[sl, :])
            acc = part if acc is None else acc + part
        accs.append(acc)
    for rs, x, acc in zip(spans, xs, accs):
        o_ref[rs, :] = x + 0.5 * _rms(acc, gout_ref[...])


def _ffn(x, g_in, g_out, wg, wu, wd, which=(), mix=None, *, tm=1024, groups=2, ff_chunk=256):
    m, d = x.shape
    d_ff = wg.shape[-1]
    assert m % tm == 0 and tm % groups == 0 and d_ff % ff_chunk == 0 and len(which) == wg.ndim - 2
    row = lambda width: pl.BlockSpec((tm, width), lambda i: (i, 0))
    lead = (None,) * len(which)
    weight = lambda shape: pl.BlockSpec(lead + shape, lambda i: tuple(which) + (0, 0),
                                        pipeline_mode=pl.Buffered(1))
    parts, w_mix, g_mix = mix if mix else ((), None, None)
    mix_specs = [row(a.shape[1]) for a in parts] + ([_resident(w_mix.shape), _resident((1, d))] if mix else [])
    mix_args = list(parts) + ([w_mix, g_mix.reshape(1, d)] if mix else [])
    return pl.pallas_call(
        functools.partial(_ffn_kernel, ff_chunk=ff_chunk, n_mix=len(parts), groups=groups),
        grid=(m // tm,),
        in_specs=mix_specs + [row(d), _resident((1, d)), _resident((1, d)),
                              weight((d, d_ff)), weight((d, d_ff)), weight((d_ff, d))],
        out_specs=row(d),
        out_shape=jax.ShapeDtypeStruct((m, d), F32),
        compiler_params=_params(("parallel",)),
        name="ffn",
    )(*mix_args, x, g_in.reshape(1, d), g_out.reshape(1, d), wg, wu, wd)


def _attn_proj_kernel(x_ref, g_ref, w_ref, sb_ref, dil_ref, *, n_chunk, scale):
    xn = _rms(x_ref[...], g_ref[...]).astype(BF16)
    w_sb = sb_ref.shape[1]
    w_q = w_sb // 3
    for c in range(w_sb // n_chunk):
        sl = slice(c * n_chunk, (c + 1) * n_chunk)
        y = _dot(xn, w_ref[:, sl])
        if (c + 1) * n_chunk <= w_q:
            y = y * scale
        sb_ref[:, sl] = y.astype(BF16)
    for c in range(dil_ref.shape[1] // n_chunk):
        sl = slice(c * n_chunk, (c + 1) * n_chunk)
        y = _dot(xn, w_ref[:, w_sb + c * n_chunk: w_sb + (c + 1) * n_chunk])
        if (c + 1) * n_chunk <= w_q:
            y = y * scale
        dil_ref[:, sl] = y


def _attn_proj(x, g, w, *, tm=512, n_chunk=512):
    m, d = x.shape
    n = w.shape[1]
    half = n // 2
    assert m % tm == 0 and (half // 3) % n_chunk == 0
    scale = 1.0 / math.sqrt(HEAD_DIM_ATTN)
    return pl.pallas_call(
        functools.partial(_attn_proj_kernel, n_chunk=n_chunk, scale=scale),
        grid=(m // tm,),
        in_specs=[pl.BlockSpec((tm, d), lambda i: (i, 0)), _resident((1, d)), _resident((d, n))],
        out_specs=[pl.BlockSpec((tm, half), lambda i: (i, 0)),
                   pl.BlockSpec((tm, half), lambda i: (i, 0))],
        out_shape=[jax.ShapeDtypeStruct((m, half), BF16), jax.ShapeDtypeStruct((m, half), F32)],
        compiler_params=_params(("parallel",)),
        name="attn_proj",
    )(x, g.reshape(1, d), w)


LOG2E = math.log2(math.e)
SB_DEAD_LOG2 = -160.0


def _sb_kernel(q_ref, k_ref, v_ref, tri_ref, o_ref, *, blk, q_blocks):
    first_q = pl.program_id(2) * q_blocks
    lane = lax.broadcasted_iota(jnp.int32, (1, LANES), 1)
    row = lax.broadcasted_iota(jnp.int32, (blk, blk), 0)
    col = lax.broadcasted_iota(jnp.int32, (blk, blk), 1)
    causal = col < row
    causal2 = jnp.concatenate([causal, causal], axis=0)
    tri = tri_ref[...]
    head0 = lane < HEAD_DIM_ATTN

    def stacked(q2):
        zeros = jnp.zeros_like(q2)
        return jnp.concatenate([jnp.where(head0, q2, zeros), jnp.where(head0, zeros, q2)], axis=0)

    def pairs(jobs):
        chains = []
        for j, (qq, kb, diag, _) in enumerate(jobs):
            chains.append((j, pl.multiple_of(kb * blk, blk), causal2 if diag else None))
            chains.append((j, pl.multiple_of(jnp.maximum(kb - 1, 0) * blk, blk), kb >= 1))
        zs = [_dot_nt(jobs[j][0], k_ref[pl.ds(ks, blk), :]) * LOG2E for j, ks, _ in chains]
        splits, sums, bases, laters = [], [], [], []
        for z, (_, _, mask) in zip(zs, chains):
            log_keep = -(jnp.maximum(z, 0.0) + jnp.log2(1.0 + jnp.exp2(-jnp.abs(z))))
            log_beta = z + log_keep
            if mask is not None:
                log_keep = jnp.where(mask, log_keep, 0.0)
                log_beta = jnp.where(mask, log_beta, MASKED)
            hi = log_keep.astype(BF16)
            splits.append((hi, (log_keep - hi.astype(F32)).astype(BF16)))
            sums.append(jnp.sum(log_keep, axis=-1, keepdims=True))
            bases.append(log_beta)
        for hi, lo in splits:
            laters.append(_dot(hi, tri) + _dot(lo, tri))
        states = [job[3] for job in jobs]
        for n, (j, ks, _) in enumerate(chains):
            carry, acc = states[j]
            p = jnp.exp2(bases[n] + laters[n] + carry)
            states[j] = (carry + sums[n], acc + _dot(p.astype(BF16), v_ref[pl.ds(ks, blk), :]))
        return states

    def alive(state):
        return (jnp.max(state[0]) > SB_DEAD_LOG2).astype(jnp.int32)

    zero = (jnp.zeros((2 * blk, 1), F32), jnp.zeros((2 * blk, LANES), F32))
    qqs = [stacked(q_ref[g * blk:(g + 1) * blk, :]) for g in range(q_blocks)]
    states = pairs([(qq, first_q + g, True, zero) for g, qq in enumerate(qqs)])

    for g, (qq, state) in enumerate(zip(qqs, states)):
        i = first_q + g
        n_pairs = lax.shift_right_logical(i, 1)

        def cond(loop, n_pairs=n_pairs):
            t, live, _ = loop
            return (t < n_pairs) & (live > 0)

        def body(loop, qq=qq, i=i):
            t, _, state = loop
            state, = pairs([(qq, i - 2 - 2 * t, False, state)])
            return t + 1, alive(state), state

        _, _, state = lax.while_loop(cond, body, (jnp.int32(0), alive(state), state))
        o_ref[g * blk:(g + 1) * blk, :] = jnp.where(head0, state[1][:blk], state[1][blk:]).astype(o_ref.dtype)


def _sb_attention(qkv, *, blk=256, q_blocks=4):
    b, t, w3 = qkv.shape
    w = w3 // 3
    pairs = w // LANES
    blk = min(blk, t)
    rows = blk * q_blocks
    assert t % rows == 0
    tri = jnp.asarray(np.tril(np.ones((blk, blk), np.float32), -1), BF16)
    return pl.pallas_call(
        functools.partial(_sb_kernel, blk=blk, q_blocks=q_blocks),
        grid=(b, pairs, t // rows),
        in_specs=[pl.BlockSpec((None, rows, LANES), lambda bi, p, i: (bi, i, p)),
                  pl.BlockSpec((None, t, LANES), lambda bi, p, i: (bi, 0, pairs + p)),
                  pl.BlockSpec((None, t, LANES), lambda bi, p, i: (bi, 0, 2 * pairs + p)),
                  _resident((blk, blk))],
        out_specs=pl.BlockSpec((None, rows, LANES), lambda bi, p, i: (bi, i, p)),
        out_shape=jax.ShapeDtypeStruct((b, t, w), BF16),
        compiler_params=_params(("parallel", "parallel", "arbitrary")),
        name="sb_attn",
    )(qkv, qkv, qkv, tri)


def _t5_bucket_np(dist):
    max_exact = NUM_BUCKETS // 2
    d = np.maximum(dist, 1).astype(np.float32)
    log_b = max_exact + (np.log(d / np.float32(max_exact)) / np.float32(math.log(MAX_DISTANCE / max_exact))
                         * np.float32(NUM_BUCKETS - max_exact)).astype(np.int32)
    log_b = np.minimum(log_b, NUM_BUCKETS - 1)
    return np.where(dist < max_exact, dist, log_b)


def _dil_tables():
    qi = np.arange(DIL_BLOCK)[:, None]
    ki = np.arange(2 * DIL_BLOCK)[None, :]
    dist = qi + DIL_BLOCK - ki
    buckets, valid = [], []
    for window, dil in DIL_CONFIGS:
        steps = window // dil
        buckets.append(_t5_bucket_np(np.maximum(dist, 0) * dil))
        valid.append((dist >= 0) & (dist <= steps))
    return np.stack(buckets).astype(np.int32), np.stack(valid).astype(np.int32)


def _dil_kernel(rb_ref, bucket_ref, valid_ref, q_ref, k_ref, v_ref, o_ref,
                bias_scr, num_scr, m_scr, l_scr, *, seq):
    p = pl.program_id(0)
    qb = DIL_BLOCK
    lane = lax.broadcasted_iota(jnp.int32, (1, LANES), 1)
    head0 = lane < HEAD_DIM_ATTN
    first_half = lax.broadcasted_iota(jnp.int32, (1, 2 * qb), 1) < qb

    @pl.when(pl.program_id(1) == 0)
    def _():
        for br in range(len(DIL_CONFIGS)):
            bucket = bucket_ref[br]
            valid = valid_ref[br] > 0
            for h in range(2):
                bias = jnp.zeros((qb, 2 * qb), F32)
                for b in range(NUM_BUCKETS):
                    bias = jnp.where(bucket == b, rb_ref[b, 2 * p + h], bias)
                bias_scr[br, h * qb:(h + 1) * qb, :] = jnp.where(valid, bias, MASKED)

    for br, (_, dil) in enumerate(DIL_CONFIGS):
        n_units = seq // qb

        n_blocks = seq // (qb * dil)
        run = min(DIL_UNROLL, n_blocks)
        runs = DIL_UNROLL // run
        assert DIL_UNROLL % run == 0 and n_blocks % run == 0 and (dil == 1 or n_blocks == run)

        def rows(ref, start, dil=dil):
            if dil == 1:
                return ref[pl.ds(start, qb), :].astype(BF16)
            return ref[pl.ds(start, qb, stride=dil), :].astype(BF16)

        def load_run(rho, dil=dil, run=run, whole=(n_blocks == run)):
            res, base = (rho, 0) if whole else (0, rho * run)
            starts = [(base + i) * (qb * dil) + res for i in range(run)]
            kb = [rows(k_ref, st) for st in starts]
            vb = [rows(v_ref, st) for st in starts]
            if whole:
                k_prev, v_prev = kb[0], vb[0]
                pen = jnp.where(first_half, MASKED, 0.0)
            else:
                st = jnp.maximum(base - 1, 0) * (qb * dil) + res
                k_prev, v_prev = rows(k_ref, st), rows(v_ref, st)
                pen = jnp.where(first_half, jnp.where(base == 0, MASKED, 0.0), 0.0)
            units = []
            for i, st in enumerate(starts):
                q2 = rows(q_ref, st)
                zeros = jnp.zeros_like(q2)
                qq = jnp.concatenate([jnp.where(head0, q2, zeros), jnp.where(head0, zeros, q2)], axis=0)
                k2 = jnp.concatenate([kb[i - 1] if i else k_prev, kb[i]], axis=0)
                v2 = jnp.concatenate([vb[i - 1] if i else v_prev, vb[i]], axis=0)
                units.append((st, pen if i == 0 else None, qq, k2, v2))
            return units

        def softmax_parts(s):
            m = jnp.max(s, axis=-1, keepdims=True)
            e = jnp.exp(s - m)
            return m, e, jnp.sum(e, axis=-1, keepdims=True)

        def group(g, _, br=br, dil=dil, runs=runs):
            units = [u for r in range(runs) for u in load_run(g * runs + r)]
            scores, parts = {}, {}
            for j in range(DIL_UNROLL + 2):
                if j < DIL_UNROLL:
                    _, pen, qq, k2, _ = units[j]
                    scores[j] = _dot_nt(qq, k2) + bias_scr[br]
                    if pen is not None:
                        scores[j] = scores[j] + pen
                if 0 <= j - 1 < DIL_UNROLL:
                    parts[j - 1] = softmax_parts(scores.pop(j - 1))
                if 0 <= j - 2 < DIL_UNROLL:
                    st, _, _, _, v2 = units[j - 2]
                    m, e, l = parts.pop(j - 2)
                    pv = _dot(e.astype(BF16), v2)
                    idx = pl.ds(st, qb) if dil == 1 else pl.ds(st, qb, stride=dil)
                    num_scr[br, idx, :] = jnp.where(head0, pv[:qb], pv[qb:])
                    m_scr[br, idx, :] = jnp.where(head0, m[:qb], m[qb:])
                    l_scr[br, idx, :] = jnp.where(head0, l[:qb], l[qb:])
            return 0

        lax.fori_loop(0, n_units // DIL_UNROLL, group, 0)

    rows_out = 256

    def finish(c, _):
        idx = pl.ds(pl.multiple_of(c * rows_out, rows_out), rows_out)
        m_all = [m_scr[br, idx, :] for br in range(len(DIL_CONFIGS))]
        m_max = functools.reduce(jnp.maximum, m_all)
        num = den = None
        for br, m_br in enumerate(m_all):
            wt = jnp.exp(m_br - m_max)
            n_br = wt * num_scr[br, idx, :]
            d_br = wt * l_scr[br, idx, :]
            num = n_br if num is None else num + n_br
            den = d_br if den is None else den + d_br
        o_ref[idx, :] = (num / den).astype(o_ref.dtype)
        return 0

    lax.fori_loop(0, seq // rows_out, finish, 0)


def _dil_attention(qkv, rel_bias):
    b, t, w3 = qkv.shape
    w = w3 // 3
    pairs = w // LANES
    assert t % (DIL_BLOCK * max(d for _, d in DIL_CONFIGS)) == 0 and t % 256 == 0
    bucket, valid = _dil_tables()
    nbr = len(DIL_CONFIGS)
    seq_spec = lambda off: pl.BlockSpec((None, t, LANES), lambda p, bi: (bi, 0, off + p))
    return pl.pallas_call(
        functools.partial(_dil_kernel, seq=t),
        grid=(pairs, b),
        in_specs=[pl.BlockSpec(memory_space=pltpu.SMEM),
                  _resident((nbr, DIL_BLOCK, 2 * DIL_BLOCK)),
                  _resident((nbr, DIL_BLOCK, 2 * DIL_BLOCK)),
                  seq_spec(0), seq_spec(pairs), seq_spec(2 * pairs)],
        out_specs=pl.BlockSpec((None, t, LANES), lambda p, bi: (bi, 0, p)),
        out_shape=jax.ShapeDtypeStruct((b, t, w), BF16),
        scratch_shapes=[pltpu.VMEM((nbr, 2 * DIL_BLOCK, 2 * DIL_BLOCK), F32),
                        pltpu.VMEM((nbr, t, LANES), F32), pltpu.VMEM((nbr, t, LANES), F32),
                        pltpu.VMEM((nbr, t, LANES), F32)],
        compiler_params=_params(("arbitrary", "arbitrary")),
        name="dil_attn",
    )(rel_bias, jnp.asarray(bucket), jnp.asarray(valid), qkv, qkv, qkv)


def _mlstm_proj_kernel(x_ref, g_ref, w_ref, wg_ref, cw_ref, bg_ref, q_ref, kt_ref, v_ref, o_ref,
                       gate_ref, *hist_scrs, tiles_per_seq, n_chunk, scale):
    i = pl.program_id(0)
    tm = x_ref.shape[0]
    width = q_ref.shape[1]
    xn = _rms(x_ref[...], g_ref[...]).astype(BF16)

    def even_rows(first, n):
        return pl.ds(2 * first, n, stride=2)

    slabs = n_chunk // LANES
    n_conv = 2 * width // n_chunk
    assert len(hist_scrs) == n_conv

    @pl.when(i % tiles_per_seq == 0)
    def _():
        for hist in hist_scrs:
            hist[:, 0:2 * CONV_PAD, :] = jnp.zeros((slabs, 2 * CONV_PAD, LANES), F32)

    def project(c):
        pre = _dot(xn, w_ref[:, c * n_chunk:(c + 1) * n_chunk])
        for s in range(slabs):
            hist_scrs[c][s, even_rows(CONV_PAD, tm), :] = pre[:, s * LANES:(s + 1) * LANES]

    def conv(c):
        sl = slice(c * n_chunk, (c + 1) * n_chunk)
        hist = hist_scrs[c]
        parts = []
        for s in range(slabs):
            y = None
            for tap in range(CONV_WIDTH):
                off = CONV_PAD - (CONV_WIDTH - 1) + tap
                w_tap = cw_ref[tap:tap + 1, (c * slabs + s) * LANES:(c * slabs + s + 1) * LANES]
                term = w_tap * hist[s, even_rows(off, tm), :]
                y = term if y is None else y + term
            parts.append(y)
            hist[s, even_rows(0, CONV_PAD), :] = hist[s, even_rows(tm, CONV_PAD), :]
        y = jnp.concatenate(parts, axis=1)
        y = y * jax.nn.sigmoid(y)
        if c * n_chunk < width:
            q_ref[:, sl] = (y * scale).astype(BF16)
        else:
            kt_ref[c * n_chunk - width:(c + 1) * n_chunk - width, :] = y.T.astype(BF16)

    for c in range(n_conv + 1):
        if c < n_conv:
            project(c)
        if c >= 1:
            conv(c - 1)
    for c in range(width // n_chunk):
        sl = slice(c * n_chunk, (c + 1) * n_chunk)
        v_ref[:, sl] = _dot(xn, w_ref[:, 2 * width + c * n_chunk:2 * width + (c + 1) * n_chunk]).astype(BF16)
        o_ref[:, sl] = _dot(xn, w_ref[:, 3 * width + c * n_chunk:3 * width + (c + 1) * n_chunk])
    gate_ref[...] = _dot(xn, wg_ref[...]) + bg_ref[...]


def _mlstm_proj(x, g, w_in, w_gates, conv_w, b_gates, *, seq, tm=512, n_chunk=512):
    m, d = x.shape
    width = conv_w.shape[1] // 2
    assert m % tm == 0 and seq % tm == 0 and width % n_chunk == 0 and w_in.shape[1] >= 4 * width
    row = lambda wd: pl.BlockSpec((tm, wd), lambda i: (i, 0))
    scale = 1.0 / math.sqrt(width // N_HEADS_MLSTM)
    return pl.pallas_call(
        functools.partial(_mlstm_proj_kernel, tiles_per_seq=seq // tm, n_chunk=n_chunk, scale=scale),
        grid=(m // tm,),
        in_specs=[row(d), _resident((1, d)), _resident(w_in.shape), _resident((d, 2 * LANES)),
                  _resident(conv_w.shape), _resident((1, 2 * LANES))],
        out_specs=[row(width), pl.BlockSpec((width, tm), lambda i: (0, i)), row(width), row(width),
                   row(2 * LANES)],
        out_shape=[jax.ShapeDtypeStruct((m, width), BF16), jax.ShapeDtypeStruct((width, m), BF16),
                   jax.ShapeDtypeStruct((m, width), BF16), jax.ShapeDtypeStruct((m, width), F32),
                   jax.ShapeDtypeStruct((m, 2 * LANES), F32)],
        scratch_shapes=[pltpu.VMEM((n_chunk // LANES, 2 * (tm + CONV_PAD), LANES), F32)
                        for _ in range(2 * width // n_chunk)],
        compiler_params=_params(("arbitrary",)),
        name="mlstm_proj",
    )(x, g.reshape(1, d), w_in, w_gates, conv_w, b_gates)


def _split3(x):
    hi = x.astype(BF16)
    r = x - hi.astype(F32)
    mid = r.astype(BF16)
    return hi, mid, (r - mid.astype(F32)).astype(BF16)


def _gate_prep_kernel(g_ref, cols_ref, rows_ref, *, heads, cl):
    tm = g_ref.shape[0]
    row = lax.broadcasted_iota(jnp.int32, (cl, cl), 0)
    col = lax.broadcasted_iota(jnp.int32, (cl, cl), 1)
    incl = jnp.where(row >= col, 1.0, 0.0).astype(BF16)
    lane = lax.broadcasted_iota(jnp.int32, (1, cl), 1)
    pad = jnp.zeros((cl - heads, cl), F32)
    spans = [slice(c * cl, (c + 1) * cl) for c in range(tm // cl)]
    bs = []
    for r in spans:
        b = None
        for part in _split3(jax.nn.log_sigmoid(g_ref[r, LANES:])):
            term = _dot(incl, part)
            b = term if b is None else b + term
        bs.append(b)
    us = [g_ref[r, :LANES] - b for r, b in zip(spans, bs)]
    u_rows = jnp.concatenate([u.T[0:heads, :] for u in us], axis=0)
    b_rows = jnp.concatenate([b.T[0:heads, :] for b in bs], axis=0)
    cmax = u_rows
    shift = 1
    while shift < cl:
        cmax = jnp.where(lane >= shift, jnp.maximum(cmax, pltpu.roll(cmax, shift, axis=1)), cmax)
        shift *= 2
    u_max = jnp.broadcast_to(jnp.max(u_rows, axis=1, keepdims=True), u_rows.shape)
    b_last = jnp.broadcast_to(jnp.min(b_rows, axis=1, keepdims=True), b_rows.shape)
    for c, (r, u, b) in enumerate(zip(spans, us, bs)):
        hs = slice(c * heads, (c + 1) * heads)
        cols_ref[r, :] = jnp.concatenate([u, b, jnp.concatenate([cmax[hs], pad], axis=0).T], axis=1)
        rows_ref[:, r] = jnp.concatenate([u_rows[hs], u_max[hs], b_last[hs]], axis=0)


def _gate_prep(gates, *, heads, cl, tm=1024):
    m = gates.shape[0]
    tm = min(tm, m)
    assert m % tm == 0 and tm % cl == 0 and cl == LANES and heads == 8
    return pl.pallas_call(
        functools.partial(_gate_prep_kernel, heads=heads, cl=cl),
        grid=(m // tm,),
        in_specs=[pl.BlockSpec((tm, 2 * LANES), lambda i: (i, 0))],
        out_specs=[pl.BlockSpec((tm, 3 * LANES), lambda i: (i, 0)),
                   pl.BlockSpec((3 * heads, tm), lambda i: (0, i))],
        out_shape=[jax.ShapeDtypeStruct((m, 3 * LANES), F32),
                   jax.ShapeDtypeStruct((3 * heads, m), F32)],
        compiler_params=_params(("parallel",)),
        name="gate_prep",
    )(gates)


def _mlstm_kernel(q_ref, kt_ref, v_ref, o_ref, gcol_ref, grow_ref, hg_ref, out_ref,
                  s_scr, m_scr, ml_scr, *, heads):
    cl = MLSTM_CHUNK
    n_chunks = q_ref.shape[0] // cl
    dh = q_ref.shape[1] // heads

    @pl.when(pl.program_id(1) == 0)
    def _():
        s_scr[...] = jnp.zeros_like(s_scr)
        m_scr[...] = jnp.zeros_like(m_scr)
        ml_scr[...] = jnp.zeros_like(ml_scr)

    row = lax.broadcasted_iota(jnp.int32, (cl, cl), 0)
    col = lax.broadcasted_iota(jnp.int32, (cl, cl), 1)
    lower = row >= col
    ones = jnp.ones((cl, dh), BF16)

    m_rows = m_scr[...]
    m_lane = ml_scr[0:1, :]
    tables = []
    for c in range(n_chunks):
        span = slice(c * cl, (c + 1) * cl)
        u_cols, b_cols, cmax_cols = (gcol_ref[span, t * LANES:(t + 1) * LANES] for t in range(3))
        u_rows, u_max_rows, b_last_rows = (grow_ref[t * heads:(t + 1) * heads, span] for t in range(3))
        mm_last_rows = jnp.maximum(m_rows, u_max_rows)
        mm_cols = jnp.maximum(cmax_cols, m_lane)
        tables.append(dict(
            span=span, u_rows=u_rows, m_prev_rows=m_rows, mm_cols=mm_cols,
            decay_rows=jnp.exp(m_rows - mm_last_rows),
            ws_rows=jnp.exp(u_rows - mm_last_rows),
            floor_cols=jnp.exp(-(b_cols + mm_cols))))
        m_rows = b_last_rows + mm_last_rows
        m_lane = b_cols[cl - 1:cl, :] + mm_cols[cl - 1:cl, :]
    m_scr[...] = m_rows
    ml_scr[...] = jnp.broadcast_to(m_lane, ml_scr.shape)

    def lanes_of(mat, c):
        return jnp.broadcast_to(mat[:, c:c + 1], (cl, LANES))

    def stage_a(tab, h):
        sl = slice(h * dh, (h + 1) * dh)
        q, kt = q_ref[tab["span"], sl], kt_ref[sl, tab["span"]]
        v1 = jnp.concatenate([v_ref[tab["span"], sl], ones], axis=1)
        s_prev = s_scr[h]
        qk = _dot(q, kt)
        qs = _dot(q, s_prev.astype(BF16))
        kw = (kt.astype(F32) * tab["ws_rows"][h:h + 1, :]).astype(BF16)
        decay = jnp.broadcast_to(tab["decay_rows"][h:h + 1, :], (dh, LANES))
        s_scr[h] = jnp.concatenate([decay, decay], axis=1) * s_prev + _dot(kw, v1)
        return tab, h, sl, v1, qk, qs

    def stage_b(tab, h, sl, v1, qk, qs):
        mm = lanes_of(tab["mm_cols"], h)
        weight = jnp.where(lower, jnp.exp(tab["u_rows"][h:h + 1, :] - mm), 0.0)
        w_inter = jnp.exp(tab["m_prev_rows"][h:h + 1, :] - mm)
        return tab, sl, v1, qs, (qk * weight).astype(BF16), w_inter, lanes_of(tab["floor_cols"], h)

    def stage_c(tab, sl, v1, qs, p, w_inter, floor):
        pv = _dot(p, v1)
        both = pv + jnp.concatenate([w_inter, w_inter], axis=1) * qs
        hid = both[:, :dh] / jnp.maximum(jnp.abs(both[:, dh:]), floor)
        hid = hid * lax.rsqrt(jnp.mean(hid * hid, axis=-1, keepdims=True) + EPS) * hg_ref[:, sl]
        out_ref[tab["span"], sl] = (hid * jax.nn.sigmoid(o_ref[tab["span"], sl])).astype(out_ref.dtype)

    jobs = [(tab, h) for tab in tables for h in range(heads)]
    after_a, after_b = {}, {}
    for j in range(len(jobs) + 2):
        if j < len(jobs):
            after_a[j] = stage_a(*jobs[j])
        if 0 <= j - 1 < len(jobs):
            after_b[j - 1] = stage_b(*after_a.pop(j - 1))
        if 0 <= j - 2 < len(jobs):
            stage_c(*after_b.pop(j - 2))


def _mlstm(q, kt, v, o, gates, head_g, *, heads=N_HEADS_MLSTM, chunks=4):
    b, t, w = q.shape
    cl = MLSTM_CHUNK
    dh = w // heads
    rows = chunks * cl
    nc = t // rows
    assert t % rows == 0 and dh == LANES and cl == LANES
    gcols, grows = _gate_prep(gates, heads=heads, cl=cl)
    blk = lambda wd: pl.BlockSpec((None, rows, wd), lambda bi, c: (bi, c, 0))
    return pl.pallas_call(
        functools.partial(_mlstm_kernel, heads=heads),
        grid=(b, nc),
        in_specs=[blk(w), pl.BlockSpec((w, rows), lambda bi, c: (0, bi * nc + c)), blk(w), blk(w),
                  pl.BlockSpec((rows, 3 * LANES), lambda bi, c: (bi * nc + c, 0)),
                  pl.BlockSpec((3 * heads, rows), lambda bi, c: (0, bi * nc + c)),
                  _resident((1, w))],
        out_specs=blk(w),
        out_shape=jax.ShapeDtypeStruct((b, t, w), BF16),
        scratch_shapes=[pltpu.VMEM((heads, dh, 2 * dh), F32), pltpu.VMEM((heads, LANES), F32),
                        pltpu.VMEM((heads, LANES), F32)],
        compiler_params=_params(("parallel", "arbitrary")),
        name="mlstm",
    )(q, kt, v, o, gcols, grows, head_g.reshape(1, w))


def kernel(x, norm_g, ffn_w_gate, ffn_w_up, ffn_w_down, attn_w_in, attn_w_out, rel_bias,
           mlstm_w_in, mlstm_b_gates, mlstm_conv_w, mlstm_head_g, mlstm_w_out):
    bsz, t, d = x.shape
    depth = norm_g.shape[0]
    h = x.reshape(bsz * t, d)
    bf = lambda a: a.astype(BF16)

    w_gate, w_up, w_down = bf(ffn_w_gate), bf(ffn_w_up), bf(ffn_w_down)

    def ffn(h, layer, half, mix=None):
        g = norm_g[layer]
        return _ffn(h, g[2 * half * 2], g[2 * half * 2 + 1], w_gate, w_up, w_down, (layer, half), mix)

    for layer in range(depth):
        g = norm_g[layer]
        j = layer // 2
        h = ffn(h, layer, 0)
        if layer % 2 == 0:
            sb, dil = _attn_proj(h, g[2], bf(attn_w_in[j]))
            out_sb = _sb_attention(sb.reshape(bsz, t, -1))
            out_dil = _dil_attention(dil.reshape(bsz, t, -1), rel_bias.astype(F32))
            mix = ([out_sb.reshape(bsz * t, -1), out_dil.reshape(bsz * t, -1)], bf(attn_w_out[j]), g[3])
        else:
            width = mlstm_w_out.shape[1]
            n_head = mlstm_b_gates.shape[1] // 2
            tiles = lambda a: jnp.concatenate(
                [jnp.pad(part, ((0, 0), (0, LANES - n_head)))
                 for part in (a[:, :n_head], a[:, n_head:])], axis=1)
            w_in = bf(mlstm_w_in[j])
            w_gates = tiles(w_in[:, 4 * width:])
            b_gates = tiles(mlstm_b_gates[j].astype(F32).reshape(1, -1))
            q, kt, v, o, gates = _mlstm_proj(h, g[2], w_in, w_gates, mlstm_conv_w[j].astype(F32),
                                             b_gates, seq=t)
            r3 = lambda a: a.reshape(bsz, t, -1)
            hid = _mlstm(r3(q), kt, r3(v), r3(o), gates, mlstm_head_g[j].astype(F32))
            mix = ([hid.reshape(bsz * t, width)], bf(mlstm_w_out[j]), g[3])
        h = ffn(h, layer, 1, mix)
    return h.reshape(bsz, t, d)
```

```python
import functools
import math

import numpy as np
import jax
import jax.numpy as jnp
from jax import lax
from jax.experimental import pallas as pl
from jax.experimental.pallas import tpu as pltpu

EPS = 1e-6
HEAD_DIM_ATTN = 64
DIL_CONFIGS = ((128, 1), (512, 4), (2048, 16))
DIL_BLOCK = 128
DIL_UNROLL = 16
NUM_BUCKETS = 32
MAX_DISTANCE = 2048
N_HEADS_MLSTM = 8
MLSTM_CHUNK = 128
CONV_WIDTH = 4
LANES = 128
CONV_PAD = 8
MASKED = -1e30
V7X_VMEM_BYTES = 64 * 1024 * 1024
VMEM_LIMIT = V7X_VMEM_BYTES * 7 // 8

F32 = jnp.float32
BF16 = jnp.bfloat16


def _params(sem, vmem=VMEM_LIMIT):
    return pltpu.CompilerParams(dimension_semantics=sem, vmem_limit_bytes=vmem)


def _resident(shape):
    zeros = (0,) * len(shape)
    return pl.BlockSpec(shape, lambda *_: zeros, pipeline_mode=pl.Buffered(1))


def _rms(x, g):
    return x * lax.rsqrt(jnp.mean(x * x, axis=-1, keepdims=True) + EPS) * g


def _dot(a, b):
    return jnp.dot(a, b, preferred_element_type=F32)


def _dot_nt(a, b):
    return lax.dot_general(a, b, (((1,), (1,)), ((), ())), preferred_element_type=F32)


def _ffn_kernel(*refs, ff_chunk, n_mix, groups):
    mix_refs, refs = refs[:n_mix], refs[n_mix:]
    if n_mix:
        (wmix_ref, gmix_ref), refs = refs[:2], refs[2:]
    x_ref, gin_ref, gout_ref, wg_ref, wu_ref, wd_ref, o_ref = refs
    d_ff = wg_ref.shape[1]
    rows = x_ref.shape[0] // groups
    spans = [slice(r * rows, (r + 1) * rows) for r in range(groups)]

    xs = []
    for rs in spans:
        x = x_ref[rs, :]
        if n_mix:
            mixed = jnp.concatenate([r[rs, :] for r in mix_refs], axis=-1)
            x = x + _rms(_dot(mixed, wmix_ref[...]), gmix_ref[...])
        xs.append(x)
    xns = [_rms(x, gin_ref[...]).astype(BF16) for x in xs]
    accs = []
    for xn in xns:
        acc = None
        for c in range(d_ff // ff_chunk):
            sl = slice(c * ff_chunk, (c + 1) * ff_chunk)
            gate = _dot(xn, wg_ref[:, sl])
            up = _dot(xn, wu_ref[:, sl])
            h = (gate * jax.nn.sigmoid(gate) * up).astype(BF16)
            part = _dot(h, wd_ref[sl, :])
            acc = part if acc is None else acc + part
        accs.append(acc)
    for rs, x, acc in zip(spans, xs, accs):
        o_ref[rs, :] = x + 0.5 * _rms(acc, gout_ref[...])


def _ffn(x, g_in, g_out, wg, wu, wd, which=(), mix=None, *, tm=1024, groups=2, ff_chunk=256):
    m, d = x.shape
    d_ff = wg.shape[-1]
    assert m % tm == 0 and tm % groups == 0 and d_ff % ff_chunk == 0 and len(which) == wg.ndim - 2
    row = lambda width: pl.BlockSpec((tm, width), lambda i: (i, 0))
    lead = (None,) * len(which)
    weight = lambda shape: pl.BlockSpec(lead + shape, lambda i: tuple(which) + (0, 0),
                                        pipeline_mode=pl.Buffered(1))
    parts, w_mix, g_mix = mix if mix else ((), None, None)
    mix_specs = [row(a.shape[1]) for a in parts] + ([_resident(w_mix.shape), _resident((1, d))] if mix else [])
    mix_args = list(parts) + ([w_mix, g_mix.reshape(1, d)] if mix else [])
    return pl.pallas_call(
        functools.partial(_ffn_kernel, ff_chunk=ff_chunk, n_mix=len(parts), groups=groups),
        grid=(m // tm,),
        in_specs=mix_specs + [row(d), _resident((1, d)), _resident((1, d)),
                              weight((d, d_ff)), weight((d, d_ff)), weight((d_ff, d))],
        out_specs=row(d),
        out_shape=jax.ShapeDtypeStruct((m, d), F32),
        compiler_params=_params(("parallel",)),
        name="ffn",
    )(*mix_args, x, g_in.reshape(1, d), g_out.reshape(1, d), wg, wu, wd)


def _attn_proj_kernel(x_ref, g_ref, w_ref, sb_ref, dil_ref, *, n_chunk, scale):
    xn = _rms(x_ref[...], g_ref[...]).astype(BF16)
    w_sb = sb_ref.shape[1]
    w_q = w_sb // 3
    for c in range(w_sb // n_chunk):
        sl = slice(c * n_chunk, (c + 1) * n_chunk)
        y = _dot(xn, w_ref[:, sl])
        if (c + 1) * n_chunk <= w_q:
            y = y * scale
        sb_ref[:, sl] = y.astype(BF16)
    for c in range(dil_ref.shape[1] // n_chunk):
        sl = slice(c * n_chunk, (c + 1) * n_chunk)
        y = _dot(xn, w_ref[:, w_sb + c * n_chunk: w_sb + (c + 1) * n_chunk])
        if (c + 1) * n_chunk <= w_q:
            y = y * scale
        dil_ref[:, sl] = y


def _attn_proj(x, g, w, *, tm=512, n_chunk=512):
    m, d = x.shape
    n = w.shape[1]
    half = n // 2
    assert m % tm == 0 and (half // 3) % n_chunk == 0
    scale = 1.0 / math.sqrt(HEAD_DIM_ATTN)
    return pl.pallas_call(
        functools.partial(_attn_proj_kernel, n_chunk=n_chunk, scale=scale),
        grid=(m // tm,),
        in_specs=[pl.BlockSpec((tm, d), lambda i: (i, 0)), _resident((1, d)), _resident((d, n))],
        out_specs=[pl.BlockSpec((tm, half), lambda i: (i, 0)),
                   pl.BlockSpec((tm, half), lambda i: (i, 0))],
        out_shape=[jax.ShapeDtypeStruct((m, half), BF16), jax.ShapeDtypeStruct((m, half), F32)],
        compiler_params=_params(("parallel",)),
        name="attn_proj",
    )(x, g.reshape(1, d), w)


LOG2E = math.log2(math.e)
SB_DEAD_LOG2 = -160.0


def _sb_kernel(q_ref, k_ref, v_ref, tri_ref, o_ref, *, blk, q_blocks):
    first_q = pl.program_id(2) * q_blocks
    lane = lax.broadcasted_iota(jnp.int32, (1, LANES), 1)
    row = lax.broadcasted_iota(jnp.int32, (blk, blk), 0)
    col = lax.broadcasted_iota(jnp.int32, (blk, blk), 1)
    causal = col < row
    causal2 = jnp.concatenate([causal, causal], axis=0)
    tri = tri_ref[...]
    head0 = lane < HEAD_DIM_ATTN

    def stacked(q2):
        zeros = jnp.zeros_like(q2)
        return jnp.concatenate([jnp.where(head0, q2, zeros), jnp.where(head0, zeros, q2)], axis=0)

    def pairs(jobs):
        chains = []
        for j, (qq, kb, diag, _) in enumerate(jobs):
            chains.append((j, pl.multiple_of(kb * blk, blk), causal2 if diag else None))
            chains.append((j, pl.multiple_of(jnp.maximum(kb - 1, 0) * blk, blk), kb >= 1))
        states = [job[3] for job in jobs]
        zs, mids = {}, {}
        for n in range(len(chains) + 2):
            if n < len(chains):
                j, ks, _ = chains[n]
                zs[n] = _dot_nt(jobs[j][0], k_ref[pl.ds(ks, blk), :]) * LOG2E
            if 0 <= n - 1 < len(chains):
                z, mask = zs.pop(n - 1), chains[n - 1][2]
                neg = -z
                log_keep = jnp.minimum(neg, 0.0) - jnp.log2(1.0 + jnp.exp2(jnp.minimum(z, neg)))
                log_beta = z + log_keep
                if mask is not None:
                    log_keep = jnp.where(mask, log_keep, 0.0)
                    log_beta = jnp.where(mask, log_beta, MASKED)
                hi = log_keep.astype(BF16)
                lo = (log_keep - hi.astype(F32)).astype(BF16)
                later = _dot(hi, tri) + _dot(lo, tri)
                mids[n - 1] = (log_beta, later, jnp.sum(log_keep, axis=-1, keepdims=True))
            if 0 <= n - 2 < len(chains):
                j, ks, _ = chains[n - 2]
                log_beta, later, total = mids.pop(n - 2)
                carry, acc = states[j]
                p = jnp.exp2(log_beta + later + carry)
                states[j] = (carry + total, acc + _dot(p.astype(BF16), v_ref[pl.ds(ks, blk), :]))
        return states

    def alive(state):
        return (jnp.max(state[0]) > SB_DEAD_LOG2).astype(jnp.int32)

    zero = (jnp.zeros((2 * blk, 1), F32), jnp.zeros((2 * blk, LANES), F32))
    qqs = [stacked(q_ref[g * blk:(g + 1) * blk, :]) for g in range(q_blocks)]
    states = pairs([(qq, first_q + g, True, zero) for g, qq in enumerate(qqs)])

    for g, (qq, state) in enumerate(zip(qqs, states)):
        i = first_q + g
        n_pairs = lax.shift_right_logical(i, 1)

        def cond(loop, n_pairs=n_pairs):
            t, live, _ = loop
            return (t < n_pairs) & (live > 0)

        def body(loop, qq=qq, i=i):
            t, _, state = loop
            state, = pairs([(qq, i - 2 - 2 * t, False, state)])
            return t + 1, alive(state), state

        _, _, state = lax.while_loop(cond, body, (jnp.int32(0), alive(state), state))
        o_ref[g * blk:(g + 1) * blk, :] = jnp.where(head0, state[1][:blk], state[1][blk:]).astype(o_ref.dtype)


def _sb_attention(qkv, *, blk=256, q_blocks=4):
    b, t, w3 = qkv.shape
    w = w3 // 3
    pairs = w // LANES
    blk = min(blk, t)
    rows = blk * q_blocks
    assert t % rows == 0
    tri = jnp.asarray(np.tril(np.ones((blk, blk), np.float32), -1), BF16)
    return pl.pallas_call(
        functools.partial(_sb_kernel, blk=blk, q_blocks=q_blocks),
        grid=(b, pairs, t // rows),
        in_specs=[pl.BlockSpec((None, rows, LANES), lambda bi, p, i: (bi, i, p)),
                  pl.BlockSpec((None, t, LANES), lambda bi, p, i: (bi, 0, pairs + p)),
                  pl.BlockSpec((None, t, LANES), lambda bi, p, i: (bi, 0, 2 * pairs + p)),
                  _resident((blk, blk))],
        out_specs=pl.BlockSpec((None, rows, LANES), lambda bi, p, i: (bi, i, p)),
        out_shape=jax.ShapeDtypeStruct((b, t, w), BF16),
        compiler_params=_params(("parallel", "parallel", "arbitrary")),
        name="sb_attn",
    )(qkv, qkv, qkv, tri)


def _t5_bucket_np(dist):
    max_exact = NUM_BUCKETS // 2
    d = np.maximum(dist, 1).astype(np.float32)
    log_b = max_exact + (np.log(d / np.float32(max_exact)) / np.float32(math.log(MAX_DISTANCE / max_exact))
                         * np.float32(NUM_BUCKETS - max_exact)).astype(np.int32)
    log_b = np.minimum(log_b, NUM_BUCKETS - 1)
    return np.where(dist < max_exact, dist, log_b)


def _dil_tables():
    qi = np.arange(DIL_BLOCK)[:, None]
    ki = np.arange(2 * DIL_BLOCK)[None, :]
    dist = qi + DIL_BLOCK - ki
    buckets, valid = [], []
    for window, dil in DIL_CONFIGS:
        steps = window // dil
        buckets.append(_t5_bucket_np(np.maximum(dist, 0) * dil))
        valid.append((dist >= 0) & (dist <= steps))
    return np.stack(buckets).astype(np.int32), np.stack(valid).astype(np.int32)


def _dil_kernel(rb_ref, bucket_ref, valid_ref, q_ref, k_ref, v_ref, o_ref,
                bias_scr, num_scr, m_scr, l_scr, *, seq):
    p = pl.program_id(0)
    qb = DIL_BLOCK
    lane = lax.broadcasted_iota(jnp.int32, (1, LANES), 1)
    head0 = lane < HEAD_DIM_ATTN
    first_half = lax.broadcasted_iota(jnp.int32, (1, 2 * qb), 1) < qb

    @pl.when(pl.program_id(1) == 0)
    def _():
        for br in range(len(DIL_CONFIGS)):
            bucket = bucket_ref[br]
            valid = valid_ref[br] > 0
            for h in range(2):
                bias = jnp.zeros((qb, 2 * qb), F32)
                for b in range(NUM_BUCKETS):
                    bias = jnp.where(bucket == b, rb_ref[b, 2 * p + h], bias)
                bias_scr[br, h * qb:(h + 1) * qb, :] = jnp.where(valid, bias, MASKED)

    for br, (_, dil) in enumerate(DIL_CONFIGS):
        n_units = seq // qb

        n_blocks = seq // (qb * dil)
        run = min(DIL_UNROLL, n_blocks)
        runs = DIL_UNROLL // run
        assert DIL_UNROLL % run == 0 and n_blocks % run == 0 and (dil == 1 or n_blocks == run)

        def rows(ref, start, dil=dil):
            if dil == 1:
                return ref[pl.ds(start, qb), :].astype(BF16)
            return ref[pl.ds(start, qb, stride=dil), :].astype(BF16)

        def load_run(rho, dil=dil, run=run, whole=(n_blocks == run)):
            res, base = (rho, 0) if whole else (0, rho * run)
            starts = [(base + i) * (qb * dil) + res for i in range(run)]
            kb = [rows(k_ref, st) for st in starts]
            vb = [rows(v_ref, st) for st in starts]
            if whole:
                k_prev, v_prev = kb[0], vb[0]
                pen = jnp.where(first_half, MASKED, 0.0)
            else:
                st = jnp.maximum(base - 1, 0) * (qb * dil) + res
                k_prev, v_prev = rows(k_ref, st), rows(v_ref, st)
                pen = jnp.where(first_half, jnp.where(base == 0, MASKED, 0.0), 0.0)
            units = []
            for i, st in enumerate(starts):
                q2 = rows(q_ref, st)
                zeros = jnp.zeros_like(q2)
                qq = jnp.concatenate([jnp.where(head0, q2, zeros), jnp.where(head0, zeros, q2)], axis=0)
                k2 = jnp.concatenate([kb[i - 1] if i else k_prev, kb[i]], axis=0)
                v2 = jnp.concatenate([vb[i - 1] if i else v_prev, vb[i]], axis=0)
                units.append((st, pen if i == 0 else None, qq, k2, v2))
            return units

        def softmax_parts(s):
            m = jnp.max(s, axis=-1, keepdims=True)
            e = jnp.exp(s - m)
            return m, e, jnp.sum(e, axis=-1, keepdims=True)

        def group(g, _, br=br, dil=dil, runs=runs):
            units = [u for r in range(runs) for u in load_run(g * runs + r)]
            scores, parts = {}, {}
            for j in range(DIL_UNROLL + 2):
                if j < DIL_UNROLL:
                    _, pen, qq, k2, _ = units[j]
                    scores[j] = _dot_nt(qq, k2) + bias_scr[br]
                    if pen is not None:
                        scores[j] = scores[j] + pen
                if 0 <= j - 1 < DIL_UNROLL:
                    parts[j - 1] = softmax_parts(scores.pop(j - 1))
                if 0 <= j - 2 < DIL_UNROLL:
                    st, _, _, _, v2 = units[j - 2]
                    m, e, l = parts.pop(j - 2)
                    pv = _dot(e.astype(BF16), v2)
                    idx = pl.ds(st, qb) if dil == 1 else pl.ds(st, qb, stride=dil)
                    num_scr[br, idx, :] = jnp.where(head0, pv[:qb], pv[qb:])
                    m_scr[br, idx, :] = jnp.where(head0, m[:qb], m[qb:])
                    l_scr[br, idx, :] = jnp.where(head0, l[:qb], l[qb:])
            return 0

        lax.fori_loop(0, n_units // DIL_UNROLL, group, 0)

    rows_out = 256

    def finish(c, _):
        idx = pl.ds(pl.multiple_of(c * rows_out, rows_out), rows_out)
        m_all = [m_scr[br, idx, :] for br in range(len(DIL_CONFIGS))]
        m_max = functools.reduce(jnp.maximum, m_all)
        num = den = None
        for br, m_br in enumerate(m_all):
            wt = jnp.exp(m_br - m_max)
            n_br = wt * num_scr[br, idx, :]
            d_br = wt * l_scr[br, idx, :]
            num = n_br if num is None else num + n_br
            den = d_br if den is None else den + d_br
        o_ref[idx, :] = (num / den).astype(o_ref.dtype)
        return 0

    lax.fori_loop(0, seq // rows_out, finish, 0)


def _dil_attention(qkv, rel_bias):
    b, t, w3 = qkv.shape
    w = w3 // 3
    pairs = w // LANES
    assert t % (DIL_BLOCK * max(d for _, d in DIL_CONFIGS)) == 0 and t % 256 == 0
    bucket, valid = _dil_tables()
    nbr = len(DIL_CONFIGS)
    seq_spec = lambda off: pl.BlockSpec((None, t, LANES), lambda p, bi: (bi, 0, off + p))
    return pl.pallas_call(
        functools.partial(_dil_kernel, seq=t),
        grid=(pairs, b),
        in_specs=[pl.BlockSpec(memory_space=pltpu.SMEM),
                  _resident((nbr, DIL_BLOCK, 2 * DIL_BLOCK)),
                  _resident((nbr, DIL_BLOCK, 2 * DIL_BLOCK)),
                  seq_spec(0), seq_spec(pairs), seq_spec(2 * pairs)],
        out_specs=pl.BlockSpec((None, t, LANES), lambda p, bi: (bi, 0, p)),
        out_shape=jax.ShapeDtypeStruct((b, t, w), BF16),
        scratch_shapes=[pltpu.VMEM((nbr, 2 * DIL_BLOCK, 2 * DIL_BLOCK), F32),
                        pltpu.VMEM((nbr, t, LANES), F32), pltpu.VMEM((nbr, t, LANES), F32),
                        pltpu.VMEM((nbr, t, LANES), F32)],
        compiler_params=_params(("arbitrary", "arbitrary")),
        name="dil_attn",
    )(rel_bias, jnp.asarray(bucket), jnp.asarray(valid), qkv, qkv, qkv)


def _mlstm_proj_kernel(x_ref, g_ref, w_ref, wg_ref, cw_ref, bg_ref, q_ref, kt_ref, v_ref, o_ref,
                       gate_ref, *hist_scrs, tiles_per_seq, n_chunk, scale):
    i = pl.program_id(0)
    tm = x_ref.shape[0]
    width = q_ref.shape[1]
    xn = _rms(x_ref[...], g_ref[...]).astype(BF16)

    def even_rows(first, n):
        return pl.ds(2 * first, n, stride=2)

    slabs = n_chunk // LANES
    n_conv = 2 * width // n_chunk
    assert len(hist_scrs) == n_conv

    @pl.when(i % tiles_per_seq == 0)
    def _():
        for hist in hist_scrs:
            hist[:, 0:2 * CONV_PAD, :] = jnp.zeros((slabs, 2 * CONV_PAD, LANES), F32)

    def project(c):
        pre = _dot(xn, w_ref[:, c * n_chunk:(c + 1) * n_chunk])
        for s in range(slabs):
            hist_scrs[c][s, even_rows(CONV_PAD, tm), :] = pre[:, s * LANES:(s + 1) * LANES]

    def conv(c):
        sl = slice(c * n_chunk, (c + 1) * n_chunk)
        hist = hist_scrs[c]
        parts = []
        for s in range(slabs):
            y = None
            for tap in range(CONV_WIDTH):
                off = CONV_PAD - (CONV_WIDTH - 1) + tap
                w_tap = cw_ref[tap:tap + 1, (c * slabs + s) * LANES:(c * slabs + s + 1) * LANES]
                term = w_tap * hist[s, even_rows(off, tm), :]
                y = term if y is None else y + term
            parts.append(y)
            hist[s, even_rows(0, CONV_PAD), :] = hist[s, even_rows(tm, CONV_PAD), :]
        y = jnp.concatenate(parts, axis=1)
        y = y * jax.nn.sigmoid(y)
        if c * n_chunk < width:
            q_ref[:, sl] = (y * scale).astype(BF16)
        else:
            kt_ref[c * n_chunk - width:(c + 1) * n_chunk - width, :] = y.T.astype(BF16)

    for c in range(n_conv + 1):
        if c < n_conv:
            project(c)
        if c >= 1:
            conv(c - 1)
    for c in range(width // n_chunk):
        sl = slice(c * n_chunk, (c + 1) * n_chunk)
        v_ref[:, sl] = _dot(xn, w_ref[:, 2 * width + c * n_chunk:2 * width + (c + 1) * n_chunk]).astype(BF16)
        o_ref[:, sl] = _dot(xn, w_ref[:, 3 * width + c * n_chunk:3 * width + (c + 1) * n_chunk])
    gate_ref[...] = _dot(xn, wg_ref[...]) + bg_ref[...]


def _mlstm_proj(x, g, w_in, w_gates, conv_w, b_gates, *, seq, tm=512, n_chunk=512):
    m, d = x.shape
    width = conv_w.shape[1] // 2
    assert m % tm == 0 and seq % tm == 0 and width % n_chunk == 0 and w_in.shape[1] >= 4 * width
    row = lambda wd: pl.BlockSpec((tm, wd), lambda i: (i, 0))
    scale = 1.0 / math.sqrt(width // N_HEADS_MLSTM)
    return pl.pallas_call(
        functools.partial(_mlstm_proj_kernel, tiles_per_seq=seq // tm, n_chunk=n_chunk, scale=scale),
        grid=(m // tm,),
        in_specs=[row(d), _resident((1, d)), _resident(w_in.shape), _resident((d, 2 * LANES)),
                  _resident(conv_w.shape), _resident((1, 2 * LANES))],
        out_specs=[row(width), pl.BlockSpec((width, tm), lambda i: (0, i)), row(width), row(width),
                   row(2 * LANES)],
        out_shape=[jax.ShapeDtypeStruct((m, width), BF16), jax.ShapeDtypeStruct((width, m), BF16),
                   jax.ShapeDtypeStruct((m, width), BF16), jax.ShapeDtypeStruct((m, width), F32),
                   jax.ShapeDtypeStruct((m, 2 * LANES), F32)],
        scratch_shapes=[pltpu.VMEM((n_chunk // LANES, 2 * (tm + CONV_PAD), LANES), F32)
                        for _ in range(2 * width // n_chunk)],
        compiler_params=_params(("arbitrary",)),
        name="mlstm_proj",
    )(x, g.reshape(1, d), w_in, w_gates, conv_w, b_gates)


def _split3(x):
    hi = x.astype(BF16)
    r = x - hi.astype(F32)
    mid = r.astype(BF16)
    return hi, mid, (r - mid.astype(F32)).astype(BF16)


def _gate_prep_kernel(g_ref, cols_ref, rows_ref, *, heads, cl):
    tm = g_ref.shape[0]
    row = lax.broadcasted_iota(jnp.int32, (cl, cl), 0)
    col = lax.broadcasted_iota(jnp.int32, (cl, cl), 1)
    incl = jnp.where(row >= col, 1.0, 0.0).astype(BF16)
    lane = lax.broadcasted_iota(jnp.int32, (1, cl), 1)
    pad = jnp.zeros((cl - heads, cl), F32)
    spans = [slice(c * cl, (c + 1) * cl) for c in range(tm // cl)]
    bs = []
    for r in spans:
        b = None
        for part in _split3(jax.nn.log_sigmoid(g_ref[r, LANES:])):
            term = _dot(incl, part)
            b = term if b is None else b + term
        bs.append(b)
    us = [g_ref[r, :LANES] - b for r, b in zip(spans, bs)]
    u_rows = jnp.concatenate([u.T[0:heads, :] for u in us], axis=0)
    b_rows = jnp.concatenate([b.T[0:heads, :] for b in bs], axis=0)
    cmax = u_rows
    shift = 1
    while shift < cl:
        cmax = jnp.where(lane >= shift, jnp.maximum(cmax, pltpu.roll(cmax, shift, axis=1)), cmax)
        shift *= 2
    u_max = jnp.broadcast_to(jnp.max(u_rows, axis=1, keepdims=True), u_rows.shape)
    b_last = jnp.broadcast_to(jnp.min(b_rows, axis=1, keepdims=True), b_rows.shape)
    for c, (r, u, b) in enumerate(zip(spans, us, bs)):
        hs = slice(c * heads, (c + 1) * heads)
        cols_ref[r, :] = jnp.concatenate([u, b, jnp.concatenate([cmax[hs], pad], axis=0).T], axis=1)
        rows_ref[:, r] = jnp.concatenate([u_rows[hs], u_max[hs], b_last[hs]], axis=0)


def _gate_prep(gates, *, heads, cl, tm=1024):
    m = gates.shape[0]
    tm = min(tm, m)
    assert m % tm == 0 and tm % cl == 0 and cl == LANES and heads == 8
    return pl.pallas_call(
        functools.partial(_gate_prep_kernel, heads=heads, cl=cl),
        grid=(m // tm,),
        in_specs=[pl.BlockSpec((tm, 2 * LANES), lambda i: (i, 0))],
        out_specs=[pl.BlockSpec((tm, 3 * LANES), lambda i: (i, 0)),
                   pl.BlockSpec((3 * heads, tm), lambda i: (0, i))],
        out_shape=[jax.ShapeDtypeStruct((m, 3 * LANES), F32),
                   jax.ShapeDtypeStruct((3 * heads, m), F32)],
        compiler_params=_params(("parallel",)),
        name="gate_prep",
    )(gates)


def _mlstm_kernel(q_ref, kt_ref, v_ref, o_ref, gcol_ref, grow_ref, hg_ref, out_ref,
                  s_scr, m_scr, ml_scr, *, heads):
    cl = MLSTM_CHUNK
    n_chunks = q_ref.shape[0] // cl
    dh = q_ref.shape[1] // heads

    @pl.when(pl.program_id(1) == 0)
    def _():
        s_scr[...] = jnp.zeros_like(s_scr)
        m_scr[...] = jnp.zeros_like(m_scr)
        ml_scr[...] = jnp.zeros_like(ml_scr)

    row = lax.broadcasted_iota(jnp.int32, (cl, cl), 0)
    col = lax.broadcasted_iota(jnp.int32, (cl, cl), 1)
    lower = row >= col
    ones = jnp.ones((cl, dh), BF16)

    m_rows = m_scr[...]
    m_lane = ml_scr[0:1, :]
    tables = []
    for c in range(n_chunks):
        span = slice(c * cl, (c + 1) * cl)
        u_cols, b_cols, cmax_cols = (gcol_ref[span, t * LANES:(t + 1) * LANES] for t in range(3))
        u_rows, u_max_rows, b_last_rows = (grow_ref[t * heads:(t + 1) * heads, span] for t in range(3))
        mm_last_rows = jnp.maximum(m_rows, u_max_rows)
        mm_cols = jnp.maximum(cmax_cols, m_lane)
        tables.append(dict(
            span=span, u_rows=u_rows, m_prev_rows=m_rows, mm_cols=mm_cols,
            decay_rows=jnp.exp(m_rows - mm_last_rows),
            ws_rows=jnp.exp(u_rows - mm_last_rows),
            floor_cols=jnp.exp(-(b_cols + mm_cols))))
        m_rows = b_last_rows + mm_last_rows
        m_lane = b_cols[cl - 1:cl, :] + mm_cols[cl - 1:cl, :]
    m_scr[...] = m_rows
    ml_scr[...] = jnp.broadcast_to(m_lane, ml_scr.shape)

    def lanes_of(mat, c):
        return jnp.broadcast_to(mat[:, c:c + 1], (cl, LANES))

    def stage_a(tab, h):
        sl = slice(h * dh, (h + 1) * dh)
        q, kt = q_ref[tab["span"], sl], kt_ref[sl, tab["span"]]
        v1 = jnp.concatenate([v_ref[tab["span"], sl], ones], axis=1)
        s_prev = s_scr[h]
        qk = _dot(q, kt)
        qs = _dot(q, s_prev.astype(BF16))
        kw = (kt.astype(F32) * tab["ws_rows"][h:h + 1, :]).astype(BF16)
        decay = jnp.broadcast_to(tab["decay_rows"][h:h + 1, :], (dh, LANES))
        s_scr[h] = jnp.concatenate([decay, decay], axis=1) * s_prev + _dot(kw, v1)
        return tab, h, sl, v1, qk, qs

    def stage_b(tab, h, sl, v1, qk, qs):
        mm = lanes_of(tab["mm_cols"], h)
        weight = jnp.where(lower, jnp.exp(tab["u_rows"][h:h + 1, :] - mm), 0.0)
        w_inter = jnp.exp(tab["m_prev_rows"][h:h + 1, :] - mm)
        return tab, sl, v1, qs, (qk * weight).astype(BF16), w_inter, lanes_of(tab["floor_cols"], h)

    def stage_c(tab, sl, v1, qs, p, w_inter, floor):
        pv = _dot(p, v1)
        both = pv + jnp.concatenate([w_inter, w_inter], axis=1) * qs
        hid = both[:, :dh] / jnp.maximum(jnp.abs(both[:, dh:]), floor)
        hid = hid * lax.rsqrt(jnp.mean(hid * hid, axis=-1, keepdims=True) + EPS) * hg_ref[:, sl]
        out_ref[tab["span"], sl] = (hid * jax.nn.sigmoid(o_ref[tab["span"], sl])).astype(out_ref.dtype)

    jobs = [(tab, h) for tab in tables for h in range(heads)]
    after_a, after_b = {}, {}
    for j in range(len(jobs) + 2):
        if j < len(jobs):
            after_a[j] = stage_a(*jobs[j])
        if 0 <= j - 1 < len(jobs):
            after_b[j - 1] = stage_b(*after_a.pop(j - 1))
        if 0 <= j - 2 < len(jobs):
            stage_c(*after_b.pop(j - 2))


def _mlstm(q, kt, v, o, gates, head_g, *, heads=N_HEADS_MLSTM, chunks=4):
    b, t, w = q.shape
    cl = MLSTM_CHUNK
    dh = w // heads
    rows = chunks * cl
    nc = t // rows
    assert t % rows == 0 and dh == LANES and cl == LANES
    gcols, grows = _gate_prep(gates, heads=heads, cl=cl)
    blk = lambda wd: pl.BlockSpec((None, rows, wd), lambda bi, c: (bi, c, 0))
    return pl.pallas_call(
        functools.partial(_mlstm_kernel, heads=heads),
        grid=(b, nc),
        in_specs=[blk(w), pl.BlockSpec((w, rows), lambda bi, c: (0, bi * nc + c)), blk(w), blk(w),
                  pl.BlockSpec((rows, 3 * LANES), lambda bi, c: (bi * nc + c, 0)),
                  pl.BlockSpec((3 * heads, rows), lambda bi, c: (0, bi * nc + c)),
                  _resident((1, w))],
        out_specs=blk(w),
        out_shape=jax.ShapeDtypeStruct((b, t, w), BF16),
        scratch_shapes=[pltpu.VMEM((heads, dh, 2 * dh), F32), pltpu.VMEM((heads, LANES), F32),
                        pltpu.VMEM((heads, LANES), F32)],
        compiler_params=_params(("parallel", "arbitrary")),
        name="mlstm",
    )(q, kt, v, o, gcols, grows, head_g.reshape(1, w))


def kernel(x, norm_g, ffn_w_gate, ffn_w_up, ffn_w_down, attn_w_in, attn_w_out, rel_bias,
           mlstm_w_in, mlstm_b_gates, mlstm_conv_w, mlstm_head_g, mlstm_w_out):
    bsz, t, d = x.shape
    depth = norm_g.shape[0]
    h = x.reshape(bsz * t, d)
    bf = lambda a: a.astype(BF16)

    w_gate, w_up, w_down = bf(ffn_w_gate), bf(ffn_w_up), bf(ffn_w_down)

    def ffn(h, layer, half, mix=None):
        g = norm_g[layer]
        return _ffn(h, g[2 * half * 2], g[2 * half * 2 + 1], w_gate, w_up, w_down, (layer, half), mix)

    for layer in range(depth):
        g = norm_g[layer]
        j = layer // 2
        h = ffn(h, layer, 0)
        if layer % 2 == 0:
            sb, dil = _attn_proj(h, g[2], bf(attn_w_in[j]))
            out_sb = _sb_attention(sb.reshape(bsz, t, -1))
            out_dil = _dil_attention(dil.reshape(bsz, t, -1), rel_bias.astype(F32))
            mix = ([out_sb.reshape(bsz * t, -1), out_dil.reshape(bsz * t, -1)], bf(attn_w_out[j]), g[3])
        else:
            width = mlstm_w_out.shape[1]
            n_head = mlstm_b_gates.shape[1] // 2
            tiles = lambda a: jnp.concatenate(
                [jnp.pad(part, ((0, 0), (0, LANES - n_head)))
                 for part in (a[:, :n_head], a[:, n_head:])], axis=1)
            w_in = bf(mlstm_w_in[j])
            w_gates = tiles(w_in[:, 4 * width:])
            b_gates = tiles(mlstm_b_gates[j].astype(F32).reshape(1, -1))
            q, kt, v, o, gates = _mlstm_proj(h, g[2], w_in, w_gates, mlstm_conv_w[j].astype(F32),
                                             b_gates, seq=t)
            r3 = lambda a: a.reshape(bsz, t, -1)
            hid = _mlstm(r3(q), kt, r3(v), r3(o), gates, mlstm_head_g[j].astype(F32))
            mix = ([hid.reshape(bsz * t, width)], bf(mlstm_w_out[j]), g[3])
        h = ffn(h, layer, 1, mix)
    return h.reshape(bsz, t, d)
```

```python
import functools
import math

import numpy as np
import jax
import jax.numpy as jnp
from jax import lax
from jax.experimental import pallas as pl
from jax.experimental.pallas import tpu as pltpu

EPS = 1e-6
HEAD_DIM_ATTN = 64
DIL_CONFIGS = ((128, 1), (512, 4), (2048, 16))
DIL_BLOCK = 128
DIL_UNROLL = 16
NUM_BUCKETS = 32
MAX_DISTANCE = 2048
N_HEADS_MLSTM = 8
MLSTM_CHUNK = 128
CONV_WIDTH = 4
LANES = 128
CONV_PAD = 8
MASKED = -1e30
V7X_VMEM_BYTES = 64 * 1024 * 1024
VMEM_LIMIT = V7X_VMEM_BYTES * 7 // 8

F32 = jnp.float32
BF16 = jnp.bfloat16


def _params(sem, vmem=VMEM_LIMIT):
    return pltpu.CompilerParams(dimension_semantics=sem, vmem_limit_bytes=vmem)


def _resident(shape):
    zeros = (0,) * len(shape)
    return pl.BlockSpec(shape, lambda *_: zeros, pipeline_mode=pl.Buffered(1))


def _rms(x, g):
    return x * lax.rsqrt(jnp.mean(x * x, axis=-1, keepdims=True) + EPS) * g


def _dot(a, b):
    return jnp.dot(a, b, preferred_element_type=F32)


def _dot_nt(a, b):
    return lax.dot_general(a, b, (((1,), (1,)), ((), ())), preferred_element_type=F32)


def _ffn_kernel(*refs, ff_chunk, n_mix, groups):
    mix_refs, refs = refs[:n_mix], refs[n_mix:]
    if n_mix:
        (wmix_ref, gmix_ref), refs = refs[:2], refs[2:]
    x_ref, gin_ref, gout_ref, wg_ref, wu_ref, wd_ref, o_ref = refs
    d_ff = wg_ref.shape[1]
    rows = x_ref.shape[0] // groups
    spans = [slice(r * rows, (r + 1) * rows) for r in range(groups)]

    xs = []
    for rs in spans:
        x = x_ref[rs, :]
        if n_mix:
            mixed = jnp.concatenate([r[rs, :] for r in mix_refs], axis=-1)
            x = x + _rms(_dot(mixed, wmix_ref[...]), gmix_ref[...])
        xs.append(x)
    xns = [_rms(x, gin_ref[...]).astype(BF16) for x in xs]
    accs = []
    for xn in xns:
        acc = None
        for c in range(d_ff // ff_chunk):
            sl = slice(c * ff_chunk, (c + 1) * ff_chunk)
            gate = _dot(xn, wg_ref[:, sl])
            up = _dot(xn, wu_ref[:, sl])
            h = (gate * jax.nn.sigmoid(gate) * up).astype(BF16)
            part = _dot(h, wd_ref[sl, :])
            acc = part if acc is None else acc + part
        accs.append(acc)
    for rs, x, acc in zip(spans, xs, accs):
        o_ref[rs, :] = x + 0.5 * _rms(acc, gout_ref[...])


def _ffn(x, g_in, g_out, wg, wu, wd, which=(), mix=None, *, tm=1024, groups=2, ff_chunk=256):
    m, d = x.shape
    d_ff = wg.shape[-1]
    assert m % tm == 0 and tm % groups == 0 and d_ff % ff_chunk == 0 and len(which) == wg.ndim - 2
    row = lambda width: pl.BlockSpec((tm, width), lambda i: (i, 0))
    lead = (None,) * len(which)
    weight = lambda shape: pl.BlockSpec(lead + shape, lambda i: tuple(which) + (0, 0),
                                        pipeline_mode=pl.Buffered(1))
    parts, w_mix, g_mix = mix if mix else ((), None, None)
    mix_specs = [row(a.shape[1]) for a in parts] + ([_resident(w_mix.shape), _resident((1, d))] if mix else [])
    mix_args = list(parts) + ([w_mix, g_mix.reshape(1, d)] if mix else [])
    return pl.pallas_call(
        functools.partial(_ffn_kernel, ff_chunk=ff_chunk, n_mix=len(parts), groups=groups),
        grid=(m // tm,),
        in_specs=mix_specs + [row(d), _resident((1, d)), _resident((1, d)),
                              weight((d, d_ff)), weight((d, d_ff)), weight((d_ff, d))],
        out_specs=row(d),
        out_shape=jax.ShapeDtypeStruct((m, d), F32),
        compiler_params=_params(("parallel",)),
        name="ffn",
    )(*mix_args, x, g_in.reshape(1, d), g_out.reshape(1, d), wg, wu, wd)


def _attn_proj_kernel(x_ref, g_ref, w_ref, sb_ref, dil_ref, *, n_chunk, scale):
    xn = _rms(x_ref[...], g_ref[...]).astype(BF16)
    w_sb = sb_ref.shape[1]
    w_q = w_sb // 3
    for c in range(w_sb // n_chunk):
        sl = slice(c * n_chunk, (c + 1) * n_chunk)
        y = _dot(xn, w_ref[:, sl])
        if (c + 1) * n_chunk <= w_q:
            y = y * scale
        sb_ref[:, sl] = y.astype(BF16)
    for c in range(dil_ref.shape[1] // n_chunk):
        sl = slice(c * n_chunk, (c + 1) * n_chunk)
        y = _dot(xn, w_ref[:, w_sb + c * n_chunk: w_sb + (c + 1) * n_chunk])
        if (c + 1) * n_chunk <= w_q:
            y = y * scale
        dil_ref[:, sl] = y


def _attn_proj(x, g, w, *, tm=512, n_chunk=512):
    m, d = x.shape
    n = w.shape[1]
    half = n // 2
    assert m % tm == 0 and (half // 3) % n_chunk == 0
    scale = 1.0 / math.sqrt(HEAD_DIM_ATTN)
    return pl.pallas_call(
        functools.partial(_attn_proj_kernel, n_chunk=n_chunk, scale=scale),
        grid=(m // tm,),
        in_specs=[pl.BlockSpec((tm, d), lambda i: (i, 0)), _resident((1, d)), _resident((d, n))],
        out_specs=[pl.BlockSpec((tm, half), lambda i: (i, 0)),
                   pl.BlockSpec((tm, half), lambda i: (i, 0))],
        out_shape=[jax.ShapeDtypeStruct((m, half), BF16), jax.ShapeDtypeStruct((m, half), F32)],
        compiler_params=_params(("parallel",)),
        name="attn_proj",
    )(x, g.reshape(1, d), w)


LOG2E = math.log2(math.e)
SB_DEAD_LOG2 = -160.0


def _sb_kernel(q_ref, k_ref, v_ref, tri_ref, o_ref, *, blk, q_blocks):
    first_q = pl.program_id(2) * q_blocks
    lane = lax.broadcasted_iota(jnp.int32, (1, LANES), 1)
    row = lax.broadcasted_iota(jnp.int32, (blk, blk), 0)
    col = lax.broadcasted_iota(jnp.int32, (blk, blk), 1)
    causal = col < row
    causal2 = jnp.concatenate([causal, causal], axis=0)
    tri = tri_ref[...]
    head0 = lane < HEAD_DIM_ATTN

    def stacked(q2):
        zeros = jnp.zeros_like(q2)
        return jnp.concatenate([jnp.where(head0, q2, zeros), jnp.where(head0, zeros, q2)], axis=0)

    def pairs(jobs):
        chains = []
        for j, (qq, kb, diag, _) in enumerate(jobs):
            chains.append((j, pl.multiple_of(kb * blk, blk), causal2 if diag else None))
            chains.append((j, pl.multiple_of(jnp.maximum(kb - 1, 0) * blk, blk), kb >= 1))
        states = [job[3] for job in jobs]
        zs, mids = {}, {}
        for n in range(len(chains) + 2):
            if n < len(chains):
                j, ks, _ = chains[n]
                zs[n] = _dot_nt(jobs[j][0], k_ref[pl.ds(ks, blk), :]) * LOG2E
            if 0 <= n - 1 < len(chains):
                z, mask = zs.pop(n - 1), chains[n - 1][2]
                neg = -z
                log_keep = jnp.minimum(neg, 0.0) - jnp.log2(1.0 + jnp.exp2(jnp.minimum(z, neg)))
                log_beta = z + log_keep
                if mask is not None:
                    log_keep = jnp.where(mask, log_keep, 0.0)
                    log_beta = jnp.where(mask, log_beta, MASKED)
                hi = log_keep.astype(BF16)
                lo = (log_keep - hi.astype(F32)).astype(BF16)
                later = _dot(hi, tri) + _dot(lo, tri)
                mids[n - 1] = (log_beta, later, jnp.sum(log_keep, axis=-1, keepdims=True))
            if 0 <= n - 2 < len(chains):
                j, ks, _ = chains[n - 2]
                log_beta, later, total = mids.pop(n - 2)
                carry, acc = states[j]
                p = jnp.exp2(log_beta + later + carry)
                states[j] = (carry + total, acc + _dot(p.astype(BF16), v_ref[pl.ds(ks, blk), :]))
        return states

    def alive(state):
        return (jnp.max(state[0]) > SB_DEAD_LOG2).astype(jnp.int32)

    zero = (jnp.zeros((2 * blk, 1), F32), jnp.zeros((2 * blk, LANES), F32))
    qqs = [stacked(q_ref[g * blk:(g + 1) * blk, :]) for g in range(q_blocks)]
    states = pairs([(qq, first_q + g, True, zero) for g, qq in enumerate(qqs)])

    for g, (qq, state) in enumerate(zip(qqs, states)):
        i = first_q + g
        n_pairs = lax.shift_right_logical(i, 1)

        def cond(loop, n_pairs=n_pairs):
            t, live, _ = loop
            return (t < n_pairs) & (live > 0)

        def body(loop, qq=qq, i=i):
            t, _, state = loop
            state, = pairs([(qq, i - 2 - 2 * t, False, state)])
            return t + 1, alive(state), state

        _, _, state = lax.while_loop(cond, body, (jnp.int32(0), alive(state), state))
        o_ref[g * blk:(g + 1) * blk, :] = jnp.where(head0, state[1][:blk], state[1][blk:]).astype(o_ref.dtype)


def _sb_attention(qkv, *, blk=256, q_blocks=4):
    b, t, w3 = qkv.shape
    w = w3 // 3
    pairs = w // LANES
    blk = min(blk, t)
    rows = blk * q_blocks
    assert t % rows == 0
    tri = jnp.asarray(np.tril(np.ones((blk, blk), np.float32), -1), BF16)
    return pl.pallas_call(
        functools.partial(_sb_kernel, blk=blk, q_blocks=q_blocks),
        grid=(b, pairs, t // rows),
        in_specs=[pl.BlockSpec((None, rows, LANES), lambda bi, p, i: (bi, i, p)),
                  pl.BlockSpec((None, t, LANES), lambda bi, p, i: (bi, 0, pairs + p)),
                  pl.BlockSpec((None, t, LANES), lambda bi, p, i: (bi, 0, 2 * pairs + p)),
                  _resident((blk, blk))],
        out_specs=pl.BlockSpec((None, rows, LANES), lambda bi, p, i: (bi, i, p)),
        out_shape=jax.ShapeDtypeStruct((b, t, w), BF16),
        compiler_params=_params(("parallel", "parallel", "arbitrary")),
        name="sb_attn",
    )(qkv, qkv, qkv, tri)


def _t5_bucket_np(dist):
    max_exact = NUM_BUCKETS // 2
    d = np.maximum(dist, 1).astype(np.float32)
    log_b = max_exact + (np.log(d / np.float32(max_exact)) / np.float32(math.log(MAX_DISTANCE / max_exact))
                         * np.float32(NUM_BUCKETS - max_exact)).astype(np.int32)
    log_b = np.minimum(log_b, NUM_BUCKETS - 1)
    return np.where(dist < max_exact, dist, log_b)


def _dil_tables():
    qi = np.arange(DIL_BLOCK)[:, None]
    ki = np.arange(2 * DIL_BLOCK)[None, :]
    dist = qi + DIL_BLOCK - ki
    buckets, valid = [], []
    for window, dil in DIL_CONFIGS:
        steps = window // dil
        buckets.append(_t5_bucket_np(np.maximum(dist, 0) * dil))
        valid.append((dist >= 0) & (dist <= steps))
    return np.stack(buckets).astype(np.int32), np.stack(valid).astype(np.int32)


def _dil_kernel(rb_ref, bucket_ref, valid_ref, q_ref, k_ref, v_ref, o_ref,
                bias_scr, num_scr, m_scr, l_scr, *, seq):
    p = pl.program_id(0)
    qb = DIL_BLOCK
    lane = lax.broadcasted_iota(jnp.int32, (1, LANES), 1)
    head0 = lane < HEAD_DIM_ATTN
    first_half = lax.broadcasted_iota(jnp.int32, (1, 2 * qb), 1) < qb

    @pl.when(pl.program_id(1) == 0)
    def _():
        for br in range(len(DIL_CONFIGS)):
            bucket = bucket_ref[br]
            valid = valid_ref[br] > 0
            for h in range(2):
                bias = jnp.zeros((qb, 2 * qb), F32)
                for b in range(NUM_BUCKETS):
                    bias = jnp.where(bucket == b, rb_ref[b, 2 * p + h], bias)
                bias_scr[br, h * qb:(h + 1) * qb, :] = jnp.where(valid, bias, MASKED)

    for br, (_, dil) in enumerate(DIL_CONFIGS):
        n_units = seq // qb

        n_blocks = seq // (qb * dil)
        run = min(DIL_UNROLL, n_blocks)
        runs = DIL_UNROLL // run
        assert DIL_UNROLL % run == 0 and n_blocks % run == 0 and (dil == 1 or n_blocks == run)

        def rows(ref, start, dil=dil):
            if dil == 1:
                return ref[pl.ds(start, qb), :].astype(BF16)
            return ref[pl.ds(start, qb, stride=dil), :].astype(BF16)

        def load_run(rho, dil=dil, run=run, whole=(n_blocks == run)):
            res, base = (rho, 0) if whole else (0, rho * run)
            starts = [(base + i) * (qb * dil) + res for i in range(run)]
            kb = [rows(k_ref, st) for st in starts]
            vb = [rows(v_ref, st) for st in starts]
            if whole:
                k_prev, v_prev = kb[0], vb[0]
                pen = jnp.where(first_half, MASKED, 0.0)
            else:
                st = jnp.maximum(base - 1, 0) * (qb * dil) + res
                k_prev, v_prev = rows(k_ref, st), rows(v_ref, st)
                pen = jnp.where(first_half, jnp.where(base == 0, MASKED, 0.0), 0.0)
            units = []
            for i, st in enumerate(starts):
                q2 = rows(q_ref, st)
                zeros = jnp.zeros_like(q2)
                qq = jnp.concatenate([jnp.where(head0, q2, zeros), jnp.where(head0, zeros, q2)], axis=0)
                k2 = jnp.concatenate([kb[i - 1] if i else k_prev, kb[i]], axis=0)
                v2 = jnp.concatenate([vb[i - 1] if i else v_prev, vb[i]], axis=0)
                units.append((st, pen if i == 0 else None, qq, k2, v2))
            return units

        def softmax_parts(s):
            m = jnp.max(s, axis=-1, keepdims=True)
            e = jnp.exp(s - m)
            return m, e, jnp.sum(e, axis=-1, keepdims=True)

        def group(g, _, br=br, dil=dil, runs=runs):
            units = [u for r in range(runs) for u in load_run(g * runs + r)]
            scores, parts = {}, {}
            for j in range(DIL_UNROLL + 2):
                if j < DIL_UNROLL:
                    _, pen, qq, k2, _ = units[j]
                    scores[j] = _dot_nt(qq, k2) + bias_scr[br]
                    if pen is not None:
                        scores[j] = scores[j] + pen
                if 0 <= j - 1 < DIL_UNROLL:
                    parts[j - 1] = softmax_parts(scores.pop(j - 1))
                if 0 <= j - 2 < DIL_UNROLL:
                    st, _, _, _, v2 = units[j - 2]
                    m, e, l = parts.pop(j - 2)
                    pv = _dot(e.astype(BF16), v2)
                    idx = pl.ds(st, qb) if dil == 1 else pl.ds(st, qb, stride=dil)
                    num_scr[br, idx, :] = jnp.where(head0, pv[:qb], pv[qb:])
                    m_scr[br, idx, :] = jnp.where(head0, m[:qb], m[qb:])
                    l_scr[br, idx, :] = jnp.where(head0, l[:qb], l[qb:])
            return 0

        lax.fori_loop(0, n_units // DIL_UNROLL, group, 0)

    rows_out = 256

    def finish(c, _):
        idx = pl.ds(pl.multiple_of(c * rows_out, rows_out), rows_out)
        m_all = [m_scr[br, idx, :] for br in range(len(DIL_CONFIGS))]
        m_max = functools.reduce(jnp.maximum, m_all)
        num = den = None
        for br, m_br in enumerate(m_all):
            wt = jnp.exp(m_br - m_max)
            n_br = wt * num_scr[br, idx, :]
            d_br = wt * l_scr[br, idx, :]
            num = n_br if num is None else num + n_br
            den = d_br if den is None else den + d_br
        o_ref[idx, :] = (num / den).astype(o_ref.dtype)
        return 0

    lax.fori_loop(0, seq // rows_out, finish, 0)


def _dil_attention(qkv, rel_bias):
    b, t, w3 = qkv.shape
    w = w3 // 3
    pairs = w // LANES
    assert t % (DIL_BLOCK * max(d for _, d in DIL_CONFIGS)) == 0 and t % 256 == 0
    bucket, valid = _dil_tables()
    nbr = len(DIL_CONFIGS)
    seq_spec = lambda off: pl.BlockSpec((None, t, LANES), lambda p, bi: (bi, 0, off + p))
    return pl.pallas_call(
        functools.partial(_dil_kernel, seq=t),
        grid=(pairs, b),
        in_specs=[pl.BlockSpec(memory_space=pltpu.SMEM),
                  _resident((nbr, DIL_BLOCK, 2 * DIL_BLOCK)),
                  _resident((nbr, DIL_BLOCK, 2 * DIL_BLOCK)),
                  seq_spec(0), seq_spec(pairs), seq_spec(2 * pairs)],
        out_specs=pl.BlockSpec((None, t, LANES), lambda p, bi: (bi, 0, p)),
        out_shape=jax.ShapeDtypeStruct((b, t, w), BF16),
        scratch_shapes=[pltpu.VMEM((nbr, 2 * DIL_BLOCK, 2 * DIL_BLOCK), F32),
                        pltpu.VMEM((nbr, t, LANES), F32), pltpu.VMEM((nbr, t, LANES), F32),
                        pltpu.VMEM((nbr, t, LANES), F32)],
        compiler_params=_params(("arbitrary", "arbitrary")),
        name="dil_attn",
    )(rel_bias, jnp.asarray(bucket), jnp.asarray(valid), qkv, qkv, qkv)


def _mlstm_proj_kernel(x_ref, g_ref, w_ref, wg_ref, cw_ref, bg_ref, q_ref, kt_ref, v_ref, o_ref,
                       gate_ref, *hist_scrs, tiles_per_seq, n_chunk, scale):
    i = pl.program_id(0)
    tm = x_ref.shape[0]
    width = q_ref.shape[1]
    xn = _rms(x_ref[...], g_ref[...]).astype(BF16)

    def even_rows(first, n):
        return pl.ds(2 * first, n, stride=2)

    slabs = n_chunk // LANES
    n_conv = 2 * width // n_chunk
    assert len(hist_scrs) == n_conv

    @pl.when(i % tiles_per_seq == 0)
    def _():
        for hist in hist_scrs:
            hist[:, 0:2 * CONV_PAD, :] = jnp.zeros((slabs, 2 * CONV_PAD, LANES), F32)

    def project(c):
        pre = _dot(xn, w_ref[:, c * n_chunk:(c + 1) * n_chunk])
        for s in range(slabs):
            hist_scrs[c][s, even_rows(CONV_PAD, tm), :] = pre[:, s * LANES:(s + 1) * LANES]

    def conv(c):
        sl = slice(c * n_chunk, (c + 1) * n_chunk)
        hist = hist_scrs[c]
        parts = []
        for s in range(slabs):
            y = None
            for tap in range(CONV_WIDTH):
                off = CONV_PAD - (CONV_WIDTH - 1) + tap
                w_tap = cw_ref[tap:tap + 1, (c * slabs + s) * LANES:(c * slabs + s + 1) * LANES]
                term = w_tap * hist[s, even_rows(off, tm), :]
                y = term if y is None else y + term
            parts.append(y)
            hist[s, even_rows(0, CONV_PAD), :] = hist[s, even_rows(tm, CONV_PAD), :]
        y = jnp.concatenate(parts, axis=1)
        y = y * jax.nn.sigmoid(y)
        if c * n_chunk < width:
            q_ref[:, sl] = (y * scale).astype(BF16)
        else:
            kt_ref[c * n_chunk - width:(c + 1) * n_chunk - width, :] = y.T.astype(BF16)

    def plain(c):
        sl = slice((c // 2) * n_chunk, (c // 2 + 1) * n_chunk)
        if c % 2 == 0:
            v_ref[:, sl] = _dot(xn, w_ref[:, 2 * width + sl.start:2 * width + sl.stop]).astype(BF16)
        else:
            o_ref[:, sl] = _dot(xn, w_ref[:, 3 * width + sl.start:3 * width + sl.stop])

    n_plain = 2 * width // n_chunk
    for c in range(n_conv + 1):
        if c < n_conv:
            project(c)
        if c >= 1:
            conv(c - 1)
            if c - 1 < n_plain:
                plain(c - 1)
    for c in range(n_conv, n_plain):
        plain(c)
    gate_ref[...] = _dot(xn, wg_ref[...]) + bg_ref[...]


def _mlstm_proj(x, g, w_in, w_gates, conv_w, b_gates, *, seq, tm=512, n_chunk=512):
    m, d = x.shape
    width = conv_w.shape[1] // 2
    assert m % tm == 0 and seq % tm == 0 and width % n_chunk == 0 and w_in.shape[1] >= 4 * width
    row = lambda wd: pl.BlockSpec((tm, wd), lambda i: (i, 0))
    scale = 1.0 / math.sqrt(width // N_HEADS_MLSTM)
    return pl.pallas_call(
        functools.partial(_mlstm_proj_kernel, tiles_per_seq=seq // tm, n_chunk=n_chunk, scale=scale),
        grid=(m // tm,),
        in_specs=[row(d), _resident((1, d)), _resident(w_in.shape), _resident((d, 2 * LANES)),
                  _resident(conv_w.shape), _resident((1, 2 * LANES))],
        out_specs=[row(width), pl.BlockSpec((width, tm), lambda i: (0, i)), row(width), row(width),
                   row(2 * LANES)],
        out_shape=[jax.ShapeDtypeStruct((m, width), BF16), jax.ShapeDtypeStruct((width, m), BF16),
                   jax.ShapeDtypeStruct((m, width), BF16), jax.ShapeDtypeStruct((m, width), F32),
                   jax.ShapeDtypeStruct((m, 2 * LANES), F32)],
        scratch_shapes=[pltpu.VMEM((n_chunk // LANES, 2 * (tm + CONV_PAD), LANES), F32)
                        for _ in range(2 * width // n_chunk)],
        compiler_params=_params(("arbitrary",)),
        name="mlstm_proj",
    )(x, g.reshape(1, d), w_in, w_gates, conv_w, b_gates)


def _split3(x):
    hi = x.astype(BF16)
    r = x - hi.astype(F32)
    mid = r.astype(BF16)
    return hi, mid, (r - mid.astype(F32)).astype(BF16)


def _gate_prep_kernel(g_ref, cols_ref, rows_ref, *, heads, cl):
    tm = g_ref.shape[0]
    row = lax.broadcasted_iota(jnp.int32, (cl, cl), 0)
    col = lax.broadcasted_iota(jnp.int32, (cl, cl), 1)
    incl = jnp.where(row >= col, 1.0, 0.0).astype(BF16)
    lane = lax.broadcasted_iota(jnp.int32, (1, cl), 1)
    pad = jnp.zeros((cl - heads, cl), F32)
    spans = [slice(c * cl, (c + 1) * cl) for c in range(tm // cl)]
    bs = []
    for r in spans:
        b = None
        for part in _split3(jax.nn.log_sigmoid(g_ref[r, LANES:])):
            term = _dot(incl, part)
            b = term if b is None else b + term
        bs.append(b)
    us = [g_ref[r, :LANES] - b for r, b in zip(spans, bs)]
    u_rows = jnp.concatenate([u.T[0:heads, :] for u in us], axis=0)
    b_rows = jnp.concatenate([b.T[0:heads, :] for b in bs], axis=0)
    cmax = u_rows
    shift = 1
    while shift < cl:
        cmax = jnp.where(lane >= shift, jnp.maximum(cmax, pltpu.roll(cmax, shift, axis=1)), cmax)
        shift *= 2
    u_max = jnp.broadcast_to(jnp.max(u_rows, axis=1, keepdims=True), u_rows.shape)
    b_last = jnp.broadcast_to(jnp.min(b_rows, axis=1, keepdims=True), b_rows.shape)
    for c, (r, u, b) in enumerate(zip(spans, us, bs)):
        hs = slice(c * heads, (c + 1) * heads)
        cols_ref[r, :] = jnp.concatenate([u, b, jnp.concatenate([cmax[hs], pad], axis=0).T], axis=1)
        rows_ref[:, r] = jnp.concatenate([u_rows[hs], u_max[hs], b_last[hs]], axis=0)


def _gate_prep(gates, *, heads, cl, tm=1024):
    m = gates.shape[0]
    tm = min(tm, m)
    assert m % tm == 0 and tm % cl == 0 and cl == LANES and heads == 8
    return pl.pallas_call(
        functools.partial(_gate_prep_kernel, heads=heads, cl=cl),
        grid=(m // tm,),
        in_specs=[pl.BlockSpec((tm, 2 * LANES), lambda i: (i, 0))],
        out_specs=[pl.BlockSpec((tm, 3 * LANES), lambda i: (i, 0)),
                   pl.BlockSpec((3 * heads, tm), lambda i: (0, i))],
        out_shape=[jax.ShapeDtypeStruct((m, 3 * LANES), F32),
                   jax.ShapeDtypeStruct((3 * heads, m), F32)],
        compiler_params=_params(("parallel",)),
        name="gate_prep",
    )(gates)


def _mlstm_kernel(q_ref, kt_ref, v_ref, o_ref, gcol_ref, grow_ref, hg_ref, out_ref,
                  s_scr, m_scr, ml_scr, *, heads):
    cl = MLSTM_CHUNK
    n_chunks = q_ref.shape[0] // cl
    dh = q_ref.shape[1] // heads

    @pl.when(pl.program_id(1) == 0)
    def _():
        s_scr[...] = jnp.zeros_like(s_scr)
        m_scr[...] = jnp.zeros_like(m_scr)
        ml_scr[...] = jnp.zeros_like(ml_scr)

    row = lax.broadcasted_iota(jnp.int32, (cl, cl), 0)
    col = lax.broadcasted_iota(jnp.int32, (cl, cl), 1)
    lower = row >= col
    ones = jnp.ones((cl, dh), BF16)

    m_rows = m_scr[...]
    m_lane = ml_scr[0:1, :]
    tables = []
    for c in range(n_chunks):
        span = slice(c * cl, (c + 1) * cl)
        u_cols, b_cols, cmax_cols = (gcol_ref[span, t * LANES:(t + 1) * LANES] for t in range(3))
        u_rows, u_max_rows, b_last_rows = (grow_ref[t * heads:(t + 1) * heads, span] for t in range(3))
        mm_last_rows = jnp.maximum(m_rows, u_max_rows)
        mm_cols = jnp.maximum(cmax_cols, m_lane)
        tables.append(dict(
            span=span, u_rows=u_rows, m_prev_rows=m_rows, mm_cols=mm_cols,
            decay_rows=jnp.exp(m_rows - mm_last_rows),
            ws_rows=jnp.exp(u_rows - mm_last_rows),
            floor_cols=jnp.exp(-(b_cols + mm_cols))))
        m_rows = b_last_rows + mm_last_rows
        m_lane = b_cols[cl - 1:cl, :] + mm_cols[cl - 1:cl, :]
    m_scr[...] = m_rows
    ml_scr[...] = jnp.broadcast_to(m_lane, ml_scr.shape)

    def lanes_of(mat, c):
        return jnp.broadcast_to(mat[:, c:c + 1], (cl, LANES))

    def stage_a(tab, h):
        sl = slice(h * dh, (h + 1) * dh)
        q, kt = q_ref[tab["span"], sl], kt_ref[sl, tab["span"]]
        v1 = jnp.concatenate([v_ref[tab["span"], sl], ones], axis=1)
        s_prev = s_scr[h]
        qk = _dot(q, kt)
        qs = _dot(q, s_prev.astype(BF16))
        kw = (kt.astype(F32) * tab["ws_rows"][h:h + 1, :]).astype(BF16)
        decay = jnp.broadcast_to(tab["decay_rows"][h:h + 1, :], (dh, LANES))
        s_scr[h] = jnp.concatenate([decay, decay], axis=1) * s_prev + _dot(kw, v1)
        return tab, h, sl, v1, qk, qs

    def stage_b(tab, h, sl, v1, qk, qs):
        mm = lanes_of(tab["mm_cols"], h)
        weight = jnp.where(lower, jnp.exp(tab["u_rows"][h:h + 1, :] - mm), 0.0)
        w_inter = jnp.exp(tab["m_prev_rows"][h:h + 1, :] - mm)
        return tab, sl, v1, qs, (qk * weight).astype(BF16), w_inter, lanes_of(tab["floor_cols"], h)

    def stage_c(tab, sl, v1, qs, p, w_inter, floor):
        pv = _dot(p, v1)
        both = pv + jnp.concatenate([w_inter, w_inter], axis=1) * qs
        hid = both[:, :dh] / jnp.maximum(jnp.abs(both[:, dh:]), floor)
        hid = hid * lax.rsqrt(jnp.mean(hid * hid, axis=-1, keepdims=True) + EPS) * hg_ref[:, sl]
        out_ref[tab["span"], sl] = (hid * jax.nn.sigmoid(o_ref[tab["span"], sl])).astype(out_ref.dtype)

    jobs = [(tab, h) for tab in tables for h in range(heads)]
    after_a, after_b = {}, {}
    for j in range(len(jobs) + 2):
        if j < len(jobs):
            after_a[j] = stage_a(*jobs[j])
        if 0 <= j - 1 < len(jobs):
            after_b[j - 1] = stage_b(*after_a.pop(j - 1))
        if 0 <= j - 2 < len(jobs):
            stage_c(*after_b.pop(j - 2))


def _mlstm(q, kt, v, o, gates, head_g, *, heads=N_HEADS_MLSTM, chunks=4):
    b, t, w = q.shape
    cl = MLSTM_CHUNK
    dh = w // heads
    rows = chunks * cl
    nc = t // rows
    assert t % rows == 0 and dh == LANES and cl == LANES
    gcols, grows = _gate_prep(gates, heads=heads, cl=cl)
    blk = lambda wd: pl.BlockSpec((None, rows, wd), lambda bi, c: (bi, c, 0))
    return pl.pallas_call(
        functools.partial(_mlstm_kernel, heads=heads),
        grid=(b, nc),
        in_specs=[blk(w), pl.BlockSpec((w, rows), lambda bi, c: (0, bi * nc + c)), blk(w), blk(w),
                  pl.BlockSpec((rows, 3 * LANES), lambda bi, c: (bi * nc + c, 0)),
                  pl.BlockSpec((3 * heads, rows), lambda bi, c: (0, bi * nc + c)),
                  _resident((1, w))],
        out_specs=blk(w),
        out_shape=jax.ShapeDtypeStruct((b, t, w), BF16),
        scratch_shapes=[pltpu.VMEM((heads, dh, 2 * dh), F32), pltpu.VMEM((heads, LANES), F32),
                        pltpu.VMEM((heads, LANES), F32)],
        compiler_params=_params(("parallel", "arbitrary")),
        name="mlstm",
    )(q, kt, v, o, gcols, grows, head_g.reshape(1, w))


def kernel(x, norm_g, ffn_w_gate, ffn_w_up, ffn_w_down, attn_w_in, attn_w_out, rel_bias,
           mlstm_w_in, mlstm_b_gates, mlstm_conv_w, mlstm_head_g, mlstm_w_out):
    bsz, t, d = x.shape
    depth = norm_g.shape[0]
    h = x.reshape(bsz * t, d)
    bf = lambda a: a.astype(BF16)

    w_gate, w_up, w_down = bf(ffn_w_gate), bf(ffn_w_up), bf(ffn_w_down)

    def ffn(h, layer, half, mix=None):
        g = norm_g[layer]
        return _ffn(h, g[2 * half * 2], g[2 * half * 2 + 1], w_gate, w_up, w_down, (layer, half), mix)

    for layer in range(depth):
        g = norm_g[layer]
        j = layer // 2
        h = ffn(h, layer, 0)
        if layer % 2 == 0:
            sb, dil = _attn_proj(h, g[2], bf(attn_w_in[j]))
            out_sb = _sb_attention(sb.reshape(bsz, t, -1))
            out_dil = _dil_attention(dil.reshape(bsz, t, -1), rel_bias.astype(F32))
            mix = ([out_sb.reshape(bsz * t, -1), out_dil.reshape(bsz * t, -1)], bf(attn_w_out[j]), g[3])
        else:
            width = mlstm_w_out.shape[1]
            n_head = mlstm_b_gates.shape[1] // 2
            tiles = lambda a: jnp.concatenate(
                [jnp.pad(part, ((0, 0), (0, LANES - n_head)))
                 for part in (a[:, :n_head], a[:, n_head:])], axis=1)
            w_in = bf(mlstm_w_in[j])
            w_gates = tiles(w_in[:, 4 * width:])
            b_gates = tiles(mlstm_b_gates[j].astype(F32).reshape(1, -1))
            q, kt, v, o, gates = _mlstm_proj(h, g[2], w_in, w_gates, mlstm_conv_w[j].astype(F32),
                                             b_gates, seq=t)
            r3 = lambda a: a.reshape(bsz, t, -1)
            hid = _mlstm(r3(q), kt, r3(v), r3(o), gates, mlstm_head_g[j].astype(F32))
            mix = ([hid.reshape(bsz * t, width)], bf(mlstm_w_out[j]), g[3])
        h = ffn(h, layer, 1, mix)
    return h.reshape(bsz, t, d)
```

```python
import functools
import math

import numpy as np
import jax
import jax.numpy as jnp
from jax import lax
from jax.experimental import pallas as pl
from jax.experimental.pallas import tpu as pltpu

EPS = 1e-6
HEAD_DIM_ATTN = 64
DIL_CONFIGS = ((128, 1), (512, 4), (2048, 16))
DIL_BLOCK = 128
DIL_UNROLL = 16
NUM_BUCKETS = 32
MAX_DISTANCE = 2048
N_HEADS_MLSTM = 8
MLSTM_CHUNK = 128
CONV_WIDTH = 4
LANES = 128
CONV_PAD = 8
MASKED = -1e30
V7X_VMEM_BYTES = 64 * 1024 * 1024
VMEM_LIMIT = V7X_VMEM_BYTES * 7 // 8

F32 = jnp.float32
BF16 = jnp.bfloat16


def _params(sem, vmem=VMEM_LIMIT):
    return pltpu.CompilerParams(dimension_semantics=sem, vmem_limit_bytes=vmem)


def _resident(shape):
    zeros = (0,) * len(shape)
    return pl.BlockSpec(shape, lambda *_: zeros, pipeline_mode=pl.Buffered(1))


def _rms(x, g):
    return x * lax.rsqrt(jnp.mean(x * x, axis=-1, keepdims=True) + EPS) * g


def _dot(a, b):
    return jnp.dot(a, b, preferred_element_type=F32)


def _dot_nt(a, b):
    return lax.dot_general(a, b, (((1,), (1,)), ((), ())), preferred_element_type=F32)


def _ffn_kernel(*refs, ff_chunk, n_mix, groups):
    mix_refs, refs = refs[:n_mix], refs[n_mix:]
    if n_mix:
        (wmix_ref, gmix_ref), refs = refs[:2], refs[2:]
    x_ref, gin_ref, gout_ref, wg_ref, wu_ref, wd_ref, o_ref = refs
    d_ff = wg_ref.shape[1]
    rows = x_ref.shape[0] // groups
    spans = [slice(r * rows, (r + 1) * rows) for r in range(groups)]

    xs = []
    for rs in spans:
        x = x_ref[rs, :]
        if n_mix:
            mixed = jnp.concatenate([r[rs, :] for r in mix_refs], axis=-1)
            x = x + _rms(_dot(mixed, wmix_ref[...]), gmix_ref[...])
        xs.append(x)
    xns = [_rms(x, gin_ref[...]).astype(BF16) for x in xs]
    accs = []
    for xn in xns:
        acc = None
        for c in range(d_ff // ff_chunk):
            sl = slice(c * ff_chunk, (c + 1) * ff_chunk)
            gate = _dot(xn, wg_ref[:, sl])
            up = _dot(xn, wu_ref[:, sl])
            h = (gate * jax.nn.sigmoid(gate) * up).astype(BF16)
            part = _dot(h, wd_ref[sl, :])
            acc = part if acc is None else acc + part
        accs.append(acc)
    for rs, x, acc in zip(spans, xs, accs):
        o_ref[rs, :] = x + 0.5 * _rms(acc, gout_ref[...])


def _ffn(x, g_in, g_out, wg, wu, wd, which=(), mix=None, *, tm=1024, groups=2, ff_chunk=256):
    m, d = x.shape
    d_ff = wg.shape[-1]
    assert m % tm == 0 and tm % groups == 0 and d_ff % ff_chunk == 0 and len(which) == wg.ndim - 2
    row = lambda width: pl.BlockSpec((tm, width), lambda i: (i, 0))
    lead = (None,) * len(which)
    weight = lambda shape: pl.BlockSpec(lead + shape, lambda i: tuple(which) + (0, 0),
                                        pipeline_mode=pl.Buffered(1))
    parts, w_mix, g_mix = mix if mix else ((), None, None)
    mix_specs = [row(a.shape[1]) for a in parts] + ([_resident(w_mix.shape), _resident((1, d))] if mix else [])
    mix_args = list(parts) + ([w_mix, g_mix.reshape(1, d)] if mix else [])
    return pl.pallas_call(
        functools.partial(_ffn_kernel, ff_chunk=ff_chunk, n_mix=len(parts), groups=groups),
        grid=(m // tm,),
        in_specs=mix_specs + [row(d), _resident((1, d)), _resident((1, d)),
                              weight((d, d_ff)), weight((d, d_ff)), weight((d_ff, d))],
        out_specs=row(d),
        out_shape=jax.ShapeDtypeStruct((m, d), F32),
        compiler_params=_params(("parallel",)),
        name="ffn",
    )(*mix_args, x, g_in.reshape(1, d), g_out.reshape(1, d), wg, wu, wd)


def _attn_proj_kernel(x_ref, g_ref, w_ref, sb_ref, dil_ref, *, n_chunk, scale):
    xn = _rms(x_ref[...], g_ref[...]).astype(BF16)
    w_sb = sb_ref.shape[1]
    w_q = w_sb // 3
    for c in range(w_sb // n_chunk):
        sl = slice(c * n_chunk, (c + 1) * n_chunk)
        y = _dot(xn, w_ref[:, sl])
        if (c + 1) * n_chunk <= w_q:
            y = y * scale
        sb_ref[:, sl] = y.astype(BF16)
    for c in range(dil_ref.shape[1] // n_chunk):
        sl = slice(c * n_chunk, (c + 1) * n_chunk)
        y = _dot(xn, w_ref[:, w_sb + c * n_chunk: w_sb + (c + 1) * n_chunk])
        if (c + 1) * n_chunk <= w_q:
            y = y * scale
        dil_ref[:, sl] = y


def _attn_proj(x, g, w, *, tm=512, n_chunk=512):
    m, d = x.shape
    n = w.shape[1]
    half = n // 2
    assert m % tm == 0 and (half // 3) % n_chunk == 0
    scale = 1.0 / math.sqrt(HEAD_DIM_ATTN)
    return pl.pallas_call(
        functools.partial(_attn_proj_kernel, n_chunk=n_chunk, scale=scale),
        grid=(m // tm,),
        in_specs=[pl.BlockSpec((tm, d), lambda i: (i, 0)), _resident((1, d)), _resident((d, n))],
        out_specs=[pl.BlockSpec((tm, half), lambda i: (i, 0)),
                   pl.BlockSpec((tm, half), lambda i: (i, 0))],
        out_shape=[jax.ShapeDtypeStruct((m, half), BF16), jax.ShapeDtypeStruct((m, half), F32)],
        compiler_params=_params(("parallel",)),
        name="attn_proj",
    )(x, g.reshape(1, d), w)


LOG2E = math.log2(math.e)
SB_DEAD_LOG2 = -160.0


def _sb_kernel(q_ref, k_ref, v_ref, tri_ref, o_ref, *, blk, q_blocks):
    first_q = pl.program_id(2) * q_blocks
    lane = lax.broadcasted_iota(jnp.int32, (1, LANES), 1)
    row = lax.broadcasted_iota(jnp.int32, (blk, blk), 0)
    col = lax.broadcasted_iota(jnp.int32, (blk, blk), 1)
    causal = col < row
    causal2 = jnp.concatenate([causal, causal], axis=0)
    tri = tri_ref[...]
    head0 = lane < HEAD_DIM_ATTN

    def stacked(q2):
        zeros = jnp.zeros_like(q2)
        return jnp.concatenate([jnp.where(head0, q2, zeros), jnp.where(head0, zeros, q2)], axis=0)

    def pairs(jobs):
        chains = []
        for j, (qq, kb, diag, _) in enumerate(jobs):
            chains.append((j, pl.multiple_of(kb * blk, blk), causal2 if diag else None))
            chains.append((j, pl.multiple_of(jnp.maximum(kb - 1, 0) * blk, blk), kb >= 1))
        states = [job[3] for job in jobs]
        zs, mids = {}, {}
        for n in range(len(chains) + 2):
            if n < len(chains):
                j, ks, _ = chains[n]
                zs[n] = _dot_nt(jobs[j][0], k_ref[pl.ds(ks, blk), :]) * LOG2E
            if 0 <= n - 1 < len(chains):
                z, mask = zs.pop(n - 1), chains[n - 1][2]
                neg = -z
                log_keep = jnp.minimum(neg, 0.0) - jnp.log2(1.0 + jnp.exp2(jnp.minimum(z, neg)))
                log_beta = z + log_keep
                if mask is not None:
                    log_keep = jnp.where(mask, log_keep, 0.0)
                    log_beta = jnp.where(mask, log_beta, MASKED)
                hi = log_keep.astype(BF16)
                lo = (log_keep - hi.astype(F32)).astype(BF16)
                later = _dot(hi, tri) + _dot(lo, tri)
                mids[n - 1] = (log_beta, later, jnp.sum(log_keep, axis=-1, keepdims=True))
            if 0 <= n - 2 < len(chains):
                j, ks, _ = chains[n - 2]
                log_beta, later, total = mids.pop(n - 2)
                carry, acc = states[j]
                p = jnp.exp2(log_beta + later + carry)
                states[j] = (carry + total, acc + _dot(p.astype(BF16), v_ref[pl.ds(ks, blk), :]))
        return states

    def alive(state):
        return (jnp.max(state[0]) > SB_DEAD_LOG2).astype(jnp.int32)

    zero = (jnp.zeros((2 * blk, 1), F32), jnp.zeros((2 * blk, LANES), F32))
    qqs = [stacked(q_ref[g * blk:(g + 1) * blk, :]) for g in range(q_blocks)]
    states = pairs([(qq, first_q + g, True, zero) for g, qq in enumerate(qqs)])

    for g, (qq, state) in enumerate(zip(qqs, states)):
        i = first_q + g
        n_pairs = lax.shift_right_logical(i, 1)

        def cond(loop, n_pairs=n_pairs):
            t, live, _ = loop
            return (t < n_pairs) & (live > 0)

        def body(loop, qq=qq, i=i):
            t, _, state = loop
            state, = pairs([(qq, i - 2 - 2 * t, False, state)])
            return t + 1, alive(state), state

        _, _, state = lax.while_loop(cond, body, (jnp.int32(0), alive(state), state))
        o_ref[g * blk:(g + 1) * blk, :] = jnp.where(head0, state[1][:blk], state[1][blk:]).astype(o_ref.dtype)


def _sb_attention(qkv, *, blk=256, q_blocks=4):
    b, t, w3 = qkv.shape
    w = w3 // 3
    pairs = w // LANES
    blk = min(blk, t)
    rows = blk * q_blocks
    assert t % rows == 0
    tri = jnp.asarray(np.tril(np.ones((blk, blk), np.float32), -1), BF16)
    return pl.pallas_call(
        functools.partial(_sb_kernel, blk=blk, q_blocks=q_blocks),
        grid=(b, pairs, t // rows),
        in_specs=[pl.BlockSpec((None, rows, LANES), lambda bi, p, i: (bi, i, p)),
                  pl.BlockSpec((None, t, LANES), lambda bi, p, i: (bi, 0, pairs + p)),
                  pl.BlockSpec((None, t, LANES), lambda bi, p, i: (bi, 0, 2 * pairs + p)),
                  _resident((blk, blk))],
        out_specs=pl.BlockSpec((None, rows, LANES), lambda bi, p, i: (bi, i, p)),
        out_shape=jax.ShapeDtypeStruct((b, t, w), BF16),
        compiler_params=_params(("parallel", "parallel", "arbitrary")),
        name="sb_attn",
    )(qkv, qkv, qkv, tri)


def _t5_bucket_np(dist):
    max_exact = NUM_BUCKETS // 2
    d = np.maximum(dist, 1).astype(np.float32)
    log_b = max_exact + (np.log(d / np.float32(max_exact)) / np.float32(math.log(MAX_DISTANCE / max_exact))
                         * np.float32(NUM_BUCKETS - max_exact)).astype(np.int32)
    log_b = np.minimum(log_b, NUM_BUCKETS - 1)
    return np.where(dist < max_exact, dist, log_b)


def _dil_tables():
    qi = np.arange(DIL_BLOCK)[:, None]
    ki = np.arange(2 * DIL_BLOCK)[None, :]
    dist = qi + DIL_BLOCK - ki
    buckets, valid = [], []
    for window, dil in DIL_CONFIGS:
        steps = window // dil
        buckets.append(_t5_bucket_np(np.maximum(dist, 0) * dil))
        valid.append((dist >= 0) & (dist <= steps))
    return np.stack(buckets).astype(np.int32), np.stack(valid).astype(np.int32)


def _dil_kernel(rb_ref, bucket_ref, valid_ref, q_ref, k_ref, v_ref, o_ref,
                bias_scr, num_scr, m_scr, l_scr, *, seq):
    p = pl.program_id(0)
    qb = DIL_BLOCK
    lane = lax.broadcasted_iota(jnp.int32, (1, LANES), 1)
    head0 = lane < HEAD_DIM_ATTN
    first_half = lax.broadcasted_iota(jnp.int32, (1, 2 * qb), 1) < qb

    @pl.when(pl.program_id(1) == 0)
    def _():
        for br in range(len(DIL_CONFIGS)):
            bucket = bucket_ref[br]
            valid = valid_ref[br] > 0
            for h in range(2):
                bias = jnp.zeros((qb, 2 * qb), F32)
                for b in range(NUM_BUCKETS):
                    bias = jnp.where(bucket == b, rb_ref[b, 2 * p + h], bias)
                bias_scr[br, h * qb:(h + 1) * qb, :] = jnp.where(valid, bias, MASKED)

    for br, (_, dil) in enumerate(DIL_CONFIGS):
        n_units = seq // qb

        n_blocks = seq // (qb * dil)
        run = min(DIL_UNROLL, n_blocks)
        runs = DIL_UNROLL // run
        assert DIL_UNROLL % run == 0 and n_blocks % run == 0 and (dil == 1 or n_blocks == run)

        def rows(ref, start, dil=dil):
            if dil == 1:
                return ref[pl.ds(start, qb), :].astype(BF16)
            return ref[pl.ds(start, qb, stride=dil), :].astype(BF16)

        def load_run(rho, dil=dil, run=run, whole=(n_blocks == run)):
            res, base = (rho, 0) if whole else (0, rho * run)
            starts = [(base + i) * (qb * dil) + res for i in range(run)]
            kb = [rows(k_ref, st) for st in starts]
            vb = [rows(v_ref, st) for st in starts]
            if whole:
                k_prev, v_prev = kb[0], vb[0]
                pen = jnp.where(first_half, MASKED, 0.0)
            else:
                st = jnp.maximum(base - 1, 0) * (qb * dil) + res
                k_prev, v_prev = rows(k_ref, st), rows(v_ref, st)
                pen = jnp.where(first_half, jnp.where(base == 0, MASKED, 0.0), 0.0)
            units = []
            for i, st in enumerate(starts):
                q2 = rows(q_ref, st)
                zeros = jnp.zeros_like(q2)
                qq = jnp.concatenate([jnp.where(head0, q2, zeros), jnp.where(head0, zeros, q2)], axis=0)
                k2 = jnp.concatenate([kb[i - 1] if i else k_prev, kb[i]], axis=0)
                v2 = jnp.concatenate([vb[i - 1] if i else v_prev, vb[i]], axis=0)
                units.append((st, pen if i == 0 else None, qq, k2, v2))
            return units

        def softmax_parts(s):
            m = jnp.max(s, axis=-1, keepdims=True)
            e = jnp.exp(s - m)
            return m, e, jnp.sum(e, axis=-1, keepdims=True)

        def group(g, _, br=br, dil=dil, runs=runs):
            units = [u for r in range(runs) for u in load_run(g * runs + r)]
            scores, parts = {}, {}
            for j in range(DIL_UNROLL + 2):
                if j < DIL_UNROLL:
                    _, pen, qq, k2, _ = units[j]
                    scores[j] = _dot_nt(qq, k2) + bias_scr[br]
                    if pen is not None:
                        scores[j] = scores[j] + pen
                if 0 <= j - 1 < DIL_UNROLL:
                    parts[j - 1] = softmax_parts(scores.pop(j - 1))
                if 0 <= j - 2 < DIL_UNROLL:
                    st, _, _, _, v2 = units[j - 2]
                    m, e, l = parts.pop(j - 2)
                    pv = _dot(e.astype(BF16), v2)
                    idx = pl.ds(st, qb) if dil == 1 else pl.ds(st, qb, stride=dil)
                    num_scr[br, idx, :] = jnp.where(head0, pv[:qb], pv[qb:])
                    m_scr[br, idx, :] = jnp.where(head0, m[:qb], m[qb:])
                    l_scr[br, idx, :] = jnp.where(head0, l[:qb], l[qb:])
            return 0

        lax.fori_loop(0, n_units // DIL_UNROLL, group, 0)

    rows_out = 256

    def finish(c, _):
        idx = pl.ds(pl.multiple_of(c * rows_out, rows_out), rows_out)
        m_all = [m_scr[br, idx, :] for br in range(len(DIL_CONFIGS))]
        m_max = functools.reduce(jnp.maximum, m_all)
        num = den = None
        for br, m_br in enumerate(m_all):
            wt = jnp.exp(m_br - m_max)
            n_br = wt * num_scr[br, idx, :]
            d_br = wt * l_scr[br, idx, :]
            num = n_br if num is None else num + n_br
            den = d_br if den is None else den + d_br
        o_ref[idx, :] = (num / den).astype(o_ref.dtype)
        return 0

    lax.fori_loop(0, seq // rows_out, finish, 0)


def _dil_attention(qkv, rel_bias):
    b, t, w3 = qkv.shape
    w = w3 // 3
    pairs = w // LANES
    assert t % (DIL_BLOCK * max(d for _, d in DIL_CONFIGS)) == 0 and t % 256 == 0
    bucket, valid = _dil_tables()
    nbr = len(DIL_CONFIGS)
    seq_spec = lambda off: pl.BlockSpec((None, t, LANES), lambda p, bi: (bi, 0, off + p))
    return pl.pallas_call(
        functools.partial(_dil_kernel, seq=t),
        grid=(pairs, b),
        in_specs=[pl.BlockSpec(memory_space=pltpu.SMEM),
                  _resident((nbr, DIL_BLOCK, 2 * DIL_BLOCK)),
                  _resident((nbr, DIL_BLOCK, 2 * DIL_BLOCK)),
                  seq_spec(0), seq_spec(pairs), seq_spec(2 * pairs)],
        out_specs=pl.BlockSpec((None, t, LANES), lambda p, bi: (bi, 0, p)),
        out_shape=jax.ShapeDtypeStruct((b, t, w), BF16),
        scratch_shapes=[pltpu.VMEM((nbr, 2 * DIL_BLOCK, 2 * DIL_BLOCK), F32),
                        pltpu.VMEM((nbr, t, LANES), F32), pltpu.VMEM((nbr, t, LANES), F32),
                        pltpu.VMEM((nbr, t, LANES), F32)],
        compiler_params=_params(("arbitrary", "arbitrary")),
        name="dil_attn",
    )(rel_bias, jnp.asarray(bucket), jnp.asarray(valid), qkv, qkv, qkv)


def _mlstm_proj_kernel(x_ref, g_ref, w_ref, wg_ref, cw_ref, bg_ref, q_ref, kt_ref, v_ref, o_ref,
                       gate_ref, *hist_scrs, tiles_per_seq, n_chunk, scale):
    i = pl.program_id(0)
    tm = x_ref.shape[0]
    width = q_ref.shape[1]
    xn = _rms(x_ref[...], g_ref[...]).astype(BF16)

    def even_rows(first, n):
        return pl.ds(2 * first, n, stride=2)

    slabs = n_chunk // LANES
    n_conv = 2 * width // n_chunk
    assert len(hist_scrs) == n_conv

    @pl.when(i % tiles_per_seq == 0)
    def _():
        for hist in hist_scrs:
            hist[:, 0:2 * CONV_PAD, :] = jnp.zeros((slabs, 2 * CONV_PAD, LANES), F32)

    def project(c):
        pre = _dot(xn, w_ref[:, c * n_chunk:(c + 1) * n_chunk])
        for s in range(slabs):
            hist_scrs[c][s, even_rows(CONV_PAD, tm), :] = pre[:, s * LANES:(s + 1) * LANES]

    def conv(c):
        sl = slice(c * n_chunk, (c + 1) * n_chunk)
        hist = hist_scrs[c]
        parts = []
        for s in range(slabs):
            y = None
            for tap in range(CONV_WIDTH):
                off = CONV_PAD - (CONV_WIDTH - 1) + tap
                w_tap = cw_ref[tap:tap + 1, (c * slabs + s) * LANES:(c * slabs + s + 1) * LANES]
                term = w_tap * hist[s, even_rows(off, tm), :]
                y = term if y is None else y + term
            parts.append(y)
            hist[s, even_rows(0, CONV_PAD), :] = hist[s, even_rows(tm, CONV_PAD), :]
        y = jnp.concatenate(parts, axis=1)
        y = y * jax.nn.sigmoid(y)
        if c * n_chunk < width:
            q_ref[:, sl] = (y * scale).astype(BF16)
        else:
            kt_ref[c * n_chunk - width:(c + 1) * n_chunk - width, :] = y.T.astype(BF16)

    def plain(c):
        sl = slice((c // 2) * n_chunk, (c // 2 + 1) * n_chunk)
        if c % 2 == 0:
            v_ref[:, sl] = _dot(xn, w_ref[:, 2 * width + sl.start:2 * width + sl.stop]).astype(BF16)
        else:
            o_ref[:, sl] = _dot(xn, w_ref[:, 3 * width + sl.start:3 * width + sl.stop])

    n_plain = 2 * width // n_chunk
    for c in range(n_conv + 1):
        if c < n_conv:
            project(c)
        if c >= 1:
            conv(c - 1)
            if c - 1 < n_plain:
                plain(c - 1)
    for c in range(n_conv, n_plain):
        plain(c)
    gate_ref[...] = _dot(xn, wg_ref[...]) + bg_ref[...]


def _mlstm_proj(x, g, w_in, w_gates, conv_w, b_gates, *, seq, tm=512, n_chunk=256):
    m, d = x.shape
    width = conv_w.shape[1] // 2
    assert m % tm == 0 and seq % tm == 0 and width % n_chunk == 0 and w_in.shape[1] >= 4 * width
    row = lambda wd: pl.BlockSpec((tm, wd), lambda i: (i, 0))
    scale = 1.0 / math.sqrt(width // N_HEADS_MLSTM)
    return pl.pallas_call(
        functools.partial(_mlstm_proj_kernel, tiles_per_seq=seq // tm, n_chunk=n_chunk, scale=scale),
        grid=(m // tm,),
        in_specs=[row(d), _resident((1, d)), _resident(w_in.shape), _resident((d, 2 * LANES)),
                  _resident(conv_w.shape), _resident((1, 2 * LANES))],
        out_specs=[row(width), pl.BlockSpec((width, tm), lambda i: (0, i)), row(width), row(width),
                   row(2 * LANES)],
        out_shape=[jax.ShapeDtypeStruct((m, width), BF16), jax.ShapeDtypeStruct((width, m), BF16),
                   jax.ShapeDtypeStruct((m, width), BF16), jax.ShapeDtypeStruct((m, width), F32),
                   jax.ShapeDtypeStruct((m, 2 * LANES), F32)],
        scratch_shapes=[pltpu.VMEM((n_chunk // LANES, 2 * (tm + CONV_PAD), LANES), F32)
                        for _ in range(2 * width // n_chunk)],
        compiler_params=_params(("arbitrary",)),
        name="mlstm_proj",
    )(x, g.reshape(1, d), w_in, w_gates, conv_w, b_gates)


def _split3(x):
    hi = x.astype(BF16)
    r = x - hi.astype(F32)
    mid = r.astype(BF16)
    return hi, mid, (r - mid.astype(F32)).astype(BF16)


def _gate_prep_kernel(g_ref, cols_ref, rows_ref, *, heads, cl):
    tm = g_ref.shape[0]
    row = lax.broadcasted_iota(jnp.int32, (cl, cl), 0)
    col = lax.broadcasted_iota(jnp.int32, (cl, cl), 1)
    incl = jnp.where(row >= col, 1.0, 0.0).astype(BF16)
    lane = lax.broadcasted_iota(jnp.int32, (1, cl), 1)
    pad = jnp.zeros((cl - heads, cl), F32)
    spans = [slice(c * cl, (c + 1) * cl) for c in range(tm // cl)]
    bs = []
    for r in spans:
        b = None
        for part in _split3(jax.nn.log_sigmoid(g_ref[r, LANES:])):
            term = _dot(incl, part)
            b = term if b is None else b + term
        bs.append(b)
    us = [g_ref[r, :LANES] - b for r, b in zip(spans, bs)]
    u_rows = jnp.concatenate([u.T[0:heads, :] for u in us], axis=0)
    b_rows = jnp.concatenate([b.T[0:heads, :] for b in bs], axis=0)
    cmax = u_rows
    shift = 1
    while shift < cl:
        cmax = jnp.where(lane >= shift, jnp.maximum(cmax, pltpu.roll(cmax, shift, axis=1)), cmax)
        shift *= 2
    u_max = jnp.broadcast_to(jnp.max(u_rows, axis=1, keepdims=True), u_rows.shape)
    b_last = jnp.broadcast_to(jnp.min(b_rows, axis=1, keepdims=True), b_rows.shape)
    for c, (r, u, b) in enumerate(zip(spans, us, bs)):
        hs = slice(c * heads, (c + 1) * heads)
        cols_ref[r, :] = jnp.concatenate([u, b, jnp.concatenate([cmax[hs], pad], axis=0).T], axis=1)
        rows_ref[:, r] = jnp.concatenate([u_rows[hs], u_max[hs], b_last[hs]], axis=0)


def _gate_prep(gates, *, heads, cl, tm=1024):
    m = gates.shape[0]
    tm = min(tm, m)
    assert m % tm == 0 and tm % cl == 0 and cl == LANES and heads == 8
    return pl.pallas_call(
        functools.partial(_gate_prep_kernel, heads=heads, cl=cl),
        grid=(m // tm,),
        in_specs=[pl.BlockSpec((tm, 2 * LANES), lambda i: (i, 0))],
        out_specs=[pl.BlockSpec((tm, 3 * LANES), lambda i: (i, 0)),
                   pl.BlockSpec((3 * heads, tm), lambda i: (0, i))],
        out_shape=[jax.ShapeDtypeStruct((m, 3 * LANES), F32),
                   jax.ShapeDtypeStruct((3 * heads, m), F32)],
        compiler_params=_params(("parallel",)),
        name="gate_prep",
    )(gates)


def _mlstm_kernel(q_ref, kt_ref, v_ref, o_ref, gcol_ref, grow_ref, hg_ref, out_ref,
                  s_scr, m_scr, ml_scr, *, heads):
    cl = MLSTM_CHUNK
    n_chunks = q_ref.shape[0] // cl
    dh = q_ref.shape[1] // heads

    @pl.when(pl.program_id(1) == 0)
    def _():
        s_scr[...] = jnp.zeros_like(s_scr)
        m_scr[...] = jnp.zeros_like(m_scr)
        ml_scr[...] = jnp.zeros_like(ml_scr)

    row = lax.broadcasted_iota(jnp.int32, (cl, cl), 0)
    col = lax.broadcasted_iota(jnp.int32, (cl, cl), 1)
    lower = row >= col
    ones = jnp.ones((cl, dh), BF16)

    m_rows = m_scr[...]
    m_lane = ml_scr[0:1, :]
    tables = []
    for c in range(n_chunks):
        span = slice(c * cl, (c + 1) * cl)
        u_cols, b_cols, cmax_cols = (gcol_ref[span, t * LANES:(t + 1) * LANES] for t in range(3))
        u_rows, u_max_rows, b_last_rows = (grow_ref[t * heads:(t + 1) * heads, span] for t in range(3))
        mm_last_rows = jnp.maximum(m_rows, u_max_rows)
        mm_cols = jnp.maximum(cmax_cols, m_lane)
        tables.append(dict(
            span=span, u_rows=u_rows, m_prev_rows=m_rows, mm_cols=mm_cols,
            decay_rows=jnp.exp(m_rows - mm_last_rows),
            ws_rows=jnp.exp(u_rows - mm_last_rows),
            floor_cols=jnp.exp(-(b_cols + mm_cols))))
        m_rows = b_last_rows + mm_last_rows
        m_lane = b_cols[cl - 1:cl, :] + mm_cols[cl - 1:cl, :]
    m_scr[...] = m_rows
    ml_scr[...] = jnp.broadcast_to(m_lane, ml_scr.shape)

    def lanes_of(mat, c):
        return jnp.broadcast_to(mat[:, c:c + 1], (cl, LANES))

    def stage_a(tab, h):
        sl = slice(h * dh, (h + 1) * dh)
        q, kt = q_ref[tab["span"], sl], kt_ref[sl, tab["span"]]
        v1 = jnp.concatenate([v_ref[tab["span"], sl], ones], axis=1)
        s_prev = s_scr[h]
        qk = _dot(q, kt)
        qs = _dot(q, s_prev.astype(BF16))
        kw = (kt.astype(F32) * tab["ws_rows"][h:h + 1, :]).astype(BF16)
        decay = jnp.broadcast_to(tab["decay_rows"][h:h + 1, :], (dh, LANES))
        s_scr[h] = jnp.concatenate([decay, decay], axis=1) * s_prev + _dot(kw, v1)
        return tab, h, sl, v1, qk, qs

    def stage_b(tab, h, sl, v1, qk, qs):
        mm = lanes_of(tab["mm_cols"], h)
        weight = jnp.where(lower, jnp.exp(tab["u_rows"][h:h + 1, :] - mm), 0.0)
        w_inter = jnp.exp(tab["m_prev_rows"][h:h + 1, :] - mm)
        return tab, sl, v1, qs, (qk * weight).astype(BF16), w_inter, lanes_of(tab["floor_cols"], h)

    def stage_c(tab, sl, v1, qs, p, w_inter, floor):
        pv = _dot(p, v1)
        both = pv + jnp.concatenate([w_inter, w_inter], axis=1) * qs
        hid = both[:, :dh] / jnp.maximum(jnp.abs(both[:, dh:]), floor)
        hid = hid * lax.rsqrt(jnp.mean(hid * hid, axis=-1, keepdims=True) + EPS) * hg_ref[:, sl]
        out_ref[tab["span"], sl] = (hid * jax.nn.sigmoid(o_ref[tab["span"], sl])).astype(out_ref.dtype)

    jobs = [(tab, h) for tab in tables for h in range(heads)]
    after_a, after_b = {}, {}
    for j in range(len(jobs) + 2):
        if j < len(jobs):
            after_a[j] = stage_a(*jobs[j])
        if 0 <= j - 1 < len(jobs):
            after_b[j - 1] = stage_b(*after_a.pop(j - 1))
        if 0 <= j - 2 < len(jobs):
            stage_c(*after_b.pop(j - 2))


def _mlstm(q, kt, v, o, gates, head_g, *, heads=N_HEADS_MLSTM, chunks=4):
    b, t, w = q.shape
    cl = MLSTM_CHUNK
    dh = w // heads
    rows = chunks * cl
    nc = t // rows
    assert t % rows == 0 and dh == LANES and cl == LANES
    gcols, grows = _gate_prep(gates, heads=heads, cl=cl)
    blk = lambda wd: pl.BlockSpec((None, rows, wd), lambda bi, c: (bi, c, 0))
    return pl.pallas_call(
        functools.partial(_mlstm_kernel, heads=heads),
        grid=(b, nc),
        in_specs=[blk(w), pl.BlockSpec((w, rows), lambda bi, c: (0, bi * nc + c)), blk(w), blk(w),
                  pl.BlockSpec((rows, 3 * LANES), lambda bi, c: (bi * nc + c, 0)),
                  pl.BlockSpec((3 * heads, rows), lambda bi, c: (0, bi * nc + c)),
                  _resident((1, w))],
        out_specs=blk(w),
        out_shape=jax.ShapeDtypeStruct((b, t, w), BF16),
        scratch_shapes=[pltpu.VMEM((heads, dh, 2 * dh), F32), pltpu.VMEM((heads, LANES), F32),
                        pltpu.VMEM((heads, LANES), F32)],
        compiler_params=_params(("parallel", "arbitrary")),
        name="mlstm",
    )(q, kt, v, o, gcols, grows, head_g.reshape(1, w))


def kernel(x, norm_g, ffn_w_gate, ffn_w_up, ffn_w_down, attn_w_in, attn_w_out, rel_bias,
           mlstm_w_in, mlstm_b_gates, mlstm_conv_w, mlstm_head_g, mlstm_w_out):
    bsz, t, d = x.shape
    depth = norm_g.shape[0]
    h = x.reshape(bsz * t, d)
    bf = lambda a: a.astype(BF16)

    w_gate, w_up, w_down = bf(ffn_w_gate), bf(ffn_w_up), bf(ffn_w_down)

    def ffn(h, layer, half, mix=None):
        g = norm_g[layer]
        return _ffn(h, g[2 * half * 2], g[2 * half * 2 + 1], w_gate, w_up, w_down, (layer, half), mix)

    for layer in range(depth):
        g = norm_g[layer]
        j = layer // 2
        h = ffn(h, layer, 0)
        if layer % 2 == 0:
            sb, dil = _attn_proj(h, g[2], bf(attn_w_in[j]))
            out_sb = _sb_attention(sb.reshape(bsz, t, -1))
            out_dil = _dil_attention(dil.reshape(bsz, t, -1), rel_bias.astype(F32))
            mix = ([out_sb.reshape(bsz * t, -1), out_dil.reshape(bsz * t, -1)], bf(attn_w_out[j]), g[3])
        else:
            width = mlstm_w_out.shape[1]
            n_head = mlstm_b_gates.shape[1] // 2
            tiles = lambda a: jnp.concatenate(
                [jnp.pad(part, ((0, 0), (0, LANES - n_head)))
                 for part in (a[:, :n_head], a[:, n_head:])], axis=1)
            w_in = bf(mlstm_w_in[j])
            w_gates = tiles(w_in[:, 4 * width:])
            b_gates = tiles(mlstm_b_gates[j].astype(F32).reshape(1, -1))
            q, kt, v, o, gates = _mlstm_proj(h, g[2], w_in, w_gates, mlstm_conv_w[j].astype(F32),
                                             b_gates, seq=t)
            r3 = lambda a: a.reshape(bsz, t, -1)
            hid = _mlstm(r3(q), kt, r3(v), r3(o), gates, mlstm_head_g[j].astype(F32))
            mix = ([hid.reshape(bsz * t, width)], bf(mlstm_w_out[j]), g[3])
        h = ffn(h, layer, 1, mix)
    return h.reshape(bsz, t, d)
```

```python
import functools
import math

import numpy as np
import jax
import jax.numpy as jnp
from jax import lax
from jax.experimental import pallas as pl
from jax.experimental.pallas import tpu as pltpu

EPS = 1e-6
HEAD_DIM_ATTN = 64
DIL_CONFIGS = ((128, 1), (512, 4), (2048, 16))
DIL_BLOCK = 128
DIL_UNROLL = 16
NUM_BUCKETS = 32
MAX_DISTANCE = 2048
N_HEADS_MLSTM = 8
MLSTM_CHUNK = 128
CONV_WIDTH = 4
LANES = 128
CONV_PAD = 8
MASKED = -1e30
V7X_VMEM_BYTES = 64 * 1024 * 1024
VMEM_LIMIT = V7X_VMEM_BYTES * 7 // 8

F32 = jnp.float32
BF16 = jnp.bfloat16


def _params(sem, vmem=VMEM_LIMIT):
    return pltpu.CompilerParams(dimension_semantics=sem, vmem_limit_bytes=vmem)


def _resident(shape):
    zeros = (0,) * len(shape)
    return pl.BlockSpec(shape, lambda *_: zeros, pipeline_mode=pl.Buffered(1))


def _rms(x, g):
    return x * lax.rsqrt(jnp.mean(x * x, axis=-1, keepdims=True) + EPS) * g


def _dot(a, b):
    return jnp.dot(a, b, preferred_element_type=F32)


def _dot_nt(a, b):
    return lax.dot_general(a, b, (((1,), (1,)), ((), ())), preferred_element_type=F32)


def _ffn_kernel(*refs, ff_chunk, n_mix, groups):
    mix_refs, refs = refs[:n_mix], refs[n_mix:]
    if n_mix:
        (wmix_ref, gmix_ref), refs = refs[:2], refs[2:]
    x_ref, gin_ref, gout_ref, wg_ref, wu_ref, wd_ref, o_ref = refs
    d_ff = wg_ref.shape[1]
    rows = x_ref.shape[0] // groups
    spans = [slice(r * rows, (r + 1) * rows) for r in range(groups)]

    xs = []
    for rs in spans:
        x = x_ref[rs, :]
        if n_mix:
            mixed = jnp.concatenate([r[rs, :] for r in mix_refs], axis=-1)
            x = x + _rms(_dot(mixed, wmix_ref[...]), gmix_ref[...])
        xs.append(x)
    xns = [_rms(x, gin_ref[...]).astype(BF16) for x in xs]
    accs = []
    for xn in xns:
        acc = None
        for c in range(d_ff // ff_chunk):
            sl = slice(c * ff_chunk, (c + 1) * ff_chunk)
            gate = _dot(xn, wg_ref[:, sl])
            up = _dot(xn, wu_ref[:, sl])
            h = (gate * jax.nn.sigmoid(gate) * up).astype(BF16)
            part = _dot(h, wd_ref[sl, :])
            acc = part if acc is None else acc + part
        accs.append(acc)
    for rs, x, acc in zip(spans, xs, accs):
        o_ref[rs, :] = x + 0.5 * _rms(acc, gout_ref[...])


def _ffn(x, g_in, g_out, wg, wu, wd, which=(), mix=None, *, tm=1024, groups=2, ff_chunk=256):
    m, d = x.shape
    d_ff = wg.shape[-1]
    assert m % tm == 0 and tm % groups == 0 and d_ff % ff_chunk == 0 and len(which) == wg.ndim - 2
    row = lambda width: pl.BlockSpec((tm, width), lambda i: (i, 0))
    lead = (None,) * len(which)
    weight = lambda shape: pl.BlockSpec(lead + shape, lambda i: tuple(which) + (0, 0),
                                        pipeline_mode=pl.Buffered(1))
    parts, w_mix, g_mix = mix if mix else ((), None, None)
    mix_specs = [row(a.shape[1]) for a in parts] + ([_resident(w_mix.shape), _resident((1, d))] if mix else [])
    mix_args = list(parts) + ([w_mix, g_mix.reshape(1, d)] if mix else [])
    return pl.pallas_call(
        functools.partial(_ffn_kernel, ff_chunk=ff_chunk, n_mix=len(parts), groups=groups),
        grid=(m // tm,),
        in_specs=mix_specs + [row(d), _resident((1, d)), _resident((1, d)),
                              weight((d, d_ff)), weight((d, d_ff)), weight((d_ff, d))],
        out_specs=row(d),
        out_shape=jax.ShapeDtypeStruct((m, d), F32),
        compiler_params=_params(("parallel",)),
        name="ffn",
    )(*mix_args, x, g_in.reshape(1, d), g_out.reshape(1, d), wg, wu, wd)


def _attn_proj_kernel(x_ref, g_ref, w_ref, sb_ref, dil_ref, *, n_chunk, scale):
    xn = _rms(x_ref[...], g_ref[...]).astype(BF16)
    w_sb = sb_ref.shape[1]
    w_q = w_sb // 3
    for c in range(w_sb // n_chunk):
        sl = slice(c * n_chunk, (c + 1) * n_chunk)
        y = _dot(xn, w_ref[:, sl])
        if (c + 1) * n_chunk <= w_q:
            y = y * scale
        sb_ref[:, sl] = y.astype(BF16)
    for c in range(dil_ref.shape[1] // n_chunk):
        sl = slice(c * n_chunk, (c + 1) * n_chunk)
        y = _dot(xn, w_ref[:, w_sb + c * n_chunk: w_sb + (c + 1) * n_chunk])
        if (c + 1) * n_chunk <= w_q:
            y = y * scale
        dil_ref[:, sl] = y


def _attn_proj(x, g, w, *, tm=512, n_chunk=512):
    m, d = x.shape
    n = w.shape[1]
    half = n // 2
    assert m % tm == 0 and (half // 3) % n_chunk == 0
    scale = 1.0 / math.sqrt(HEAD_DIM_ATTN)
    return pl.pallas_call(
        functools.partial(_attn_proj_kernel, n_chunk=n_chunk, scale=scale),
        grid=(m // tm,),
        in_specs=[pl.BlockSpec((tm, d), lambda i: (i, 0)), _resident((1, d)), _resident((d, n))],
        out_specs=[pl.BlockSpec((tm, half), lambda i: (i, 0)),
                   pl.BlockSpec((tm, half), lambda i: (i, 0))],
        out_shape=[jax.ShapeDtypeStruct((m, half), BF16), jax.ShapeDtypeStruct((m, half), F32)],
        compiler_params=_params(("parallel",)),
        name="attn_proj",
    )(x, g.reshape(1, d), w)


LOG2E = math.log2(math.e)
SB_DEAD_LOG2 = -160.0


def _sb_kernel(q_ref, k_ref, v_ref, tri_ref, o_ref, *, blk, q_blocks):
    first_q = pl.program_id(2) * q_blocks
    lane = lax.broadcasted_iota(jnp.int32, (1, LANES), 1)
    row = lax.broadcasted_iota(jnp.int32, (blk, blk), 0)
    col = lax.broadcasted_iota(jnp.int32, (blk, blk), 1)
    causal = col < row
    causal2 = jnp.concatenate([causal, causal], axis=0)
    tri = tri_ref[...]
    head0 = lane < HEAD_DIM_ATTN

    def stacked(q2):
        zeros = jnp.zeros_like(q2)
        return jnp.concatenate([jnp.where(head0, q2, zeros), jnp.where(head0, zeros, q2)], axis=0)

    def pairs(jobs):
        chains = []
        for j, (qq, kb, diag, _) in enumerate(jobs):
            chains.append((j, pl.multiple_of(kb * blk, blk), causal2 if diag else None))
            chains.append((j, pl.multiple_of(jnp.maximum(kb - 1, 0) * blk, blk), kb >= 1))
        states = [job[3] for job in jobs]
        zs, mids = {}, {}
        for n in range(len(chains) + 2):
            if n < len(chains):
                j, ks, _ = chains[n]
                zs[n] = _dot_nt(jobs[j][0], k_ref[pl.ds(ks, blk), :]) * LOG2E
            if 0 <= n - 1 < len(chains):
                z, mask = zs.pop(n - 1), chains[n - 1][2]
                neg = -z
                log_keep = jnp.minimum(neg, 0.0) - jnp.log2(1.0 + jnp.exp2(jnp.minimum(z, neg)))
                log_beta = z + log_keep
                if mask is not None:
                    log_keep = jnp.where(mask, log_keep, 0.0)
                    log_beta = jnp.where(mask, log_beta, MASKED)
                hi = log_keep.astype(BF16)
                lo = (log_keep - hi.astype(F32)).astype(BF16)
                later = _dot(hi, tri) + _dot(lo, tri)
                mids[n - 1] = (log_beta, later, jnp.sum(log_keep, axis=-1, keepdims=True))
            if 0 <= n - 2 < len(chains):
                j, ks, _ = chains[n - 2]
                log_beta, later, total = mids.pop(n - 2)
                carry, acc = states[j]
                p = jnp.exp2(log_beta + later + carry)
                states[j] = (carry + total, acc + _dot(p.astype(BF16), v_ref[pl.ds(ks, blk), :]))
        return states

    def alive(state):
        return (jnp.max(state[0]) > SB_DEAD_LOG2).astype(jnp.int32)

    zero = (jnp.zeros((2 * blk, 1), F32), jnp.zeros((2 * blk, LANES), F32))
    qqs = [stacked(q_ref[g * blk:(g + 1) * blk, :]) for g in range(q_blocks)]
    states = pairs([(qq, first_q + g, True, zero) for g, qq in enumerate(qqs)])

    def store(g, state):
        o_ref[g * blk:(g + 1) * blk, :] = jnp.where(head0, state[1][:blk], state[1][blk:]).astype(o_ref.dtype)

    live = [alive(state) for state in states]
    for g, state in enumerate(states):
        store(g, state)

    for g, (qq, state) in enumerate(zip(qqs, states)):
        i = first_q + g
        n_pairs = lax.shift_right_logical(i, 1)

        @pl.when((live[g] > 0) & (n_pairs > 0))
        def _(g=g, qq=qq, state=state, i=i, n_pairs=n_pairs):
            def cond(loop):
                t, alive_now, _ = loop
                return (t < n_pairs) & (alive_now > 0)

            def body(loop):
                t, _, state = loop
                state, = pairs([(qq, i - 2 - 2 * t, False, state)])
                return t + 1, alive(state), state

            _, _, final = lax.while_loop(cond, body, (jnp.int32(0), live[g], state))
            store(g, final)


def _sb_attention(qkv, *, blk=256, q_blocks=4):
    b, t, w3 = qkv.shape
    w = w3 // 3
    pairs = w // LANES
    blk = min(blk, t)
    rows = blk * q_blocks
    assert t % rows == 0
    tri = jnp.asarray(np.tril(np.ones((blk, blk), np.float32), -1), BF16)
    return pl.pallas_call(
        functools.partial(_sb_kernel, blk=blk, q_blocks=q_blocks),
        grid=(b, pairs, t // rows),
        in_specs=[pl.BlockSpec((None, rows, LANES), lambda bi, p, i: (bi, i, p)),
                  pl.BlockSpec((None, t, LANES), lambda bi, p, i: (bi, 0, pairs + p)),
                  pl.BlockSpec((None, t, LANES), lambda bi, p, i: (bi, 0, 2 * pairs + p)),
                  _resident((blk, blk))],
        out_specs=pl.BlockSpec((None, rows, LANES), lambda bi, p, i: (bi, i, p)),
        out_shape=jax.ShapeDtypeStruct((b, t, w), BF16),
        compiler_params=_params(("parallel", "parallel", "arbitrary")),
        name="sb_attn",
    )(qkv, qkv, qkv, tri)


def _t5_bucket_np(dist):
    max_exact = NUM_BUCKETS // 2
    d = np.maximum(dist, 1).astype(np.float32)
    log_b = max_exact + (np.log(d / np.float32(max_exact)) / np.float32(math.log(MAX_DISTANCE / max_exact))
                         * np.float32(NUM_BUCKETS - max_exact)).astype(np.int32)
    log_b = np.minimum(log_b, NUM_BUCKETS - 1)
    return np.where(dist < max_exact, dist, log_b)


def _dil_tables():
    qi = np.arange(DIL_BLOCK)[:, None]
    ki = np.arange(2 * DIL_BLOCK)[None, :]
    dist = qi + DIL_BLOCK - ki
    buckets, valid = [], []
    for window, dil in DIL_CONFIGS:
        steps = window // dil
        buckets.append(_t5_bucket_np(np.maximum(dist, 0) * dil))
        valid.append((dist >= 0) & (dist <= steps))
    return np.stack(buckets).astype(np.int32), np.stack(valid).astype(np.int32)


def _dil_kernel(rb_ref, bucket_ref, valid_ref, q_ref, k_ref, v_ref, o_ref,
                bias_scr, num_scr, m_scr, l_scr, *, seq):
    p = pl.program_id(0)
    qb = DIL_BLOCK
    lane = lax.broadcasted_iota(jnp.int32, (1, LANES), 1)
    head0 = lane < HEAD_DIM_ATTN
    first_half = lax.broadcasted_iota(jnp.int32, (1, 2 * qb), 1) < qb

    @pl.when(pl.program_id(1) == 0)
    def _():
        for br in range(len(DIL_CONFIGS)):
            bucket = bucket_ref[br]
            valid = valid_ref[br] > 0
            for h in range(2):
                bias = jnp.zeros((qb, 2 * qb), F32)
                for b in range(NUM_BUCKETS):
                    bias = jnp.where(bucket == b, rb_ref[b, 2 * p + h], bias)
                bias_scr[br, h * qb:(h + 1) * qb, :] = jnp.where(valid, bias, MASKED)

    for br, (_, dil) in enumerate(DIL_CONFIGS):
        n_units = seq // qb

        n_blocks = seq // (qb * dil)
        run = min(DIL_UNROLL, n_blocks)
        runs = DIL_UNROLL // run
        assert DIL_UNROLL % run == 0 and n_blocks % run == 0 and (dil == 1 or n_blocks == run)

        def rows(ref, start, dil=dil):
            if dil == 1:
                return ref[pl.ds(start, qb), :].astype(BF16)
            return ref[pl.ds(start, qb, stride=dil), :].astype(BF16)

        def load_run(rho, dil=dil, run=run, whole=(n_blocks == run)):
            res, base = (rho, 0) if whole else (0, rho * run)
            starts = [(base + i) * (qb * dil) + res for i in range(run)]
            kb = [rows(k_ref, st) for st in starts]
            vb = [rows(v_ref, st) for st in starts]
            if whole:
                k_prev, v_prev = kb[0], vb[0]
                pen = jnp.where(first_half, MASKED, 0.0)
            else:
                st = jnp.maximum(base - 1, 0) * (qb * dil) + res
                k_prev, v_prev = rows(k_ref, st), rows(v_ref, st)
                pen = jnp.where(first_half, jnp.where(base == 0, MASKED, 0.0), 0.0)
            units = []
            for i, st in enumerate(starts):
                q2 = rows(q_ref, st)
                zeros = jnp.zeros_like(q2)
                qq = jnp.concatenate([jnp.where(head0, q2, zeros), jnp.where(head0, zeros, q2)], axis=0)
                k2 = jnp.concatenate([kb[i - 1] if i else k_prev, kb[i]], axis=0)
                v2 = jnp.concatenate([vb[i - 1] if i else v_prev, vb[i]], axis=0)
                units.append((st, pen if i == 0 else None, qq, k2, v2))
            return units

        def softmax_parts(s):
            m = jnp.max(s, axis=-1, keepdims=True)
            e = jnp.exp(s - m)
            return m, e, jnp.sum(e, axis=-1, keepdims=True)

        def group(g, _, br=br, dil=dil, runs=runs):
            units = [u for r in range(runs) for u in load_run(g * runs + r)]
            scores, parts = {}, {}
            for j in range(DIL_UNROLL + 2):
                if j < DIL_UNROLL:
                    _, pen, qq, k2, _ = units[j]
                    scores[j] = _dot_nt(qq, k2) + bias_scr[br]
                    if pen is not None:
                        scores[j] = scores[j] + pen
                if 0 <= j - 1 < DIL_UNROLL:
                    parts[j - 1] = softmax_parts(scores.pop(j - 1))
                if 0 <= j - 2 < DIL_UNROLL:
                    st, _, _, _, v2 = units[j - 2]
                    m, e, l = parts.pop(j - 2)
                    pv = _dot(e.astype(BF16), v2)
                    idx = pl.ds(st, qb) if dil == 1 else pl.ds(st, qb, stride=dil)
                    num_scr[br, idx, :] = jnp.where(head0, pv[:qb], pv[qb:])
                    m_scr[br, idx, :] = jnp.where(head0, m[:qb], m[qb:])
                    l_scr[br, idx, :] = jnp.where(head0, l[:qb], l[qb:])
            return 0

        lax.fori_loop(0, n_units // DIL_UNROLL, group, 0)

    rows_out = 256

    def finish(c, _):
        idx = pl.ds(pl.multiple_of(c * rows_out, rows_out), rows_out)
        m_all = [m_scr[br, idx, :] for br in range(len(DIL_CONFIGS))]
        m_max = functools.reduce(jnp.maximum, m_all)
        num = den = None
        for br, m_br in enumerate(m_all):
            wt = jnp.exp(m_br - m_max)
            n_br = wt * num_scr[br, idx, :]
            d_br = wt * l_scr[br, idx, :]
            num = n_br if num is None else num + n_br
            den = d_br if den is None else den + d_br
        o_ref[idx, :] = (num / den).astype(o_ref.dtype)
        return 0

    lax.fori_loop(0, seq // rows_out, finish, 0)


def _dil_attention(qkv, rel_bias):
    b, t, w3 = qkv.shape
    w = w3 // 3
    pairs = w // LANES
    assert t % (DIL_BLOCK * max(d for _, d in DIL_CONFIGS)) == 0 and t % 256 == 0
    bucket, valid = _dil_tables()
    nbr = len(DIL_CONFIGS)
    seq_spec = lambda off: pl.BlockSpec((None, t, LANES), lambda p, bi: (bi, 0, off + p))
    return pl.pallas_call(
        functools.partial(_dil_kernel, seq=t),
        grid=(pairs, b),
        in_specs=[pl.BlockSpec(memory_space=pltpu.SMEM),
                  _resident((nbr, DIL_BLOCK, 2 * DIL_BLOCK)),
                  _resident((nbr, DIL_BLOCK, 2 * DIL_BLOCK)),
                  seq_spec(0), seq_spec(pairs), seq_spec(2 * pairs)],
        out_specs=pl.BlockSpec((None, t, LANES), lambda p, bi: (bi, 0, p)),
        out_shape=jax.ShapeDtypeStruct((b, t, w), BF16),
        scratch_shapes=[pltpu.VMEM((nbr, 2 * DIL_BLOCK, 2 * DIL_BLOCK), F32),
                        pltpu.VMEM((nbr, t, LANES), F32), pltpu.VMEM((nbr, t, LANES), F32),
                        pltpu.VMEM((nbr, t, LANES), F32)],
        compiler_params=_params(("arbitrary", "arbitrary")),
        name="dil_attn",
    )(rel_bias, jnp.asarray(bucket), jnp.asarray(valid), qkv, qkv, qkv)


def _mlstm_proj_kernel(x_ref, g_ref, w_ref, wg_ref, cw_ref, bg_ref, q_ref, kt_ref, v_ref, o_ref,
                       gate_ref, *hist_scrs, tiles_per_seq, n_chunk, scale):
    i = pl.program_id(0)
    tm = x_ref.shape[0]
    width = q_ref.shape[1]
    xn = _rms(x_ref[...], g_ref[...]).astype(BF16)

    def even_rows(first, n):
        return pl.ds(2 * first, n, stride=2)

    slabs = n_chunk // LANES
    n_conv = 2 * width // n_chunk
    assert len(hist_scrs) == n_conv

    @pl.when(i % tiles_per_seq == 0)
    def _():
        for hist in hist_scrs:
            hist[:, 0:2 * CONV_PAD, :] = jnp.zeros((slabs, 2 * CONV_PAD, LANES), F32)

    def project(c):
        pre = _dot(xn, w_ref[:, c * n_chunk:(c + 1) * n_chunk])
        for s in range(slabs):
            hist_scrs[c][s, even_rows(CONV_PAD, tm), :] = pre[:, s * LANES:(s + 1) * LANES]

    def conv(c):
        sl = slice(c * n_chunk, (c + 1) * n_chunk)
        hist = hist_scrs[c]
        parts = []
        for s in range(slabs):
            y = None
            for tap in range(CONV_WIDTH):
                off = CONV_PAD - (CONV_WIDTH - 1) + tap
                w_tap = cw_ref[tap:tap + 1, (c * slabs + s) * LANES:(c * slabs + s + 1) * LANES]
                term = w_tap * hist[s, even_rows(off, tm), :]
                y = term if y is None else y + term
            parts.append(y)
            hist[s, even_rows(0, CONV_PAD), :] = hist[s, even_rows(tm, CONV_PAD), :]
        y = jnp.concatenate(parts, axis=1)
        y = y * jax.nn.sigmoid(y)
        if c * n_chunk < width:
            q_ref[:, sl] = (y * scale).astype(BF16)
        else:
            kt_ref[c * n_chunk - width:(c + 1) * n_chunk - width, :] = y.T.astype(BF16)

    def plain(c):
        sl = slice((c // 2) * n_chunk, (c // 2 + 1) * n_chunk)
        if c % 2 == 0:
            v_ref[:, sl] = _dot(xn, w_ref[:, 2 * width + sl.start:2 * width + sl.stop]).astype(BF16)
        else:
            o_ref[:, sl] = _dot(xn, w_ref[:, 3 * width + sl.start:3 * width + sl.stop])

    n_plain = 2 * width // n_chunk
    for c in range(n_conv + 1):
        if c < n_conv:
            project(c)
        if c >= 1:
            conv(c - 1)
            if c - 1 < n_plain:
                plain(c - 1)
    for c in range(n_conv, n_plain):
        plain(c)
    gate_ref[...] = _dot(xn, wg_ref[...]) + bg_ref[...]


def _mlstm_proj(x, g, w_in, w_gates, conv_w, b_gates, *, seq, tm=512, n_chunk=512):
    m, d = x.shape
    width = conv_w.shape[1] // 2
    assert m % tm == 0 and seq % tm == 0 and width % n_chunk == 0 and w_in.shape[1] >= 4 * width
    row = lambda wd: pl.BlockSpec((tm, wd), lambda i: (i, 0))
    scale = 1.0 / math.sqrt(width // N_HEADS_MLSTM)
    return pl.pallas_call(
        functools.partial(_mlstm_proj_kernel, tiles_per_seq=seq // tm, n_chunk=n_chunk, scale=scale),
        grid=(m // tm,),
        in_specs=[row(d), _resident((1, d)), _resident(w_in.shape), _resident((d, 2 * LANES)),
                  _resident(conv_w.shape), _resident((1, 2 * LANES))],
        out_specs=[row(width), pl.BlockSpec((width, tm), lambda i: (0, i)), row(width), row(width),
                   row(2 * LANES)],
        out_shape=[jax.ShapeDtypeStruct((m, width), BF16), jax.ShapeDtypeStruct((width, m), BF16),
                   jax.ShapeDtypeStruct((m, width), BF16), jax.ShapeDtypeStruct((m, width), F32),
                   jax.ShapeDtypeStruct((m, 2 * LANES), F32)],
        scratch_shapes=[pltpu.VMEM((n_chunk // LANES, 2 * (tm + CONV_PAD), LANES), F32)
                        for _ in range(2 * width // n_chunk)],
        compiler_params=_params(("arbitrary",)),
        name="mlstm_proj",
    )(x, g.reshape(1, d), w_in, w_gates, conv_w, b_gates)


def _split3(x):
    hi = x.astype(BF16)
    r = x - hi.astype(F32)
    mid = r.astype(BF16)
    return hi, mid, (r - mid.astype(F32)).astype(BF16)


def _gate_prep_kernel(g_ref, cols_ref, rows_ref, *, heads, cl):
    tm = g_ref.shape[0]
    row = lax.broadcasted_iota(jnp.int32, (cl, cl), 0)
    col = lax.broadcasted_iota(jnp.int32, (cl, cl), 1)
    incl = jnp.where(row >= col, 1.0, 0.0).astype(BF16)
    lane = lax.broadcasted_iota(jnp.int32, (1, cl), 1)
    pad = jnp.zeros((cl - heads, cl), F32)
    spans = [slice(c * cl, (c + 1) * cl) for c in range(tm // cl)]
    bs = []
    for r in spans:
        b = None
        for part in _split3(jax.nn.log_sigmoid(g_ref[r, LANES:])):
            term = _dot(incl, part)
            b = term if b is None else b + term
        bs.append(b)
    us = [g_ref[r, :LANES] - b for r, b in zip(spans, bs)]
    u_rows = jnp.concatenate([u.T[0:heads, :] for u in us], axis=0)
    b_rows = jnp.concatenate([b.T[0:heads, :] for b in bs], axis=0)
    cmax = u_rows
    shift = 1
    while shift < cl:
        cmax = jnp.where(lane >= shift, jnp.maximum(cmax, pltpu.roll(cmax, shift, axis=1)), cmax)
        shift *= 2
    u_max = jnp.broadcast_to(jnp.max(u_rows, axis=1, keepdims=True), u_rows.shape)
    b_last = jnp.broadcast_to(jnp.min(b_rows, axis=1, keepdims=True), b_rows.shape)
    for c, (r, u, b) in enumerate(zip(spans, us, bs)):
        hs = slice(c * heads, (c + 1) * heads)
        cols_ref[r, :] = jnp.concatenate([u, b, jnp.concatenate([cmax[hs], pad], axis=0).T], axis=1)
        rows_ref[:, r] = jnp.concatenate([u_rows[hs], u_max[hs], b_last[hs]], axis=0)


def _gate_prep(gates, *, heads, cl, tm=1024):
    m = gates.shape[0]
    tm = min(tm, m)
    assert m % tm == 0 and tm % cl == 0 and cl == LANES and heads == 8
    return pl.pallas_call(
        functools.partial(_gate_prep_kernel, heads=heads, cl=cl),
        grid=(m // tm,),
        in_specs=[pl.BlockSpec((tm, 2 * LANES), lambda i: (i, 0))],
        out_specs=[pl.BlockSpec((tm, 3 * LANES), lambda i: (i, 0)),
                   pl.BlockSpec((3 * heads, tm), lambda i: (0, i))],
        out_shape=[jax.ShapeDtypeStruct((m, 3 * LANES), F32),
                   jax.ShapeDtypeStruct((3 * heads, m), F32)],
        compiler_params=_params(("parallel",)),
        name="gate_prep",
    )(gates)


def _mlstm_kernel(q_ref, kt_ref, v_ref, o_ref, gcol_ref, grow_ref, hg_ref, out_ref,
                  s_scr, m_scr, ml_scr, *, heads):
    cl = MLSTM_CHUNK
    n_chunks = q_ref.shape[0] // cl
    dh = q_ref.shape[1] // heads

    @pl.when(pl.program_id(1) == 0)
    def _():
        s_scr[...] = jnp.zeros_like(s_scr)
        m_scr[...] = jnp.zeros_like(m_scr)
        ml_scr[...] = jnp.zeros_like(ml_scr)

    row = lax.broadcasted_iota(jnp.int32, (cl, cl), 0)
    col = lax.broadcasted_iota(jnp.int32, (cl, cl), 1)
    lower = row >= col
    ones = jnp.ones((cl, dh), BF16)

    m_rows = m_scr[...]
    m_lane = ml_scr[0:1, :]
    tables = []
    for c in range(n_chunks):
        span = slice(c * cl, (c + 1) * cl)
        u_cols, b_cols, cmax_cols = (gcol_ref[span, t * LANES:(t + 1) * LANES] for t in range(3))
        u_rows, u_max_rows, b_last_rows = (grow_ref[t * heads:(t + 1) * heads, span] for t in range(3))
        mm_last_rows = jnp.maximum(m_rows, u_max_rows)
        mm_cols = jnp.maximum(cmax_cols, m_lane)
        tables.append(dict(
            span=span, u_rows=u_rows, m_prev_rows=m_rows, mm_cols=mm_cols,
            decay_rows=jnp.exp(m_rows - mm_last_rows),
            ws_rows=jnp.exp(u_rows - mm_last_rows),
            floor_cols=jnp.exp(-(b_cols + mm_cols))))
        m_rows = b_last_rows + mm_last_rows
        m_lane = b_cols[cl - 1:cl, :] + mm_cols[cl - 1:cl, :]
    m_scr[...] = m_rows
    ml_scr[...] = jnp.broadcast_to(m_lane, ml_scr.shape)

    def lanes_of(mat, c):
        return jnp.broadcast_to(mat[:, c:c + 1], (cl, LANES))

    def stage_a(tab, h):
        sl = slice(h * dh, (h + 1) * dh)
        q, kt = q_ref[tab["span"], sl], kt_ref[sl, tab["span"]]
        v1 = jnp.concatenate([v_ref[tab["span"], sl], ones], axis=1)
        s_prev = s_scr[h]
        qk = _dot(q, kt)
        qs = _dot(q, s_prev.astype(BF16))
        kw = (kt.astype(F32) * tab["ws_rows"][h:h + 1, :]).astype(BF16)
        decay = jnp.broadcast_to(tab["decay_rows"][h:h + 1, :], (dh, LANES))
        s_scr[h] = jnp.concatenate([decay, decay], axis=1) * s_prev + _dot(kw, v1)
        return tab, h, sl, v1, qk, qs

    def stage_b(tab, h, sl, v1, qk, qs):
        mm = lanes_of(tab["mm_cols"], h)
        weight = jnp.where(lower, jnp.exp(tab["u_rows"][h:h + 1, :] - mm), 0.0)
        w_inter = jnp.exp(tab["m_prev_rows"][h:h + 1, :] - mm)
        return tab, sl, v1, qs, (qk * weight).astype(BF16), w_inter, lanes_of(tab["floor_cols"], h)

    def stage_c(tab, sl, v1, qs, p, w_inter, floor):
        pv = _dot(p, v1)
        both = pv + jnp.concatenate([w_inter, w_inter], axis=1) * qs
        hid = both[:, :dh] / jnp.maximum(jnp.abs(both[:, dh:]), floor)
        hid = hid * lax.rsqrt(jnp.mean(hid * hid, axis=-1, keepdims=True) + EPS) * hg_ref[:, sl]
        out_ref[tab["span"], sl] = (hid * jax.nn.sigmoid(o_ref[tab["span"], sl])).astype(out_ref.dtype)

    jobs = [(tab, h) for tab in tables for h in range(heads)]
    after_a, after_b = {}, {}
    for j in range(len(jobs) + 2):
        if j < len(jobs):
            after_a[j] = stage_a(*jobs[j])
        if 0 <= j - 1 < len(jobs):
            after_b[j - 1] = stage_b(*after_a.pop(j - 1))
        if 0 <= j - 2 < len(jobs):
            stage_c(*after_b.pop(j - 2))


def _mlstm(q, kt, v, o, gates, head_g, *, heads=N_HEADS_MLSTM, chunks=4):
    b, t, w = q.shape
    cl = MLSTM_CHUNK
    dh = w // heads
    rows = chunks * cl
    nc = t // rows
    assert t % rows == 0 and dh == LANES and cl == LANES
    gcols, grows = _gate_prep(gates, heads=heads, cl=cl)
    blk = lambda wd: pl.BlockSpec((None, rows, wd), lambda bi, c: (bi, c, 0))
    return pl.pallas_call(
        functools.partial(_mlstm_kernel, heads=heads),
        grid=(b, nc),
        in_specs=[blk(w), pl.BlockSpec((w, rows), lambda bi, c: (0, bi * nc + c)), blk(w), blk(w),
                  pl.BlockSpec((rows, 3 * LANES), lambda bi, c: (bi * nc + c, 0)),
                  pl.BlockSpec((3 * heads, rows), lambda bi, c: (0, bi * nc + c)),
                  _resident((1, w))],
        out_specs=blk(w),
        out_shape=jax.ShapeDtypeStruct((b, t, w), BF16),
        scratch_shapes=[pltpu.VMEM((heads, dh, 2 * dh), F32), pltpu.VMEM((heads, LANES), F32),
                        pltpu.VMEM((heads, LANES), F32)],
        compiler_params=_params(("parallel", "arbitrary")),
        name="mlstm",
    )(q, kt, v, o, gcols, grows, head_g.reshape(1, w))


def kernel(x, norm_g, ffn_w_gate, ffn_w_up, ffn_w_down, attn_w_in, attn_w_out, rel_bias,
           mlstm_w_in, mlstm_b_gates, mlstm_conv_w, mlstm_head_g, mlstm_w_out):
    bsz, t, d = x.shape
    depth = norm_g.shape[0]
    h = x.reshape(bsz * t, d)
    bf = lambda a: a.astype(BF16)

    w_gate, w_up, w_down = bf(ffn_w_gate), bf(ffn_w_up), bf(ffn_w_down)

    def ffn(h, layer, half, mix=None):
        g = norm_g[layer]
        return _ffn(h, g[2 * half * 2], g[2 * half * 2 + 1], w_gate, w_up, w_down, (layer, half), mix)

    for layer in range(depth):
        g = norm_g[layer]
        j = layer // 2
        h = ffn(h, layer, 0)
        if layer % 2 == 0:
            sb, dil = _attn_proj(h, g[2], bf(attn_w_in[j]))
            out_sb = _sb_attention(sb.reshape(bsz, t, -1))
            out_dil = _dil_attention(dil.reshape(bsz, t, -1), rel_bias.astype(F32))
            mix = ([out_sb.reshape(bsz * t, -1), out_dil.reshape(bsz * t, -1)], bf(attn_w_out[j]), g[3])
        else:
            width = mlstm_w_out.shape[1]
            n_head = mlstm_b_gates.shape[1] // 2
            tiles = lambda a: jnp.concatenate(
                [jnp.pad(part, ((0, 0), (0, LANES - n_head)))
                 for part in (a[:, :n_head], a[:, n_head:])], axis=1)
            w_in = bf(mlstm_w_in[j])
            w_gates = tiles(w_in[:, 4 * width:])
            b_gates = tiles(mlstm_b_gates[j].astype(F32).reshape(1, -1))
            q, kt, v, o, gates = _mlstm_proj(h, g[2], w_in, w_gates, mlstm_conv_w[j].astype(F32),
                                             b_gates, seq=t)
            r3 = lambda a: a.reshape(bsz, t, -1)
            hid = _mlstm(r3(q), kt, r3(v), r3(o), gates, mlstm_head_g[j].astype(F32))
            mix = ([hid.reshape(bsz * t, width)], bf(mlstm_w_out[j]), g[3])
        h = ffn(h, layer, 1, mix)
    return h.reshape(bsz, t, d)
```

```python
import functools
import math

import numpy as np
import jax
import jax.numpy as jnp
from jax import lax
from jax.experimental import pallas as pl
from jax.experimental.pallas import tpu as pltpu

EPS = 1e-6
HEAD_DIM_ATTN = 64
DIL_CONFIGS = ((128, 1), (512, 4), (2048, 16))
DIL_BLOCK = 128
DIL_UNROLL = 16
NUM_BUCKETS = 32
MAX_DISTANCE = 2048
N_HEADS_MLSTM = 8
MLSTM_CHUNK = 128
CONV_WIDTH = 4
LANES = 128
CONV_PAD = 8
MASKED = -1e30
V7X_VMEM_BYTES = 64 * 1024 * 1024
VMEM_LIMIT = V7X_VMEM_BYTES * 7 // 8

F32 = jnp.float32
BF16 = jnp.bfloat16


def _params(sem, vmem=VMEM_LIMIT):
    return pltpu.CompilerParams(dimension_semantics=sem, vmem_limit_bytes=vmem)


def _resident(shape):
    zeros = (0,) * len(shape)
    return pl.BlockSpec(shape, lambda *_: zeros, pipeline_mode=pl.Buffered(1))


def _rms(x, g):
    return x * lax.rsqrt(jnp.mean(x * x, axis=-1, keepdims=True) + EPS) * g


def _dot(a, b):
    return jnp.dot(a, b, preferred_element_type=F32)


def _dot_nt(a, b):
    return lax.dot_general(a, b, (((1,), (1,)), ((), ())), preferred_element_type=F32)


def _ffn_kernel(*refs, ff_chunk, n_mix, groups):
    mix_refs, refs = refs[:n_mix], refs[n_mix:]
    if n_mix:
        (wmix_ref, gmix_ref), refs = refs[:2], refs[2:]
    x_ref, gin_ref, gout_ref, wg_ref, wu_ref, wd_ref, o_ref = refs
    d_ff = wg_ref.shape[1]
    rows = x_ref.shape[0] // groups
    spans = [slice(r * rows, (r + 1) * rows) for r in range(groups)]

    xs = []
    for rs in spans:
        x = x_ref[rs, :]
        if n_mix:
            mixed = jnp.concatenate([r[rs, :] for r in mix_refs], axis=-1)
            x = x + _rms(_dot(mixed, wmix_ref[...]), gmix_ref[...])
        xs.append(x)
    xns = [_rms(x, gin_ref[...]).astype(BF16) for x in xs]
    accs = []
    for xn in xns:
        acc = None
        for c in range(d_ff // ff_chunk):
            sl = slice(c * ff_chunk, (c + 1) * ff_chunk)
            gate = _dot(xn, wg_ref[:, sl])
            up = _dot(xn, wu_ref[:, sl])
            h = (gate * jax.nn.sigmoid(gate) * up).astype(BF16)
            part = _dot(h, wd_ref[sl, :])
            acc = part if acc is None else acc + part
        accs.append(acc)
    for rs, x, acc in zip(spans, xs, accs):
        o_ref[rs, :] = x + 0.5 * _rms(acc, gout_ref[...])


def _ffn(x, g_in, g_out, wg, wu, wd, which=(), mix=None, *, tm=1024, groups=2, ff_chunk=256):
    m, d = x.shape
    d_ff = wg.shape[-1]
    assert m % tm == 0 and tm % groups == 0 and d_ff % ff_chunk == 0 and len(which) == wg.ndim - 2
    row = lambda width: pl.BlockSpec((tm, width), lambda i: (i, 0))
    lead = (None,) * len(which)
    weight = lambda shape: pl.BlockSpec(lead + shape, lambda i: tuple(which) + (0, 0),
                                        pipeline_mode=pl.Buffered(1))
    parts, w_mix, g_mix = mix if mix else ((), None, None)
    mix_specs = [row(a.shape[1]) for a in parts] + ([_resident(w_mix.shape), _resident((1, d))] if mix else [])
    mix_args = list(parts) + ([w_mix, g_mix.reshape(1, d)] if mix else [])
    return pl.pallas_call(
        functools.partial(_ffn_kernel, ff_chunk=ff_chunk, n_mix=len(parts), groups=groups),
        grid=(m // tm,),
        in_specs=mix_specs + [row(d), _resident((1, d)), _resident((1, d)),
                              weight((d, d_ff)), weight((d, d_ff)), weight((d_ff, d))],
        out_specs=row(d),
        out_shape=jax.ShapeDtypeStruct((m, d), F32),
        compiler_params=_params(("parallel",)),
        name="ffn",
    )(*mix_args, x, g_in.reshape(1, d), g_out.reshape(1, d), wg, wu, wd)


def _attn_proj_kernel(x_ref, g_ref, w_ref, sb_ref, dil_ref, *, n_chunk, scale, groups):
    w_sb = sb_ref.shape[1]
    w_q = w_sb // 3
    rows = x_ref.shape[0] // groups
    spans = [slice(r * rows, (r + 1) * rows) for r in range(groups)]
    xns = [_rms(x_ref[rs, :], g_ref[...]).astype(BF16) for rs in spans]
    for rs, xn in zip(spans, xns):
        for c in range(w_sb // n_chunk):
            sl = slice(c * n_chunk, (c + 1) * n_chunk)
            y = _dot(xn, w_ref[:, sl])
            if (c + 1) * n_chunk <= w_q:
                y = y * scale
            sb_ref[rs, sl] = y.astype(BF16)
        for c in range(dil_ref.shape[1] // n_chunk):
            sl = slice(c * n_chunk, (c + 1) * n_chunk)
            y = _dot(xn, w_ref[:, w_sb + c * n_chunk: w_sb + (c + 1) * n_chunk])
            if (c + 1) * n_chunk <= w_q:
                y = y * scale
            dil_ref[rs, sl] = y


def _attn_proj(x, g, w, *, tm=1024, n_chunk=512, groups=2):
    m, d = x.shape
    n = w.shape[1]
    half = n // 2
    assert m % tm == 0 and (half // 3) % n_chunk == 0
    scale = 1.0 / math.sqrt(HEAD_DIM_ATTN)
    return pl.pallas_call(
        functools.partial(_attn_proj_kernel, n_chunk=n_chunk, scale=scale, groups=groups),
        grid=(m // tm,),
        in_specs=[pl.BlockSpec((tm, d), lambda i: (i, 0)), _resident((1, d)), _resident((d, n))],
        out_specs=[pl.BlockSpec((tm, half), lambda i: (i, 0)),
                   pl.BlockSpec((tm, half), lambda i: (i, 0))],
        out_shape=[jax.ShapeDtypeStruct((m, half), BF16), jax.ShapeDtypeStruct((m, half), F32)],
        compiler_params=_params(("parallel",)),
        name="attn_proj",
    )(x, g.reshape(1, d), w)


LOG2E = math.log2(math.e)
SB_DEAD_LOG2 = -160.0


def _sb_kernel(q_ref, k_ref, v_ref, tri_ref, o_ref, *, blk, q_blocks):
    first_q = pl.program_id(2) * q_blocks
    lane = lax.broadcasted_iota(jnp.int32, (1, LANES), 1)
    row = lax.broadcasted_iota(jnp.int32, (blk, blk), 0)
    col = lax.broadcasted_iota(jnp.int32, (blk, blk), 1)
    causal = col < row
    causal2 = jnp.concatenate([causal, causal], axis=0)
    tri = tri_ref[...]
    head0 = lane < HEAD_DIM_ATTN

    def stacked(q2):
        zeros = jnp.zeros_like(q2)
        return jnp.concatenate([jnp.where(head0, q2, zeros), jnp.where(head0, zeros, q2)], axis=0)

    def pairs(jobs):
        chains = []
        for j, (qq, kb, diag, _) in enumerate(jobs):
            chains.append((j, pl.multiple_of(kb * blk, blk), causal2 if diag else None))
            chains.append((j, pl.multiple_of(jnp.maximum(kb - 1, 0) * blk, blk), kb >= 1))
        states = [job[3] for job in jobs]
        zs, mids = {}, {}
        for n in range(len(chains) + 2):
            if n < len(chains):
                j, ks, _ = chains[n]
                zs[n] = _dot_nt(jobs[j][0], k_ref[pl.ds(ks, blk), :]) * LOG2E
            if 0 <= n - 1 < len(chains):
                z, mask = zs.pop(n - 1), chains[n - 1][2]
                neg = -z
                log_keep = jnp.minimum(neg, 0.0) - jnp.log2(1.0 + jnp.exp2(jnp.minimum(z, neg)))
                log_beta = z + log_keep
                if mask is not None:
                    log_keep = jnp.where(mask, log_keep, 0.0)
                    log_beta = jnp.where(mask, log_beta, MASKED)
                hi = log_keep.astype(BF16)
                lo = (log_keep - hi.astype(F32)).astype(BF16)
                later = _dot(hi, tri) + _dot(lo, tri)
                mids[n - 1] = (log_beta, later, jnp.sum(log_keep, axis=-1, keepdims=True))
            if 0 <= n - 2 < len(chains):
                j, ks, _ = chains[n - 2]
                log_beta, later, total = mids.pop(n - 2)
                carry, acc = states[j]
                p = jnp.exp2(log_beta + later + carry)
                states[j] = (carry + total, acc + _dot(p.astype(BF16), v_ref[pl.ds(ks, blk), :]))
        return states

    def alive(state):
        return (jnp.max(state[0]) > SB_DEAD_LOG2).astype(jnp.int32)

    zero = (jnp.zeros((2 * blk, 1), F32), jnp.zeros((2 * blk, LANES), F32))
    qqs = [stacked(q_ref[g * blk:(g + 1) * blk, :]) for g in range(q_blocks)]
    states = pairs([(qq, first_q + g, True, zero) for g, qq in enumerate(qqs)])

    def store(g, state):
        o_ref[g * blk:(g + 1) * blk, :] = jnp.where(head0, state[1][:blk], state[1][blk:]).astype(o_ref.dtype)

    live = [alive(state) for state in states]
    for g, state in enumerate(states):
        store(g, state)

    for g, (qq, state) in enumerate(zip(qqs, states)):
        i = first_q + g
        n_pairs = lax.shift_right_logical(i, 1)

        @pl.when((live[g] > 0) & (n_pairs > 0))
        def _(g=g, qq=qq, state=state, i=i, n_pairs=n_pairs):
            def cond(loop):
                t, alive_now, _ = loop
                return (t < n_pairs) & (alive_now > 0)

            def body(loop):
                t, _, state = loop
                state, = pairs([(qq, i - 2 - 2 * t, False, state)])
                return t + 1, alive(state), state

            _, _, final = lax.while_loop(cond, body, (jnp.int32(0), live[g], state))
            store(g, final)


def _sb_attention(qkv, *, blk=256, q_blocks=4):
    b, t, w3 = qkv.shape
    w = w3 // 3
    pairs = w // LANES
    blk = min(blk, t)
    rows = blk * q_blocks
    assert t % rows == 0
    tri = jnp.asarray(np.tril(np.ones((blk, blk), np.float32), -1), BF16)
    return pl.pallas_call(
        functools.partial(_sb_kernel, blk=blk, q_blocks=q_blocks),
        grid=(b, pairs, t // rows),
        in_specs=[pl.BlockSpec((None, rows, LANES), lambda bi, p, i: (bi, i, p)),
                  pl.BlockSpec((None, t, LANES), lambda bi, p, i: (bi, 0, pairs + p)),
                  pl.BlockSpec((None, t, LANES), lambda bi, p, i: (bi, 0, 2 * pairs + p)),
                  _resident((blk, blk))],
        out_specs=pl.BlockSpec((None, rows, LANES), lambda bi, p, i: (bi, i, p)),
        out_shape=jax.ShapeDtypeStruct((b, t, w), BF16),
        compiler_params=_params(("parallel", "parallel", "arbitrary")),
        name="sb_attn",
    )(qkv, qkv, qkv, tri)


def _t5_bucket_np(dist):
    max_exact = NUM_BUCKETS // 2
    d = np.maximum(dist, 1).astype(np.float32)
    log_b = max_exact + (np.log(d / np.float32(max_exact)) / np.float32(math.log(MAX_DISTANCE / max_exact))
                         * np.float32(NUM_BUCKETS - max_exact)).astype(np.int32)
    log_b = np.minimum(log_b, NUM_BUCKETS - 1)
    return np.where(dist < max_exact, dist, log_b)


def _dil_tables():
    qi = np.arange(DIL_BLOCK)[:, None]
    ki = np.arange(2 * DIL_BLOCK)[None, :]
    dist = qi + DIL_BLOCK - ki
    buckets, valid = [], []
    for window, dil in DIL_CONFIGS:
        steps = window // dil
        buckets.append(_t5_bucket_np(np.maximum(dist, 0) * dil))
        valid.append((dist >= 0) & (dist <= steps))
    return np.stack(buckets).astype(np.int32), np.stack(valid).astype(np.int32)


def _dil_kernel(rb_ref, bucket_ref, valid_ref, q_ref, k_ref, v_ref, o_ref,
                bias_scr, num_scr, m_scr, l_scr, *, seq):
    p = pl.program_id(0)
    qb = DIL_BLOCK
    lane = lax.broadcasted_iota(jnp.int32, (1, LANES), 1)
    head0 = lane < HEAD_DIM_ATTN
    first_half = lax.broadcasted_iota(jnp.int32, (1, 2 * qb), 1) < qb

    @pl.when(pl.program_id(1) == 0)
    def _():
        for br in range(len(DIL_CONFIGS)):
            bucket = bucket_ref[br]
            valid = valid_ref[br] > 0
            for h in range(2):
                bias = jnp.zeros((qb, 2 * qb), F32)
                for b in range(NUM_BUCKETS):
                    bias = jnp.where(bucket == b, rb_ref[b, 2 * p + h], bias)
                bias_scr[br, h * qb:(h + 1) * qb, :] = jnp.where(valid, bias, MASKED)

    for br, (_, dil) in enumerate(DIL_CONFIGS):
        n_units = seq // qb

        n_blocks = seq // (qb * dil)
        run = min(DIL_UNROLL, n_blocks)
        runs = DIL_UNROLL // run
        assert DIL_UNROLL % run == 0 and n_blocks % run == 0 and (dil == 1 or n_blocks == run)

        def rows(ref, start, dil=dil):
            if dil == 1:
                return ref[pl.ds(start, qb), :].astype(BF16)
            return ref[pl.ds(start, qb, stride=dil), :].astype(BF16)

        def load_run(rho, dil=dil, run=run, whole=(n_blocks == run)):
            res, base = (rho, 0) if whole else (0, rho * run)
            starts = [(base + i) * (qb * dil) + res for i in range(run)]
            kb = [rows(k_ref, st) for st in starts]
            vb = [rows(v_ref, st) for st in starts]
            if whole:
                k_prev, v_prev = kb[0], vb[0]
                pen = jnp.where(first_half, MASKED, 0.0)
            else:
                st = jnp.maximum(base - 1, 0) * (qb * dil) + res
                k_prev, v_prev = rows(k_ref, st), rows(v_ref, st)
                pen = jnp.where(first_half, jnp.where(base == 0, MASKED, 0.0), 0.0)
            units = []
            for i, st in enumerate(starts):
                q2 = rows(q_ref, st)
                zeros = jnp.zeros_like(q2)
                qq = jnp.concatenate([jnp.where(head0, q2, zeros), jnp.where(head0, zeros, q2)], axis=0)
                k2 = jnp.concatenate([kb[i - 1] if i else k_prev, kb[i]], axis=0)
                v2 = jnp.concatenate([vb[i - 1] if i else v_prev, vb[i]], axis=0)
                units.append((st, pen if i == 0 else None, qq, k2, v2))
            return units

        def softmax_parts(s):
            m = jnp.max(s, axis=-1, keepdims=True)
            e = jnp.exp(s - m)
            return m, e, jnp.sum(e, axis=-1, keepdims=True)

        def group(g, _, br=br, dil=dil, runs=runs):
            units = [u for r in range(runs) for u in load_run(g * runs + r)]
            scores, parts = {}, {}
            for j in range(DIL_UNROLL + 2):
                if j < DIL_UNROLL:
                    _, pen, qq, k2, _ = units[j]
                    scores[j] = _dot_nt(qq, k2) + bias_scr[br]
                    if pen is not None:
                        scores[j] = scores[j] + pen
                if 0 <= j - 1 < DIL_UNROLL:
                    parts[j - 1] = softmax_parts(scores.pop(j - 1))
                if 0 <= j - 2 < DIL_UNROLL:
                    st, _, _, _, v2 = units[j - 2]
                    m, e, l = parts.pop(j - 2)
                    pv = _dot(e.astype(BF16), v2)
                    idx = pl.ds(st, qb) if dil == 1 else pl.ds(st, qb, stride=dil)
                    num_scr[br, idx, :] = jnp.where(head0, pv[:qb], pv[qb:])
                    m_scr[br, idx, :] = jnp.where(head0, m[:qb], m[qb:])
                    l_scr[br, idx, :] = jnp.where(head0, l[:qb], l[qb:])
            return 0

        lax.fori_loop(0, n_units // DIL_UNROLL, group, 0)

    rows_out = 256

    def finish(c, _):
        idx = pl.ds(pl.multiple_of(c * rows_out, rows_out), rows_out)
        m_all = [m_scr[br, idx, :] for br in range(len(DIL_CONFIGS))]
        m_max = functools.reduce(jnp.maximum, m_all)
        num = den = None
        for br, m_br in enumerate(m_all):
            wt = jnp.exp(m_br - m_max)
            n_br = wt * num_scr[br, idx, :]
            d_br = wt * l_scr[br, idx, :]
            num = n_br if num is None else num + n_br
            den = d_br if den is None else den + d_br
        o_ref[idx, :] = (num / den).astype(o_ref.dtype)
        return 0

    lax.fori_loop(0, seq // rows_out, finish, 0)


def _dil_attention(qkv, rel_bias):
    b, t, w3 = qkv.shape
    w = w3 // 3
    pairs = w // LANES
    assert t % (DIL_BLOCK * max(d for _, d in DIL_CONFIGS)) == 0 and t % 256 == 0
    bucket, valid = _dil_tables()
    nbr = len(DIL_CONFIGS)
    seq_spec = lambda off: pl.BlockSpec((None, t, LANES), lambda p, bi: (bi, 0, off + p))
    return pl.pallas_call(
        functools.partial(_dil_kernel, seq=t),
        grid=(pairs, b),
        in_specs=[pl.BlockSpec(memory_space=pltpu.SMEM),
                  _resident((nbr, DIL_BLOCK, 2 * DIL_BLOCK)),
                  _resident((nbr, DIL_BLOCK, 2 * DIL_BLOCK)),
                  seq_spec(0), seq_spec(pairs), seq_spec(2 * pairs)],
        out_specs=pl.BlockSpec((None, t, LANES), lambda p, bi: (bi, 0, p)),
        out_shape=jax.ShapeDtypeStruct((b, t, w), BF16),
        scratch_shapes=[pltpu.VMEM((nbr, 2 * DIL_BLOCK, 2 * DIL_BLOCK), F32),
                        pltpu.VMEM((nbr, t, LANES), F32), pltpu.VMEM((nbr, t, LANES), F32),
                        pltpu.VMEM((nbr, t, LANES), F32)],
        compiler_params=_params(("arbitrary", "arbitrary")),
        name="dil_attn",
    )(rel_bias, jnp.asarray(bucket), jnp.asarray(valid), qkv, qkv, qkv)


def _mlstm_proj_kernel(x_ref, g_ref, w_ref, wg_ref, cw_ref, bg_ref, q_ref, kt_ref, v_ref, o_ref,
                       gate_ref, *hist_scrs, tiles_per_seq, n_chunk, scale):
    i = pl.program_id(0)
    tm = x_ref.shape[0]
    width = q_ref.shape[1]
    xn = _rms(x_ref[...], g_ref[...]).astype(BF16)

    def even_rows(first, n):
        return pl.ds(2 * first, n, stride=2)

    slabs = n_chunk // LANES
    n_conv = 2 * width // n_chunk
    assert len(hist_scrs) == n_conv

    @pl.when(i % tiles_per_seq == 0)
    def _():
        for hist in hist_scrs:
            hist[:, 0:2 * CONV_PAD, :] = jnp.zeros((slabs, 2 * CONV_PAD, LANES), F32)

    def project(c):
        pre = _dot(xn, w_ref[:, c * n_chunk:(c + 1) * n_chunk])
        for s in range(slabs):
            hist_scrs[c][s, even_rows(CONV_PAD, tm), :] = pre[:, s * LANES:(s + 1) * LANES]

    def conv(c):
        sl = slice(c * n_chunk, (c + 1) * n_chunk)
        hist = hist_scrs[c]
        parts = []
        for s in range(slabs):
            y = None
            for tap in range(CONV_WIDTH):
                off = CONV_PAD - (CONV_WIDTH - 1) + tap
                w_tap = cw_ref[tap:tap + 1, (c * slabs + s) * LANES:(c * slabs + s + 1) * LANES]
                term = w_tap * hist[s, even_rows(off, tm), :]
                y = term if y is None else y + term
            parts.append(y)
            hist[s, even_rows(0, CONV_PAD), :] = hist[s, even_rows(tm, CONV_PAD), :]
        y = jnp.concatenate(parts, axis=1)
        y = y * jax.nn.sigmoid(y)
        if c * n_chunk < width:
            q_ref[:, sl] = (y * scale).astype(BF16)
        else:
            kt_ref[c * n_chunk - width:(c + 1) * n_chunk - width, :] = y.T.astype(BF16)

    def plain(c):
        sl = slice((c // 2) * n_chunk, (c // 2 + 1) * n_chunk)
        if c % 2 == 0:
            v_ref[:, sl] = _dot(xn, w_ref[:, 2 * width + sl.start:2 * width + sl.stop]).astype(BF16)
        else:
            o_ref[:, sl] = _dot(xn, w_ref[:, 3 * width + sl.start:3 * width + sl.stop])

    n_plain = 2 * width // n_chunk
    for c in range(n_conv + 1):
        if c < n_conv:
            project(c)
        if c >= 1:
            conv(c - 1)
            if c - 1 < n_plain:
                plain(c - 1)
    for c in range(n_conv, n_plain):
        plain(c)
    gate_ref[...] = _dot(xn, wg_ref[...]) + bg_ref[...]


def _mlstm_proj(x, g, w_in, w_gates, conv_w, b_gates, *, seq, tm=512, n_chunk=512):
    m, d = x.shape
    width = conv_w.shape[1] // 2
    assert m % tm == 0 and seq % tm == 0 and width % n_chunk == 0 and w_in.shape[1] >= 4 * width
    row = lambda wd: pl.BlockSpec((tm, wd), lambda i: (i, 0))
    scale = 1.0 / math.sqrt(width // N_HEADS_MLSTM)
    return pl.pallas_call(
        functools.partial(_mlstm_proj_kernel, tiles_per_seq=seq // tm, n_chunk=n_chunk, scale=scale),
        grid=(m // tm,),
        in_specs=[row(d), _resident((1, d)), _resident(w_in.shape), _resident((d, 2 * LANES)),
                  _resident(conv_w.shape), _resident((1, 2 * LANES))],
        out_specs=[row(width), pl.BlockSpec((width, tm), lambda i: (0, i)), row(width), row(width),
                   row(2 * LANES)],
        out_shape=[jax.ShapeDtypeStruct((m, width), BF16), jax.ShapeDtypeStruct((width, m), BF16),
                   jax.ShapeDtypeStruct((m, width), BF16), jax.ShapeDtypeStruct((m, width), F32),
                   jax.ShapeDtypeStruct((m, 2 * LANES), F32)],
        scratch_shapes=[pltpu.VMEM((n_chunk // LANES, 2 * (tm + CONV_PAD), LANES), F32)
                        for _ in range(2 * width // n_chunk)],
        compiler_params=_params(("arbitrary",)),
        name="mlstm_proj",
    )(x, g.reshape(1, d), w_in, w_gates, conv_w, b_gates)


def _split3(x):
    hi = x.astype(BF16)
    r = x - hi.astype(F32)
    mid = r.astype(BF16)
    return hi, mid, (r - mid.astype(F32)).astype(BF16)


def _gate_prep_kernel(g_ref, cols_ref, rows_ref, *, heads, cl):
    tm = g_ref.shape[0]
    row = lax.broadcasted_iota(jnp.int32, (cl, cl), 0)
    col = lax.broadcasted_iota(jnp.int32, (cl, cl), 1)
    incl = jnp.where(row >= col, 1.0, 0.0).astype(BF16)
    lane = lax.broadcasted_iota(jnp.int32, (1, cl), 1)
    pad = jnp.zeros((cl - heads, cl), F32)
    spans = [slice(c * cl, (c + 1) * cl) for c in range(tm // cl)]
    bs = []
    for r in spans:
        b = None
        for part in _split3(jax.nn.log_sigmoid(g_ref[r, LANES:])):
            term = _dot(incl, part)
            b = term if b is None else b + term
        bs.append(b)
    us = [g_ref[r, :LANES] - b for r, b in zip(spans, bs)]
    u_rows = jnp.concatenate([u.T[0:heads, :] for u in us], axis=0)
    b_rows = jnp.concatenate([b.T[0:heads, :] for b in bs], axis=0)
    cmax = u_rows
    shift = 1
    while shift < cl:
        cmax = jnp.where(lane >= shift, jnp.maximum(cmax, pltpu.roll(cmax, shift, axis=1)), cmax)
        shift *= 2
    u_max = jnp.broadcast_to(jnp.max(u_rows, axis=1, keepdims=True), u_rows.shape)
    b_last = jnp.broadcast_to(jnp.min(b_rows, axis=1, keepdims=True), b_rows.shape)
    for c, (r, u, b) in enumerate(zip(spans, us, bs)):
        hs = slice(c * heads, (c + 1) * heads)
        cols_ref[r, :] = jnp.concatenate([u, b, jnp.concatenate([cmax[hs], pad], axis=0).T], axis=1)
        rows_ref[:, r] = jnp.concatenate([u_rows[hs], u_max[hs], b_last[hs]], axis=0)


def _gate_prep(gates, *, heads, cl, tm=1024):
    m = gates.shape[0]
    tm = min(tm, m)
    assert m % tm == 0 and tm % cl == 0 and cl == LANES and heads == 8
    return pl.pallas_call(
        functools.partial(_gate_prep_kernel, heads=heads, cl=cl),
        grid=(m // tm,),
        in_specs=[pl.BlockSpec((tm, 2 * LANES), lambda i: (i, 0))],
        out_specs=[pl.BlockSpec((tm, 3 * LANES), lambda i: (i, 0)),
                   pl.BlockSpec((3 * heads, tm), lambda i: (0, i))],
        out_shape=[jax.ShapeDtypeStruct((m, 3 * LANES), F32),
                   jax.ShapeDtypeStruct((3 * heads, m), F32)],
        compiler_params=_params(("parallel",)),
        name="gate_prep",
    )(gates)


def _mlstm_kernel(q_ref, kt_ref, v_ref, o_ref, gcol_ref, grow_ref, hg_ref, out_ref,
                  s_scr, m_scr, ml_scr, *, heads):
    cl = MLSTM_CHUNK
    n_chunks = q_ref.shape[0] // cl
    dh = q_ref.shape[1] // heads

    @pl.when(pl.program_id(1) == 0)
    def _():
        s_scr[...] = jnp.zeros_like(s_scr)
        m_scr[...] = jnp.zeros_like(m_scr)
        ml_scr[...] = jnp.zeros_like(ml_scr)

    row = lax.broadcasted_iota(jnp.int32, (cl, cl), 0)
    col = lax.broadcasted_iota(jnp.int32, (cl, cl), 1)
    lower = row >= col
    ones = jnp.ones((cl, dh), BF16)

    m_rows = m_scr[...]
    m_lane = ml_scr[0:1, :]
    tables = []
    for c in range(n_chunks):
        span = slice(c * cl, (c + 1) * cl)
        u_cols, b_cols, cmax_cols = (gcol_ref[span, t * LANES:(t + 1) * LANES] for t in range(3))
        u_rows, u_max_rows, b_last_rows = (grow_ref[t * heads:(t + 1) * heads, span] for t in range(3))
        mm_last_rows = jnp.maximum(m_rows, u_max_rows)
        mm_cols = jnp.maximum(cmax_cols, m_lane)
        tables.append(dict(
            span=span, u_rows=u_rows, m_prev_rows=m_rows, mm_cols=mm_cols,
            decay_rows=jnp.exp(m_rows - mm_last_rows),
            ws_rows=jnp.exp(u_rows - mm_last_rows),
            floor_cols=jnp.exp(-(b_cols + mm_cols))))
        m_rows = b_last_rows + mm_last_rows
        m_lane = b_cols[cl - 1:cl, :] + mm_cols[cl - 1:cl, :]
    m_scr[...] = m_rows
    ml_scr[...] = jnp.broadcast_to(m_lane, ml_scr.shape)

    def lanes_of(mat, c):
        return jnp.broadcast_to(mat[:, c:c + 1], (cl, LANES))

    def stage_a(tab, h):
        sl = slice(h * dh, (h + 1) * dh)
        q, kt = q_ref[tab["span"], sl], kt_ref[sl, tab["span"]]
        v1 = jnp.concatenate([v_ref[tab["span"], sl], ones], axis=1)
        s_prev = s_scr[h]
        qk = _dot(q, kt)
        qs = _dot(q, s_prev.astype(BF16))
        kw = (kt.astype(F32) * tab["ws_rows"][h:h + 1, :]).astype(BF16)
        decay = jnp.broadcast_to(tab["decay_rows"][h:h + 1, :], (dh, LANES))
        s_scr[h] = jnp.concatenate([decay, decay], axis=1) * s_prev + _dot(kw, v1)
        return tab, h, sl, v1, qk, qs

    def stage_b(tab, h, sl, v1, qk, qs):
        mm = lanes_of(tab["mm_cols"], h)
        weight = jnp.where(lower, jnp.exp(tab["u_rows"][h:h + 1, :] - mm), 0.0)
        w_inter = jnp.exp(tab["m_prev_rows"][h:h + 1, :] - mm)
        return tab, sl, v1, qs, (qk * weight).astype(BF16), w_inter, lanes_of(tab["floor_cols"], h)

    def stage_c(tab, sl, v1, qs, p, w_inter, floor):
        pv = _dot(p, v1)
        both = pv + jnp.concatenate([w_inter, w_inter], axis=1) * qs
        hid = both[:, :dh] / jnp.maximum(jnp.abs(both[:, dh:]), floor)
        hid = hid * lax.rsqrt(jnp.mean(hid * hid, axis=-1, keepdims=True) + EPS) * hg_ref[:, sl]
        out_ref[tab["span"], sl] = (hid * jax.nn.sigmoid(o_ref[tab["span"], sl])).astype(out_ref.dtype)

    jobs = [(tab, h) for tab in tables for h in range(heads)]
    after_a, after_b = {}, {}
    for j in range(len(jobs) + 2):
        if j < len(jobs):
            after_a[j] = stage_a(*jobs[j])
        if 0 <= j - 1 < len(jobs):
            after_b[j - 1] = stage_b(*after_a.pop(j - 1))
        if 0 <= j - 2 < len(jobs):
            stage_c(*after_b.pop(j - 2))


def _mlstm(q, kt, v, o, gates, head_g, *, heads=N_HEADS_MLSTM, chunks=4):
    b, t, w = q.shape
    cl = MLSTM_CHUNK
    dh = w // heads
    rows = chunks * cl
    nc = t // rows
    assert t % rows == 0 and dh == LANES and cl == LANES
    gcols, grows = _gate_prep(gates, heads=heads, cl=cl)
    blk = lambda wd: pl.BlockSpec((None, rows, wd), lambda bi, c: (bi, c, 0))
    return pl.pallas_call(
        functools.partial(_mlstm_kernel, heads=heads),
        grid=(b, nc),
        in_specs=[blk(w), pl.BlockSpec((w, rows), lambda bi, c: (0, bi * nc + c)), blk(w), blk(w),
                  pl.BlockSpec((rows, 3 * LANES), lambda bi, c: (bi * nc + c, 0)),
                  pl.BlockSpec((3 * heads, rows), lambda bi, c: (0, bi * nc + c)),
                  _resident((1, w))],
        out_specs=blk(w),
        out_shape=jax.ShapeDtypeStruct((b, t, w), BF16),
        scratch_shapes=[pltpu.VMEM((heads, dh, 2 * dh), F32), pltpu.VMEM((heads, LANES), F32),
                        pltpu.VMEM((heads, LANES), F32)],
        compiler_params=_params(("parallel", "arbitrary")),
        name="mlstm",
    )(q, kt, v, o, gcols, grows, head_g.reshape(1, w))


def kernel(x, norm_g, ffn_w_gate, ffn_w_up, ffn_w_down, attn_w_in, attn_w_out, rel_bias,
           mlstm_w_in, mlstm_b_gates, mlstm_conv_w, mlstm_head_g, mlstm_w_out):
    bsz, t, d = x.shape
    depth = norm_g.shape[0]
    h = x.reshape(bsz * t, d)
    bf = lambda a: a.astype(BF16)

    w_gate, w_up, w_down = bf(ffn_w_gate), bf(ffn_w_up), bf(ffn_w_down)

    def ffn(h, layer, half, mix=None):
        g = norm_g[layer]
        return _ffn(h, g[2 * half * 2], g[2 * half * 2 + 1], w_gate, w_up, w_down, (layer, half), mix)

    for layer in range(depth):
        g = norm_g[layer]
        j = layer // 2
        h = ffn(h, layer, 0)
        if layer % 2 == 0:
            sb, dil = _attn_proj(h, g[2], bf(attn_w_in[j]))
            out_sb = _sb_attention(sb.reshape(bsz, t, -1))
            out_dil = _dil_attention(dil.reshape(bsz, t, -1), rel_bias.astype(F32))
            mix = ([out_sb.reshape(bsz * t, -1), out_dil.reshape(bsz * t, -1)], bf(attn_w_out[j]), g[3])
        else:
            width = mlstm_w_out.shape[1]
            n_head = mlstm_b_gates.shape[1] // 2
            tiles = lambda a: jnp.concatenate(
                [jnp.pad(part, ((0, 0), (0, LANES - n_head)))
                 for part in (a[:, :n_head], a[:, n_head:])], axis=1)
            w_in = bf(mlstm_w_in[j])
            w_gates = tiles(w_in[:, 4 * width:])
            b_gates = tiles(mlstm_b_gates[j].astype(F32).reshape(1, -1))
            q, kt, v, o, gates = _mlstm_proj(h, g[2], w_in, w_gates, mlstm_conv_w[j].astype(F32),
                                             b_gates, seq=t)
            r3 = lambda a: a.reshape(bsz, t, -1)
            hid = _mlstm(r3(q), kt, r3(v), r3(o), gates, mlstm_head_g[j].astype(F32))
            mix = ([hid.reshape(bsz * t, width)], bf(mlstm_w_out[j]), g[3])
        h = ffn(h, layer, 1, mix)
    return h.reshape(bsz, t, d)
```

```python
import functools
import math

import numpy as np
import jax
import jax.numpy as jnp
from jax import lax
from jax.experimental import pallas as pl
from jax.experimental.pallas import tpu as pltpu

EPS = 1e-6
HEAD_DIM_ATTN = 64
DIL_CONFIGS = ((128, 1), (512, 4), (2048, 16))
DIL_BLOCK = 128
DIL_UNROLL = 16
NUM_BUCKETS = 32
MAX_DISTANCE = 2048
N_HEADS_MLSTM = 8
MLSTM_CHUNK = 128
CONV_WIDTH = 4
LANES = 128
CONV_PAD = 8
MASKED = -1e30
V7X_VMEM_BYTES = 64 * 1024 * 1024
VMEM_LIMIT = V7X_VMEM_BYTES * 7 // 8

F32 = jnp.float32
BF16 = jnp.bfloat16


def _params(sem, vmem=VMEM_LIMIT):
    return pltpu.CompilerParams(dimension_semantics=sem, vmem_limit_bytes=vmem)


def _resident(shape):
    zeros = (0,) * len(shape)
    return pl.BlockSpec(shape, lambda *_: zeros, pipeline_mode=pl.Buffered(1))


def _rms(x, g):
    return x * lax.rsqrt(jnp.mean(x * x, axis=-1, keepdims=True) + EPS) * g


def _dot(a, b):
    return jnp.dot(a, b, preferred_element_type=F32)


def _dot_nt(a, b):
    return lax.dot_general(a, b, (((1,), (1,)), ((), ())), preferred_element_type=F32)


def _ffn_kernel(*refs, ff_chunk, n_mix, groups):
    mix_refs, refs = refs[:n_mix], refs[n_mix:]
    if n_mix:
        (wmix_ref, gmix_ref), refs = refs[:2], refs[2:]
    x_ref, gin_ref, gout_ref, wg_ref, wu_ref, wd_ref, o_ref = refs
    d_ff = wg_ref.shape[1]
    rows = x_ref.shape[0] // groups
    spans = [slice(r * rows, (r + 1) * rows) for r in range(groups)]

    xs = []
    for rs in spans:
        x = x_ref[rs, :]
        if n_mix:
            mixed = jnp.concatenate([r[rs, :] for r in mix_refs], axis=-1)
            x = x + _rms(_dot(mixed, wmix_ref[...]), gmix_ref[...])
        xs.append(x)
    xns = [_rms(x, gin_ref[...]).astype(BF16) for x in xs]
    accs = []
    for xn in xns:
        acc = None
        for c in range(d_ff // ff_chunk):
            sl = slice(c * ff_chunk, (c + 1) * ff_chunk)
            gate = _dot(xn, wg_ref[:, sl])
            up = _dot(xn, wu_ref[:, sl])
            h = (gate * jax.nn.sigmoid(gate) * up).astype(BF16)
            part = _dot(h, wd_ref[sl, :])
            acc = part if acc is None else acc + part
        accs.append(acc)
    for rs, x, acc in zip(spans, xs, accs):
        o_ref[rs, :] = x + 0.5 * _rms(acc, gout_ref[...])


def _ffn(x, g_in, g_out, wg, wu, wd, which=(), mix=None, *, tm=1024, groups=2, ff_chunk=256):
    m, d = x.shape
    d_ff = wg.shape[-1]
    assert m % tm == 0 and tm % groups == 0 and d_ff % ff_chunk == 0 and len(which) == wg.ndim - 2
    row = lambda width: pl.BlockSpec((tm, width), lambda i: (i, 0))
    lead = (None,) * len(which)
    weight = lambda shape: pl.BlockSpec(lead + shape, lambda i: tuple(which) + (0, 0),
                                        pipeline_mode=pl.Buffered(1))
    parts, w_mix, g_mix = mix if mix else ((), None, None)
    mix_specs = [row(a.shape[1]) for a in parts] + ([_resident(w_mix.shape), _resident((1, d))] if mix else [])
    mix_args = list(parts) + ([w_mix, g_mix.reshape(1, d)] if mix else [])
    return pl.pallas_call(
        functools.partial(_ffn_kernel, ff_chunk=ff_chunk, n_mix=len(parts), groups=groups),
        grid=(m // tm,),
        in_specs=mix_specs + [row(d), _resident((1, d)), _resident((1, d)),
                              weight((d, d_ff)), weight((d, d_ff)), weight((d_ff, d))],
        out_specs=row(d),
        out_shape=jax.ShapeDtypeStruct((m, d), F32),
        compiler_params=_params(("parallel",)),
        name="ffn",
    )(*mix_args, x, g_in.reshape(1, d), g_out.reshape(1, d), wg, wu, wd)


def _attn_proj_kernel(x_ref, g_ref, w_ref, sb_ref, dil_ref, *, n_chunk, scale, groups):
    w_sb = sb_ref.shape[1]
    w_q = w_sb // 3
    rows = x_ref.shape[0] // groups
    spans = [slice(r * rows, (r + 1) * rows) for r in range(groups)]
    xns = [_rms(x_ref[rs, :], g_ref[...]).astype(BF16) for rs in spans]
    for rs, xn in zip(spans, xns):
        for c in range(w_sb // n_chunk):
            sl = slice(c * n_chunk, (c + 1) * n_chunk)
            y = _dot(xn, w_ref[:, sl])
            if (c + 1) * n_chunk <= w_q:
                y = y * scale
            sb_ref[rs, sl] = y.astype(BF16)
        for c in range(dil_ref.shape[1] // n_chunk):
            sl = slice(c * n_chunk, (c + 1) * n_chunk)
            y = _dot(xn, w_ref[:, w_sb + c * n_chunk: w_sb + (c + 1) * n_chunk])
            if (c + 1) * n_chunk <= w_q:
                y = y * scale
            dil_ref[rs, sl] = y


def _attn_proj(x, g, w, *, tm=1024, n_chunk=512, groups=2):
    m, d = x.shape
    n = w.shape[1]
    half = n // 2
    assert m % tm == 0 and (half // 3) % n_chunk == 0
    scale = 1.0 / math.sqrt(HEAD_DIM_ATTN)
    return pl.pallas_call(
        functools.partial(_attn_proj_kernel, n_chunk=n_chunk, scale=scale, groups=groups),
        grid=(m // tm,),
        in_specs=[pl.BlockSpec((tm, d), lambda i: (i, 0)), _resident((1, d)), _resident((d, n))],
        out_specs=[pl.BlockSpec((tm, half), lambda i: (i, 0)),
                   pl.BlockSpec((tm, half), lambda i: (i, 0))],
        out_shape=[jax.ShapeDtypeStruct((m, half), BF16), jax.ShapeDtypeStruct((m, half), F32)],
        compiler_params=_params(("parallel",)),
        name="attn_proj",
    )(x, g.reshape(1, d), w)


LOG2E = math.log2(math.e)
SB_DEAD_LOG2 = -160.0


def _sb_kernel(q_ref, k_ref, v_ref, tri_ref, o_ref, *, blk, q_blocks):
    first_q = pl.program_id(2) * q_blocks
    lane = lax.broadcasted_iota(jnp.int32, (1, LANES), 1)
    row = lax.broadcasted_iota(jnp.int32, (blk, blk), 0)
    col = lax.broadcasted_iota(jnp.int32, (blk, blk), 1)
    causal = col < row
    causal2 = jnp.concatenate([causal, causal], axis=0)
    tri = tri_ref[...]
    head0 = lane < HEAD_DIM_ATTN

    def stacked(q2):
        zeros = jnp.zeros_like(q2)
        return jnp.concatenate([jnp.where(head0, q2, zeros), jnp.where(head0, zeros, q2)], axis=0)

    def pairs(jobs):
        chains = []
        for j, (qq, kb, diag, _) in enumerate(jobs):
            chains.append((j, pl.multiple_of(kb * blk, blk), causal2 if diag else None))
            chains.append((j, pl.multiple_of(jnp.maximum(kb - 1, 0) * blk, blk), kb >= 1))
        states = [job[3] for job in jobs]
        zs, mids = {}, {}
        for n in range(len(chains) + 2):
            if n < len(chains):
                j, ks, _ = chains[n]
                zs[n] = _dot_nt(jobs[j][0], k_ref[pl.ds(ks, blk), :]) * LOG2E
            if 0 <= n - 1 < len(chains):
                z, mask = zs.pop(n - 1), chains[n - 1][2]
                neg = -z
                log_keep = jnp.minimum(neg, 0.0) - jnp.log2(1.0 + jnp.exp2(jnp.minimum(z, neg)))
                log_beta = z + log_keep
                if mask is not None:
                    log_keep = jnp.where(mask, log_keep, 0.0)
                    log_beta = jnp.where(mask, log_beta, MASKED)
                hi = log_keep.astype(BF16)
                lo = (log_keep - hi.astype(F32)).astype(BF16)
                later = _dot(hi, tri) + _dot(lo, tri)
                mids[n - 1] = (log_beta, later, jnp.sum(log_keep, axis=-1, keepdims=True))
            if 0 <= n - 2 < len(chains):
                j, ks, _ = chains[n - 2]
                log_beta, later, total = mids.pop(n - 2)
                carry, acc = states[j]
                p = jnp.exp2(log_beta + later + carry)
                states[j] = (carry + total, acc + _dot(p.astype(BF16), v_ref[pl.ds(ks, blk), :]))
        return states

    def alive(state):
        return (jnp.max(state[0]) > SB_DEAD_LOG2).astype(jnp.int32)

    zero = (jnp.zeros((2 * blk, 1), F32), jnp.zeros((2 * blk, LANES), F32))
    qqs = [stacked(q_ref[g * blk:(g + 1) * blk, :]) for g in range(q_blocks)]
    states = pairs([(qq, first_q + g, True, zero) for g, qq in enumerate(qqs)])

    def store(g, state):
        o_ref[g * blk:(g + 1) * blk, :] = jnp.where(head0, state[1][:blk], state[1][blk:]).astype(o_ref.dtype)

    live = [alive(state) for state in states]
    for g, state in enumerate(states):
        store(g, state)

    for g, (qq, state) in enumerate(zip(qqs, states)):
        i = first_q + g
        n_pairs = lax.shift_right_logical(i, 1)

        @pl.when((live[g] > 0) & (n_pairs > 0))
        def _(g=g, qq=qq, state=state, i=i, n_pairs=n_pairs):
            def cond(loop):
                t, alive_now, _ = loop
                return (t < n_pairs) & (alive_now > 0)

            def body(loop):
                t, _, state = loop
                state, = pairs([(qq, i - 2 - 2 * t, False, state)])
                return t + 1, alive(state), state

            _, _, final = lax.while_loop(cond, body, (jnp.int32(0), live[g], state))
            store(g, final)


def _sb_attention(qkv, *, blk=256, q_blocks=4):
    b, t, w3 = qkv.shape
    w = w3 // 3
    pairs = w // LANES
    blk = min(blk, t)
    rows = blk * q_blocks
    assert t % rows == 0
    tri = jnp.asarray(np.tril(np.ones((blk, blk), np.float32), -1), BF16)
    return pl.pallas_call(
        functools.partial(_sb_kernel, blk=blk, q_blocks=q_blocks),
        grid=(b, pairs, t // rows),
        in_specs=[pl.BlockSpec((None, rows, LANES), lambda bi, p, i: (bi, i, p)),
                  pl.BlockSpec((None, t, LANES), lambda bi, p, i: (bi, 0, pairs + p)),
                  pl.BlockSpec((None, t, LANES), lambda bi, p, i: (bi, 0, 2 * pairs + p)),
                  _resident((blk, blk))],
        out_specs=pl.BlockSpec((None, rows, LANES), lambda bi, p, i: (bi, i, p)),
        out_shape=jax.ShapeDtypeStruct((b, t, w), BF16),
        compiler_params=_params(("parallel", "parallel", "arbitrary")),
        name="sb_attn",
    )(qkv, qkv, qkv, tri)


def _t5_bucket_np(dist):
    max_exact = NUM_BUCKETS // 2
    d = np.maximum(dist, 1).astype(np.float32)
    log_b = max_exact + (np.log(d / np.float32(max_exact)) / np.float32(math.log(MAX_DISTANCE / max_exact))
                         * np.float32(NUM_BUCKETS - max_exact)).astype(np.int32)
    log_b = np.minimum(log_b, NUM_BUCKETS - 1)
    return np.where(dist < max_exact, dist, log_b)


def _dil_tables():
    qi = np.arange(DIL_BLOCK)[:, None]
    ki = np.arange(2 * DIL_BLOCK)[None, :]
    dist = qi + DIL_BLOCK - ki
    buckets, valid = [], []
    for window, dil in DIL_CONFIGS:
        steps = window // dil
        buckets.append(_t5_bucket_np(np.maximum(dist, 0) * dil))
        valid.append((dist >= 0) & (dist <= steps))
    return np.stack(buckets).astype(np.int32), np.stack(valid).astype(np.int32)


def _dil_kernel(rb_ref, bucket_ref, valid_ref, q_ref, k_ref, v_ref, o_ref,
                bias_scr, num_scr, m_scr, l_scr, *, seq):
    p = pl.program_id(0)
    qb = DIL_BLOCK
    lane = lax.broadcasted_iota(jnp.int32, (1, LANES), 1)
    head0 = lane < HEAD_DIM_ATTN
    first_half = lax.broadcasted_iota(jnp.int32, (1, 2 * qb), 1) < qb

    @pl.when(pl.program_id(1) == 0)
    def _():
        for br in range(len(DIL_CONFIGS)):
            bucket = bucket_ref[br]
            valid = valid_ref[br] > 0
            for h in range(2):
                bias = jnp.zeros((qb, 2 * qb), F32)
                for b in range(NUM_BUCKETS):
                    bias = jnp.where(bucket == b, rb_ref[b, 2 * p + h], bias)
                bias_scr[br, h * qb:(h + 1) * qb, :] = jnp.where(valid, bias, MASKED)

    for br, (_, dil) in enumerate(DIL_CONFIGS):
        n_units = seq // qb

        n_blocks = seq // (qb * dil)
        run = min(DIL_UNROLL, n_blocks)
        runs = DIL_UNROLL // run
        assert DIL_UNROLL % run == 0 and n_blocks % run == 0 and (dil == 1 or n_blocks == run)

        def rows(ref, start, dil=dil):
            if dil == 1:
                return ref[pl.ds(start, qb), :].astype(BF16)
            return ref[pl.ds(start, qb, stride=dil), :].astype(BF16)

        def load_run(rho, dil=dil, run=run, whole=(n_blocks == run)):
            res, base = (rho, 0) if whole else (0, rho * run)
            starts = [(base + i) * (qb * dil) + res for i in range(run)]
            kb = [rows(k_ref, st) for st in starts]
            vb = [rows(v_ref, st) for st in starts]
            if whole:
                k_prev, v_prev = kb[0], vb[0]
                pen = jnp.where(first_half, MASKED, 0.0)
            else:
                st = jnp.maximum(base - 1, 0) * (qb * dil) + res
                k_prev, v_prev = rows(k_ref, st), rows(v_ref, st)
                pen = jnp.where(first_half, jnp.where(base == 0, MASKED, 0.0), 0.0)
            units = []
            for i, st in enumerate(starts):
                q2 = rows(q_ref, st)
                zeros = jnp.zeros_like(q2)
                qq = jnp.concatenate([jnp.where(head0, q2, zeros), jnp.where(head0, zeros, q2)], axis=0)
                k2 = jnp.concatenate([kb[i - 1] if i else k_prev, kb[i]], axis=0)
                v2 = jnp.concatenate([vb[i - 1] if i else v_prev, vb[i]], axis=0)
                units.append((st, pen if i == 0 else None, qq, k2, v2))
            return units

        def softmax_parts(s):
            m = jnp.max(s, axis=-1, keepdims=True)
            e = jnp.exp(s - m)
            return m, e, jnp.sum(e, axis=-1, keepdims=True)

        def group(g, _, br=br, dil=dil, runs=runs):
            units = [u for r in range(runs) for u in load_run(g * runs + r)]
            scores, parts = {}, {}
            for j in range(DIL_UNROLL + 2):
                if j < DIL_UNROLL:
                    _, pen, qq, k2, _ = units[j]
                    scores[j] = _dot_nt(qq, k2) + bias_scr[br]
                    if pen is not None:
                        scores[j] = scores[j] + pen
                if 0 <= j - 1 < DIL_UNROLL:
                    parts[j - 1] = softmax_parts(scores.pop(j - 1))
                if 0 <= j - 2 < DIL_UNROLL:
                    st, _, _, _, v2 = units[j - 2]
                    m, e, l = parts.pop(j - 2)
                    pv = _dot(e.astype(BF16), v2)
                    idx = pl.ds(st, qb) if dil == 1 else pl.ds(st, qb, stride=dil)
                    num_scr[br, idx, :] = jnp.where(head0, pv[:qb], pv[qb:])
                    m_scr[br, idx, :] = jnp.where(head0, m[:qb], m[qb:])
                    l_scr[br, idx, :] = jnp.where(head0, l[:qb], l[qb:])
            return 0

        lax.fori_loop(0, n_units // DIL_UNROLL, group, 0)

    rows_out = 256

    def finish(c, _):
        idx = pl.ds(pl.multiple_of(c * rows_out, rows_out), rows_out)
        m_all = [m_scr[br, idx, :] for br in range(len(DIL_CONFIGS))]
        m_max = functools.reduce(jnp.maximum, m_all)
        num = den = None
        for br, m_br in enumerate(m_all):
            wt = jnp.exp(m_br - m_max)
            n_br = wt * num_scr[br, idx, :]
            d_br = wt * l_scr[br, idx, :]
            num = n_br if num is None else num + n_br
            den = d_br if den is None else den + d_br
        o_ref[idx, :] = (num / den).astype(o_ref.dtype)
        return 0

    lax.fori_loop(0, seq // rows_out, finish, 0)


def _dil_attention(qkv, rel_bias):
    b, t, w3 = qkv.shape
    w = w3 // 3
    pairs = w // LANES
    assert t % (DIL_BLOCK * max(d for _, d in DIL_CONFIGS)) == 0 and t % 256 == 0
    bucket, valid = _dil_tables()
    nbr = len(DIL_CONFIGS)
    seq_spec = lambda off: pl.BlockSpec((None, t, LANES), lambda p, bi: (bi, 0, off + p))
    return pl.pallas_call(
        functools.partial(_dil_kernel, seq=t),
        grid=(pairs, b),
        in_specs=[pl.BlockSpec(memory_space=pltpu.SMEM),
                  _resident((nbr, DIL_BLOCK, 2 * DIL_BLOCK)),
                  _resident((nbr, DIL_BLOCK, 2 * DIL_BLOCK)),
                  seq_spec(0), seq_spec(pairs), seq_spec(2 * pairs)],
        out_specs=pl.BlockSpec((None, t, LANES), lambda p, bi: (bi, 0, p)),
        out_shape=jax.ShapeDtypeStruct((b, t, w), BF16),
        scratch_shapes=[pltpu.VMEM((nbr, 2 * DIL_BLOCK, 2 * DIL_BLOCK), F32),
                        pltpu.VMEM((nbr, t, LANES), F32), pltpu.VMEM((nbr, t, LANES), F32),
                        pltpu.VMEM((nbr, t, LANES), F32)],
        compiler_params=_params(("arbitrary", "arbitrary")),
        name="dil_attn",
    )(rel_bias, jnp.asarray(bucket), jnp.asarray(valid), qkv, qkv, qkv)


def _mlstm_proj_kernel(x_ref, g_ref, w_ref, wg_ref, cw_ref, bg_ref, q_ref, kt_ref, v_ref, o_ref,
                       gate_ref, *hist_scrs, tiles_per_seq, n_chunk, scale):
    i = pl.program_id(0)
    tm = x_ref.shape[0]
    width = q_ref.shape[1]
    xn = _rms(x_ref[...], g_ref[...]).astype(BF16)

    def even_rows(first, n):
        return pl.ds(2 * first, n, stride=2)

    slabs = n_chunk // LANES
    n_conv = 2 * width // n_chunk
    assert len(hist_scrs) == n_conv

    @pl.when(i % tiles_per_seq == 0)
    def _():
        for hist in hist_scrs:
            hist[:, 0:2 * CONV_PAD, :] = jnp.zeros((slabs, 2 * CONV_PAD, LANES), F32)

    def project(c):
        pre = _dot(xn, w_ref[:, c * n_chunk:(c + 1) * n_chunk])
        for s in range(slabs):
            hist_scrs[c][s, even_rows(CONV_PAD, tm), :] = pre[:, s * LANES:(s + 1) * LANES]

    def conv(c):
        sl = slice(c * n_chunk, (c + 1) * n_chunk)
        hist = hist_scrs[c]
        parts = []
        for s in range(slabs):
            y = None
            for tap in range(CONV_WIDTH):
                off = CONV_PAD - (CONV_WIDTH - 1) + tap
                w_tap = cw_ref[tap:tap + 1, (c * slabs + s) * LANES:(c * slabs + s + 1) * LANES]
                term = w_tap * hist[s, even_rows(off, tm), :]
                y = term if y is None else y + term
            parts.append(y)
            hist[s, even_rows(0, CONV_PAD), :] = hist[s, even_rows(tm, CONV_PAD), :]
        y = jnp.concatenate(parts, axis=1)
        y = y * jax.nn.sigmoid(y)
        if c * n_chunk < width:
            q_ref[:, sl] = (y * scale).astype(BF16)
        else:
            kt_ref[c * n_chunk - width:(c + 1) * n_chunk - width, :] = y.T.astype(BF16)

    def plain(c):
        sl = slice((c // 2) * n_chunk, (c // 2 + 1) * n_chunk)
        if c % 2 == 0:
            v_ref[:, sl] = _dot(xn, w_ref[:, 2 * width + sl.start:2 * width + sl.stop]).astype(BF16)
        else:
            o_ref[:, sl] = _dot(xn, w_ref[:, 3 * width + sl.start:3 * width + sl.stop])

    n_plain = 2 * width // n_chunk
    for c in range(n_conv + 1):
        if c < n_conv:
            project(c)
        if c >= 1:
            conv(c - 1)
            if c - 1 < n_plain:
                plain(c - 1)
    for c in range(n_conv, n_plain):
        plain(c)
    gate_ref[...] = _dot(xn, wg_ref[...]) + bg_ref[...]


def _mlstm_proj(x, g, w_in, w_gates, conv_w, b_gates, *, seq, tm=512, n_chunk=512):
    m, d = x.shape
    width = conv_w.shape[1] // 2
    assert m % tm == 0 and seq % tm == 0 and width % n_chunk == 0 and w_in.shape[1] >= 4 * width
    row = lambda wd: pl.BlockSpec((tm, wd), lambda i: (i, 0))
    scale = 1.0 / math.sqrt(width // N_HEADS_MLSTM)
    return pl.pallas_call(
        functools.partial(_mlstm_proj_kernel, tiles_per_seq=seq // tm, n_chunk=n_chunk, scale=scale),
        grid=(m // tm,),
        in_specs=[row(d), _resident((1, d)), _resident(w_in.shape), _resident((d, 2 * LANES)),
                  _resident(conv_w.shape), _resident((1, 2 * LANES))],
        out_specs=[row(width), pl.BlockSpec((width, tm), lambda i: (0, i)), row(width), row(width),
                   row(2 * LANES)],
        out_shape=[jax.ShapeDtypeStruct((m, width), BF16), jax.ShapeDtypeStruct((width, m), BF16),
                   jax.ShapeDtypeStruct((m, width), BF16), jax.ShapeDtypeStruct((m, width), F32),
                   jax.ShapeDtypeStruct((m, 2 * LANES), F32)],
        scratch_shapes=[pltpu.VMEM((n_chunk // LANES, 2 * (tm + CONV_PAD), LANES), F32)
                        for _ in range(2 * width // n_chunk)],
        compiler_params=_params(("arbitrary",)),
        name="mlstm_proj",
    )(x, g.reshape(1, d), w_in, w_gates, conv_w, b_gates)


def _split3(x):
    hi = x.astype(BF16)
    r = x - hi.astype(F32)
    mid = r.astype(BF16)
    return hi, mid, (r - mid.astype(F32)).astype(BF16)


def _gate_prep_kernel(g_ref, cols_ref, rows_ref, *, heads, cl):
    tm = g_ref.shape[0]
    row = lax.broadcasted_iota(jnp.int32, (cl, cl), 0)
    col = lax.broadcasted_iota(jnp.int32, (cl, cl), 1)
    incl = jnp.where(row >= col, 1.0, 0.0).astype(BF16)
    lane = lax.broadcasted_iota(jnp.int32, (1, cl), 1)
    pad = jnp.zeros((cl - heads, cl), F32)
    spans = [slice(c * cl, (c + 1) * cl) for c in range(tm // cl)]
    bs = []
    for r in spans:
        b = None
        for part in _split3(jax.nn.log_sigmoid(g_ref[r, LANES:])):
            term = _dot(incl, part)
            b = term if b is None else b + term
        bs.append(b)
    us = [g_ref[r, :LANES] - b for r, b in zip(spans, bs)]
    u_rows = jnp.concatenate([u.T[0:heads, :] for u in us], axis=0)
    b_rows = jnp.concatenate([b.T[0:heads, :] for b in bs], axis=0)
    cmax = u_rows
    shift = 1
    while shift < cl:
        cmax = jnp.where(lane >= shift, jnp.maximum(cmax, pltpu.roll(cmax, shift, axis=1)), cmax)
        shift *= 2
    u_max = jnp.broadcast_to(jnp.max(u_rows, axis=1, keepdims=True), u_rows.shape)
    b_last = jnp.broadcast_to(jnp.min(b_rows, axis=1, keepdims=True), b_rows.shape)
    for c, (r, u, b) in enumerate(zip(spans, us, bs)):
        hs = slice(c * heads, (c + 1) * heads)
        cols_ref[r, :] = jnp.concatenate([u, b, jnp.concatenate([cmax[hs], pad], axis=0).T], axis=1)
        rows_ref[:, r] = jnp.concatenate([u_rows[hs], u_max[hs], b_last[hs]], axis=0)


def _gate_prep(gates, *, heads, cl, tm=1024):
    m = gates.shape[0]
    tm = min(tm, m)
    assert m % tm == 0 and tm % cl == 0 and cl == LANES and heads == 8
    return pl.pallas_call(
        functools.partial(_gate_prep_kernel, heads=heads, cl=cl),
        grid=(m // tm,),
        in_specs=[pl.BlockSpec((tm, 2 * LANES), lambda i: (i, 0))],
        out_specs=[pl.BlockSpec((tm, 3 * LANES), lambda i: (i, 0)),
                   pl.BlockSpec((3 * heads, tm), lambda i: (0, i))],
        out_shape=[jax.ShapeDtypeStruct((m, 3 * LANES), F32),
                   jax.ShapeDtypeStruct((3 * heads, m), F32)],
        compiler_params=_params(("parallel",)),
        name="gate_prep",
    )(gates)


def _mlstm_kernel(q_ref, kt_ref, v_ref, o_ref, gcol_ref, grow_ref, hg_ref, out_ref,
                  s_scr, m_scr, ml_scr, *, heads):
    cl = MLSTM_CHUNK
    n_chunks = q_ref.shape[0] // cl
    dh = q_ref.shape[1] // heads

    @pl.when(pl.program_id(1) == 0)
    def _():
        s_scr[...] = jnp.zeros_like(s_scr)
        m_scr[...] = jnp.zeros_like(m_scr)
        ml_scr[...] = jnp.zeros_like(ml_scr)

    row = lax.broadcasted_iota(jnp.int32, (cl, cl), 0)
    col = lax.broadcasted_iota(jnp.int32, (cl, cl), 1)
    lower = row >= col
    ones = jnp.ones((cl, dh), BF16)

    m_rows = m_scr[...]
    m_lane = ml_scr[0:1, :]
    tables = []
    for c in range(n_chunks):
        span = slice(c * cl, (c + 1) * cl)
        u_cols, b_cols, cmax_cols = (gcol_ref[span, t * LANES:(t + 1) * LANES] for t in range(3))
        u_rows, u_max_rows, b_last_rows = (grow_ref[t * heads:(t + 1) * heads, span] for t in range(3))
        mm_last_rows = jnp.maximum(m_rows, u_max_rows)
        mm_cols = jnp.maximum(cmax_cols, m_lane)
        tables.append(dict(
            span=span, u_rows=u_rows, m_prev_rows=m_rows, mm_cols=mm_cols,
            decay_rows=jnp.exp(m_rows - mm_last_rows),
            ws_rows=jnp.exp(u_rows - mm_last_rows),
            floor_cols=jnp.exp(-(b_cols + mm_cols))))
        m_rows = b_last_rows + mm_last_rows
        m_lane = b_cols[cl - 1:cl, :] + mm_cols[cl - 1:cl, :]
    m_scr[...] = m_rows
    ml_scr[...] = jnp.broadcast_to(m_lane, ml_scr.shape)

    def lanes_of(mat, c):
        return jnp.broadcast_to(mat[:, c:c + 1], (cl, LANES))

    def stage_a(tab, h):
        sl = slice(h * dh, (h + 1) * dh)
        q, kt = q_ref[tab["span"], sl], kt_ref[sl, tab["span"]]
        v1 = jnp.concatenate([v_ref[tab["span"], sl], ones], axis=1)
        s_prev = s_scr[h]
        qk = _dot(q, kt)
        qs = _dot(q, s_prev.astype(BF16))
        kw = (kt.astype(F32) * tab["ws_rows"][h:h + 1, :]).astype(BF16)
        decay = jnp.broadcast_to(tab["decay_rows"][h:h + 1, :], (dh, LANES))
        s_scr[h] = jnp.concatenate([decay, decay], axis=1) * s_prev + _dot(kw, v1)
        return tab, h, sl, v1, qk, qs

    def stage_b(tab, h, sl, v1, qk, qs):
        mm = lanes_of(tab["mm_cols"], h)
        weight = jnp.where(lower, jnp.exp(tab["u_rows"][h:h + 1, :] - mm), 0.0)
        w_inter = jnp.exp(tab["m_prev_rows"][h:h + 1, :] - mm)
        return tab, sl, v1, qs, (qk * weight).astype(BF16), w_inter, lanes_of(tab["floor_cols"], h)

    def stage_c(tab, sl, v1, qs, p, w_inter, floor):
        pv = _dot(p, v1)
        both = pv + jnp.concatenate([w_inter, w_inter], axis=1) * qs
        hid = both[:, :dh] / jnp.maximum(jnp.abs(both[:, dh:]), floor)
        hid = hid * lax.rsqrt(jnp.mean(hid * hid, axis=-1, keepdims=True) + EPS) * hg_ref[:, sl]
        out_ref[tab["span"], sl] = (hid * jax.nn.sigmoid(o_ref[tab["span"], sl])).astype(out_ref.dtype)

    jobs = [(tab, h) for tab in tables for h in range(heads)]
    after_a, after_b = {}, {}
    for j in range(len(jobs) + 2):
        if j < len(jobs):
            after_a[j] = stage_a(*jobs[j])
        if 0 <= j - 1 < len(jobs):
            after_b[j - 1] = stage_b(*after_a.pop(j - 1))
        if 0 <= j - 2 < len(jobs):
            stage_c(*after_b.pop(j - 2))


def _mlstm(q, kt, v, o, gates, head_g, *, heads=N_HEADS_MLSTM, chunks=4):
    b, t, w = q.shape
    cl = MLSTM_CHUNK
    dh = w // heads
    rows = chunks * cl
    nc = t // rows
    assert t % rows == 0 and dh == LANES and cl == LANES
    gcols, grows = _gate_prep(gates, heads=heads, cl=cl)
    blk = lambda wd: pl.BlockSpec((None, rows, wd), lambda bi, c: (bi, c, 0))
    return pl.pallas_call(
        functools.partial(_mlstm_kernel, heads=heads),
        grid=(b, nc),
        in_specs=[blk(w), pl.BlockSpec((w, rows), lambda bi, c: (0, bi * nc + c)), blk(w), blk(w),
                  pl.BlockSpec((rows, 3 * LANES), lambda bi, c: (bi * nc + c, 0)),
                  pl.BlockSpec((3 * heads, rows), lambda bi, c: (0, bi * nc + c)),
                  _resident((1, w))],
        out_specs=blk(w),
        out_shape=jax.ShapeDtypeStruct((b, t, w), BF16),
        scratch_shapes=[pltpu.VMEM((heads, dh, 2 * dh), F32), pltpu.VMEM((heads, LANES), F32),
                        pltpu.VMEM((heads, LANES), F32)],
        compiler_params=_params(("parallel", "arbitrary")),
        name="mlstm",
    )(q, kt, v, o, gcols, grows, head_g.reshape(1, w))


def kernel(x, norm_g, ffn_w_gate, ffn_w_up, ffn_w_down, attn_w_in, attn_w_out, rel_bias,
           mlstm_w_in, mlstm_b_gates, mlstm_conv_w, mlstm_head_g, mlstm_w_out):
    bsz, t, d = x.shape
    depth = norm_g.shape[0]
    h = x.reshape(bsz * t, d)
    bf = lambda a: a.astype(BF16)

    w_gate, w_up, w_down = bf(ffn_w_gate), bf(ffn_w_up), bf(ffn_w_down)

    def ffn(h, layer, half, mix=None):
        g = norm_g[layer]
        return _ffn(h, g[2 * half * 2], g[2 * half * 2 + 1], w_gate, w_up, w_down, (layer, half), mix)

    for layer in range(depth):
        g = norm_g[layer]
        j = layer // 2
        h = ffn(h, layer, 0)
        if layer % 2 == 0:
            sb, dil = _attn_proj(h, g[2], bf(attn_w_in[j]))
            out_sb = _sb_attention(sb.reshape(bsz, t, -1))
            out_dil = _dil_attention(dil.reshape(bsz, t, -1), rel_bias.astype(F32))
            mix = ([out_sb.reshape(bsz * t, -1), out_dil.reshape(bsz * t, -1)], bf(attn_w_out[j]), g[3])
        else:
            width = mlstm_w_out.shape[1]
            n_head = mlstm_b_gates.shape[1] // 2
            tiles = lambda a: jnp.concatenate(
                [jnp.pad(part, ((0, 0), (0, LANES - n_head)))
                 for part in (a[:, :n_head], a[:, n_head:])], axis=1)
            w_in = bf(mlstm_w_in[j][:, :4 * width])
            w_gates = tiles(bf(mlstm_w_in[j][:, 4 * width:]))
            b_gates = tiles(mlstm_b_gates[j].astype(F32).reshape(1, -1))
            q, kt, v, o, gates = _mlstm_proj(h, g[2], w_in, w_gates, mlstm_conv_w[j].astype(F32),
                                             b_gates, seq=t)
            r3 = lambda a: a.reshape(bsz, t, -1)
            hid = _mlstm(r3(q), kt, r3(v), r3(o), gates, mlstm_head_g[j].astype(F32))
            mix = ([hid.reshape(bsz * t, width)], bf(mlstm_w_out[j]), g[3])
        h = ffn(h, layer, 1, mix)
    return h.reshape(bsz, t, d)
```

```python
import functools
import math

import numpy as np
import jax
import jax.numpy as jnp
from jax import lax
from jax.experimental import pallas as pl
from jax.experimental.pallas import tpu as pltpu

EPS = 1e-6
HEAD_DIM_ATTN = 64
DIL_CONFIGS = ((128, 1), (512, 4), (2048, 16))
DIL_BLOCK = 128
DIL_UNROLL = 16
NUM_BUCKETS = 32
MAX_DISTANCE = 2048
N_HEADS_MLSTM = 8
MLSTM_CHUNK = 128
CONV_WIDTH = 4
LANES = 128
CONV_PAD = 8
MASKED = -1e30
V7X_VMEM_BYTES = 64 * 1024 * 1024
VMEM_LIMIT = V7X_VMEM_BYTES * 7 // 8

F32 = jnp.float32
BF16 = jnp.bfloat16


def _params(sem, vmem=VMEM_LIMIT):
    return pltpu.CompilerParams(dimension_semantics=sem, vmem_limit_bytes=vmem)


def _resident(shape):
    zeros = (0,) * len(shape)
    return pl.BlockSpec(shape, lambda *_: zeros, pipeline_mode=pl.Buffered(1))


def _rms(x, g):
    return x * lax.rsqrt(jnp.mean(x * x, axis=-1, keepdims=True) + EPS) * g


def _dot(a, b):
    return jnp.dot(a, b, preferred_element_type=F32)


def _dot_nt(a, b):
    return lax.dot_general(a, b, (((1,), (1,)), ((), ())), preferred_element_type=F32)


def _ffn_kernel(*refs, ff_chunk, n_mix, groups):
    mix_refs, refs = refs[:n_mix], refs[n_mix:]
    if n_mix:
        (wmix_ref, gmix_ref), refs = refs[:2], refs[2:]
    x_ref, gin_ref, gout_ref, wg_ref, wu_ref, wd_ref, o_ref = refs
    d_ff = wg_ref.shape[1]
    rows = x_ref.shape[0] // groups
    spans = [slice(r * rows, (r + 1) * rows) for r in range(groups)]

    xs = []
    for rs in spans:
        x = x_ref[rs, :]
        if n_mix:
            mixed = jnp.concatenate([r[rs, :] for r in mix_refs], axis=-1)
            x = x + _rms(_dot(mixed, wmix_ref[...]), gmix_ref[...])
        xs.append(x)
    xns = [_rms(x, gin_ref[...]).astype(BF16) for x in xs]
    accs = []
    for xn in xns:
        acc = None
        for c in range(d_ff // ff_chunk):
            sl = slice(c * ff_chunk, (c + 1) * ff_chunk)
            gate = _dot(xn, wg_ref[:, sl])
            up = _dot(xn, wu_ref[:, sl])
            h = (gate * jax.nn.sigmoid(gate) * up).astype(BF16)
            part = _dot(h, wd_ref[sl, :])
            acc = part if acc is None else acc + part
        accs.append(acc)
    for rs, x, acc in zip(spans, xs, accs):
        o_ref[rs, :] = x + 0.5 * _rms(acc, gout_ref[...])


def _ffn(x, g_in, g_out, wg, wu, wd, which=(), mix=None, *, tm=1024, groups=2, ff_chunk=256):
    m, d = x.shape
    d_ff = wg.shape[-1]
    assert m % tm == 0 and tm % groups == 0 and d_ff % ff_chunk == 0 and len(which) == wg.ndim - 2
    row = lambda width: pl.BlockSpec((tm, width), lambda i: (i, 0))
    lead = (None,) * len(which)
    weight = lambda shape: pl.BlockSpec(lead + shape, lambda i: tuple(which) + (0, 0),
                                        pipeline_mode=pl.Buffered(1))
    parts, w_mix, g_mix = mix if mix else ((), None, None)
    mix_specs = [row(a.shape[1]) for a in parts] + ([_resident(w_mix.shape), _resident((1, d))] if mix else [])
    mix_args = list(parts) + ([w_mix, g_mix.reshape(1, d)] if mix else [])
    return pl.pallas_call(
        functools.partial(_ffn_kernel, ff_chunk=ff_chunk, n_mix=len(parts), groups=groups),
        grid=(m // tm,),
        in_specs=mix_specs + [row(d), _resident((1, d)), _resident((1, d)),
                              weight((d, d_ff)), weight((d, d_ff)), weight((d_ff, d))],
        out_specs=row(d),
        out_shape=jax.ShapeDtypeStruct((m, d), F32),
        compiler_params=_params(("parallel",)),
        name="ffn",
    )(*mix_args, x, g_in.reshape(1, d), g_out.reshape(1, d), wg, wu, wd)


def _attn_proj_kernel(x_ref, g_ref, w_ref, sb_ref, dil_ref, *, n_chunk, scale, groups):
    w_sb = sb_ref.shape[1]
    w_q = w_sb // 3
    rows = x_ref.shape[0] // groups
    spans = [slice(r * rows, (r + 1) * rows) for r in range(groups)]
    xns = [_rms(x_ref[rs, :], g_ref[...]).astype(BF16) for rs in spans]
    for rs, xn in zip(spans, xns):
        for c in range(w_sb // n_chunk):
            sl = slice(c * n_chunk, (c + 1) * n_chunk)
            y = _dot(xn, w_ref[:, sl])
            if (c + 1) * n_chunk <= w_q:
                y = y * scale
            sb_ref[rs, sl] = y.astype(BF16)
        for c in range(dil_ref.shape[1] // n_chunk):
            sl = slice(c * n_chunk, (c + 1) * n_chunk)
            y = _dot(xn, w_ref[:, w_sb + c * n_chunk: w_sb + (c + 1) * n_chunk])
            if (c + 1) * n_chunk <= w_q:
                y = y * scale
            dil_ref[rs, sl] = y


def _attn_proj(x, g, w, *, tm=1024, n_chunk=512, groups=2):
    m, d = x.shape
    n = w.shape[1]
    half = n // 2
    assert m % tm == 0 and (half // 3) % n_chunk == 0
    scale = 1.0 / math.sqrt(HEAD_DIM_ATTN)
    return pl.pallas_call(
        functools.partial(_attn_proj_kernel, n_chunk=n_chunk, scale=scale, groups=groups),
        grid=(m // tm,),
        in_specs=[pl.BlockSpec((tm, d), lambda i: (i, 0)), _resident((1, d)), _resident((d, n))],
        out_specs=[pl.BlockSpec((tm, half), lambda i: (i, 0)),
                   pl.BlockSpec((tm, half), lambda i: (i, 0))],
        out_shape=[jax.ShapeDtypeStruct((m, half), BF16), jax.ShapeDtypeStruct((m, half), F32)],
        compiler_params=_params(("parallel",)),
        name="attn_proj",
    )(x, g.reshape(1, d), w)


LOG2E = math.log2(math.e)
SB_DEAD_LOG2 = -160.0


def _sb_kernel(q_ref, k_ref, v_ref, tri_ref, o_ref, *, blk, q_blocks):
    first_q = pl.program_id(2) * q_blocks
    lane = lax.broadcasted_iota(jnp.int32, (1, LANES), 1)
    row = lax.broadcasted_iota(jnp.int32, (blk, blk), 0)
    col = lax.broadcasted_iota(jnp.int32, (blk, blk), 1)
    causal = col < row
    tri = tri_ref[...]
    head0 = lane < HEAD_DIM_ATTN

    def one_head(q2, h):
        zeros = jnp.zeros_like(q2)
        return jnp.where(head0, q2, zeros) if h == 0 else jnp.where(head0, zeros, q2)

    def pairs(jobs):
        chains = []
        for j, (qq, kb, diag, _) in enumerate(jobs):
            chains.append((j, pl.multiple_of(kb * blk, blk), causal if diag else None))
            chains.append((j, pl.multiple_of(jnp.maximum(kb - 1, 0) * blk, blk), kb >= 1))
        states = [job[3] for job in jobs]
        zs, mids = {}, {}
        for n in range(len(chains) + 2):
            if n < len(chains):
                j, ks, _ = chains[n]
                zs[n] = _dot_nt(jobs[j][0], k_ref[pl.ds(ks, blk), :]) * LOG2E
            if 0 <= n - 1 < len(chains):
                z, mask = zs.pop(n - 1), chains[n - 1][2]
                neg = -z
                log_keep = jnp.minimum(neg, 0.0) - jnp.log2(1.0 + jnp.exp2(jnp.minimum(z, neg)))
                log_beta = z + log_keep
                if mask is not None:
                    log_keep = jnp.where(mask, log_keep, 0.0)
                    log_beta = jnp.where(mask, log_beta, MASKED)
                hi = log_keep.astype(BF16)
                lo = (log_keep - hi.astype(F32)).astype(BF16)
                later = _dot(hi, tri) + _dot(lo, tri)
                mids[n - 1] = (log_beta, later, jnp.sum(log_keep, axis=-1, keepdims=True))
            if 0 <= n - 2 < len(chains):
                j, ks, _ = chains[n - 2]
                log_beta, later, total = mids.pop(n - 2)
                carry, acc = states[j]
                p = jnp.exp2(log_beta + later + carry)
                states[j] = (carry + total, acc + _dot(p.astype(BF16), v_ref[pl.ds(ks, blk), :]))
        return states

    def alive(state):
        return (jnp.max(state[0]) > SB_DEAD_LOG2).astype(jnp.int32)

    zero = (jnp.zeros((blk, 1), F32), jnp.zeros((blk, LANES), F32))
    qqs = [one_head(q_ref[g * blk:(g + 1) * blk, :], h) for g in range(q_blocks) for h in range(2)]
    states = pairs([(qq, first_q + n // 2, True, zero) for n, qq in enumerate(qqs)])

    def store(g, both):
        o_ref[g * blk:(g + 1) * blk, :] = jnp.where(head0, both[0][1], both[1][1]).astype(o_ref.dtype)

    live = [jnp.maximum(alive(states[2 * g]), alive(states[2 * g + 1])) for g in range(q_blocks)]
    for g in range(q_blocks):
        store(g, states[2 * g:2 * g + 2])

    for g in range(q_blocks):
        i = first_q + g
        n_pairs = lax.shift_right_logical(i, 1)

        @pl.when((live[g] > 0) & (n_pairs > 0))
        def _(g=g, i=i, n_pairs=n_pairs):
            def cond(loop):
                t, alive_now, _ = loop
                return (t < n_pairs) & (alive_now > 0)

            def body(loop):
                t, _, both = loop
                kb = i - 2 - 2 * t
                both = tuple(pairs([(qqs[2 * g + h], kb, False, both[h]) for h in range(2)]))
                return t + 1, jnp.maximum(alive(both[0]), alive(both[1])), both

            _, _, final = lax.while_loop(cond, body, (jnp.int32(0), live[g], tuple(states[2 * g:2 * g + 2])))
            store(g, final)


def _sb_attention(qkv, *, blk=256, q_blocks=4):
    b, t, w3 = qkv.shape
    w = w3 // 3
    pairs = w // LANES
    blk = min(blk, t)
    rows = blk * q_blocks
    assert t % rows == 0
    tri = jnp.asarray(np.tril(np.ones((blk, blk), np.float32), -1), BF16)
    return pl.pallas_call(
        functools.partial(_sb_kernel, blk=blk, q_blocks=q_blocks),
        grid=(b, pairs, t // rows),
        in_specs=[pl.BlockSpec((None, rows, LANES), lambda bi, p, i: (bi, i, p)),
                  pl.BlockSpec((None, t, LANES), lambda bi, p, i: (bi, 0, pairs + p)),
                  pl.BlockSpec((None, t, LANES), lambda bi, p, i: (bi, 0, 2 * pairs + p)),
                  _resident((blk, blk))],
        out_specs=pl.BlockSpec((None, rows, LANES), lambda bi, p, i: (bi, i, p)),
        out_shape=jax.ShapeDtypeStruct((b, t, w), BF16),
        compiler_params=_params(("parallel", "parallel", "arbitrary")),
        name="sb_attn",
    )(qkv, qkv, qkv, tri)


def _t5_bucket_np(dist):
    max_exact = NUM_BUCKETS // 2
    d = np.maximum(dist, 1).astype(np.float32)
    log_b = max_exact + (np.log(d / np.float32(max_exact)) / np.float32(math.log(MAX_DISTANCE / max_exact))
                         * np.float32(NUM_BUCKETS - max_exact)).astype(np.int32)
    log_b = np.minimum(log_b, NUM_BUCKETS - 1)
    return np.where(dist < max_exact, dist, log_b)


def _dil_tables():
    qi = np.arange(DIL_BLOCK)[:, None]
    ki = np.arange(2 * DIL_BLOCK)[None, :]
    dist = qi + DIL_BLOCK - ki
    buckets, valid = [], []
    for window, dil in DIL_CONFIGS:
        steps = window // dil
        buckets.append(_t5_bucket_np(np.maximum(dist, 0) * dil))
        valid.append((dist >= 0) & (dist <= steps))
    return np.stack(buckets).astype(np.int32), np.stack(valid).astype(np.int32)


def _dil_kernel(rb_ref, bucket_ref, valid_ref, q_ref, k_ref, v_ref, o_ref,
                bias_scr, num_scr, m_scr, l_scr, *, seq):
    p = pl.program_id(0)
    qb = DIL_BLOCK
    lane = lax.broadcasted_iota(jnp.int32, (1, LANES), 1)
    head0 = lane < HEAD_DIM_ATTN
    first_half = lax.broadcasted_iota(jnp.int32, (1, 2 * qb), 1) < qb

    @pl.when(pl.program_id(1) == 0)
    def _():
        for br in range(len(DIL_CONFIGS)):
            bucket = bucket_ref[br]
            valid = valid_ref[br] > 0
            for h in range(2):
                bias = jnp.zeros((qb, 2 * qb), F32)
                for b in range(NUM_BUCKETS):
                    bias = jnp.where(bucket == b, rb_ref[b, 2 * p + h], bias)
                bias_scr[br, h * qb:(h + 1) * qb, :] = jnp.where(valid, bias, MASKED)

    for br, (_, dil) in enumerate(DIL_CONFIGS):
        n_units = seq // qb

        n_blocks = seq // (qb * dil)
        run = min(DIL_UNROLL, n_blocks)
        runs = DIL_UNROLL // run
        assert DIL_UNROLL % run == 0 and n_blocks % run == 0 and (dil == 1 or n_blocks == run)

        def rows(ref, start, dil=dil):
            if dil == 1:
                return ref[pl.ds(start, qb), :].astype(BF16)
            return ref[pl.ds(start, qb, stride=dil), :].astype(BF16)

        def load_run(rho, dil=dil, run=run, whole=(n_blocks == run)):
            res, base = (rho, 0) if whole else (0, rho * run)
            starts = [(base + i) * (qb * dil) + res for i in range(run)]
            kb = [rows(k_ref, st) for st in starts]
            vb = [rows(v_ref, st) for st in starts]
            if whole:
                k_prev, v_prev = kb[0], vb[0]
                pen = jnp.where(first_half, MASKED, 0.0)
            else:
                st = jnp.maximum(base - 1, 0) * (qb * dil) + res
                k_prev, v_prev = rows(k_ref, st), rows(v_ref, st)
                pen = jnp.where(first_half, jnp.where(base == 0, MASKED, 0.0), 0.0)
            units = []
            for i, st in enumerate(starts):
                q2 = rows(q_ref, st)
                zeros = jnp.zeros_like(q2)
                qq = jnp.concatenate([jnp.where(head0, q2, zeros), jnp.where(head0, zeros, q2)], axis=0)
                k2 = jnp.concatenate([kb[i - 1] if i else k_prev, kb[i]], axis=0)
                v2 = jnp.concatenate([vb[i - 1] if i else v_prev, vb[i]], axis=0)
                units.append((st, pen if i == 0 else None, qq, k2, v2))
            return units

        def softmax_parts(s):
            m = jnp.max(s, axis=-1, keepdims=True)
            e = jnp.exp(s - m)
            return m, e, jnp.sum(e, axis=-1, keepdims=True)

        def group(g, _, br=br, dil=dil, runs=runs):
            units = [u for r in range(runs) for u in load_run(g * runs + r)]
            scores, parts = {}, {}
            for j in range(DIL_UNROLL + 2):
                if j < DIL_UNROLL:
                    _, pen, qq, k2, _ = units[j]
                    scores[j] = _dot_nt(qq, k2) + bias_scr[br]
                    if pen is not None:
                        scores[j] = scores[j] + pen
                if 0 <= j - 1 < DIL_UNROLL:
                    parts[j - 1] = softmax_parts(scores.pop(j - 1))
                if 0 <= j - 2 < DIL_UNROLL:
                    st, _, _, _, v2 = units[j - 2]
                    m, e, l = parts.pop(j - 2)
                    pv = _dot(e.astype(BF16), v2)
                    idx = pl.ds(st, qb) if dil == 1 else pl.ds(st, qb, stride=dil)
                    num_scr[br, idx, :] = jnp.where(head0, pv[:qb], pv[qb:])
                    m_scr[br, idx, :] = jnp.where(head0, m[:qb], m[qb:])
                    l_scr[br, idx, :] = jnp.where(head0, l[:qb], l[qb:])
            return 0

        lax.fori_loop(0, n_units // DIL_UNROLL, group, 0)

    rows_out = 256

    def finish(c, _):
        idx = pl.ds(pl.multiple_of(c * rows_out, rows_out), rows_out)
        m_all = [m_scr[br, idx, :] for br in range(len(DIL_CONFIGS))]
        m_max = functools.reduce(jnp.maximum, m_all)
        num = den = None
        for br, m_br in enumerate(m_all):
            wt = jnp.exp(m_br - m_max)
            n_br = wt * num_scr[br, idx, :]
            d_br = wt * l_scr[br, idx, :]
            num = n_br if num is None else num + n_br
            den = d_br if den is None else den + d_br
        o_ref[idx, :] = (num / den).astype(o_ref.dtype)
        return 0

    lax.fori_loop(0, seq // rows_out, finish, 0)


def _dil_attention(qkv, rel_bias):
    b, t, w3 = qkv.shape
    w = w3 // 3
    pairs = w // LANES
    assert t % (DIL_BLOCK * max(d for _, d in DIL_CONFIGS)) == 0 and t % 256 == 0
    bucket, valid = _dil_tables()
    nbr = len(DIL_CONFIGS)
    seq_spec = lambda off: pl.BlockSpec((None, t, LANES), lambda p, bi: (bi, 0, off + p))
    return pl.pallas_call(
        functools.partial(_dil_kernel, seq=t),
        grid=(pairs, b),
        in_specs=[pl.BlockSpec(memory_space=pltpu.SMEM),
                  _resident((nbr, DIL_BLOCK, 2 * DIL_BLOCK)),
                  _resident((nbr, DIL_BLOCK, 2 * DIL_BLOCK)),
                  seq_spec(0), seq_spec(pairs), seq_spec(2 * pairs)],
        out_specs=pl.BlockSpec((None, t, LANES), lambda p, bi: (bi, 0, p)),
        out_shape=jax.ShapeDtypeStruct((b, t, w), BF16),
        scratch_shapes=[pltpu.VMEM((nbr, 2 * DIL_BLOCK, 2 * DIL_BLOCK), F32),
                        pltpu.VMEM((nbr, t, LANES), F32), pltpu.VMEM((nbr, t, LANES), F32),
                        pltpu.VMEM((nbr, t, LANES), F32)],
        compiler_params=_params(("arbitrary", "arbitrary")),
        name="dil_attn",
    )(rel_bias, jnp.asarray(bucket), jnp.asarray(valid), qkv, qkv, qkv)


def _mlstm_proj_kernel(x_ref, g_ref, w_ref, wg_ref, cw_ref, bg_ref, q_ref, kt_ref, v_ref, o_ref,
                       gate_ref, *hist_scrs, tiles_per_seq, n_chunk, scale):
    i = pl.program_id(0)
    tm = x_ref.shape[0]
    width = q_ref.shape[1]
    xn = _rms(x_ref[...], g_ref[...]).astype(BF16)

    def even_rows(first, n):
        return pl.ds(2 * first, n, stride=2)

    slabs = n_chunk // LANES
    n_conv = 2 * width // n_chunk
    assert len(hist_scrs) == n_conv

    @pl.when(i % tiles_per_seq == 0)
    def _():
        for hist in hist_scrs:
            hist[:, 0:2 * CONV_PAD, :] = jnp.zeros((slabs, 2 * CONV_PAD, LANES), F32)

    def project(c):
        pre = _dot(xn, w_ref[:, c * n_chunk:(c + 1) * n_chunk])
        for s in range(slabs):
            hist_scrs[c][s, even_rows(CONV_PAD, tm), :] = pre[:, s * LANES:(s + 1) * LANES]

    def conv(c):
        sl = slice(c * n_chunk, (c + 1) * n_chunk)
        hist = hist_scrs[c]
        parts = []
        for s in range(slabs):
            y = None
            for tap in range(CONV_WIDTH):
                off = CONV_PAD - (CONV_WIDTH - 1) + tap
                w_tap = cw_ref[tap:tap + 1, (c * slabs + s) * LANES:(c * slabs + s + 1) * LANES]
                term = w_tap * hist[s, even_rows(off, tm), :]
                y = term if y is None else y + term
            parts.append(y)
            hist[s, even_rows(0, CONV_PAD), :] = hist[s, even_rows(tm, CONV_PAD), :]
        y = jnp.concatenate(parts, axis=1)
        y = y * jax.nn.sigmoid(y)
        if c * n_chunk < width:
            q_ref[:, sl] = (y * scale).astype(BF16)
        else:
            kt_ref[c * n_chunk - width:(c + 1) * n_chunk - width, :] = y.T.astype(BF16)

    def plain(c):
        sl = slice((c // 2) * n_chunk, (c // 2 + 1) * n_chunk)
        if c % 2 == 0:
            v_ref[:, sl] = _dot(xn, w_ref[:, 2 * width + sl.start:2 * width + sl.stop]).astype(BF16)
        else:
            o_ref[:, sl] = _dot(xn, w_ref[:, 3 * width + sl.start:3 * width + sl.stop])

    n_plain = 2 * width // n_chunk
    for c in range(n_conv + 1):
        if c < n_conv:
            project(c)
        if c >= 1:
            conv(c - 1)
            if c - 1 < n_plain:
                plain(c - 1)
    for c in range(n_conv, n_plain):
        plain(c)
    gate_ref[...] = _dot(xn, wg_ref[...]) + bg_ref[...]


def _mlstm_proj(x, g, w_in, w_gates, conv_w, b_gates, *, seq, tm=512, n_chunk=512):
    m, d = x.shape
    width = conv_w.shape[1] // 2
    assert m % tm == 0 and seq % tm == 0 and width % n_chunk == 0 and w_in.shape[1] >= 4 * width
    row = lambda wd: pl.BlockSpec((tm, wd), lambda i: (i, 0))
    scale = 1.0 / math.sqrt(width // N_HEADS_MLSTM)
    return pl.pallas_call(
        functools.partial(_mlstm_proj_kernel, tiles_per_seq=seq // tm, n_chunk=n_chunk, scale=scale),
        grid=(m // tm,),
        in_specs=[row(d), _resident((1, d)), _resident(w_in.shape), _resident((d, 2 * LANES)),
                  _resident(conv_w.shape), _resident((1, 2 * LANES))],
        out_specs=[row(width), pl.BlockSpec((width, tm), lambda i: (0, i)), row(width), row(width),
                   row(2 * LANES)],
        out_shape=[jax.ShapeDtypeStruct((m, width), BF16), jax.ShapeDtypeStruct((width, m), BF16),
                   jax.ShapeDtypeStruct((m, width), BF16), jax.ShapeDtypeStruct((m, width), F32),
                   jax.ShapeDtypeStruct((m, 2 * LANES), F32)],
        scratch_shapes=[pltpu.VMEM((n_chunk // LANES, 2 * (tm + CONV_PAD), LANES), F32)
                        for _ in range(2 * width // n_chunk)],
        compiler_params=_params(("arbitrary",)),
        name="mlstm_proj",
    )(x, g.reshape(1, d), w_in, w_gates, conv_w, b_gates)


def _split3(x):
    hi = x.astype(BF16)
    r = x - hi.astype(F32)
    mid = r.astype(BF16)
    return hi, mid, (r - mid.astype(F32)).astype(BF16)


def _gate_prep_kernel(g_ref, cols_ref, rows_ref, *, heads, cl):
    tm = g_ref.shape[0]
    row = lax.broadcasted_iota(jnp.int32, (cl, cl), 0)
    col = lax.broadcasted_iota(jnp.int32, (cl, cl), 1)
    incl = jnp.where(row >= col, 1.0, 0.0).astype(BF16)
    lane = lax.broadcasted_iota(jnp.int32, (1, cl), 1)
    pad = jnp.zeros((cl - heads, cl), F32)
    spans = [slice(c * cl, (c + 1) * cl) for c in range(tm // cl)]
    bs = []
    for r in spans:
        b = None
        for part in _split3(jax.nn.log_sigmoid(g_ref[r, LANES:])):
            term = _dot(incl, part)
            b = term if b is None else b + term
        bs.append(b)
    us = [g_ref[r, :LANES] - b for r, b in zip(spans, bs)]
    u_rows = jnp.concatenate([u.T[0:heads, :] for u in us], axis=0)
    b_rows = jnp.concatenate([b.T[0:heads, :] for b in bs], axis=0)
    cmax = u_rows
    shift = 1
    while shift < cl:
        cmax = jnp.where(lane >= shift, jnp.maximum(cmax, pltpu.roll(cmax, shift, axis=1)), cmax)
        shift *= 2
    u_max = jnp.broadcast_to(jnp.max(u_rows, axis=1, keepdims=True), u_rows.shape)
    b_last = jnp.broadcast_to(jnp.min(b_rows, axis=1, keepdims=True), b_rows.shape)
    for c, (r, u, b) in enumerate(zip(spans, us, bs)):
        hs = slice(c * heads, (c + 1) * heads)
        cols_ref[r, :] = jnp.concatenate([u, b, jnp.concatenate([cmax[hs], pad], axis=0).T], axis=1)
        rows_ref[:, r] = jnp.concatenate([u_rows[hs], u_max[hs], b_last[hs]], axis=0)


def _gate_prep(gates, *, heads, cl, tm=1024):
    m = gates.shape[0]
    tm = min(tm, m)
    assert m % tm == 0 and tm % cl == 0 and cl == LANES and heads == 8
    return pl.pallas_call(
        functools.partial(_gate_prep_kernel, heads=heads, cl=cl),
        grid=(m // tm,),
        in_specs=[pl.BlockSpec((tm, 2 * LANES), lambda i: (i, 0))],
        out_specs=[pl.BlockSpec((tm, 3 * LANES), lambda i: (i, 0)),
                   pl.BlockSpec((3 * heads, tm), lambda i: (0, i))],
        out_shape=[jax.ShapeDtypeStruct((m, 3 * LANES), F32),
                   jax.ShapeDtypeStruct((3 * heads, m), F32)],
        compiler_params=_params(("parallel",)),
        name="gate_prep",
    )(gates)


def _mlstm_kernel(q_ref, kt_ref, v_ref, o_ref, gcol_ref, grow_ref, hg_ref, out_ref,
                  s_scr, m_scr, ml_scr, *, heads):
    cl = MLSTM_CHUNK
    n_chunks = q_ref.shape[0] // cl
    dh = q_ref.shape[1] // heads

    @pl.when(pl.program_id(1) == 0)
    def _():
        s_scr[...] = jnp.zeros_like(s_scr)
        m_scr[...] = jnp.zeros_like(m_scr)
        ml_scr[...] = jnp.zeros_like(ml_scr)

    row = lax.broadcasted_iota(jnp.int32, (cl, cl), 0)
    col = lax.broadcasted_iota(jnp.int32, (cl, cl), 1)
    lower = row >= col
    ones = jnp.ones((cl, dh), BF16)

    m_rows = m_scr[...]
    m_lane = ml_scr[0:1, :]
    tables = []
    for c in range(n_chunks):
        span = slice(c * cl, (c + 1) * cl)
        u_cols, b_cols, cmax_cols = (gcol_ref[span, t * LANES:(t + 1) * LANES] for t in range(3))
        u_rows, u_max_rows, b_last_rows = (grow_ref[t * heads:(t + 1) * heads, span] for t in range(3))
        mm_last_rows = jnp.maximum(m_rows, u_max_rows)
        mm_cols = jnp.maximum(cmax_cols, m_lane)
        tables.append(dict(
            span=span, u_rows=u_rows, m_prev_rows=m_rows, mm_cols=mm_cols,
            decay_rows=jnp.exp(m_rows - mm_last_rows),
            ws_rows=jnp.exp(u_rows - mm_last_rows),
            floor_cols=jnp.exp(-(b_cols + mm_cols))))
        m_rows = b_last_rows + mm_last_rows
        m_lane = b_cols[cl - 1:cl, :] + mm_cols[cl - 1:cl, :]
    m_scr[...] = m_rows
    ml_scr[...] = jnp.broadcast_to(m_lane, ml_scr.shape)

    def lanes_of(mat, c):
        return jnp.broadcast_to(mat[:, c:c + 1], (cl, LANES))

    def stage_a(tab, h):
        sl = slice(h * dh, (h + 1) * dh)
        q, kt = q_ref[tab["span"], sl], kt_ref[sl, tab["span"]]
        v1 = jnp.concatenate([v_ref[tab["span"], sl], ones], axis=1)
        s_prev = s_scr[h]
        qk = _dot(q, kt)
        qs = _dot(q, s_prev.astype(BF16))
        kw = (kt.astype(F32) * tab["ws_rows"][h:h + 1, :]).astype(BF16)
        decay = jnp.broadcast_to(tab["decay_rows"][h:h + 1, :], (dh, LANES))
        s_scr[h] = jnp.concatenate([decay, decay], axis=1) * s_prev + _dot(kw, v1)
        return tab, h, sl, v1, qk, qs

    def stage_b(tab, h, sl, v1, qk, qs):
        mm = lanes_of(tab["mm_cols"], h)
        weight = jnp.where(lower, jnp.exp(tab["u_rows"][h:h + 1, :] - mm), 0.0)
        w_inter = jnp.exp(tab["m_prev_rows"][h:h + 1, :] - mm)
        return tab, sl, v1, qs, (qk * weight).astype(BF16), w_inter, lanes_of(tab["floor_cols"], h)

    def stage_c(tab, sl, v1, qs, p, w_inter, floor):
        pv = _dot(p, v1)
        both = pv + jnp.concatenate([w_inter, w_inter], axis=1) * qs
        hid = both[:, :dh] / jnp.maximum(jnp.abs(both[:, dh:]), floor)
        hid = hid * lax.rsqrt(jnp.mean(hid * hid, axis=-1, keepdims=True) + EPS) * hg_ref[:, sl]
        out_ref[tab["span"], sl] = (hid * jax.nn.sigmoid(o_ref[tab["span"], sl])).astype(out_ref.dtype)

    jobs = [(tab, h) for tab in tables for h in range(heads)]
    after_a, after_b = {}, {}
    for j in range(len(jobs) + 2):
        if j < len(jobs):
            after_a[j] = stage_a(*jobs[j])
        if 0 <= j - 1 < len(jobs):
            after_b[j - 1] = stage_b(*after_a.pop(j - 1))
        if 0 <= j - 2 < len(jobs):
            stage_c(*after_b.pop(j - 2))


def _mlstm(q, kt, v, o, gates, head_g, *, heads=N_HEADS_MLSTM, chunks=4):
    b, t, w = q.shape
    cl = MLSTM_CHUNK
    dh = w // heads
    rows = chunks * cl
    nc = t // rows
    assert t % rows == 0 and dh == LANES and cl == LANES
    gcols, grows = _gate_prep(gates, heads=heads, cl=cl)
    blk = lambda wd: pl.BlockSpec((None, rows, wd), lambda bi, c: (bi, c, 0))
    return pl.pallas_call(
        functools.partial(_mlstm_kernel, heads=heads),
        grid=(b, nc),
        in_specs=[blk(w), pl.BlockSpec((w, rows), lambda bi, c: (0, bi * nc + c)), blk(w), blk(w),
                  pl.BlockSpec((rows, 3 * LANES), lambda bi, c: (bi * nc + c, 0)),
                  pl.BlockSpec((3 * heads, rows), lambda bi, c: (0, bi * nc + c)),
                  _resident((1, w))],
        out_specs=blk(w),
        out_shape=jax.ShapeDtypeStruct((b, t, w), BF16),
        scratch_shapes=[pltpu.VMEM((heads, dh, 2 * dh), F32), pltpu.VMEM((heads, LANES), F32),
                        pltpu.VMEM((heads, LANES), F32)],
        compiler_params=_params(("parallel", "arbitrary")),
        name="mlstm",
    )(q, kt, v, o, gcols, grows, head_g.reshape(1, w))


def kernel(x, norm_g, ffn_w_gate, ffn_w_up, ffn_w_down, attn_w_in, attn_w_out, rel_bias,
           mlstm_w_in, mlstm_b_gates, mlstm_conv_w, mlstm_head_g, mlstm_w_out):
    bsz, t, d = x.shape
    depth = norm_g.shape[0]
    h = x.reshape(bsz * t, d)
    bf = lambda a: a.astype(BF16)

    w_gate, w_up, w_down = bf(ffn_w_gate), bf(ffn_w_up), bf(ffn_w_down)

    def ffn(h, layer, half, mix=None):
        g = norm_g[layer]
        return _ffn(h, g[2 * half * 2], g[2 * half * 2 + 1], w_gate, w_up, w_down, (layer, half), mix)

    for layer in range(depth):
        g = norm_g[layer]
        j = layer // 2
        h = ffn(h, layer, 0)
        if layer % 2 == 0:
            sb, dil = _attn_proj(h, g[2], bf(attn_w_in[j]))
            out_sb = _sb_attention(sb.reshape(bsz, t, -1))
            out_dil = _dil_attention(dil.reshape(bsz, t, -1), rel_bias.astype(F32))
            mix = ([out_sb.reshape(bsz * t, -1), out_dil.reshape(bsz * t, -1)], bf(attn_w_out[j]), g[3])
        else:
            width = mlstm_w_out.shape[1]
            n_head = mlstm_b_gates.shape[1] // 2
            tiles = lambda a: jnp.concatenate(
                [jnp.pad(part, ((0, 0), (0, LANES - n_head)))
                 for part in (a[:, :n_head], a[:, n_head:])], axis=1)
            w_in = bf(mlstm_w_in[j])
            w_gates = tiles(w_in[:, 4 * width:])
            b_gates = tiles(mlstm_b_gates[j].astype(F32).reshape(1, -1))
            q, kt, v, o, gates = _mlstm_proj(h, g[2], w_in, w_gates, mlstm_conv_w[j].astype(F32),
                                             b_gates, seq=t)
            r3 = lambda a: a.reshape(bsz, t, -1)
            hid = _mlstm(r3(q), kt, r3(v), r3(o), gates, mlstm_head_g[j].astype(F32))
            mix = ([hid.reshape(bsz * t, width)], bf(mlstm_w_out[j]), g[3])
        h = ffn(h, layer, 1, mix)
    return h.reshape(bsz, t, d)
```

```python
import functools
import math

import numpy as np
import jax
import jax.numpy as jnp
from jax import lax
from jax.experimental import pallas as pl
from jax.experimental.pallas import tpu as pltpu

EPS = 1e-6
HEAD_DIM_ATTN = 64
DIL_CONFIGS = ((128, 1), (512, 4), (2048, 16))
DIL_BLOCK = 128
DIL_UNROLL = 32
NUM_BUCKETS = 32
MAX_DISTANCE = 2048
N_HEADS_MLSTM = 8
MLSTM_CHUNK = 128
CONV_WIDTH = 4
LANES = 128
CONV_PAD = 8
MASKED = -1e30
V7X_VMEM_BYTES = 64 * 1024 * 1024
VMEM_LIMIT = V7X_VMEM_BYTES * 7 // 8

F32 = jnp.float32
BF16 = jnp.bfloat16


def _params(sem, vmem=VMEM_LIMIT):
    return pltpu.CompilerParams(dimension_semantics=sem, vmem_limit_bytes=vmem)


def _resident(shape):
    zeros = (0,) * len(shape)
    return pl.BlockSpec(shape, lambda *_: zeros, pipeline_mode=pl.Buffered(1))


def _rms(x, g):
    return x * lax.rsqrt(jnp.mean(x * x, axis=-1, keepdims=True) + EPS) * g


def _dot(a, b):
    return jnp.dot(a, b, preferred_element_type=F32)


def _dot_nt(a, b):
    return lax.dot_general(a, b, (((1,), (1,)), ((), ())), preferred_element_type=F32)


def _ffn_kernel(*refs, ff_chunk, n_mix, groups):
    mix_refs, refs = refs[:n_mix], refs[n_mix:]
    if n_mix:
        (wmix_ref, gmix_ref), refs = refs[:2], refs[2:]
    x_ref, gin_ref, gout_ref, wg_ref, wu_ref, wd_ref, o_ref = refs
    d_ff = wg_ref.shape[1]
    rows = x_ref.shape[0] // groups
    spans = [slice(r * rows, (r + 1) * rows) for r in range(groups)]

    xs = []
    for rs in spans:
        x = x_ref[rs, :]
        if n_mix:
            mixed = jnp.concatenate([r[rs, :] for r in mix_refs], axis=-1)
            x = x + _rms(_dot(mixed, wmix_ref[...]), gmix_ref[...])
        xs.append(x)
    xns = [_rms(x, gin_ref[...]).astype(BF16) for x in xs]
    accs = []
    for xn in xns:
        acc = None
        for c in range(d_ff // ff_chunk):
            sl = slice(c * ff_chunk, (c + 1) * ff_chunk)
            gate = _dot(xn, wg_ref[:, sl])
            up = _dot(xn, wu_ref[:, sl])
            h = (gate * jax.nn.sigmoid(gate) * up).astype(BF16)
            part = _dot(h, wd_ref[sl, :])
            acc = part if acc is None else acc + part
        accs.append(acc)
    for rs, x, acc in zip(spans, xs, accs):
        o_ref[rs, :] = x + 0.5 * _rms(acc, gout_ref[...])


def _ffn(x, g_in, g_out, wg, wu, wd, which=(), mix=None, *, tm=1024, groups=2, ff_chunk=256):
    m, d = x.shape
    d_ff = wg.shape[-1]
    assert m % tm == 0 and tm % groups == 0 and d_ff % ff_chunk == 0 and len(which) == wg.ndim - 2
    row = lambda width: pl.BlockSpec((tm, width), lambda i: (i, 0))
    lead = (None,) * len(which)
    weight = lambda shape: pl.BlockSpec(lead + shape, lambda i: tuple(which) + (0, 0),
                                        pipeline_mode=pl.Buffered(1))
    parts, w_mix, g_mix = mix if mix else ((), None, None)
    mix_specs = [row(a.shape[1]) for a in parts] + ([_resident(w_mix.shape), _resident((1, d))] if mix else [])
    mix_args = list(parts) + ([w_mix, g_mix.reshape(1, d)] if mix else [])
    return pl.pallas_call(
        functools.partial(_ffn_kernel, ff_chunk=ff_chunk, n_mix=len(parts), groups=groups),
        grid=(m // tm,),
        in_specs=mix_specs + [row(d), _resident((1, d)), _resident((1, d)),
                              weight((d, d_ff)), weight((d, d_ff)), weight((d_ff, d))],
        out_specs=row(d),
        out_shape=jax.ShapeDtypeStruct((m, d), F32),
        compiler_params=_params(("parallel",)),
        name="ffn",
    )(*mix_args, x, g_in.reshape(1, d), g_out.reshape(1, d), wg, wu, wd)


def _attn_proj_kernel(x_ref, g_ref, w_ref, sb_ref, dil_ref, *, n_chunk, scale, groups):
    w_sb = sb_ref.shape[1]
    w_q = w_sb // 3
    rows = x_ref.shape[0] // groups
    spans = [slice(r * rows, (r + 1) * rows) for r in range(groups)]
    xns = [_rms(x_ref[rs, :], g_ref[...]).astype(BF16) for rs in spans]
    for rs, xn in zip(spans, xns):
        for c in range(w_sb // n_chunk):
            sl = slice(c * n_chunk, (c + 1) * n_chunk)
            y = _dot(xn, w_ref[:, sl])
            if (c + 1) * n_chunk <= w_q:
                y = y * scale
            sb_ref[rs, sl] = y.astype(BF16)
        for c in range(dil_ref.shape[1] // n_chunk):
            sl = slice(c * n_chunk, (c + 1) * n_chunk)
            y = _dot(xn, w_ref[:, w_sb + c * n_chunk: w_sb + (c + 1) * n_chunk])
            if (c + 1) * n_chunk <= w_q:
                y = y * scale
            dil_ref[rs, sl] = y


def _attn_proj(x, g, w, *, tm=1024, n_chunk=512, groups=2):
    m, d = x.shape
    n = w.shape[1]
    half = n // 2
    assert m % tm == 0 and (half // 3) % n_chunk == 0
    scale = 1.0 / math.sqrt(HEAD_DIM_ATTN)
    return pl.pallas_call(
        functools.partial(_attn_proj_kernel, n_chunk=n_chunk, scale=scale, groups=groups),
        grid=(m // tm,),
        in_specs=[pl.BlockSpec((tm, d), lambda i: (i, 0)), _resident((1, d)), _resident((d, n))],
        out_specs=[pl.BlockSpec((tm, half), lambda i: (i, 0)),
                   pl.BlockSpec((tm, half), lambda i: (i, 0))],
        out_shape=[jax.ShapeDtypeStruct((m, half), BF16), jax.ShapeDtypeStruct((m, half), F32)],
        compiler_params=_params(("parallel",)),
        name="attn_proj",
    )(x, g.reshape(1, d), w)


LOG2E = math.log2(math.e)
SB_DEAD_LOG2 = -160.0


def _sb_kernel(q_ref, k_ref, v_ref, tri_ref, o_ref, *, blk, q_blocks):
    first_q = pl.program_id(2) * q_blocks
    lane = lax.broadcasted_iota(jnp.int32, (1, LANES), 1)
    row = lax.broadcasted_iota(jnp.int32, (blk, blk), 0)
    col = lax.broadcasted_iota(jnp.int32, (blk, blk), 1)
    causal = col < row
    causal2 = jnp.concatenate([causal, causal], axis=0)
    tri = tri_ref[...]
    head0 = lane < HEAD_DIM_ATTN

    def stacked(q2):
        zeros = jnp.zeros_like(q2)
        return jnp.concatenate([jnp.where(head0, q2, zeros), jnp.where(head0, zeros, q2)], axis=0)

    def pairs(jobs):
        chains = []
        for j, (qq, kb, diag, _) in enumerate(jobs):
            chains.append((j, pl.multiple_of(kb * blk, blk), causal2 if diag else None))
            chains.append((j, pl.multiple_of(jnp.maximum(kb - 1, 0) * blk, blk), kb >= 1))
        states = [job[3] for job in jobs]
        zs, mids = {}, {}
        for n in range(len(chains) + 2):
            if n < len(chains):
                j, ks, _ = chains[n]
                zs[n] = _dot_nt(jobs[j][0], k_ref[pl.ds(ks, blk), :]) * LOG2E
            if 0 <= n - 1 < len(chains):
                z, mask = zs.pop(n - 1), chains[n - 1][2]
                neg = -z
                log_keep = jnp.minimum(neg, 0.0) - jnp.log2(1.0 + jnp.exp2(jnp.minimum(z, neg)))
                log_beta = z + log_keep
                if mask is not None:
                    log_keep = jnp.where(mask, log_keep, 0.0)
                    log_beta = jnp.where(mask, log_beta, MASKED)
                hi = log_keep.astype(BF16)
                lo = (log_keep - hi.astype(F32)).astype(BF16)
                later = _dot(hi, tri) + _dot(lo, tri)
                mids[n - 1] = (log_beta, later, jnp.sum(log_keep, axis=-1, keepdims=True))
            if 0 <= n - 2 < len(chains):
                j, ks, _ = chains[n - 2]
                log_beta, later, total = mids.pop(n - 2)
                carry, acc = states[j]
                p = jnp.exp2(log_beta + later + carry)
                states[j] = (carry + total, acc + _dot(p.astype(BF16), v_ref[pl.ds(ks, blk), :]))
        return states

    def alive(state):
        return (jnp.max(state[0]) > SB_DEAD_LOG2).astype(jnp.int32)

    zero = (jnp.zeros((2 * blk, 1), F32), jnp.zeros((2 * blk, LANES), F32))
    qqs = [stacked(q_ref[g * blk:(g + 1) * blk, :]) for g in range(q_blocks)]
    states = pairs([(qq, first_q + g, True, zero) for g, qq in enumerate(qqs)])

    def store(g, state):
        o_ref[g * blk:(g + 1) * blk, :] = jnp.where(head0, state[1][:blk], state[1][blk:]).astype(o_ref.dtype)

    live = [alive(state) for state in states]
    for g, state in enumerate(states):
        store(g, state)

    for g, (qq, state) in enumerate(zip(qqs, states)):
        i = first_q + g
        n_pairs = lax.shift_right_logical(i, 1)

        @pl.when((live[g] > 0) & (n_pairs > 0))
        def _(g=g, qq=qq, state=state, i=i, n_pairs=n_pairs):
            def cond(loop):
                t, alive_now, _ = loop
                return (t < n_pairs) & (alive_now > 0)

            def body(loop):
                t, _, state = loop
                state, = pairs([(qq, i - 2 - 2 * t, False, state)])
                return t + 1, alive(state), state

            _, _, final = lax.while_loop(cond, body, (jnp.int32(0), live[g], state))
            store(g, final)


def _sb_attention(qkv, *, blk=256, q_blocks=4):
    b, t, w3 = qkv.shape
    w = w3 // 3
    pairs = w // LANES
    blk = min(blk, t)
    rows = blk * q_blocks
    assert t % rows == 0
    tri = jnp.asarray(np.tril(np.ones((blk, blk), np.float32), -1), BF16)
    return pl.pallas_call(
        functools.partial(_sb_kernel, blk=blk, q_blocks=q_blocks),
        grid=(b, pairs, t // rows),
        in_specs=[pl.BlockSpec((None, rows, LANES), lambda bi, p, i: (bi, i, p)),
                  pl.BlockSpec((None, t, LANES), lambda bi, p, i: (bi, 0, pairs + p)),
                  pl.BlockSpec((None, t, LANES), lambda bi, p, i: (bi, 0, 2 * pairs + p)),
                  _resident((blk, blk))],
        out_specs=pl.BlockSpec((None, rows, LANES), lambda bi, p, i: (bi, i, p)),
        out_shape=jax.ShapeDtypeStruct((b, t, w), BF16),
        compiler_params=_params(("parallel", "parallel", "arbitrary")),
        name="sb_attn",
    )(qkv, qkv, qkv, tri)


def _t5_bucket_np(dist):
    max_exact = NUM_BUCKETS // 2
    d = np.maximum(dist, 1).astype(np.float32)
    log_b = max_exact + (np.log(d / np.float32(max_exact)) / np.float32(math.log(MAX_DISTANCE / max_exact))
                         * np.float32(NUM_BUCKETS - max_exact)).astype(np.int32)
    log_b = np.minimum(log_b, NUM_BUCKETS - 1)
    return np.where(dist < max_exact, dist, log_b)


def _dil_tables():
    qi = np.arange(DIL_BLOCK)[:, None]
    ki = np.arange(2 * DIL_BLOCK)[None, :]
    dist = qi + DIL_BLOCK - ki
    buckets, valid = [], []
    for window, dil in DIL_CONFIGS:
        steps = window // dil
        buckets.append(_t5_bucket_np(np.maximum(dist, 0) * dil))
        valid.append((dist >= 0) & (dist <= steps))
    return np.stack(buckets).astype(np.int32), np.stack(valid).astype(np.int32)


def _dil_kernel(rb_ref, bucket_ref, valid_ref, q_ref, k_ref, v_ref, o_ref,
                bias_scr, num_scr, m_scr, l_scr, *, seq):
    p = pl.program_id(0)
    qb = DIL_BLOCK
    lane = lax.broadcasted_iota(jnp.int32, (1, LANES), 1)
    head0 = lane < HEAD_DIM_ATTN
    first_half = lax.broadcasted_iota(jnp.int32, (1, 2 * qb), 1) < qb

    @pl.when(pl.program_id(1) == 0)
    def _():
        for br in range(len(DIL_CONFIGS)):
            bucket = bucket_ref[br]
            valid = valid_ref[br] > 0
            for h in range(2):
                bias = jnp.zeros((qb, 2 * qb), F32)
                for b in range(NUM_BUCKETS):
                    bias = jnp.where(bucket == b, rb_ref[b, 2 * p + h], bias)
                bias_scr[br, h * qb:(h + 1) * qb, :] = jnp.where(valid, bias, MASKED)

    for br, (_, dil) in enumerate(DIL_CONFIGS):
        n_units = seq // qb

        n_blocks = seq // (qb * dil)
        run = min(DIL_UNROLL, n_blocks)
        runs = DIL_UNROLL // run
        assert DIL_UNROLL % run == 0 and n_blocks % run == 0 and (dil == 1 or n_blocks == run)

        def rows(ref, start, dil=dil):
            if dil == 1:
                return ref[pl.ds(start, qb), :].astype(BF16)
            return ref[pl.ds(start, qb, stride=dil), :].astype(BF16)

        def load_run(rho, dil=dil, run=run, whole=(n_blocks == run)):
            res, base = (rho, 0) if whole else (0, rho * run)
            starts = [(base + i) * (qb * dil) + res for i in range(run)]
            kb = [rows(k_ref, st) for st in starts]
            vb = [rows(v_ref, st) for st in starts]
            if whole:
                k_prev, v_prev = kb[0], vb[0]
                pen = jnp.where(first_half, MASKED, 0.0)
            else:
                st = jnp.maximum(base - 1, 0) * (qb * dil) + res
                k_prev, v_prev = rows(k_ref, st), rows(v_ref, st)
                pen = jnp.where(first_half, jnp.where(base == 0, MASKED, 0.0), 0.0)
            units = []
            for i, st in enumerate(starts):
                q2 = rows(q_ref, st)
                zeros = jnp.zeros_like(q2)
                qq = jnp.concatenate([jnp.where(head0, q2, zeros), jnp.where(head0, zeros, q2)], axis=0)
                k2 = jnp.concatenate([kb[i - 1] if i else k_prev, kb[i]], axis=0)
                v2 = jnp.concatenate([vb[i - 1] if i else v_prev, vb[i]], axis=0)
                units.append((st, pen if i == 0 else None, qq, k2, v2))
            return units

        def softmax_parts(s):
            m = jnp.max(s, axis=-1, keepdims=True)
            e = jnp.exp(s - m)
            return m, e, jnp.sum(e, axis=-1, keepdims=True)

        def group(g, _, br=br, dil=dil, runs=runs):
            units = [u for r in range(runs) for u in load_run(g * runs + r)]
            scores, parts = {}, {}
            for j in range(DIL_UNROLL + 2):
                if j < DIL_UNROLL:
                    _, pen, qq, k2, _ = units[j]
                    scores[j] = _dot_nt(qq, k2) + bias_scr[br]
                    if pen is not None:
                        scores[j] = scores[j] + pen
                if 0 <= j - 1 < DIL_UNROLL:
                    parts[j - 1] = softmax_parts(scores.pop(j - 1))
                if 0 <= j - 2 < DIL_UNROLL:
                    st, _, _, _, v2 = units[j - 2]
                    m, e, l = parts.pop(j - 2)
                    pv = _dot(e.astype(BF16), v2)
                    idx = pl.ds(st, qb) if dil == 1 else pl.ds(st, qb, stride=dil)
                    num_scr[br, idx, :] = jnp.where(head0, pv[:qb], pv[qb:])
                    m_scr[br, idx, :] = jnp.where(head0, m[:qb], m[qb:])
                    l_scr[br, idx, :] = jnp.where(head0, l[:qb], l[qb:])
            return 0

        lax.fori_loop(0, n_units // DIL_UNROLL, group, 0)

    rows_out = 256

    def finish(c, _):
        idx = pl.ds(pl.multiple_of(c * rows_out, rows_out), rows_out)
        m_all = [m_scr[br, idx, :] for br in range(len(DIL_CONFIGS))]
        m_max = functools.reduce(jnp.maximum, m_all)
        num = den = None
        for br, m_br in enumerate(m_all):
            wt = jnp.exp(m_br - m_max)
            n_br = wt * num_scr[br, idx, :]
            d_br = wt * l_scr[br, idx, :]
            num = n_br if num is None else num + n_br
            den = d_br if den is None else den + d_br
        o_ref[idx, :] = (num / den).astype(o_ref.dtype)
        return 0

    lax.fori_loop(0, seq // rows_out, finish, 0)


def _dil_attention(qkv, rel_bias):
    b, t, w3 = qkv.shape
    w = w3 // 3
    pairs = w // LANES
    assert t % (DIL_BLOCK * max(d for _, d in DIL_CONFIGS)) == 0 and t % 256 == 0
    bucket, valid = _dil_tables()
    nbr = len(DIL_CONFIGS)
    seq_spec = lambda off: pl.BlockSpec((None, t, LANES), lambda p, bi: (bi, 0, off + p))
    return pl.pallas_call(
        functools.partial(_dil_kernel, seq=t),
        grid=(pairs, b),
        in_specs=[pl.BlockSpec(memory_space=pltpu.SMEM),
                  _resident((nbr, DIL_BLOCK, 2 * DIL_BLOCK)),
                  _resident((nbr, DIL_BLOCK, 2 * DIL_BLOCK)),
                  seq_spec(0), seq_spec(pairs), seq_spec(2 * pairs)],
        out_specs=pl.BlockSpec((None, t, LANES), lambda p, bi: (bi, 0, p)),
        out_shape=jax.ShapeDtypeStruct((b, t, w), BF16),
        scratch_shapes=[pltpu.VMEM((nbr, 2 * DIL_BLOCK, 2 * DIL_BLOCK), F32),
                        pltpu.VMEM((nbr, t, LANES), F32), pltpu.VMEM((nbr, t, LANES), F32),
                        pltpu.VMEM((nbr, t, LANES), F32)],
        compiler_params=_params(("arbitrary", "arbitrary")),
        name="dil_attn",
    )(rel_bias, jnp.asarray(bucket), jnp.asarray(valid), qkv, qkv, qkv)


def _mlstm_proj_kernel(x_ref, g_ref, w_ref, wg_ref, cw_ref, bg_ref, q_ref, kt_ref, v_ref, o_ref,
                       gate_ref, *hist_scrs, tiles_per_seq, n_chunk, scale):
    i = pl.program_id(0)
    tm = x_ref.shape[0]
    width = q_ref.shape[1]
    xn = _rms(x_ref[...], g_ref[...]).astype(BF16)

    def even_rows(first, n):
        return pl.ds(2 * first, n, stride=2)

    slabs = n_chunk // LANES
    n_conv = 2 * width // n_chunk
    assert len(hist_scrs) == n_conv

    @pl.when(i % tiles_per_seq == 0)
    def _():
        for hist in hist_scrs:
            hist[:, 0:2 * CONV_PAD, :] = jnp.zeros((slabs, 2 * CONV_PAD, LANES), F32)

    def project(c):
        pre = _dot(xn, w_ref[:, c * n_chunk:(c + 1) * n_chunk])
        for s in range(slabs):
            hist_scrs[c][s, even_rows(CONV_PAD, tm), :] = pre[:, s * LANES:(s + 1) * LANES]

    def conv(c):
        sl = slice(c * n_chunk, (c + 1) * n_chunk)
        hist = hist_scrs[c]
        parts = []
        for s in range(slabs):
            y = None
            for tap in range(CONV_WIDTH):
                off = CONV_PAD - (CONV_WIDTH - 1) + tap
                w_tap = cw_ref[tap:tap + 1, (c * slabs + s) * LANES:(c * slabs + s + 1) * LANES]
                term = w_tap * hist[s, even_rows(off, tm), :]
                y = term if y is None else y + term
            parts.append(y)
            hist[s, even_rows(0, CONV_PAD), :] = hist[s, even_rows(tm, CONV_PAD), :]
        y = jnp.concatenate(parts, axis=1)
        y = y * jax.nn.sigmoid(y)
        if c * n_chunk < width:
            q_ref[:, sl] = (y * scale).astype(BF16)
        else:
            kt_ref[c * n_chunk - width:(c + 1) * n_chunk - width, :] = y.T.astype(BF16)

    def plain(c):
        sl = slice((c // 2) * n_chunk, (c // 2 + 1) * n_chunk)
        if c % 2 == 0:
            v_ref[:, sl] = _dot(xn, w_ref[:, 2 * width + sl.start:2 * width + sl.stop]).astype(BF16)
        else:
            o_ref[:, sl] = _dot(xn, w_ref[:, 3 * width + sl.start:3 * width + sl.stop])

    n_plain = 2 * width // n_chunk
    for c in range(n_conv + 1):
        if c < n_conv:
            project(c)
        if c >= 1:
            conv(c - 1)
            if c - 1 < n_plain:
                plain(c - 1)
    for c in range(n_conv, n_plain):
        plain(c)
    gate_ref[...] = _dot(xn, wg_ref[...]) + bg_ref[...]


def _mlstm_proj(x, g, w_in, w_gates, conv_w, b_gates, *, seq, tm=512, n_chunk=512):
    m, d = x.shape
    width = conv_w.shape[1] // 2
    assert m % tm == 0 and seq % tm == 0 and width % n_chunk == 0 and w_in.shape[1] >= 4 * width
    row = lambda wd: pl.BlockSpec((tm, wd), lambda i: (i, 0))
    scale = 1.0 / math.sqrt(width // N_HEADS_MLSTM)
    return pl.pallas_call(
        functools.partial(_mlstm_proj_kernel, tiles_per_seq=seq // tm, n_chunk=n_chunk, scale=scale),
        grid=(m // tm,),
        in_specs=[row(d), _resident((1, d)), _resident(w_in.shape), _resident((d, 2 * LANES)),
                  _resident(conv_w.shape), _resident((1, 2 * LANES))],
        out_specs=[row(width), pl.BlockSpec((width, tm), lambda i: (0, i)), row(width), row(width),
                   row(2 * LANES)],
        out_shape=[jax.ShapeDtypeStruct((m, width), BF16), jax.ShapeDtypeStruct((width, m), BF16),
                   jax.ShapeDtypeStruct((m, width), BF16), jax.ShapeDtypeStruct((m, width), F32),
                   jax.ShapeDtypeStruct((m, 2 * LANES), F32)],
        scratch_shapes=[pltpu.VMEM((n_chunk // LANES, 2 * (tm + CONV_PAD), LANES), F32)
                        for _ in range(2 * width // n_chunk)],
        compiler_params=_params(("arbitrary",)),
        name="mlstm_proj",
    )(x, g.reshape(1, d), w_in, w_gates, conv_w, b_gates)


def _split3(x):
    hi = x.astype(BF16)
    r = x - hi.astype(F32)
    mid = r.astype(BF16)
    return hi, mid, (r - mid.astype(F32)).astype(BF16)


def _gate_prep_kernel(g_ref, cols_ref, rows_ref, *, heads, cl):
    tm = g_ref.shape[0]
    row = lax.broadcasted_iota(jnp.int32, (cl, cl), 0)
    col = lax.broadcasted_iota(jnp.int32, (cl, cl), 1)
    incl = jnp.where(row >= col, 1.0, 0.0).astype(BF16)
    lane = lax.broadcasted_iota(jnp.int32, (1, cl), 1)
    pad = jnp.zeros((cl - heads, cl), F32)
    spans = [slice(c * cl, (c + 1) * cl) for c in range(tm // cl)]
    bs = []
    for r in spans:
        b = None
        for part in _split3(jax.nn.log_sigmoid(g_ref[r, LANES:])):
            term = _dot(incl, part)
            b = term if b is None else b + term
        bs.append(b)
    us = [g_ref[r, :LANES] - b for r, b in zip(spans, bs)]
    u_rows = jnp.concatenate([u.T[0:heads, :] for u in us], axis=0)
    b_rows = jnp.concatenate([b.T[0:heads, :] for b in bs], axis=0)
    cmax = u_rows
    shift = 1
    while shift < cl:
        cmax = jnp.where(lane >= shift, jnp.maximum(cmax, pltpu.roll(cmax, shift, axis=1)), cmax)
        shift *= 2
    u_max = jnp.broadcast_to(jnp.max(u_rows, axis=1, keepdims=True), u_rows.shape)
    b_last = jnp.broadcast_to(jnp.min(b_rows, axis=1, keepdims=True), b_rows.shape)
    for c, (r, u, b) in enumerate(zip(spans, us, bs)):
        hs = slice(c * heads, (c + 1) * heads)
        cols_ref[r, :] = jnp.concatenate([u, b, jnp.concatenate([cmax[hs], pad], axis=0).T], axis=1)
        rows_ref[:, r] = jnp.concatenate([u_rows[hs], u_max[hs], b_last[hs]], axis=0)


def _gate_prep(gates, *, heads, cl, tm=1024):
    m = gates.shape[0]
    tm = min(tm, m)
    assert m % tm == 0 and tm % cl == 0 and cl == LANES and heads == 8
    return pl.pallas_call(
        functools.partial(_gate_prep_kernel, heads=heads, cl=cl),
        grid=(m // tm,),
        in_specs=[pl.BlockSpec((tm, 2 * LANES), lambda i: (i, 0))],
        out_specs=[pl.BlockSpec((tm, 3 * LANES), lambda i: (i, 0)),
                   pl.BlockSpec((3 * heads, tm), lambda i: (0, i))],
        out_shape=[jax.ShapeDtypeStruct((m, 3 * LANES), F32),
                   jax.ShapeDtypeStruct((3 * heads, m), F32)],
        compiler_params=_params(("parallel",)),
        name="gate_prep",
    )(gates)


def _mlstm_kernel(q_ref, kt_ref, v_ref, o_ref, gcol_ref, grow_ref, hg_ref, out_ref,
                  s_scr, m_scr, ml_scr, *, heads):
    cl = MLSTM_CHUNK
    n_chunks = q_ref.shape[0] // cl
    dh = q_ref.shape[1] // heads

    @pl.when(pl.program_id(1) == 0)
    def _():
        s_scr[...] = jnp.zeros_like(s_scr)
        m_scr[...] = jnp.zeros_like(m_scr)
        ml_scr[...] = jnp.zeros_like(ml_scr)

    row = lax.broadcasted_iota(jnp.int32, (cl, cl), 0)
    col = lax.broadcasted_iota(jnp.int32, (cl, cl), 1)
    lower = row >= col
    ones = jnp.ones((cl, dh), BF16)

    m_rows = m_scr[...]
    m_lane = ml_scr[0:1, :]
    tables = []
    for c in range(n_chunks):
        span = slice(c * cl, (c + 1) * cl)
        u_cols, b_cols, cmax_cols = (gcol_ref[span, t * LANES:(t + 1) * LANES] for t in range(3))
        u_rows, u_max_rows, b_last_rows = (grow_ref[t * heads:(t + 1) * heads, span] for t in range(3))
        mm_last_rows = jnp.maximum(m_rows, u_max_rows)
        mm_cols = jnp.maximum(cmax_cols, m_lane)
        tables.append(dict(
            span=span, u_rows=u_rows, m_prev_rows=m_rows, mm_cols=mm_cols,
            decay_rows=jnp.exp(m_rows - mm_last_rows),
            ws_rows=jnp.exp(u_rows - mm_last_rows),
            floor_cols=jnp.exp(-(b_cols + mm_cols))))
        m_rows = b_last_rows + mm_last_rows
        m_lane = b_cols[cl - 1:cl, :] + mm_cols[cl - 1:cl, :]
    m_scr[...] = m_rows
    ml_scr[...] = jnp.broadcast_to(m_lane, ml_scr.shape)

    def lanes_of(mat, c):
        return jnp.broadcast_to(mat[:, c:c + 1], (cl, LANES))

    def stage_a(tab, h):
        sl = slice(h * dh, (h + 1) * dh)
        q, kt = q_ref[tab["span"], sl], kt_ref[sl, tab["span"]]
        v1 = jnp.concatenate([v_ref[tab["span"], sl], ones], axis=1)
        s_prev = s_scr[h]
        qk = _dot(q, kt)
        qs = _dot(q, s_prev.astype(BF16))
        kw = (kt.astype(F32) * tab["ws_rows"][h:h + 1, :]).astype(BF16)
        decay = jnp.broadcast_to(tab["decay_rows"][h:h + 1, :], (dh, LANES))
        s_scr[h] = jnp.concatenate([decay, decay], axis=1) * s_prev + _dot(kw, v1)
        return tab, h, sl, v1, qk, qs

    def stage_b(tab, h, sl, v1, qk, qs):
        mm = lanes_of(tab["mm_cols"], h)
        weight = jnp.where(lower, jnp.exp(tab["u_rows"][h:h + 1, :] - mm), 0.0)
        w_inter = jnp.exp(tab["m_prev_rows"][h:h + 1, :] - mm)
        return tab, sl, v1, qs, (qk * weight).astype(BF16), w_inter, lanes_of(tab["floor_cols"], h)

    def stage_c(tab, sl, v1, qs, p, w_inter, floor):
        pv = _dot(p, v1)
        both = pv + jnp.concatenate([w_inter, w_inter], axis=1) * qs
        hid = both[:, :dh] / jnp.maximum(jnp.abs(both[:, dh:]), floor)
        hid = hid * lax.rsqrt(jnp.mean(hid * hid, axis=-1, keepdims=True) + EPS) * hg_ref[:, sl]
        out_ref[tab["span"], sl] = (hid * jax.nn.sigmoid(o_ref[tab["span"], sl])).astype(out_ref.dtype)

    jobs = [(tab, h) for tab in tables for h in range(heads)]
    after_a, after_b = {}, {}
    for j in range(len(jobs) + 2):
        if j < len(jobs):
            after_a[j] = stage_a(*jobs[j])
        if 0 <= j - 1 < len(jobs):
            after_b[j - 1] = stage_b(*after_a.pop(j - 1))
        if 0 <= j - 2 < len(jobs):
            stage_c(*after_b.pop(j - 2))


def _mlstm(q, kt, v, o, gates, head_g, *, heads=N_HEADS_MLSTM, chunks=8):
    b, t, w = q.shape
    cl = MLSTM_CHUNK
    dh = w // heads
    rows = chunks * cl
    nc = t // rows
    assert t % rows == 0 and dh == LANES and cl == LANES
    gcols, grows = _gate_prep(gates, heads=heads, cl=cl)
    blk = lambda wd: pl.BlockSpec((None, rows, wd), lambda bi, c: (bi, c, 0))
    return pl.pallas_call(
        functools.partial(_mlstm_kernel, heads=heads),
        grid=(b, nc),
        in_specs=[blk(w), pl.BlockSpec((w, rows), lambda bi, c: (0, bi * nc + c)), blk(w), blk(w),
                  pl.BlockSpec((rows, 3 * LANES), lambda bi, c: (bi * nc + c, 0)),
                  pl.BlockSpec((3 * heads, rows), lambda bi, c: (0, bi * nc + c)),
                  _resident((1, w))],
        out_specs=blk(w),
        out_shape=jax.ShapeDtypeStruct((b, t, w), BF16),
        scratch_shapes=[pltpu.VMEM((heads, dh, 2 * dh), F32), pltpu.VMEM((heads, LANES), F32),
                        pltpu.VMEM((heads, LANES), F32)],
        compiler_params=_params(("parallel", "arbitrary")),
        name="mlstm",
    )(q, kt, v, o, gcols, grows, head_g.reshape(1, w))


def kernel(x, norm_g, ffn_w_gate, ffn_w_up, ffn_w_down, attn_w_in, attn_w_out, rel_bias,
           mlstm_w_in, mlstm_b_gates, mlstm_conv_w, mlstm_head_g, mlstm_w_out):
    bsz, t, d = x.shape
    depth = norm_g.shape[0]
    h = x.reshape(bsz * t, d)
    bf = lambda a: a.astype(BF16)

    w_gate, w_up, w_down = bf(ffn_w_gate), bf(ffn_w_up), bf(ffn_w_down)

    def ffn(h, layer, half, mix=None):
        g = norm_g[layer]
        return _ffn(h, g[2 * half * 2], g[2 * half * 2 + 1], w_gate, w_up, w_down, (layer, half), mix)

    for layer in range(depth):
        g = norm_g[layer]
        j = layer // 2
        h = ffn(h, layer, 0)
        if layer % 2 == 0:
            sb, dil = _attn_proj(h, g[2], bf(attn_w_in[j]))
            out_sb = _sb_attention(sb.reshape(bsz, t, -1))
            out_dil = _dil_attention(dil.reshape(bsz, t, -1), rel_bias.astype(F32))
            mix = ([out_sb.reshape(bsz * t, -1), out_dil.reshape(bsz * t, -1)], bf(attn_w_out[j]), g[3])
        else:
            width = mlstm_w_out.shape[1]
            n_head = mlstm_b_gates.shape[1] // 2
            tiles = lambda a: jnp.concatenate(
                [jnp.pad(part, ((0, 0), (0, LANES - n_head)))
                 for part in (a[:, :n_head], a[:, n_head:])], axis=1)
            w_in = bf(mlstm_w_in[j])
            w_gates = tiles(w_in[:, 4 * width:])
            b_gates = tiles(mlstm_b_gates[j].astype(F32).reshape(1, -1))
            q, kt, v, o, gates = _mlstm_proj(h, g[2], w_in, w_gates, mlstm_conv_w[j].astype(F32),
                                             b_gates, seq=t)
            r3 = lambda a: a.reshape(bsz, t, -1)
            hid = _mlstm(r3(q), kt, r3(v), r3(o), gates, mlstm_head_g[j].astype(F32))
            mix = ([hid.reshape(bsz * t, width)], bf(mlstm_w_out[j]), g[3])
        h = ffn(h, layer, 1, mix)
    return h.reshape(bsz, t, d)
```

```python
import functools
import math

import numpy as np
import jax
import jax.numpy as jnp
from jax import lax
from jax.experimental import pallas as pl
from jax.experimental.pallas import tpu as pltpu

EPS = 1e-6
HEAD_DIM_ATTN = 64
DIL_CONFIGS = ((128, 1), (512, 4), (2048, 16))
DIL_BLOCK = 128
DIL_UNROLL = 32
NUM_BUCKETS = 32
MAX_DISTANCE = 2048
N_HEADS_MLSTM = 8
MLSTM_CHUNK = 128
CONV_WIDTH = 4
LANES = 128
CONV_PAD = 8
MASKED = -1e30
V7X_VMEM_BYTES = 64 * 1024 * 1024
VMEM_LIMIT = V7X_VMEM_BYTES * 7 // 8

F32 = jnp.float32
BF16 = jnp.bfloat16


def _params(sem, vmem=VMEM_LIMIT):
    return pltpu.CompilerParams(dimension_semantics=sem, vmem_limit_bytes=vmem)


def _resident(shape):
    zeros = (0,) * len(shape)
    return pl.BlockSpec(shape, lambda *_: zeros, pipeline_mode=pl.Buffered(1))


def _rms(x, g):
    return x * lax.rsqrt(jnp.mean(x * x, axis=-1, keepdims=True) + EPS) * g


def _dot(a, b):
    return jnp.dot(a, b, preferred_element_type=F32)


def _dot_nt(a, b):
    return lax.dot_general(a, b, (((1,), (1,)), ((), ())), preferred_element_type=F32)


def _ffn_kernel(*refs, ff_chunk, n_mix, groups):
    mix_refs, refs = refs[:n_mix], refs[n_mix:]
    if n_mix:
        (wmix_ref, gmix_ref), refs = refs[:2], refs[2:]
    x_ref, gin_ref, gout_ref, wg_ref, wu_ref, wd_ref, o_ref = refs
    d_ff = wg_ref.shape[1]
    rows = x_ref.shape[0] // groups
    spans = [slice(r * rows, (r + 1) * rows) for r in range(groups)]

    xs = []
    for rs in spans:
        x = x_ref[rs, :]
        if n_mix:
            mixed = jnp.concatenate([r[rs, :] for r in mix_refs], axis=-1)
            x = x + _rms(_dot(mixed, wmix_ref[...]), gmix_ref[...])
        xs.append(x)
    xns = [_rms(x, gin_ref[...]).astype(BF16) for x in xs]
    accs = []
    for xn in xns:
        acc = None
        for c in range(d_ff // ff_chunk):
            sl = slice(c * ff_chunk, (c + 1) * ff_chunk)
            gate = _dot(xn, wg_ref[:, sl])
            up = _dot(xn, wu_ref[:, sl])
            h = (gate * jax.nn.sigmoid(gate) * up).astype(BF16)
            part = _dot(h, wd_ref[sl, :])
            acc = part if acc is None else acc + part
        accs.append(acc)
    for rs, x, acc in zip(spans, xs, accs):
        o_ref[rs, :] = x + 0.5 * _rms(acc, gout_ref[...])


def _ffn(x, g_in, g_out, wg, wu, wd, which=(), mix=None, *, tm=1024, groups=2, ff_chunk=256):
    m, d = x.shape
    d_ff = wg.shape[-1]
    assert m % tm == 0 and tm % groups == 0 and d_ff % ff_chunk == 0 and len(which) == wg.ndim - 2
    row = lambda width: pl.BlockSpec((tm, width), lambda i: (i, 0))
    lead = (None,) * len(which)
    weight = lambda shape: pl.BlockSpec(lead + shape, lambda i: tuple(which) + (0, 0),
                                        pipeline_mode=pl.Buffered(1))
    parts, w_mix, g_mix = mix if mix else ((), None, None)
    mix_specs = [row(a.shape[1]) for a in parts] + ([_resident(w_mix.shape), _resident((1, d))] if mix else [])
    mix_args = list(parts) + ([w_mix, g_mix.reshape(1, d)] if mix else [])
    return pl.pallas_call(
        functools.partial(_ffn_kernel, ff_chunk=ff_chunk, n_mix=len(parts), groups=groups),
        grid=(m // tm,),
        in_specs=mix_specs + [row(d), _resident((1, d)), _resident((1, d)),
                              weight((d, d_ff)), weight((d, d_ff)), weight((d_ff, d))],
        out_specs=row(d),
        out_shape=jax.ShapeDtypeStruct((m, d), F32),
        compiler_params=_params(("parallel",)),
        name="ffn",
    )(*mix_args, x, g_in.reshape(1, d), g_out.reshape(1, d), wg, wu, wd)


def _attn_proj_kernel(x_ref, g_ref, w_ref, sb_ref, dil_ref, *, n_chunk, scale, groups):
    w_sb = sb_ref.shape[1]
    w_q = w_sb // 3
    rows = x_ref.shape[0] // groups
    spans = [slice(r * rows, (r + 1) * rows) for r in range(groups)]
    xns = [_rms(x_ref[rs, :], g_ref[...]).astype(BF16) for rs in spans]
    for rs, xn in zip(spans, xns):
        for c in range(w_sb // n_chunk):
            sl = slice(c * n_chunk, (c + 1) * n_chunk)
            y = _dot(xn, w_ref[:, sl])
            if (c + 1) * n_chunk <= w_q:
                y = y * scale
            sb_ref[rs, sl] = y.astype(BF16)
        for c in range(dil_ref.shape[1] // n_chunk):
            sl = slice(c * n_chunk, (c + 1) * n_chunk)
            y = _dot(xn, w_ref[:, w_sb + c * n_chunk: w_sb + (c + 1) * n_chunk])
            if (c + 1) * n_chunk <= w_q:
                y = y * scale
            dil_ref[rs, sl] = y


def _attn_proj(x, g, w, *, tm=1024, n_chunk=512, groups=2):
    m, d = x.shape
    n = w.shape[1]
    half = n // 2
    assert m % tm == 0 and (half // 3) % n_chunk == 0
    scale = 1.0 / math.sqrt(HEAD_DIM_ATTN)
    return pl.pallas_call(
        functools.partial(_attn_proj_kernel, n_chunk=n_chunk, scale=scale, groups=groups),
        grid=(m // tm,),
        in_specs=[pl.BlockSpec((tm, d), lambda i: (i, 0)), _resident((1, d)), _resident((d, n))],
        out_specs=[pl.BlockSpec((tm, half), lambda i: (i, 0)),
                   pl.BlockSpec((tm, half), lambda i: (i, 0))],
        out_shape=[jax.ShapeDtypeStruct((m, half), BF16), jax.ShapeDtypeStruct((m, half), F32)],
        compiler_params=_params(("parallel",)),
        name="attn_proj",
    )(x, g.reshape(1, d), w)


LOG2E = math.log2(math.e)
SB_DEAD_LOG2 = -160.0


def _sb_kernel(q_ref, k_ref, v_ref, tri_ref, o_ref, *, blk, q_blocks):
    first_q = pl.program_id(2) * q_blocks
    lane = lax.broadcasted_iota(jnp.int32, (1, LANES), 1)
    row = lax.broadcasted_iota(jnp.int32, (blk, blk), 0)
    col = lax.broadcasted_iota(jnp.int32, (blk, blk), 1)
    causal = col < row
    causal2 = jnp.concatenate([causal, causal], axis=0)
    tri = tri_ref[...]
    head0 = lane < HEAD_DIM_ATTN

    def stacked(q2):
        zeros = jnp.zeros_like(q2)
        return jnp.concatenate([jnp.where(head0, q2, zeros), jnp.where(head0, zeros, q2)], axis=0)

    def pairs(jobs):
        chains = []
        for j, (qq, kb, diag, _) in enumerate(jobs):
            chains.append((j, pl.multiple_of(kb * blk, blk), causal2 if diag else None))
            chains.append((j, pl.multiple_of(jnp.maximum(kb - 1, 0) * blk, blk), kb >= 1))
        states = [job[3] for job in jobs]
        zs, mids = {}, {}
        for n in range(len(chains) + 2):
            if n < len(chains):
                j, ks, _ = chains[n]
                zs[n] = _dot_nt(jobs[j][0], k_ref[pl.ds(ks, blk), :]) * LOG2E
            if 0 <= n - 1 < len(chains):
                z, mask = zs.pop(n - 1), chains[n - 1][2]
                neg = -z
                log_keep = jnp.minimum(neg, 0.0) - jnp.log2(1.0 + jnp.exp2(jnp.minimum(z, neg)))
                log_beta = z + log_keep
                if mask is not None:
                    log_keep = jnp.where(mask, log_keep, 0.0)
                    log_beta = jnp.where(mask, log_beta, MASKED)
                hi = log_keep.astype(BF16)
                lo = (log_keep - hi.astype(F32)).astype(BF16)
                later = _dot(hi, tri) + _dot(lo, tri)
                mids[n - 1] = (log_beta, later, jnp.sum(log_keep, axis=-1, keepdims=True))
            if 0 <= n - 2 < len(chains):
                j, ks, _ = chains[n - 2]
                log_beta, later, total = mids.pop(n - 2)
                carry, acc = states[j]
                p = jnp.exp2(log_beta + later + carry)
                states[j] = (carry + total, acc + _dot(p.astype(BF16), v_ref[pl.ds(ks, blk), :]))
        return states

    def alive(state):
        return (jnp.max(state[0]) > SB_DEAD_LOG2).astype(jnp.int32)

    zero = (jnp.zeros((2 * blk, 1), F32), jnp.zeros((2 * blk, LANES), F32))
    qqs = [stacked(q_ref[g * blk:(g + 1) * blk, :]) for g in range(q_blocks)]
    states = pairs([(qq, first_q + g, True, zero) for g, qq in enumerate(qqs)])

    def store(g, state):
        o_ref[g * blk:(g + 1) * blk, :] = jnp.where(head0, state[1][:blk], state[1][blk:]).astype(o_ref.dtype)

    live = [alive(state) for state in states]
    for g, state in enumerate(states):
        store(g, state)

    for g, (qq, state) in enumerate(zip(qqs, states)):
        i = first_q + g
        n_pairs = lax.shift_right_logical(i, 1)

        @pl.when((live[g] > 0) & (n_pairs > 0))
        def _(g=g, qq=qq, state=state, i=i, n_pairs=n_pairs):
            def cond(loop):
                t, alive_now, _ = loop
                return (t < n_pairs) & (alive_now > 0)

            def body(loop):
                t, _, state = loop
                state, = pairs([(qq, i - 2 - 2 * t, False, state)])
                return t + 1, alive(state), state

            _, _, final = lax.while_loop(cond, body, (jnp.int32(0), live[g], state))
            store(g, final)


def _sb_attention(qkv, *, blk=256, q_blocks=4):
    b, t, w3 = qkv.shape
    w = w3 // 3
    pairs = w // LANES
    blk = min(blk, t)
    rows = blk * q_blocks
    assert t % rows == 0
    tri = jnp.asarray(np.tril(np.ones((blk, blk), np.float32), -1), BF16)
    return pl.pallas_call(
        functools.partial(_sb_kernel, blk=blk, q_blocks=q_blocks),
        grid=(b, pairs, t // rows),
        in_specs=[pl.BlockSpec((None, rows, LANES), lambda bi, p, i: (bi, i, p)),
                  pl.BlockSpec((None, t, LANES), lambda bi, p, i: (bi, 0, pairs + p)),
                  pl.BlockSpec((None, t, LANES), lambda bi, p, i: (bi, 0, 2 * pairs + p)),
                  _resident((blk, blk))],
        out_specs=pl.BlockSpec((None, rows, LANES), lambda bi, p, i: (bi, i, p)),
        out_shape=jax.ShapeDtypeStruct((b, t, w), BF16),
        compiler_params=_params(("parallel", "parallel", "arbitrary")),
        name="sb_attn",
    )(qkv, qkv, qkv, tri)


def _t5_bucket_np(dist):
    max_exact = NUM_BUCKETS // 2
    d = np.maximum(dist, 1).astype(np.float32)
    log_b = max_exact + (np.log(d / np.float32(max_exact)) / np.float32(math.log(MAX_DISTANCE / max_exact))
                         * np.float32(NUM_BUCKETS - max_exact)).astype(np.int32)
    log_b = np.minimum(log_b, NUM_BUCKETS - 1)
    return np.where(dist < max_exact, dist, log_b)


def _dil_tables():
    qi = np.arange(DIL_BLOCK)[:, None]
    ki = np.arange(2 * DIL_BLOCK)[None, :]
    dist = qi + DIL_BLOCK - ki
    buckets, valid = [], []
    for window, dil in DIL_CONFIGS:
        steps = window // dil
        buckets.append(_t5_bucket_np(np.maximum(dist, 0) * dil))
        valid.append((dist >= 0) & (dist <= steps))
    return np.stack(buckets).astype(np.int32), np.stack(valid).astype(np.int32)


def _dil_kernel(rb_ref, bucket_ref, valid_ref, q_ref, k_ref, v_ref, o_ref,
                bias_scr, num_scr, m_scr, l_scr, *, seq):
    p = pl.program_id(0)
    qb = DIL_BLOCK
    lane = lax.broadcasted_iota(jnp.int32, (1, LANES), 1)
    head0 = lane < HEAD_DIM_ATTN
    first_half = lax.broadcasted_iota(jnp.int32, (1, 2 * qb), 1) < qb

    @pl.when(pl.program_id(1) == 0)
    def _():
        for br in range(len(DIL_CONFIGS)):
            bucket = bucket_ref[br]
            valid = valid_ref[br] > 0
            for h in range(2):
                bias = jnp.zeros((qb, 2 * qb), F32)
                for b in range(NUM_BUCKETS):
                    bias = jnp.where(bucket == b, rb_ref[b, 2 * p + h], bias)
                bias_scr[br, h * qb:(h + 1) * qb, :] = jnp.where(valid, bias, MASKED)

    for br, (_, dil) in enumerate(DIL_CONFIGS):
        n_units = seq // qb

        n_blocks = seq // (qb * dil)
        run = min(DIL_UNROLL, n_blocks)
        runs = DIL_UNROLL // run
        assert DIL_UNROLL % run == 0 and n_blocks % run == 0 and (dil == 1 or n_blocks == run)

        def rows(ref, start, dil=dil):
            if dil == 1:
                return ref[pl.ds(start, qb), :].astype(BF16)
            return ref[pl.ds(start, qb, stride=dil), :].astype(BF16)

        def load_run(rho, dil=dil, run=run, whole=(n_blocks == run)):
            res, base = (rho, 0) if whole else (0, rho * run)
            starts = [(base + i) * (qb * dil) + res for i in range(run)]
            kb = [rows(k_ref, st) for st in starts]
            vb = [rows(v_ref, st) for st in starts]
            if whole:
                k_prev, v_prev = kb[0], vb[0]
                pen = jnp.where(first_half, MASKED, 0.0)
            else:
                st = jnp.maximum(base - 1, 0) * (qb * dil) + res
                k_prev, v_prev = rows(k_ref, st), rows(v_ref, st)
                pen = jnp.where(first_half, jnp.where(base == 0, MASKED, 0.0), 0.0)
            units = []
            for i, st in enumerate(starts):
                q2 = rows(q_ref, st)
                zeros = jnp.zeros_like(q2)
                qq = jnp.concatenate([jnp.where(head0, q2, zeros), jnp.where(head0, zeros, q2)], axis=0)
                k2 = jnp.concatenate([kb[i - 1] if i else k_prev, kb[i]], axis=0)
                v2 = jnp.concatenate([vb[i - 1] if i else v_prev, vb[i]], axis=0)
                units.append((st, pen if i == 0 else None, qq, k2, v2))
            return units

        def softmax_parts(s):
            m = jnp.max(s, axis=-1, keepdims=True)
            e = jnp.exp(s - m)
            return m, e, jnp.sum(e, axis=-1, keepdims=True)

        def group(g, _, br=br, dil=dil, runs=runs):
            units = [u for r in range(runs) for u in load_run(g * runs + r)]
            scores, parts = {}, {}
            for j in range(DIL_UNROLL + 2):
                if j < DIL_UNROLL:
                    _, pen, qq, k2, _ = units[j]
                    scores[j] = _dot_nt(qq, k2) + bias_scr[br]
                    if pen is not None:
                        scores[j] = scores[j] + pen
                if 0 <= j - 1 < DIL_UNROLL:
                    parts[j - 1] = softmax_parts(scores.pop(j - 1))
                if 0 <= j - 2 < DIL_UNROLL:
                    st, _, _, _, v2 = units[j - 2]
                    m, e, l = parts.pop(j - 2)
                    pv = _dot(e.astype(BF16), v2)
                    idx = pl.ds(st, qb) if dil == 1 else pl.ds(st, qb, stride=dil)
                    num_scr[br, idx, :] = jnp.where(head0, pv[:qb], pv[qb:])
                    m_scr[br, idx, :] = jnp.where(head0, m[:qb], m[qb:])
                    l_scr[br, idx, :] = jnp.where(head0, l[:qb], l[qb:])
            return 0

        lax.fori_loop(0, n_units // DIL_UNROLL, group, 0)

    rows_out = 256

    def finish(c, _):
        idx = pl.ds(pl.multiple_of(c * rows_out, rows_out), rows_out)
        m_all = [m_scr[br, idx, :] for br in range(len(DIL_CONFIGS))]
        m_max = functools.reduce(jnp.maximum, m_all)
        num = den = None
        for br, m_br in enumerate(m_all):
            wt = jnp.exp(m_br - m_max)
            n_br = wt * num_scr[br, idx, :]
            d_br = wt * l_scr[br, idx, :]
            num = n_br if num is None else num + n_br
            den = d_br if den is None else den + d_br
        o_ref[idx, :] = (num / den).astype(o_ref.dtype)
        return 0

    lax.fori_loop(0, seq // rows_out, finish, 0)


def _dil_attention(qkv, rel_bias):
    b, t, w3 = qkv.shape
    w = w3 // 3
    pairs = w // LANES
    assert t % (DIL_BLOCK * max(d for _, d in DIL_CONFIGS)) == 0 and t % 256 == 0
    bucket, valid = _dil_tables()
    nbr = len(DIL_CONFIGS)
    seq_spec = lambda off: pl.BlockSpec((None, t, LANES), lambda p, bi: (bi, 0, off + p))
    return pl.pallas_call(
        functools.partial(_dil_kernel, seq=t),
        grid=(pairs, b),
        in_specs=[pl.BlockSpec(memory_space=pltpu.SMEM),
                  _resident((nbr, DIL_BLOCK, 2 * DIL_BLOCK)),
                  _resident((nbr, DIL_BLOCK, 2 * DIL_BLOCK)),
                  seq_spec(0), seq_spec(pairs), seq_spec(2 * pairs)],
        out_specs=pl.BlockSpec((None, t, LANES), lambda p, bi: (bi, 0, p)),
        out_shape=jax.ShapeDtypeStruct((b, t, w), BF16),
        scratch_shapes=[pltpu.VMEM((nbr, 2 * DIL_BLOCK, 2 * DIL_BLOCK), F32),
                        pltpu.VMEM((nbr, t, LANES), F32), pltpu.VMEM((nbr, t, LANES), F32),
                        pltpu.VMEM((nbr, t, LANES), F32)],
        compiler_params=_params(("arbitrary", "arbitrary")),
        name="dil_attn",
    )(rel_bias, jnp.asarray(bucket), jnp.asarray(valid), qkv, qkv, qkv)


def _mlstm_proj_kernel(x_ref, g_ref, w_ref, wg_ref, cw_ref, bg_ref, q_ref, kt_ref, v_ref, o_ref,
                       gate_ref, *hist_scrs, tiles_per_seq, n_chunk, scale):
    i = pl.program_id(0)
    tm = x_ref.shape[0]
    width = q_ref.shape[1]
    xn = _rms(x_ref[...], g_ref[...]).astype(BF16)

    def even_rows(first, n):
        return pl.ds(2 * first, n, stride=2)

    slabs = n_chunk // LANES
    n_conv = 2 * width // n_chunk
    assert len(hist_scrs) == n_conv

    @pl.when(i % tiles_per_seq == 0)
    def _():
        for hist in hist_scrs:
            hist[:, 0:2 * CONV_PAD, :] = jnp.zeros((slabs, 2 * CONV_PAD, LANES), F32)

    def project(c):
        pre = _dot(xn, w_ref[:, c * n_chunk:(c + 1) * n_chunk])
        for s in range(slabs):
            hist_scrs[c][s, even_rows(CONV_PAD, tm), :] = pre[:, s * LANES:(s + 1) * LANES]

    def conv(c):
        sl = slice(c * n_chunk, (c + 1) * n_chunk)
        hist = hist_scrs[c]
        parts = []
        for s in range(slabs):
            y = None
            for tap in range(CONV_WIDTH):
                off = CONV_PAD - (CONV_WIDTH - 1) + tap
                w_tap = cw_ref[tap:tap + 1, (c * slabs + s) * LANES:(c * slabs + s + 1) * LANES]
                term = w_tap * hist[s, even_rows(off, tm), :]
                y = term if y is None else y + term
            parts.append(y)
            hist[s, even_rows(0, CONV_PAD), :] = hist[s, even_rows(tm, CONV_PAD), :]
        y = jnp.concatenate(parts, axis=1)
        y = y * jax.nn.sigmoid(y)
        if c * n_chunk < width:
            q_ref[:, sl] = (y * scale).astype(BF16)
        else:
            kt_ref[c * n_chunk - width:(c + 1) * n_chunk - width, :] = y.T.astype(BF16)

    def plain(c):
        sl = slice((c // 2) * n_chunk, (c // 2 + 1) * n_chunk)
        if c % 2 == 0:
            v_ref[:, sl] = _dot(xn, w_ref[:, 2 * width + sl.start:2 * width + sl.stop]).astype(BF16)
        else:
            o_ref[:, sl] = _dot(xn, w_ref[:, 3 * width + sl.start:3 * width + sl.stop])

    n_plain = 2 * width // n_chunk
    for c in range(n_conv + 1):
        if c < n_conv:
            project(c)
        if c >= 1:
            conv(c - 1)
            if c - 1 < n_plain:
                plain(c - 1)
    for c in range(n_conv, n_plain):
        plain(c)
    gate_ref[...] = _dot(xn, wg_ref[...]) + bg_ref[...]


def _mlstm_proj(x, g, w_in, w_gates, conv_w, b_gates, *, seq, tm=512, n_chunk=512):
    m, d = x.shape
    width = conv_w.shape[1] // 2
    assert m % tm == 0 and seq % tm == 0 and width % n_chunk == 0 and w_in.shape[1] >= 4 * width
    row = lambda wd: pl.BlockSpec((tm, wd), lambda i: (i, 0))
    scale = 1.0 / math.sqrt(width // N_HEADS_MLSTM)
    return pl.pallas_call(
        functools.partial(_mlstm_proj_kernel, tiles_per_seq=seq // tm, n_chunk=n_chunk, scale=scale),
        grid=(m // tm,),
        in_specs=[row(d), _resident((1, d)), _resident(w_in.shape), _resident((d, 2 * LANES)),
                  _resident(conv_w.shape), _resident((1, 2 * LANES))],
        out_specs=[row(width), pl.BlockSpec((width, tm), lambda i: (0, i)), row(width), row(width),
                   row(2 * LANES)],
        out_shape=[jax.ShapeDtypeStruct((m, width), BF16), jax.ShapeDtypeStruct((width, m), BF16),
                   jax.ShapeDtypeStruct((m, width), BF16), jax.ShapeDtypeStruct((m, width), F32),
                   jax.ShapeDtypeStruct((m, 2 * LANES), F32)],
        scratch_shapes=[pltpu.VMEM((n_chunk // LANES, 2 * (tm + CONV_PAD), LANES), F32)
                        for _ in range(2 * width // n_chunk)],
        compiler_params=_params(("arbitrary",)),
        name="mlstm_proj",
    )(x, g.reshape(1, d), w_in, w_gates, conv_w, b_gates)


def _split3(x):
    hi = x.astype(BF16)
    r = x - hi.astype(F32)
    mid = r.astype(BF16)
    return hi, mid, (r - mid.astype(F32)).astype(BF16)


def _gate_prep_kernel(g_ref, cols_ref, rows_ref, *, heads, cl):
    tm = g_ref.shape[0]
    row = lax.broadcasted_iota(jnp.int32, (cl, cl), 0)
    col = lax.broadcasted_iota(jnp.int32, (cl, cl), 1)
    incl = jnp.where(row >= col, 1.0, 0.0).astype(BF16)
    lane = lax.broadcasted_iota(jnp.int32, (1, cl), 1)
    pad = jnp.zeros((cl - heads, cl), F32)
    spans = [slice(c * cl, (c + 1) * cl) for c in range(tm // cl)]
    bs = []
    for r in spans:
        b = None
        for part in _split3(jax.nn.log_sigmoid(g_ref[r, LANES:])):
            term = _dot(incl, part)
            b = term if b is None else b + term
        bs.append(b)
    us = [g_ref[r, :LANES] - b for r, b in zip(spans, bs)]
    u_rows = jnp.concatenate([u.T[0:heads, :] for u in us], axis=0)
    b_rows = jnp.concatenate([b.T[0:heads, :] for b in bs], axis=0)
    cmax = u_rows
    shift = 1
    while shift < cl:
        cmax = jnp.where(lane >= shift, jnp.maximum(cmax, pltpu.roll(cmax, shift, axis=1)), cmax)
        shift *= 2
    u_max = jnp.broadcast_to(jnp.max(u_rows, axis=1, keepdims=True), u_rows.shape)
    b_last = jnp.broadcast_to(jnp.min(b_rows, axis=1, keepdims=True), b_rows.shape)
    for c, (r, u, b) in enumerate(zip(spans, us, bs)):
        hs = slice(c * heads, (c + 1) * heads)
        cols_ref[r, :] = jnp.concatenate([u, b, jnp.concatenate([cmax[hs], pad], axis=0).T], axis=1)
        rows_ref[:, r] = jnp.concatenate([u_rows[hs], u_max[hs], b_last[hs]], axis=0)


def _gate_prep(gates, *, heads, cl, tm=1024):
    m = gates.shape[0]
    tm = min(tm, m)
    assert m % tm == 0 and tm % cl == 0 and cl == LANES and heads == 8
    return pl.pallas_call(
        functools.partial(_gate_prep_kernel, heads=heads, cl=cl),
        grid=(m // tm,),
        in_specs=[pl.BlockSpec((tm, 2 * LANES), lambda i: (i, 0))],
        out_specs=[pl.BlockSpec((tm, 3 * LANES), lambda i: (i, 0)),
                   pl.BlockSpec((3 * heads, tm), lambda i: (0, i))],
        out_shape=[jax.ShapeDtypeStruct((m, 3 * LANES), F32),
                   jax.ShapeDtypeStruct((3 * heads, m), F32)],
        compiler_params=_params(("parallel",)),
        name="gate_prep",
    )(gates)


def _mlstm_kernel(q_ref, kt_ref, v_ref, o_ref, gcol_ref, grow_ref, hg_ref, out_ref,
                  s_scr, m_scr, ml_scr, *, heads):
    cl = MLSTM_CHUNK
    n_chunks = q_ref.shape[0] // cl
    dh = q_ref.shape[1] // heads

    @pl.when(pl.program_id(1) == 0)
    def _():
        s_scr[...] = jnp.zeros_like(s_scr)
        m_scr[...] = jnp.zeros_like(m_scr)
        ml_scr[...] = jnp.zeros_like(ml_scr)

    row = lax.broadcasted_iota(jnp.int32, (cl, cl), 0)
    col = lax.broadcasted_iota(jnp.int32, (cl, cl), 1)
    lower = row >= col
    ones = jnp.ones((cl, dh), BF16)

    m_rows = m_scr[...]
    m_lane = ml_scr[0:1, :]
    tables = []
    for c in range(n_chunks):
        span = slice(c * cl, (c + 1) * cl)
        u_cols, b_cols, cmax_cols = (gcol_ref[span, t * LANES:(t + 1) * LANES] for t in range(3))
        u_rows, u_max_rows, b_last_rows = (grow_ref[t * heads:(t + 1) * heads, span] for t in range(3))
        mm_last_rows = jnp.maximum(m_rows, u_max_rows)
        mm_cols = jnp.maximum(cmax_cols, m_lane)
        tables.append(dict(
            span=span, u_rows=u_rows, m_prev_rows=m_rows, mm_cols=mm_cols,
            decay_rows=jnp.exp(m_rows - mm_last_rows),
            ws_rows=jnp.exp(u_rows - mm_last_rows),
            floor_cols=jnp.exp(-(b_cols + mm_cols))))
        m_rows = b_last_rows + mm_last_rows
        m_lane = b_cols[cl - 1:cl, :] + mm_cols[cl - 1:cl, :]
    m_scr[...] = m_rows
    ml_scr[...] = jnp.broadcast_to(m_lane, ml_scr.shape)

    def lanes_of(mat, c):
        return jnp.broadcast_to(mat[:, c:c + 1], (cl, LANES))

    def stage_a(tab, h):
        sl = slice(h * dh, (h + 1) * dh)
        q, kt = q_ref[tab["span"], sl], kt_ref[sl, tab["span"]]
        v1 = jnp.concatenate([v_ref[tab["span"], sl], ones], axis=1)
        s_prev = s_scr[h]
        qk = _dot(q, kt)
        qs = _dot(q, s_prev.astype(BF16))
        kw = (kt.astype(F32) * tab["ws_rows"][h:h + 1, :]).astype(BF16)
        decay = jnp.broadcast_to(tab["decay_rows"][h:h + 1, :], (dh, LANES))
        s_scr[h] = jnp.concatenate([decay, decay], axis=1) * s_prev + _dot(kw, v1)
        return tab, h, sl, v1, qk, qs

    def stage_b(tab, h, sl, v1, qk, qs):
        mm = lanes_of(tab["mm_cols"], h)
        weight = jnp.where(lower, jnp.exp(tab["u_rows"][h:h + 1, :] - mm), 0.0)
        w_inter = jnp.exp(tab["m_prev_rows"][h:h + 1, :] - mm)
        return tab, sl, v1, qs, (qk * weight).astype(BF16), w_inter, lanes_of(tab["floor_cols"], h)

    def stage_c(tab, sl, v1, qs, p, w_inter, floor):
        pv = _dot(p, v1)
        both = pv + jnp.concatenate([w_inter, w_inter], axis=1) * qs
        hid = both[:, :dh] / jnp.maximum(jnp.abs(both[:, dh:]), floor)
        hid = hid * lax.rsqrt(jnp.mean(hid * hid, axis=-1, keepdims=True) + EPS) * hg_ref[:, sl]
        out_ref[tab["span"], sl] = (hid * jax.nn.sigmoid(o_ref[tab["span"], sl])).astype(out_ref.dtype)

    jobs = [(tab, h) for tab in tables for h in range(heads)]
    after_a, after_b = {}, {}
    for j in range(len(jobs) + 2):
        if j < len(jobs):
            after_a[j] = stage_a(*jobs[j])
        if 0 <= j - 1 < len(jobs):
            after_b[j - 1] = stage_b(*after_a.pop(j - 1))
        if 0 <= j - 2 < len(jobs):
            stage_c(*after_b.pop(j - 2))


def _mlstm(q, kt, v, o, gates, head_g, *, heads=N_HEADS_MLSTM, chunks=4):
    b, t, w = q.shape
    cl = MLSTM_CHUNK
    dh = w // heads
    rows = chunks * cl
    nc = t // rows
    assert t % rows == 0 and dh == LANES and cl == LANES
    gcols, grows = _gate_prep(gates, heads=heads, cl=cl)
    blk = lambda wd: pl.BlockSpec((None, rows, wd), lambda bi, c: (bi, c, 0))
    return pl.pallas_call(
        functools.partial(_mlstm_kernel, heads=heads),
        grid=(b, nc),
        in_specs=[blk(w), pl.BlockSpec((w, rows), lambda bi, c: (0, bi * nc + c)), blk(w), blk(w),
                  pl.BlockSpec((rows, 3 * LANES), lambda bi, c: (bi * nc + c, 0)),
                  pl.BlockSpec((3 * heads, rows), lambda bi, c: (0, bi * nc + c)),
                  _resident((1, w))],
        out_specs=blk(w),
        out_shape=jax.ShapeDtypeStruct((b, t, w), BF16),
        scratch_shapes=[pltpu.VMEM((heads, dh, 2 * dh), F32), pltpu.VMEM((heads, LANES), F32),
                        pltpu.VMEM((heads, LANES), F32)],
        compiler_params=_params(("parallel", "arbitrary")),
        name="mlstm",
    )(q, kt, v, o, gcols, grows, head_g.reshape(1, w))


def kernel(x, norm_g, ffn_w_gate, ffn_w_up, ffn_w_down, attn_w_in, attn_w_out, rel_bias,
           mlstm_w_in, mlstm_b_gates, mlstm_conv_w, mlstm_head_g, mlstm_w_out):
    bsz, t, d = x.shape
    depth = norm_g.shape[0]
    h = x.reshape(bsz * t, d)
    bf = lambda a: a.astype(BF16)

    w_gate, w_up, w_down = bf(ffn_w_gate), bf(ffn_w_up), bf(ffn_w_down)

    def ffn(h, layer, half, mix=None):
        g = norm_g[layer]
        return _ffn(h, g[2 * half * 2], g[2 * half * 2 + 1], w_gate, w_up, w_down, (layer, half), mix)

    for layer in range(depth):
        g = norm_g[layer]
        j = layer // 2
        h = ffn(h, layer, 0)
        if layer % 2 == 0:
            sb, dil = _attn_proj(h, g[2], bf(attn_w_in[j]))
            out_sb = _sb_attention(sb.reshape(bsz, t, -1))
            out_dil = _dil_attention(dil.reshape(bsz, t, -1), rel_bias.astype(F32))
            mix = ([out_sb.reshape(bsz * t, -1), out_dil.reshape(bsz * t, -1)], bf(attn_w_out[j]), g[3])
        else:
            width = mlstm_w_out.shape[1]
            n_head = mlstm_b_gates.shape[1] // 2
            tiles = lambda a: jnp.concatenate(
                [jnp.pad(part, ((0, 0), (0, LANES - n_head)))
                 for part in (a[:, :n_head], a[:, n_head:])], axis=1)
            w_in = bf(mlstm_w_in[j])
            w_gates = tiles(w_in[:, 4 * width:])
            b_gates = tiles(mlstm_b_gates[j].astype(F32).reshape(1, -1))
            q, kt, v, o, gates = _mlstm_proj(h, g[2], w_in, w_gates, mlstm_conv_w[j].astype(F32),
                                             b_gates, seq=t)
            r3 = lambda a: a.reshape(bsz, t, -1)
            hid = _mlstm(r3(q), kt, r3(v), r3(o), gates, mlstm_head_g[j].astype(F32))
            mix = ([hid.reshape(bsz * t, width)], bf(mlstm_w_out[j]), g[3])
        h = ffn(h, layer, 1, mix)
    return h.reshape(bsz, t, d)
```

```python
import functools
import math

import numpy as np
import jax
import jax.numpy as jnp
from jax import lax
from jax.experimental import pallas as pl
from jax.experimental.pallas import tpu as pltpu

EPS = 1e-6
HEAD_DIM_ATTN = 64
DIL_CONFIGS = ((128, 1), (512, 4), (2048, 16))
DIL_BLOCK = 128
DIL_UNROLL = 32
NUM_BUCKETS = 32
MAX_DISTANCE = 2048
N_HEADS_MLSTM = 8
MLSTM_CHUNK = 128
CONV_WIDTH = 4
LANES = 128
CONV_PAD = 8
MASKED = -1e30
V7X_VMEM_BYTES = 64 * 1024 * 1024
VMEM_LIMIT = V7X_VMEM_BYTES * 7 // 8

F32 = jnp.float32
BF16 = jnp.bfloat16


def _params(sem, vmem=VMEM_LIMIT):
    return pltpu.CompilerParams(dimension_semantics=sem, vmem_limit_bytes=vmem)


def _resident(shape):
    zeros = (0,) * len(shape)
    return pl.BlockSpec(shape, lambda *_: zeros, pipeline_mode=pl.Buffered(1))


def _rms(x, g):
    return x * lax.rsqrt(jnp.mean(x * x, axis=-1, keepdims=True) + EPS) * g


def _dot(a, b):
    return jnp.dot(a, b, preferred_element_type=F32)


def _dot_nt(a, b):
    return lax.dot_general(a, b, (((1,), (1,)), ((), ())), preferred_element_type=F32)


def _ffn_kernel(*refs, ff_chunk, n_mix, groups):
    mix_refs, refs = refs[:n_mix], refs[n_mix:]
    if n_mix:
        (wmix_ref, gmix_ref), refs = refs[:2], refs[2:]
    x_ref, gin_ref, gout_ref, wg_ref, wu_ref, wd_ref, o_ref = refs
    d_ff = wg_ref.shape[1]
    rows = x_ref.shape[0] // groups
    spans = [slice(r * rows, (r + 1) * rows) for r in range(groups)]

    xs = []
    for rs in spans:
        x = x_ref[rs, :]
        if n_mix:
            mixed = jnp.concatenate([r[rs, :] for r in mix_refs], axis=-1)
            x = x + _rms(_dot(mixed, wmix_ref[...]), gmix_ref[...])
        xs.append(x)
    xns = [_rms(x, gin_ref[...]).astype(BF16) for x in xs]
    accs = []
    for xn in xns:
        acc = None
        for c in range(d_ff // ff_chunk):
            sl = slice(c * ff_chunk, (c + 1) * ff_chunk)
            gate = _dot(xn, wg_ref[:, sl])
            up = _dot(xn, wu_ref[:, sl])
            h = (gate * jax.nn.sigmoid(gate) * up).astype(BF16)
            part = _dot(h, wd_ref[sl, :])
            acc = part if acc is None else acc + part
        accs.append(acc)
    for rs, x, acc in zip(spans, xs, accs):
        o_ref[rs, :] = x + 0.5 * _rms(acc, gout_ref[...])


def _ffn(x, g_in, g_out, wg, wu, wd, which=(), mix=None, *, tm=1024, groups=2, ff_chunk=256):
    m, d = x.shape
    d_ff = wg.shape[-1]
    assert m % tm == 0 and tm % groups == 0 and d_ff % ff_chunk == 0 and len(which) == wg.ndim - 2
    row = lambda width: pl.BlockSpec((tm, width), lambda i: (i, 0))
    lead = (None,) * len(which)
    weight = lambda shape: pl.BlockSpec(lead + shape, lambda i: tuple(which) + (0, 0),
                                        pipeline_mode=pl.Buffered(1))
    parts, w_mix, g_mix = mix if mix else ((), None, None)
    mix_specs = [row(a.shape[1]) for a in parts] + ([_resident(w_mix.shape), _resident((1, d))] if mix else [])
    mix_args = list(parts) + ([w_mix, g_mix.reshape(1, d)] if mix else [])
    return pl.pallas_call(
        functools.partial(_ffn_kernel, ff_chunk=ff_chunk, n_mix=len(parts), groups=groups),
        grid=(m // tm,),
        in_specs=mix_specs + [row(d), _resident((1, d)), _resident((1, d)),
                              weight((d, d_ff)), weight((d, d_ff)), weight((d_ff, d))],
        out_specs=row(d),
        out_shape=jax.ShapeDtypeStruct((m, d), F32),
        compiler_params=_params(("parallel",)),
        name="ffn",
    )(*mix_args, x, g_in.reshape(1, d), g_out.reshape(1, d), wg, wu, wd)


def _attn_proj_kernel(x_ref, g_ref, w_ref, sb_ref, dil_ref, *, n_chunk, scale, groups):
    w_sb = sb_ref.shape[1]
    w_q = w_sb // 3
    rows = x_ref.shape[0] // groups
    spans = [slice(r * rows, (r + 1) * rows) for r in range(groups)]
    xns = [_rms(x_ref[rs, :], g_ref[...]).astype(BF16) for rs in spans]
    for rs, xn in zip(spans, xns):
        for c in range(w_sb // n_chunk):
            sl = slice(c * n_chunk, (c + 1) * n_chunk)
            y = _dot(xn, w_ref[:, sl])
            if (c + 1) * n_chunk <= w_q:
                y = y * scale
            sb_ref[rs, sl] = y.astype(BF16)
        for c in range(dil_ref.shape[1] // n_chunk):
            sl = slice(c * n_chunk, (c + 1) * n_chunk)
            y = _dot(xn, w_ref[:, w_sb + c * n_chunk: w_sb + (c + 1) * n_chunk])
            if (c + 1) * n_chunk <= w_q:
                y = y * scale
            dil_ref[rs, sl] = y


def _attn_proj(x, g, w, *, tm=1024, n_chunk=512, groups=2):
    m, d = x.shape
    n = w.shape[1]
    half = n // 2
    assert m % tm == 0 and (half // 3) % n_chunk == 0
    scale = 1.0 / math.sqrt(HEAD_DIM_ATTN)
    return pl.pallas_call(
        functools.partial(_attn_proj_kernel, n_chunk=n_chunk, scale=scale, groups=groups),
        grid=(m // tm,),
        in_specs=[pl.BlockSpec((tm, d), lambda i: (i, 0)), _resident((1, d)), _resident((d, n))],
        out_specs=[pl.BlockSpec((tm, half), lambda i: (i, 0)),
                   pl.BlockSpec((tm, half), lambda i: (i, 0))],
        out_shape=[jax.ShapeDtypeStruct((m, half), BF16), jax.ShapeDtypeStruct((m, half), F32)],
        compiler_params=_params(("parallel",)),
        name="attn_proj",
    )(x, g.reshape(1, d), w)


LOG2E = math.log2(math.e)
SB_DEAD_LOG2 = -160.0


def _sb_kernel(q_ref, k_ref, v_ref, tri_ref, o_ref, *, blk, q_blocks):
    first_q = pl.program_id(2) * q_blocks
    lane = lax.broadcasted_iota(jnp.int32, (1, LANES), 1)
    row = lax.broadcasted_iota(jnp.int32, (blk, blk), 0)
    col = lax.broadcasted_iota(jnp.int32, (blk, blk), 1)
    causal = col < row
    causal2 = jnp.concatenate([causal, causal], axis=0)
    tri = tri_ref[...]
    head0 = lane < HEAD_DIM_ATTN

    def stacked(q2):
        zeros = jnp.zeros_like(q2)
        return jnp.concatenate([jnp.where(head0, q2, zeros), jnp.where(head0, zeros, q2)], axis=0)

    def pairs(jobs):
        chains = []
        for j, (qq, kb, diag, _) in enumerate(jobs):
            chains.append((j, pl.multiple_of(kb * blk, blk), causal2 if diag else None))
            chains.append((j, pl.multiple_of(jnp.maximum(kb - 1, 0) * blk, blk), kb >= 1))
        states = [job[3] for job in jobs]
        zs, mids = {}, {}
        for n in range(len(chains) + 2):
            if n < len(chains):
                j, ks, _ = chains[n]
                zs[n] = _dot_nt(jobs[j][0], k_ref[pl.ds(ks, blk), :]) * LOG2E
            if 0 <= n - 1 < len(chains):
                z, mask = zs.pop(n - 1), chains[n - 1][2]
                neg = -z
                log_keep = jnp.minimum(neg, 0.0) - jnp.log2(1.0 + jnp.exp2(jnp.minimum(z, neg)))
                log_beta = z + log_keep
                if mask is not None:
                    log_keep = jnp.where(mask, log_keep, 0.0)
                    log_beta = jnp.where(mask, log_beta, MASKED)
                hi = log_keep.astype(BF16)
                lo = (log_keep - hi.astype(F32)).astype(BF16)
                later = _dot(hi, tri) + _dot(lo, tri)
                mids[n - 1] = (log_beta, later, jnp.sum(log_keep, axis=-1, keepdims=True))
            if 0 <= n - 2 < len(chains):
                j, ks, _ = chains[n - 2]
                log_beta, later, total = mids.pop(n - 2)
                carry, acc = states[j]
                p = jnp.exp2(log_beta + later + carry)
                states[j] = (carry + total, acc + _dot(p.astype(BF16), v_ref[pl.ds(ks, blk), :]))
        return states

    def alive(state):
        return (jnp.max(state[0]) > SB_DEAD_LOG2).astype(jnp.int32)

    zero = (jnp.zeros((2 * blk, 1), F32), jnp.zeros((2 * blk, LANES), F32))
    qqs = [stacked(q_ref[g * blk:(g + 1) * blk, :]) for g in range(q_blocks)]
    states = pairs([(qq, first_q + g, True, zero) for g, qq in enumerate(qqs)])

    def store(g, state):
        o_ref[g * blk:(g + 1) * blk, :] = jnp.where(head0, state[1][:blk], state[1][blk:]).astype(o_ref.dtype)

    live = [alive(state) for state in states]
    for g, state in enumerate(states):
        store(g, state)

    for g, (qq, state) in enumerate(zip(qqs, states)):
        i = first_q + g
        n_pairs = lax.shift_right_logical(i, 1)

        @pl.when((live[g] > 0) & (n_pairs > 0))
        def _(g=g, qq=qq, state=state, i=i, n_pairs=n_pairs):
            def cond(loop):
                t, alive_now, _ = loop
                return (t < n_pairs) & (alive_now > 0)

            def body(loop):
                t, _, state = loop
                state, = pairs([(qq, i - 2 - 2 * t, False, state)])
                return t + 1, alive(state), state

            _, _, final = lax.while_loop(cond, body, (jnp.int32(0), live[g], state))
            store(g, final)


def _sb_attention(qkv, *, blk=256, q_blocks=8):
    b, t, w3 = qkv.shape
    w = w3 // 3
    pairs = w // LANES
    blk = min(blk, t)
    rows = blk * q_blocks
    assert t % rows == 0
    tri = jnp.asarray(np.tril(np.ones((blk, blk), np.float32), -1), BF16)
    return pl.pallas_call(
        functools.partial(_sb_kernel, blk=blk, q_blocks=q_blocks),
        grid=(b, pairs, t // rows),
        in_specs=[pl.BlockSpec((None, rows, LANES), lambda bi, p, i: (bi, i, p)),
                  pl.BlockSpec((None, t, LANES), lambda bi, p, i: (bi, 0, pairs + p)),
                  pl.BlockSpec((None, t, LANES), lambda bi, p, i: (bi, 0, 2 * pairs + p)),
                  _resident((blk, blk))],
        out_specs=pl.BlockSpec((None, rows, LANES), lambda bi, p, i: (bi, i, p)),
        out_shape=jax.ShapeDtypeStruct((b, t, w), BF16),
        compiler_params=_params(("parallel", "parallel", "arbitrary")),
        name="sb_attn",
    )(qkv, qkv, qkv, tri)


def _t5_bucket_np(dist):
    max_exact = NUM_BUCKETS // 2
    d = np.maximum(dist, 1).astype(np.float32)
    log_b = max_exact + (np.log(d / np.float32(max_exact)) / np.float32(math.log(MAX_DISTANCE / max_exact))
                         * np.float32(NUM_BUCKETS - max_exact)).astype(np.int32)
    log_b = np.minimum(log_b, NUM_BUCKETS - 1)
    return np.where(dist < max_exact, dist, log_b)


def _dil_tables():
    qi = np.arange(DIL_BLOCK)[:, None]
    ki = np.arange(2 * DIL_BLOCK)[None, :]
    dist = qi + DIL_BLOCK - ki
    buckets, valid = [], []
    for window, dil in DIL_CONFIGS:
        steps = window // dil
        buckets.append(_t5_bucket_np(np.maximum(dist, 0) * dil))
        valid.append((dist >= 0) & (dist <= steps))
    return np.stack(buckets).astype(np.int32), np.stack(valid).astype(np.int32)


def _dil_kernel(rb_ref, bucket_ref, valid_ref, q_ref, k_ref, v_ref, o_ref,
                bias_scr, num_scr, m_scr, l_scr, *, seq):
    p = pl.program_id(0)
    qb = DIL_BLOCK
    lane = lax.broadcasted_iota(jnp.int32, (1, LANES), 1)
    head0 = lane < HEAD_DIM_ATTN
    first_half = lax.broadcasted_iota(jnp.int32, (1, 2 * qb), 1) < qb

    @pl.when(pl.program_id(1) == 0)
    def _():
        for br in range(len(DIL_CONFIGS)):
            bucket = bucket_ref[br]
            valid = valid_ref[br] > 0
            for h in range(2):
                bias = jnp.zeros((qb, 2 * qb), F32)
                for b in range(NUM_BUCKETS):
                    bias = jnp.where(bucket == b, rb_ref[b, 2 * p + h], bias)
                bias_scr[br, h * qb:(h + 1) * qb, :] = jnp.where(valid, bias, MASKED)

    for br, (_, dil) in enumerate(DIL_CONFIGS):
        n_units = seq // qb

        n_blocks = seq // (qb * dil)
        run = min(DIL_UNROLL, n_blocks)
        runs = DIL_UNROLL // run
        assert DIL_UNROLL % run == 0 and n_blocks % run == 0 and (dil == 1 or n_blocks == run)

        def rows(ref, start, dil=dil):
            if dil == 1:
                return ref[pl.ds(start, qb), :].astype(BF16)
            return ref[pl.ds(start, qb, stride=dil), :].astype(BF16)

        def load_run(rho, dil=dil, run=run, whole=(n_blocks == run)):
            res, base = (rho, 0) if whole else (0, rho * run)
            starts = [(base + i) * (qb * dil) + res for i in range(run)]
            kb = [rows(k_ref, st) for st in starts]
            vb = [rows(v_ref, st) for st in starts]
            if whole:
                k_prev, v_prev = kb[0], vb[0]
                pen = jnp.where(first_half, MASKED, 0.0)
            else:
                st = jnp.maximum(base - 1, 0) * (qb * dil) + res
                k_prev, v_prev = rows(k_ref, st), rows(v_ref, st)
                pen = jnp.where(first_half, jnp.where(base == 0, MASKED, 0.0), 0.0)
            units = []
            for i, st in enumerate(starts):
                q2 = rows(q_ref, st)
                zeros = jnp.zeros_like(q2)
                qq = jnp.concatenate([jnp.where(head0, q2, zeros), jnp.where(head0, zeros, q2)], axis=0)
                k2 = jnp.concatenate([kb[i - 1] if i else k_prev, kb[i]], axis=0)
                v2 = jnp.concatenate([vb[i - 1] if i else v_prev, vb[i]], axis=0)
                units.append((st, pen if i == 0 else None, qq, k2, v2))
            return units

        def softmax_parts(s):
            m = jnp.max(s, axis=-1, keepdims=True)
            e = jnp.exp(s - m)
            return m, e, jnp.sum(e, axis=-1, keepdims=True)

        def group(g, _, br=br, dil=dil, runs=runs):
            units = [u for r in range(runs) for u in load_run(g * runs + r)]
            scores, parts = {}, {}
            for j in range(DIL_UNROLL + 2):
                if j < DIL_UNROLL:
                    _, pen, qq, k2, _ = units[j]
                    scores[j] = _dot_nt(qq, k2) + bias_scr[br]
                    if pen is not None:
                        scores[j] = scores[j] + pen
                if 0 <= j - 1 < DIL_UNROLL:
                    parts[j - 1] = softmax_parts(scores.pop(j - 1))
                if 0 <= j - 2 < DIL_UNROLL:
                    st, _, _, _, v2 = units[j - 2]
                    m, e, l = parts.pop(j - 2)
                    pv = _dot(e.astype(BF16), v2)
                    idx = pl.ds(st, qb) if dil == 1 else pl.ds(st, qb, stride=dil)
                    num_scr[br, idx, :] = jnp.where(head0, pv[:qb], pv[qb:])
                    m_scr[br, idx, :] = jnp.where(head0, m[:qb], m[qb:])
                    l_scr[br, idx, :] = jnp.where(head0, l[:qb], l[qb:])
            return 0

        lax.fori_loop(0, n_units // DIL_UNROLL, group, 0)

    rows_out = 256

    def finish(c, _):
        idx = pl.ds(pl.multiple_of(c * rows_out, rows_out), rows_out)
        m_all = [m_scr[br, idx, :] for br in range(len(DIL_CONFIGS))]
        m_max = functools.reduce(jnp.maximum, m_all)
        num = den = None
        for br, m_br in enumerate(m_all):
            wt = jnp.exp(m_br - m_max)
            n_br = wt * num_scr[br, idx, :]
            d_br = wt * l_scr[br, idx, :]
            num = n_br if num is None else num + n_br
            den = d_br if den is None else den + d_br
        o_ref[idx, :] = (num / den).astype(o_ref.dtype)
        return 0

    lax.fori_loop(0, seq // rows_out, finish, 0)


def _dil_attention(qkv, rel_bias):
    b, t, w3 = qkv.shape
    w = w3 // 3
    pairs = w // LANES
    assert t % (DIL_BLOCK * max(d for _, d in DIL_CONFIGS)) == 0 and t % 256 == 0
    bucket, valid = _dil_tables()
    nbr = len(DIL_CONFIGS)
    seq_spec = lambda off: pl.BlockSpec((None, t, LANES), lambda p, bi: (bi, 0, off + p))
    return pl.pallas_call(
        functools.partial(_dil_kernel, seq=t),
        grid=(pairs, b),
        in_specs=[pl.BlockSpec(memory_space=pltpu.SMEM),
                  _resident((nbr, DIL_BLOCK, 2 * DIL_BLOCK)),
                  _resident((nbr, DIL_BLOCK, 2 * DIL_BLOCK)),
                  seq_spec(0), seq_spec(pairs), seq_spec(2 * pairs)],
        out_specs=pl.BlockSpec((None, t, LANES), lambda p, bi: (bi, 0, p)),
        out_shape=jax.ShapeDtypeStruct((b, t, w), BF16),
        scratch_shapes=[pltpu.VMEM((nbr, 2 * DIL_BLOCK, 2 * DIL_BLOCK), F32),
                        pltpu.VMEM((nbr, t, LANES), F32), pltpu.VMEM((nbr, t, LANES), F32),
                        pltpu.VMEM((nbr, t, LANES), F32)],
        compiler_params=_params(("arbitrary", "arbitrary")),
        name="dil_attn",
    )(rel_bias, jnp.asarray(bucket), jnp.asarray(valid), qkv, qkv, qkv)


def _mlstm_proj_kernel(x_ref, g_ref, w_ref, wg_ref, cw_ref, bg_ref, q_ref, kt_ref, v_ref, o_ref,
                       gate_ref, *hist_scrs, tiles_per_seq, n_chunk, scale):
    i = pl.program_id(0)
    tm = x_ref.shape[0]
    width = q_ref.shape[1]
    xn = _rms(x_ref[...], g_ref[...]).astype(BF16)

    def even_rows(first, n):
        return pl.ds(2 * first, n, stride=2)

    slabs = n_chunk // LANES
    n_conv = 2 * width // n_chunk
    assert len(hist_scrs) == n_conv

    @pl.when(i % tiles_per_seq == 0)
    def _():
        for hist in hist_scrs:
            hist[:, 0:2 * CONV_PAD, :] = jnp.zeros((slabs, 2 * CONV_PAD, LANES), F32)

    def project(c):
        pre = _dot(xn, w_ref[:, c * n_chunk:(c + 1) * n_chunk])
        for s in range(slabs):
            hist_scrs[c][s, even_rows(CONV_PAD, tm), :] = pre[:, s * LANES:(s + 1) * LANES]

    def conv(c):
        sl = slice(c * n_chunk, (c + 1) * n_chunk)
        hist = hist_scrs[c]
        parts = []
        for s in range(slabs):
            y = None
            for tap in range(CONV_WIDTH):
                off = CONV_PAD - (CONV_WIDTH - 1) + tap
                w_tap = cw_ref[tap:tap + 1, (c * slabs + s) * LANES:(c * slabs + s + 1) * LANES]
                term = w_tap * hist[s, even_rows(off, tm), :]
                y = term if y is None else y + term
            parts.append(y)
            hist[s, even_rows(0, CONV_PAD), :] = hist[s, even_rows(tm, CONV_PAD), :]
        y = jnp.concatenate(parts, axis=1)
        y = y * jax.nn.sigmoid(y)
        if c * n_chunk < width:
            q_ref[:, sl] = (y * scale).astype(BF16)
        else:
            kt_ref[c * n_chunk - width:(c + 1) * n_chunk - width, :] = y.T.astype(BF16)

    def plain(c):
        sl = slice((c // 2) * n_chunk, (c // 2 + 1) * n_chunk)
        if c % 2 == 0:
            v_ref[:, sl] = _dot(xn, w_ref[:, 2 * width + sl.start:2 * width + sl.stop]).astype(BF16)
        else:
            o_ref[:, sl] = _dot(xn, w_ref[:, 3 * width + sl.start:3 * width + sl.stop])

    n_plain = 2 * width // n_chunk
    for c in range(n_conv + 1):
        if c < n_conv:
            project(c)
        if c >= 1:
            conv(c - 1)
            if c - 1 < n_plain:
                plain(c - 1)
    for c in range(n_conv, n_plain):
        plain(c)
    gate_ref[...] = _dot(xn, wg_ref[...]) + bg_ref[...]


def _mlstm_proj(x, g, w_in, w_gates, conv_w, b_gates, *, seq, tm=512, n_chunk=512):
    m, d = x.shape
    width = conv_w.shape[1] // 2
    assert m % tm == 0 and seq % tm == 0 and width % n_chunk == 0 and w_in.shape[1] >= 4 * width
    row = lambda wd: pl.BlockSpec((tm, wd), lambda i: (i, 0))
    scale = 1.0 / math.sqrt(width // N_HEADS_MLSTM)
    return pl.pallas_call(
        functools.partial(_mlstm_proj_kernel, tiles_per_seq=seq // tm, n_chunk=n_chunk, scale=scale),
        grid=(m // tm,),
        in_specs=[row(d), _resident((1, d)), _resident(w_in.shape), _resident((d, 2 * LANES)),
                  _resident(conv_w.shape), _resident((1, 2 * LANES))],
        out_specs=[row(width), pl.BlockSpec((width, tm), lambda i: (0, i)), row(width), row(width),
                   row(2 * LANES)],
        out_shape=[jax.ShapeDtypeStruct((m, width), BF16), jax.ShapeDtypeStruct((width, m), BF16),
                   jax.ShapeDtypeStruct((m, width), BF16), jax.ShapeDtypeStruct((m, width), F32),
                   jax.ShapeDtypeStruct((m, 2 * LANES), F32)],
        scratch_shapes=[pltpu.VMEM((n_chunk // LANES, 2 * (tm + CONV_PAD), LANES), F32)
                        for _ in range(2 * width // n_chunk)],
        compiler_params=_params(("arbitrary",)),
        name="mlstm_proj",
    )(x, g.reshape(1, d), w_in, w_gates, conv_w, b_gates)


def _split3(x):
    hi = x.astype(BF16)
    r = x - hi.astype(F32)
    mid = r.astype(BF16)
    return hi, mid, (r - mid.astype(F32)).astype(BF16)


def _gate_prep_kernel(g_ref, cols_ref, rows_ref, *, heads, cl):
    tm = g_ref.shape[0]
    row = lax.broadcasted_iota(jnp.int32, (cl, cl), 0)
    col = lax.broadcasted_iota(jnp.int32, (cl, cl), 1)
    incl = jnp.where(row >= col, 1.0, 0.0).astype(BF16)
    lane = lax.broadcasted_iota(jnp.int32, (1, cl), 1)
    pad = jnp.zeros((cl - heads, cl), F32)
    spans = [slice(c * cl, (c + 1) * cl) for c in range(tm // cl)]
    bs = []
    for r in spans:
        b = None
        for part in _split3(jax.nn.log_sigmoid(g_ref[r, LANES:])):
            term = _dot(incl, part)
            b = term if b is None else b + term
        bs.append(b)
    us = [g_ref[r, :LANES] - b for r, b in zip(spans, bs)]
    u_rows = jnp.concatenate([u.T[0:heads, :] for u in us], axis=0)
    b_rows = jnp.concatenate([b.T[0:heads, :] for b in bs], axis=0)
    cmax = u_rows
    shift = 1
    while shift < cl:
        cmax = jnp.where(lane >= shift, jnp.maximum(cmax, pltpu.roll(cmax, shift, axis=1)), cmax)
        shift *= 2
    u_max = jnp.broadcast_to(jnp.max(u_rows, axis=1, keepdims=True), u_rows.shape)
    b_last = jnp.broadcast_to(jnp.min(b_rows, axis=1, keepdims=True), b_rows.shape)
    for c, (r, u, b) in enumerate(zip(spans, us, bs)):
        hs = slice(c * heads, (c + 1) * heads)
        cols_ref[r, :] = jnp.concatenate([u, b, jnp.concatenate([cmax[hs], pad], axis=0).T], axis=1)
        rows_ref[:, r] = jnp.concatenate([u_rows[hs], u_max[hs], b_last[hs]], axis=0)


def _gate_prep(gates, *, heads, cl, tm=1024):
    m = gates.shape[0]
    tm = min(tm, m)
    assert m % tm == 0 and tm % cl == 0 and cl == LANES and heads == 8
    return pl.pallas_call(
        functools.partial(_gate_prep_kernel, heads=heads, cl=cl),
        grid=(m // tm,),
        in_specs=[pl.BlockSpec((tm, 2 * LANES), lambda i: (i, 0))],
        out_specs=[pl.BlockSpec((tm, 3 * LANES), lambda i: (i, 0)),
                   pl.BlockSpec((3 * heads, tm), lambda i: (0, i))],
        out_shape=[jax.ShapeDtypeStruct((m, 3 * LANES), F32),
                   jax.ShapeDtypeStruct((3 * heads, m), F32)],
        compiler_params=_params(("parallel",)),
        name="gate_prep",
    )(gates)


def _mlstm_kernel(q_ref, kt_ref, v_ref, o_ref, gcol_ref, grow_ref, hg_ref, out_ref,
                  s_scr, m_scr, ml_scr, *, heads):
    cl = MLSTM_CHUNK
    n_chunks = q_ref.shape[0] // cl
    dh = q_ref.shape[1] // heads

    @pl.when(pl.program_id(1) == 0)
    def _():
        s_scr[...] = jnp.zeros_like(s_scr)
        m_scr[...] = jnp.zeros_like(m_scr)
        ml_scr[...] = jnp.zeros_like(ml_scr)

    row = lax.broadcasted_iota(jnp.int32, (cl, cl), 0)
    col = lax.broadcasted_iota(jnp.int32, (cl, cl), 1)
    lower = row >= col
    ones = jnp.ones((cl, dh), BF16)

    m_rows = m_scr[...]
    m_lane = ml_scr[0:1, :]
    tables = []
    for c in range(n_chunks):
        span = slice(c * cl, (c + 1) * cl)
        u_cols, b_cols, cmax_cols = (gcol_ref[span, t * LANES:(t + 1) * LANES] for t in range(3))
        u_rows, u_max_rows, b_last_rows = (grow_ref[t * heads:(t + 1) * heads, span] for t in range(3))
        mm_last_rows = jnp.maximum(m_rows, u_max_rows)
        mm_cols = jnp.maximum(cmax_cols, m_lane)
        tables.append(dict(
            span=span, u_rows=u_rows, m_prev_rows=m_rows, mm_cols=mm_cols,
            decay_rows=jnp.exp(m_rows - mm_last_rows),
            ws_rows=jnp.exp(u_rows - mm_last_rows),
            floor_cols=jnp.exp(-(b_cols + mm_cols))))
        m_rows = b_last_rows + mm_last_rows
        m_lane = b_cols[cl - 1:cl, :] + mm_cols[cl - 1:cl, :]
    m_scr[...] = m_rows
    ml_scr[...] = jnp.broadcast_to(m_lane, ml_scr.shape)

    def lanes_of(mat, c):
        return jnp.broadcast_to(mat[:, c:c + 1], (cl, LANES))

    def stage_a(tab, h):
        sl = slice(h * dh, (h + 1) * dh)
        q, kt = q_ref[tab["span"], sl], kt_ref[sl, tab["span"]]
        v1 = jnp.concatenate([v_ref[tab["span"], sl], ones], axis=1)
        s_prev = s_scr[h]
        qk = _dot(q, kt)
        qs = _dot(q, s_prev.astype(BF16))
        kw = (kt.astype(F32) * tab["ws_rows"][h:h + 1, :]).astype(BF16)
        decay = jnp.broadcast_to(tab["decay_rows"][h:h + 1, :], (dh, LANES))
        s_scr[h] = jnp.concatenate([decay, decay], axis=1) * s_prev + _dot(kw, v1)
        return tab, h, sl, v1, qk, qs

    def stage_b(tab, h, sl, v1, qk, qs):
        mm = lanes_of(tab["mm_cols"], h)
        weight = jnp.where(lower, jnp.exp(tab["u_rows"][h:h + 1, :] - mm), 0.0)
        w_inter = jnp.exp(tab["m_prev_rows"][h:h + 1, :] - mm)
        return tab, sl, v1, qs, (qk * weight).astype(BF16), w_inter, lanes_of(tab["floor_cols"], h)

    def stage_c(tab, sl, v1, qs, p, w_inter, floor):
        pv = _dot(p, v1)
        both = pv + jnp.concatenate([w_inter, w_inter], axis=1) * qs
        hid = both[:, :dh] / jnp.maximum(jnp.abs(both[:, dh:]), floor)
        hid = hid * lax.rsqrt(jnp.mean(hid * hid, axis=-1, keepdims=True) + EPS) * hg_ref[:, sl]
        out_ref[tab["span"], sl] = (hid * jax.nn.sigmoid(o_ref[tab["span"], sl])).astype(out_ref.dtype)

    jobs = [(tab, h) for tab in tables for h in range(heads)]
    after_a, after_b = {}, {}
    for j in range(len(jobs) + 2):
        if j < len(jobs):
            after_a[j] = stage_a(*jobs[j])
        if 0 <= j - 1 < len(jobs):
            after_b[j - 1] = stage_b(*after_a.pop(j - 1))
        if 0 <= j - 2 < len(jobs):
            stage_c(*after_b.pop(j - 2))


def _mlstm(q, kt, v, o, gates, head_g, *, heads=N_HEADS_MLSTM, chunks=4):
    b, t, w = q.shape
    cl = MLSTM_CHUNK
    dh = w // heads
    rows = chunks * cl
    nc = t // rows
    assert t % rows == 0 and dh == LANES and cl == LANES
    gcols, grows = _gate_prep(gates, heads=heads, cl=cl)
    blk = lambda wd: pl.BlockSpec((None, rows, wd), lambda bi, c: (bi, c, 0))
    return pl.pallas_call(
        functools.partial(_mlstm_kernel, heads=heads),
        grid=(b, nc),
        in_specs=[blk(w), pl.BlockSpec((w, rows), lambda bi, c: (0, bi * nc + c)), blk(w), blk(w),
                  pl.BlockSpec((rows, 3 * LANES), lambda bi, c: (bi * nc + c, 0)),
                  pl.BlockSpec((3 * heads, rows), lambda bi, c: (0, bi * nc + c)),
                  _resident((1, w))],
        out_specs=blk(w),
        out_shape=jax.ShapeDtypeStruct((b, t, w), BF16),
        scratch_shapes=[pltpu.VMEM((heads, dh, 2 * dh), F32), pltpu.VMEM((heads, LANES), F32),
                        pltpu.VMEM((heads, LANES), F32)],
        compiler_params=_params(("parallel", "arbitrary")),
        name="mlstm",
    )(q, kt, v, o, gcols, grows, head_g.reshape(1, w))


def kernel(x, norm_g, ffn_w_gate, ffn_w_up, ffn_w_down, attn_w_in, attn_w_out, rel_bias,
           mlstm_w_in, mlstm_b_gates, mlstm_conv_w, mlstm_head_g, mlstm_w_out):
    bsz, t, d = x.shape
    depth = norm_g.shape[0]
    h = x.reshape(bsz * t, d)
    bf = lambda a: a.astype(BF16)

    w_gate, w_up, w_down = bf(ffn_w_gate), bf(ffn_w_up), bf(ffn_w_down)

    def ffn(h, layer, half, mix=None):
        g = norm_g[layer]
        return _ffn(h, g[2 * half * 2], g[2 * half * 2 + 1], w_gate, w_up, w_down, (layer, half), mix)

    for layer in range(depth):
        g = norm_g[layer]
        j = layer // 2
        h = ffn(h, layer, 0)
        if layer % 2 == 0:
            sb, dil = _attn_proj(h, g[2], bf(attn_w_in[j]))
            out_sb = _sb_attention(sb.reshape(bsz, t, -1))
            out_dil = _dil_attention(dil.reshape(bsz, t, -1), rel_bias.astype(F32))
            mix = ([out_sb.reshape(bsz * t, -1), out_dil.reshape(bsz * t, -1)], bf(attn_w_out[j]), g[3])
        else:
            width = mlstm_w_out.shape[1]
            n_head = mlstm_b_gates.shape[1] // 2
            tiles = lambda a: jnp.concatenate(
                [jnp.pad(part, ((0, 0), (0, LANES - n_head)))
                 for part in (a[:, :n_head], a[:, n_head:])], axis=1)
            w_in = bf(mlstm_w_in[j])
            w_gates = tiles(w_in[:, 4 * width:])
            b_gates = tiles(mlstm_b_gates[j].astype(F32).reshape(1, -1))
            q, kt, v, o, gates = _mlstm_proj(h, g[2], w_in, w_gates, mlstm_conv_w[j].astype(F32),
                                             b_gates, seq=t)
            r3 = lambda a: a.reshape(bsz, t, -1)
            hid = _mlstm(r3(q), kt, r3(v), r3(o), gates, mlstm_head_g[j].astype(F32))
            mix = ([hid.reshape(bsz * t, width)], bf(mlstm_w_out[j]), g[3])
        h = ffn(h, layer, 1, mix)
    return h.reshape(bsz, t, d)
```

```python
import functools
import math

import numpy as np
import jax
import jax.numpy as jnp
from jax import lax
from jax.experimental import pallas as pl
from jax.experimental.pallas import tpu as pltpu

EPS = 1e-6
HEAD_DIM_ATTN = 64
DIL_CONFIGS = ((128, 1), (512, 4), (2048, 16))
DIL_BLOCK = 128
DIL_UNROLL = 32
NUM_BUCKETS = 32
MAX_DISTANCE = 2048
N_HEADS_MLSTM = 8
MLSTM_CHUNK = 128
CONV_WIDTH = 4
LANES = 128
CONV_PAD = 8
MASKED = -1e30
V7X_VMEM_BYTES = 64 * 1024 * 1024
VMEM_LIMIT = V7X_VMEM_BYTES * 7 // 8

F32 = jnp.float32
BF16 = jnp.bfloat16


def _params(sem, vmem=VMEM_LIMIT):
    return pltpu.CompilerParams(dimension_semantics=sem, vmem_limit_bytes=vmem)


def _resident(shape):
    zeros = (0,) * len(shape)
    return pl.BlockSpec(shape, lambda *_: zeros, pipeline_mode=pl.Buffered(1))


def _rms(x, g):
    return x * lax.rsqrt(jnp.mean(x * x, axis=-1, keepdims=True) + EPS) * g


def _dot(a, b):
    return jnp.dot(a, b, preferred_element_type=F32)


def _dot_nt(a, b):
    return lax.dot_general(a, b, (((1,), (1,)), ((), ())), preferred_element_type=F32)


def _ffn_kernel(*refs, ff_chunk, n_mix, groups):
    mix_refs, refs = refs[:n_mix], refs[n_mix:]
    if n_mix:
        (wmix_ref, gmix_ref), refs = refs[:2], refs[2:]
    x_ref, gin_ref, gout_ref, wg_ref, wu_ref, wd_ref, o_ref = refs
    d_ff = wg_ref.shape[1]
    rows = x_ref.shape[0] // groups
    spans = [slice(r * rows, (r + 1) * rows) for r in range(groups)]

    xs = []
    for rs in spans:
        x = x_ref[rs, :]
        if n_mix:
            mixed = jnp.concatenate([r[rs, :] for r in mix_refs], axis=-1)
            x = x + _rms(_dot(mixed, wmix_ref[...]), gmix_ref[...])
        xs.append(x)
    xns = [_rms(x, gin_ref[...]).astype(BF16) for x in xs]
    accs = []
    for xn in xns:
        acc = None
        for c in range(d_ff // ff_chunk):
            sl = slice(c * ff_chunk, (c + 1) * ff_chunk)
            gate = _dot(xn, wg_ref[:, sl])
            up = _dot(xn, wu_ref[:, sl])
            h = (gate * jax.nn.sigmoid(gate) * up).astype(BF16)
            part = _dot(h, wd_ref[sl, :])
            acc = part if acc is None else acc + part
        accs.append(acc)
    for rs, x, acc in zip(spans, xs, accs):
        o_ref[rs, :] = x + 0.5 * _rms(acc, gout_ref[...])


def _ffn(x, g_in, g_out, wg, wu, wd, which=(), mix=None, *, tm=1024, groups=2, ff_chunk=256):
    m, d = x.shape
    d_ff = wg.shape[-1]
    assert m % tm == 0 and tm % groups == 0 and d_ff % ff_chunk == 0 and len(which) == wg.ndim - 2
    row = lambda width: pl.BlockSpec((tm, width), lambda i: (i, 0))
    lead = (None,) * len(which)
    weight = lambda shape: pl.BlockSpec(lead + shape, lambda i: tuple(which) + (0, 0),
                                        pipeline_mode=pl.Buffered(1))
    parts, w_mix, g_mix = mix if mix else ((), None, None)
    mix_specs = [row(a.shape[1]) for a in parts] + ([_resident(w_mix.shape), _resident((1, d))] if mix else [])
    mix_args = list(parts) + ([w_mix, g_mix.reshape(1, d)] if mix else [])
    return pl.pallas_call(
        functools.partial(_ffn_kernel, ff_chunk=ff_chunk, n_mix=len(parts), groups=groups),
        grid=(m // tm,),
        in_specs=mix_specs + [row(d), _resident((1, d)), _resident((1, d)),
                              weight((d, d_ff)), weight((d, d_ff)), weight((d_ff, d))],
        out_specs=row(d),
        out_shape=jax.ShapeDtypeStruct((m, d), F32),
        compiler_params=_params(("parallel",)),
        name="ffn",
    )(*mix_args, x, g_in.reshape(1, d), g_out.reshape(1, d), wg, wu, wd)


def _attn_proj_kernel(x_ref, g_ref, w_ref, sb_ref, dil_ref, *, n_chunk, scale, groups):
    w_sb = sb_ref.shape[1]
    w_q = w_sb // 3
    rows = x_ref.shape[0] // groups
    spans = [slice(r * rows, (r + 1) * rows) for r in range(groups)]
    xns = [_rms(x_ref[rs, :], g_ref[...]).astype(BF16) for rs in spans]
    for rs, xn in zip(spans, xns):
        for c in range(w_sb // n_chunk):
            sl = slice(c * n_chunk, (c + 1) * n_chunk)
            y = _dot(xn, w_ref[:, sl])
            if (c + 1) * n_chunk <= w_q:
                y = y * scale
            sb_ref[rs, sl] = y.astype(BF16)
        for c in range(dil_ref.shape[1] // n_chunk):
            sl = slice(c * n_chunk, (c + 1) * n_chunk)
            y = _dot(xn, w_ref[:, w_sb + c * n_chunk: w_sb + (c + 1) * n_chunk])
            if (c + 1) * n_chunk <= w_q:
                y = y * scale
            dil_ref[rs, sl] = y


def _attn_proj(x, g, w, *, tm=1024, n_chunk=512, groups=2):
    m, d = x.shape
    n = w.shape[1]
    half = n // 2
    assert m % tm == 0 and (half // 3) % n_chunk == 0
    scale = 1.0 / math.sqrt(HEAD_DIM_ATTN)
    return pl.pallas_call(
        functools.partial(_attn_proj_kernel, n_chunk=n_chunk, scale=scale, groups=groups),
        grid=(m // tm,),
        in_specs=[pl.BlockSpec((tm, d), lambda i: (i, 0)), _resident((1, d)), _resident((d, n))],
        out_specs=[pl.BlockSpec((tm, half), lambda i: (i, 0)),
                   pl.BlockSpec((tm, half), lambda i: (i, 0))],
        out_shape=[jax.ShapeDtypeStruct((m, half), BF16), jax.ShapeDtypeStruct((m, half), F32)],
        compiler_params=_params(("parallel",)),
        name="attn_proj",
    )(x, g.reshape(1, d), w)


LOG2E = math.log2(math.e)
SB_DEAD_LOG2 = -160.0


def _sb_kernel(q_ref, k_ref, v_ref, tri_ref, o_ref, *, blk, q_blocks):
    first_q = pl.program_id(2) * q_blocks
    lane = lax.broadcasted_iota(jnp.int32, (1, LANES), 1)
    row = lax.broadcasted_iota(jnp.int32, (blk, blk), 0)
    col = lax.broadcasted_iota(jnp.int32, (blk, blk), 1)
    causal = col < row
    causal2 = jnp.concatenate([causal, causal], axis=0)
    tri = tri_ref[...]
    head0 = lane < HEAD_DIM_ATTN

    def stacked(q2):
        zeros = jnp.zeros_like(q2)
        return jnp.concatenate([jnp.where(head0, q2, zeros), jnp.where(head0, zeros, q2)], axis=0)

    def pairs(jobs):
        chains = []
        for j, (qq, kb, diag, _) in enumerate(jobs):
            chains.append((j, pl.multiple_of(kb * blk, blk), causal2 if diag else None))
            chains.append((j, pl.multiple_of(jnp.maximum(kb - 1, 0) * blk, blk), kb >= 1))
        states = [job[3] for job in jobs]
        zs, mids = {}, {}
        for n in range(len(chains) + 2):
            if n < len(chains):
                j, ks, _ = chains[n]
                zs[n] = _dot_nt(jobs[j][0], k_ref[pl.ds(ks, blk), :]) * LOG2E
            if 0 <= n - 1 < len(chains):
                z, mask = zs.pop(n - 1), chains[n - 1][2]
                neg = -z
                log_keep = jnp.minimum(neg, 0.0) - jnp.log2(1.0 + jnp.exp2(jnp.minimum(z, neg)))
                log_beta = z + log_keep
                if mask is not None:
                    log_keep = jnp.where(mask, log_keep, 0.0)
                    log_beta = jnp.where(mask, log_beta, MASKED)
                hi = log_keep.astype(BF16)
                lo = (log_keep - hi.astype(F32)).astype(BF16)
                later = _dot(hi, tri) + _dot(lo, tri)
                mids[n - 1] = (log_beta, later, jnp.sum(log_keep, axis=-1, keepdims=True))
            if 0 <= n - 2 < len(chains):
                j, ks, _ = chains[n - 2]
                log_beta, later, total = mids.pop(n - 2)
                carry, acc = states[j]
                p = jnp.exp2(log_beta + later + carry)
                states[j] = (carry + total, acc + _dot(p.astype(BF16), v_ref[pl.ds(ks, blk), :]))
        return states

    def alive(state):
        return (jnp.max(state[0]) > SB_DEAD_LOG2).astype(jnp.int32)

    zero = (jnp.zeros((2 * blk, 1), F32), jnp.zeros((2 * blk, LANES), F32))
    qqs = [stacked(q_ref[g * blk:(g + 1) * blk, :]) for g in range(q_blocks)]
    states = pairs([(qq, first_q + g, True, zero) for g, qq in enumerate(qqs)])

    def store(g, state):
        o_ref[g * blk:(g + 1) * blk, :] = jnp.where(head0, state[1][:blk], state[1][blk:]).astype(o_ref.dtype)

    live = [alive(state) for state in states]
    for g, state in enumerate(states):
        store(g, state)

    for g, (qq, state) in enumerate(zip(qqs, states)):
        i = first_q + g
        n_pairs = lax.shift_right_logical(i, 1)

        @pl.when((live[g] > 0) & (n_pairs > 0))
        def _(g=g, qq=qq, state=state, i=i, n_pairs=n_pairs):
            def cond(loop):
                t, alive_now, _ = loop
                return (t < n_pairs) & (alive_now > 0)

            def body(loop):
                t, _, state = loop
                state, = pairs([(qq, i - 2 - 2 * t, False, state)])
                return t + 1, alive(state), state

            _, _, final = lax.while_loop(cond, body, (jnp.int32(0), live[g], state))
            store(g, final)


def _sb_attention(qkv, *, blk=256, q_blocks=8):
    b, t, w3 = qkv.shape
    w = w3 // 3
    pairs = w // LANES
    blk = min(blk, t)
    rows = blk * q_blocks
    assert t % rows == 0
    tri = jnp.asarray(np.tril(np.ones((blk, blk), np.float32), -1), BF16)
    return pl.pallas_call(
        functools.partial(_sb_kernel, blk=blk, q_blocks=q_blocks),
        grid=(b, pairs, t // rows),
        in_specs=[pl.BlockSpec((None, rows, LANES), lambda bi, p, i: (bi, i, p)),
                  pl.BlockSpec((None, t, LANES), lambda bi, p, i: (bi, 0, pairs + p)),
                  pl.BlockSpec((None, t, LANES), lambda bi, p, i: (bi, 0, 2 * pairs + p)),
                  _resident((blk, blk))],
        out_specs=pl.BlockSpec((None, rows, LANES), lambda bi, p, i: (bi, i, p)),
        out_shape=jax.ShapeDtypeStruct((b, t, w), BF16),
        compiler_params=_params(("parallel", "parallel", "arbitrary")),
        name="sb_attn",
    )(qkv, qkv, qkv, tri)


def _t5_bucket_np(dist):
    max_exact = NUM_BUCKETS // 2
    d = np.maximum(dist, 1).astype(np.float32)
    log_b = max_exact + (np.log(d / np.float32(max_exact)) / np.float32(math.log(MAX_DISTANCE / max_exact))
                         * np.float32(NUM_BUCKETS - max_exact)).astype(np.int32)
    log_b = np.minimum(log_b, NUM_BUCKETS - 1)
    return np.where(dist < max_exact, dist, log_b)


def _dil_tables():
    qi = np.arange(DIL_BLOCK)[:, None]
    ki = np.arange(2 * DIL_BLOCK)[None, :]
    dist = qi + DIL_BLOCK - ki
    buckets, valid = [], []
    for window, dil in DIL_CONFIGS:
        steps = window // dil
        buckets.append(_t5_bucket_np(np.maximum(dist, 0) * dil))
        valid.append((dist >= 0) & (dist <= steps))
    return np.stack(buckets).astype(np.int32), np.stack(valid).astype(np.int32)


def _dil_kernel(rb_ref, bucket_ref, valid_ref, q_ref, k_ref, v_ref, o_ref,
                bias_scr, num_scr, m_scr, l_scr, *, seq):
    p = pl.program_id(0)
    qb = DIL_BLOCK
    lane = lax.broadcasted_iota(jnp.int32, (1, LANES), 1)
    head0 = lane < HEAD_DIM_ATTN
    first_half = lax.broadcasted_iota(jnp.int32, (1, 2 * qb), 1) < qb

    @pl.when(pl.program_id(1) == 0)
    def _():
        for br in range(len(DIL_CONFIGS)):
            bucket = bucket_ref[br]
            valid = valid_ref[br] > 0
            for h in range(2):
                bias = jnp.zeros((qb, 2 * qb), F32)
                for b in range(NUM_BUCKETS):
                    bias = jnp.where(bucket == b, rb_ref[b, 2 * p + h], bias)
                bias_scr[br, h * qb:(h + 1) * qb, :] = jnp.where(valid, bias, MASKED)

    for br, (_, dil) in enumerate(DIL_CONFIGS):
        n_units = seq // qb

        n_blocks = seq // (qb * dil)
        run = min(DIL_UNROLL, n_blocks)
        runs = DIL_UNROLL // run
        assert DIL_UNROLL % run == 0 and n_blocks % run == 0 and (dil == 1 or n_blocks == run)

        def rows(ref, start, dil=dil):
            if dil == 1:
                return ref[pl.ds(start, qb), :].astype(BF16)
            return ref[pl.ds(start, qb, stride=dil), :].astype(BF16)

        def load_run(rho, dil=dil, run=run, whole=(n_blocks == run)):
            res, base = (rho, 0) if whole else (0, rho * run)
            starts = [(base + i) * (qb * dil) + res for i in range(run)]
            kb = [rows(k_ref, st) for st in starts]
            vb = [rows(v_ref, st) for st in starts]
            if whole:
                k_prev, v_prev = kb[0], vb[0]
                pen = jnp.where(first_half, MASKED, 0.0)
            else:
                st = jnp.maximum(base - 1, 0) * (qb * dil) + res
                k_prev, v_prev = rows(k_ref, st), rows(v_ref, st)
                pen = jnp.where(first_half, jnp.where(base == 0, MASKED, 0.0), 0.0)
            units = []
            for i, st in enumerate(starts):
                q2 = rows(q_ref, st)
                zeros = jnp.zeros_like(q2)
                qq = jnp.concatenate([jnp.where(head0, q2, zeros), jnp.where(head0, zeros, q2)], axis=0)
                k2 = jnp.concatenate([kb[i - 1] if i else k_prev, kb[i]], axis=0)
                v2 = jnp.concatenate([vb[i - 1] if i else v_prev, vb[i]], axis=0)
                units.append((st, pen if i == 0 else None, qq, k2, v2))
            return units

        def softmax_parts(s):
            m = jnp.max(s, axis=-1, keepdims=True)
            e = jnp.exp(s - m)
            return m, e, jnp.sum(e, axis=-1, keepdims=True)

        def group(g, _, br=br, dil=dil, runs=runs):
            units = [u for r in range(runs) for u in load_run(g * runs + r)]
            scores, parts = {}, {}
            for j in range(DIL_UNROLL + 2):
                if j < DIL_UNROLL:
                    _, pen, qq, k2, _ = units[j]
                    scores[j] = _dot_nt(qq, k2) + bias_scr[br]
                    if pen is not None:
                        scores[j] = scores[j] + pen
                if 0 <= j - 1 < DIL_UNROLL:
                    parts[j - 1] = softmax_parts(scores.pop(j - 1))
                if 0 <= j - 2 < DIL_UNROLL:
                    st, _, _, _, v2 = units[j - 2]
                    m, e, l = parts.pop(j - 2)
                    pv = _dot(e.astype(BF16), v2)
                    idx = pl.ds(st, qb) if dil == 1 else pl.ds(st, qb, stride=dil)
                    num_scr[br, idx, :] = jnp.where(head0, pv[:qb], pv[qb:])
                    m_scr[br, idx, :] = jnp.where(head0, m[:qb], m[qb:])
                    l_scr[br, idx, :] = jnp.where(head0, l[:qb], l[qb:])
            return 0

        lax.fori_loop(0, n_units // DIL_UNROLL, group, 0)

    rows_out = 256

    def finish(c, _):
        idx = pl.ds(pl.multiple_of(c * rows_out, rows_out), rows_out)
        m_all = [m_scr[br, idx, :] for br in range(len(DIL_CONFIGS))]
        m_max = functools.reduce(jnp.maximum, m_all)
        num = den = None
        for br, m_br in enumerate(m_all):
            wt = jnp.exp(m_br - m_max)
            n_br = wt * num_scr[br, idx, :]
            d_br = wt * l_scr[br, idx, :]
            num = n_br if num is None else num + n_br
            den = d_br if den is None else den + d_br
        o_ref[idx, :] = (num / den).astype(o_ref.dtype)
        return 0

    lax.fori_loop(0, seq // rows_out, finish, 0)


def _dil_attention(qkv, rel_bias):
    b, t, w3 = qkv.shape
    w = w3 // 3
    pairs = w // LANES
    assert t % (DIL_BLOCK * max(d for _, d in DIL_CONFIGS)) == 0 and t % 256 == 0
    bucket, valid = _dil_tables()
    nbr = len(DIL_CONFIGS)
    seq_spec = lambda off: pl.BlockSpec((None, t, LANES), lambda p, bi: (bi, 0, off + p))
    return pl.pallas_call(
        functools.partial(_dil_kernel, seq=t),
        grid=(pairs, b),
        in_specs=[pl.BlockSpec(memory_space=pltpu.SMEM),
                  _resident((nbr, DIL_BLOCK, 2 * DIL_BLOCK)),
                  _resident((nbr, DIL_BLOCK, 2 * DIL_BLOCK)),
                  seq_spec(0), seq_spec(pairs), seq_spec(2 * pairs)],
        out_specs=pl.BlockSpec((None, t, LANES), lambda p, bi: (bi, 0, p)),
        out_shape=jax.ShapeDtypeStruct((b, t, w), BF16),
        scratch_shapes=[pltpu.VMEM((nbr, 2 * DIL_BLOCK, 2 * DIL_BLOCK), F32),
                        pltpu.VMEM((nbr, t, LANES), F32), pltpu.VMEM((nbr, t, LANES), F32),
                        pltpu.VMEM((nbr, t, LANES), F32)],
        compiler_params=_params(("arbitrary", "arbitrary")),
        name="dil_attn",
    )(rel_bias, jnp.asarray(bucket), jnp.asarray(valid), qkv, qkv, qkv)


def _mlstm_proj_kernel(x_ref, g_ref, w_ref, wg_ref, cw_ref, bg_ref, q_ref, kt_ref, v_ref, o_ref,
                       gate_ref, *hist_scrs, tiles_per_seq, n_chunk, scale):
    i = pl.program_id(0)
    tm = x_ref.shape[0]
    width = q_ref.shape[1]
    xn = _rms(x_ref[...], g_ref[...]).astype(BF16)

    def even_rows(first, n):
        return pl.ds(2 * first, n, stride=2)

    slabs = n_chunk // LANES
    n_conv = 2 * width // n_chunk
    assert len(hist_scrs) == n_conv

    @pl.when(i % tiles_per_seq == 0)
    def _():
        for hist in hist_scrs:
            hist[:, 0:2 * CONV_PAD, :] = jnp.zeros((slabs, 2 * CONV_PAD, LANES), F32)

    def project(c):
        pre = _dot(xn, w_ref[:, c * n_chunk:(c + 1) * n_chunk])
        for s in range(slabs):
            hist_scrs[c][s, even_rows(CONV_PAD, tm), :] = pre[:, s * LANES:(s + 1) * LANES]

    def conv(c):
        sl = slice(c * n_chunk, (c + 1) * n_chunk)
        hist = hist_scrs[c]
        parts = []
        for s in range(slabs):
            y = None
            for tap in range(CONV_WIDTH):
                off = CONV_PAD - (CONV_WIDTH - 1) + tap
                w_tap = cw_ref[tap:tap + 1, (c * slabs + s) * LANES:(c * slabs + s + 1) * LANES]
                term = w_tap * hist[s, even_rows(off, tm), :]
                y = term if y is None else y + term
            parts.append(y)
            hist[s, even_rows(0, CONV_PAD), :] = hist[s, even_rows(tm, CONV_PAD), :]
        y = jnp.concatenate(parts, axis=1)
        y = y * jax.nn.sigmoid(y)
        if c * n_chunk < width:
            q_ref[:, sl] = (y * scale).astype(BF16)
        else:
            kt_ref[c * n_chunk - width:(c + 1) * n_chunk - width, :] = y.T.astype(BF16)

    def plain(c):
        sl = slice((c // 2) * n_chunk, (c // 2 + 1) * n_chunk)
        if c % 2 == 0:
            v_ref[:, sl] = _dot(xn, w_ref[:, 2 * width + sl.start:2 * width + sl.stop]).astype(BF16)
        else:
            o_ref[:, sl] = _dot(xn, w_ref[:, 3 * width + sl.start:3 * width + sl.stop])

    n_plain = 2 * width // n_chunk
    for c in range(n_conv + 1):
        if c < n_conv:
            project(c)
        if c >= 1:
            conv(c - 1)
            if c - 1 < n_plain:
                plain(c - 1)
    for c in range(n_conv, n_plain):
        plain(c)
    gate_ref[...] = _dot(xn, wg_ref[...]) + bg_ref[...]


def _mlstm_proj(x, g, w_in, w_gates, conv_w, b_gates, *, seq, tm=512, n_chunk=512):
    m, d = x.shape
    width = conv_w.shape[1] // 2
    assert m % tm == 0 and seq % tm == 0 and width % n_chunk == 0 and w_in.shape[1] >= 4 * width
    row = lambda wd: pl.BlockSpec((tm, wd), lambda i: (i, 0))
    scale = 1.0 / math.sqrt(width // N_HEADS_MLSTM)
    return pl.pallas_call(
        functools.partial(_mlstm_proj_kernel, tiles_per_seq=seq // tm, n_chunk=n_chunk, scale=scale),
        grid=(m // tm,),
        in_specs=[row(d), _resident((1, d)), _resident(w_in.shape), _resident((d, 2 * LANES)),
                  _resident(conv_w.shape), _resident((1, 2 * LANES))],
        out_specs=[row(width), pl.BlockSpec((width, tm), lambda i: (0, i)), row(width), row(width),
                   row(2 * LANES)],
        out_shape=[jax.ShapeDtypeStruct((m, width), BF16), jax.ShapeDtypeStruct((width, m), BF16),
                   jax.ShapeDtypeStruct((m, width), BF16), jax.ShapeDtypeStruct((m, width), F32),
                   jax.ShapeDtypeStruct((m, 2 * LANES), F32)],
        scratch_shapes=[pltpu.VMEM((n_chunk // LANES, 2 * (tm + CONV_PAD), LANES), F32)
                        for _ in range(2 * width // n_chunk)],
        compiler_params=_params(("arbitrary",)),
        name="mlstm_proj",
    )(x, g.reshape(1, d), w_in, w_gates, conv_w, b_gates)


def _split3(x):
    hi = x.astype(BF16)
    r = x - hi.astype(F32)
    mid = r.astype(BF16)
    return hi, mid, (r - mid.astype(F32)).astype(BF16)


def _gate_prep_kernel(g_ref, cols_ref, rows_ref, *, heads, cl):
    tm = g_ref.shape[0]
    row = lax.broadcasted_iota(jnp.int32, (cl, cl), 0)
    col = lax.broadcasted_iota(jnp.int32, (cl, cl), 1)
    incl = jnp.where(row >= col, 1.0, 0.0).astype(BF16)
    lane = lax.broadcasted_iota(jnp.int32, (1, cl), 1)
    pad = jnp.zeros((cl - heads, cl), F32)
    spans = [slice(c * cl, (c + 1) * cl) for c in range(tm // cl)]
    bs = []
    for r in spans:
        b = None
        for part in _split3(jax.nn.log_sigmoid(g_ref[r, LANES:])):
            term = _dot(incl, part)
            b = term if b is None else b + term
        bs.append(b)
    us = [g_ref[r, :LANES] - b for r, b in zip(spans, bs)]
    u_rows = jnp.concatenate([u.T[0:heads, :] for u in us], axis=0)
    b_rows = jnp.concatenate([b.T[0:heads, :] for b in bs], axis=0)
    cmax = u_rows
    shift = 1
    while shift < cl:
        cmax = jnp.where(lane >= shift, jnp.maximum(cmax, pltpu.roll(cmax, shift, axis=1)), cmax)
        shift *= 2
    u_max = jnp.broadcast_to(jnp.max(u_rows, axis=1, keepdims=True), u_rows.shape)
    b_last = jnp.broadcast_to(jnp.min(b_rows, axis=1, keepdims=True), b_rows.shape)
    for c, (r, u, b) in enumerate(zip(spans, us, bs)):
        hs = slice(c * heads, (c + 1) * heads)
        cols_ref[r, :] = jnp.concatenate([u, b, jnp.concatenate([cmax[hs], pad], axis=0).T], axis=1)
        rows_ref[:, r] = jnp.concatenate([u_rows[hs], u_max[hs], b_last[hs]], axis=0)


def _gate_prep(gates, *, heads, cl, tm=4096):
    m = gates.shape[0]
    tm = min(tm, m)
    assert m % tm == 0 and tm % cl == 0 and cl == LANES and heads == 8
    return pl.pallas_call(
        functools.partial(_gate_prep_kernel, heads=heads, cl=cl),
        grid=(m // tm,),
        in_specs=[pl.BlockSpec((tm, 2 * LANES), lambda i: (i, 0))],
        out_specs=[pl.BlockSpec((tm, 3 * LANES), lambda i: (i, 0)),
                   pl.BlockSpec((3 * heads, tm), lambda i: (0, i))],
        out_shape=[jax.ShapeDtypeStruct((m, 3 * LANES), F32),
                   jax.ShapeDtypeStruct((3 * heads, m), F32)],
        compiler_params=_params(("parallel",)),
        name="gate_prep",
    )(gates)


def _mlstm_kernel(q_ref, kt_ref, v_ref, o_ref, gcol_ref, grow_ref, hg_ref, out_ref,
                  s_scr, m_scr, ml_scr, *, heads):
    cl = MLSTM_CHUNK
    n_chunks = q_ref.shape[0] // cl
    dh = q_ref.shape[1] // heads

    @pl.when(pl.program_id(1) == 0)
    def _():
        s_scr[...] = jnp.zeros_like(s_scr)
        m_scr[...] = jnp.zeros_like(m_scr)
        ml_scr[...] = jnp.zeros_like(ml_scr)

    row = lax.broadcasted_iota(jnp.int32, (cl, cl), 0)
    col = lax.broadcasted_iota(jnp.int32, (cl, cl), 1)
    lower = row >= col
    ones = jnp.ones((cl, dh), BF16)

    m_rows = m_scr[...]
    m_lane = ml_scr[0:1, :]
    tables = []
    for c in range(n_chunks):
        span = slice(c * cl, (c + 1) * cl)
        u_cols, b_cols, cmax_cols = (gcol_ref[span, t * LANES:(t + 1) * LANES] for t in range(3))
        u_rows, u_max_rows, b_last_rows = (grow_ref[t * heads:(t + 1) * heads, span] for t in range(3))
        mm_last_rows = jnp.maximum(m_rows, u_max_rows)
        mm_cols = jnp.maximum(cmax_cols, m_lane)
        tables.append(dict(
            span=span, u_rows=u_rows, m_prev_rows=m_rows, mm_cols=mm_cols,
            decay_rows=jnp.exp(m_rows - mm_last_rows),
            ws_rows=jnp.exp(u_rows - mm_last_rows),
            floor_cols=jnp.exp(-(b_cols + mm_cols))))
        m_rows = b_last_rows + mm_last_rows
        m_lane = b_cols[cl - 1:cl, :] + mm_cols[cl - 1:cl, :]
    m_scr[...] = m_rows
    ml_scr[...] = jnp.broadcast_to(m_lane, ml_scr.shape)

    def lanes_of(mat, c):
        return jnp.broadcast_to(mat[:, c:c + 1], (cl, LANES))

    def stage_a(tab, h):
        sl = slice(h * dh, (h + 1) * dh)
        q, kt = q_ref[tab["span"], sl], kt_ref[sl, tab["span"]]
        v1 = jnp.concatenate([v_ref[tab["span"], sl], ones], axis=1)
        s_prev = s_scr[h]
        qk = _dot(q, kt)
        qs = _dot(q, s_prev.astype(BF16))
        kw = (kt.astype(F32) * tab["ws_rows"][h:h + 1, :]).astype(BF16)
        decay = jnp.broadcast_to(tab["decay_rows"][h:h + 1, :], (dh, LANES))
        s_scr[h] = jnp.concatenate([decay, decay], axis=1) * s_prev + _dot(kw, v1)
        return tab, h, sl, v1, qk, qs

    def stage_b(tab, h, sl, v1, qk, qs):
        mm = lanes_of(tab["mm_cols"], h)
        weight = jnp.where(lower, jnp.exp(tab["u_rows"][h:h + 1, :] - mm), 0.0)
        w_inter = jnp.exp(tab["m_prev_rows"][h:h + 1, :] - mm)
        return tab, sl, v1, qs, (qk * weight).astype(BF16), w_inter, lanes_of(tab["floor_cols"], h)

    def stage_c(tab, sl, v1, qs, p, w_inter, floor):
        pv = _dot(p, v1)
        both = pv + jnp.concatenate([w_inter, w_inter], axis=1) * qs
        hid = both[:, :dh] / jnp.maximum(jnp.abs(both[:, dh:]), floor)
        hid = hid * lax.rsqrt(jnp.mean(hid * hid, axis=-1, keepdims=True) + EPS) * hg_ref[:, sl]
        out_ref[tab["span"], sl] = (hid * jax.nn.sigmoid(o_ref[tab["span"], sl])).astype(out_ref.dtype)

    jobs = [(tab, h) for tab in tables for h in range(heads)]
    after_a, after_b = {}, {}
    for j in range(len(jobs) + 2):
        if j < len(jobs):
            after_a[j] = stage_a(*jobs[j])
        if 0 <= j - 1 < len(jobs):
            after_b[j - 1] = stage_b(*after_a.pop(j - 1))
        if 0 <= j - 2 < len(jobs):
            stage_c(*after_b.pop(j - 2))


def _mlstm(q, kt, v, o, gates, head_g, *, heads=N_HEADS_MLSTM, chunks=4):
    b, t, w = q.shape
    cl = MLSTM_CHUNK
    dh = w // heads
    rows = chunks * cl
    nc = t // rows
    assert t % rows == 0 and dh == LANES and cl == LANES
    gcols, grows = _gate_prep(gates, heads=heads, cl=cl)
    blk = lambda wd: pl.BlockSpec((None, rows, wd), lambda bi, c: (bi, c, 0))
    return pl.pallas_call(
        functools.partial(_mlstm_kernel, heads=heads),
        grid=(b, nc),
        in_specs=[blk(w), pl.BlockSpec((w, rows), lambda bi, c: (0, bi * nc + c)), blk(w), blk(w),
                  pl.BlockSpec((rows, 3 * LANES), lambda bi, c: (bi * nc + c, 0)),
                  pl.BlockSpec((3 * heads, rows), lambda bi, c: (0, bi * nc + c)),
                  _resident((1, w))],
        out_specs=blk(w),
        out_shape=jax.ShapeDtypeStruct((b, t, w), BF16),
        scratch_shapes=[pltpu.VMEM((heads, dh, 2 * dh), F32), pltpu.VMEM((heads, LANES), F32),
                        pltpu.VMEM((heads, LANES), F32)],
        compiler_params=_params(("parallel", "arbitrary")),
        name="mlstm",
    )(q, kt, v, o, gcols, grows, head_g.reshape(1, w))


def kernel(x, norm_g, ffn_w_gate, ffn_w_up, ffn_w_down, attn_w_in, attn_w_out, rel_bias,
           mlstm_w_in, mlstm_b_gates, mlstm_conv_w, mlstm_head_g, mlstm_w_out):
    bsz, t, d = x.shape
    depth = norm_g.shape[0]
    h = x.reshape(bsz * t, d)
    bf = lambda a: a.astype(BF16)

    w_gate, w_up, w_down = bf(ffn_w_gate), bf(ffn_w_up), bf(ffn_w_down)

    def ffn(h, layer, half, mix=None):
        g = norm_g[layer]
        return _ffn(h, g[2 * half * 2], g[2 * half * 2 + 1], w_gate, w_up, w_down, (layer, half), mix)

    for layer in range(depth):
        g = norm_g[layer]
        j = layer // 2
        h = ffn(h, layer, 0)
        if layer % 2 == 0:
            sb, dil = _attn_proj(h, g[2], bf(attn_w_in[j]))
            out_sb = _sb_attention(sb.reshape(bsz, t, -1))
            out_dil = _dil_attention(dil.reshape(bsz, t, -1), rel_bias.astype(F32))
            mix = ([out_sb.reshape(bsz * t, -1), out_dil.reshape(bsz * t, -1)], bf(attn_w_out[j]), g[3])
        else:
            width = mlstm_w_out.shape[1]
            n_head = mlstm_b_gates.shape[1] // 2
            tiles = lambda a: jnp.concatenate(
                [jnp.pad(part, ((0, 0), (0, LANES - n_head)))
                 for part in (a[:, :n_head], a[:, n_head:])], axis=1)
            w_in = bf(mlstm_w_in[j])
            w_gates = tiles(w_in[:, 4 * width:])
            b_gates = tiles(mlstm_b_gates[j].astype(F32).reshape(1, -1))
            q, kt, v, o, gates = _mlstm_proj(h, g[2], w_in, w_gates, mlstm_conv_w[j].astype(F32),
                                             b_gates, seq=t)
            r3 = lambda a: a.reshape(bsz, t, -1)
            hid = _mlstm(r3(q), kt, r3(v), r3(o), gates, mlstm_head_g[j].astype(F32))
            mix = ([hid.reshape(bsz * t, width)], bf(mlstm_w_out[j]), g[3])
        h = ffn(h, layer, 1, mix)
    return h.reshape(bsz, t, d)
```

```python
import functools
import math

import numpy as np
import jax
import jax.numpy as jnp
from jax import lax
from jax.experimental import pallas as pl
from jax.experimental.pallas import tpu as pltpu

EPS = 1e-6
HEAD_DIM_ATTN = 64
DIL_CONFIGS = ((128, 1), (512, 4), (2048, 16))
DIL_BLOCK = 128
DIL_UNROLL = 32
NUM_BUCKETS = 32
MAX_DISTANCE = 2048
N_HEADS_MLSTM = 8
MLSTM_CHUNK = 128
CONV_WIDTH = 4
LANES = 128
CONV_PAD = 8
MASKED = -1e30
V7X_VMEM_BYTES = 64 * 1024 * 1024
VMEM_LIMIT = V7X_VMEM_BYTES * 7 // 8

F32 = jnp.float32
BF16 = jnp.bfloat16


def _params(sem, vmem=VMEM_LIMIT):
    return pltpu.CompilerParams(dimension_semantics=sem, vmem_limit_bytes=vmem)


def _resident(shape):
    zeros = (0,) * len(shape)
    return pl.BlockSpec(shape, lambda *_: zeros, pipeline_mode=pl.Buffered(1))


def _rms(x, g):
    return x * lax.rsqrt(jnp.mean(x * x, axis=-1, keepdims=True) + EPS) * g


def _dot(a, b):
    return jnp.dot(a, b, preferred_element_type=F32)


def _dot_nt(a, b):
    return lax.dot_general(a, b, (((1,), (1,)), ((), ())), preferred_element_type=F32)


def _ffn_kernel(*refs, ff_chunk, n_mix, groups):
    mix_refs, refs = refs[:n_mix], refs[n_mix:]
    if n_mix:
        (wmix_ref, gmix_ref), refs = refs[:2], refs[2:]
    x_ref, gin_ref, gout_ref, wg_ref, wu_ref, wd_ref, o_ref = refs
    d_ff = wg_ref.shape[1]
    rows = x_ref.shape[0] // groups
    spans = [slice(r * rows, (r + 1) * rows) for r in range(groups)]

    xs = []
    for rs in spans:
        x = x_ref[rs, :]
        if n_mix:
            mixed = jnp.concatenate([r[rs, :] for r in mix_refs], axis=-1)
            x = x + _rms(_dot(mixed, wmix_ref[...]), gmix_ref[...])
        xs.append(x)
    xns = [_rms(x, gin_ref[...]).astype(BF16) for x in xs]
    accs = []
    for xn in xns:
        acc = None
        for c in range(d_ff // ff_chunk):
            sl = slice(c * ff_chunk, (c + 1) * ff_chunk)
            gate = _dot(xn, wg_ref[:, sl])
            up = _dot(xn, wu_ref[:, sl])
            h = (gate * jax.nn.sigmoid(gate) * up).astype(BF16)
            part = _dot(h, wd_ref[sl, :])
            acc = part if acc is None else acc + part
        accs.append(acc)
    for rs, x, acc in zip(spans, xs, accs):
        o_ref[rs, :] = x + 0.5 * _rms(acc, gout_ref[...])


def _ffn(x, g_in, g_out, wg, wu, wd, which=(), mix=None, *, tm=1024, groups=2, ff_chunk=256):
    m, d = x.shape
    d_ff = wg.shape[-1]
    assert m % tm == 0 and tm % groups == 0 and d_ff % ff_chunk == 0 and len(which) == wg.ndim - 2
    row = lambda width: pl.BlockSpec((tm, width), lambda i: (i, 0))
    lead = (None,) * len(which)
    weight = lambda shape: pl.BlockSpec(lead + shape, lambda i: tuple(which) + (0, 0),
                                        pipeline_mode=pl.Buffered(1))
    parts, w_mix, g_mix = mix if mix else ((), None, None)
    mix_specs = [row(a.shape[1]) for a in parts] + ([_resident(w_mix.shape), _resident((1, d))] if mix else [])
    mix_args = list(parts) + ([w_mix, g_mix.reshape(1, d)] if mix else [])
    return pl.pallas_call(
        functools.partial(_ffn_kernel, ff_chunk=ff_chunk, n_mix=len(parts), groups=groups),
        grid=(m // tm,),
        in_specs=mix_specs + [row(d), _resident((1, d)), _resident((1, d)),
                              weight((d, d_ff)), weight((d, d_ff)), weight((d_ff, d))],
        out_specs=row(d),
        out_shape=jax.ShapeDtypeStruct((m, d), F32),
        compiler_params=_params(("parallel",)),
        name="ffn",
    )(*mix_args, x, g_in.reshape(1, d), g_out.reshape(1, d), wg, wu, wd)


def _attn_proj_kernel(x_ref, g_ref, w_ref, sb_ref, dil_ref, *, n_chunk, scale, groups):
    w_sb = sb_ref.shape[1]
    w_q = w_sb // 3
    rows = x_ref.shape[0] // groups
    spans = [slice(r * rows, (r + 1) * rows) for r in range(groups)]
    xns = [_rms(x_ref[rs, :], g_ref[...]).astype(BF16) for rs in spans]
    for rs, xn in zip(spans, xns):
        for c in range(w_sb // n_chunk):
            sl = slice(c * n_chunk, (c + 1) * n_chunk)
            y = _dot(xn, w_ref[:, sl])
            if (c + 1) * n_chunk <= w_q:
                y = y * scale
            sb_ref[rs, sl] = y.astype(BF16)
        for c in range(dil_ref.shape[1] // n_chunk):
            sl = slice(c * n_chunk, (c + 1) * n_chunk)
            y = _dot(xn, w_ref[:, w_sb + c * n_chunk: w_sb + (c + 1) * n_chunk])
            if (c + 1) * n_chunk <= w_q:
                y = y * scale
            dil_ref[rs, sl] = y


def _attn_proj(x, g, w, *, tm=1024, n_chunk=512, groups=2):
    m, d = x.shape
    n = w.shape[1]
    half = n // 2
    assert m % tm == 0 and (half // 3) % n_chunk == 0
    scale = 1.0 / math.sqrt(HEAD_DIM_ATTN)
    return pl.pallas_call(
        functools.partial(_attn_proj_kernel, n_chunk=n_chunk, scale=scale, groups=groups),
        grid=(m // tm,),
        in_specs=[pl.BlockSpec((tm, d), lambda i: (i, 0)), _resident((1, d)), _resident((d, n))],
        out_specs=[pl.BlockSpec((tm, half), lambda i: (i, 0)),
                   pl.BlockSpec((tm, half), lambda i: (i, 0))],
        out_shape=[jax.ShapeDtypeStruct((m, half), BF16), jax.ShapeDtypeStruct((m, half), F32)],
        compiler_params=_params(("parallel",)),
        name="attn_proj",
    )(x, g.reshape(1, d), w)


LOG2E = math.log2(math.e)
SB_DEAD_LOG2 = -160.0


def _sb_kernel(q_ref, k_ref, v_ref, tri_ref, o_ref, *, blk, q_blocks):
    first_q = pl.program_id(2) * q_blocks
    lane = lax.broadcasted_iota(jnp.int32, (1, LANES), 1)
    row = lax.broadcasted_iota(jnp.int32, (blk, blk), 0)
    col = lax.broadcasted_iota(jnp.int32, (blk, blk), 1)
    causal = col < row
    causal2 = jnp.concatenate([causal, causal], axis=0)
    tri = tri_ref[...]
    head0 = lane < HEAD_DIM_ATTN

    def stacked(q2):
        zeros = jnp.zeros_like(q2)
        return jnp.concatenate([jnp.where(head0, q2, zeros), jnp.where(head0, zeros, q2)], axis=0)

    def pairs(jobs):
        chains = []
        for j, (qq, kb, diag, _) in enumerate(jobs):
            chains.append((j, pl.multiple_of(kb * blk, blk), causal2 if diag else None))
            chains.append((j, pl.multiple_of(jnp.maximum(kb - 1, 0) * blk, blk), kb >= 1))
        states = [job[3] for job in jobs]
        zs, mids = {}, {}
        for n in range(len(chains) + 2):
            if n < len(chains):
                j, ks, _ = chains[n]
                zs[n] = _dot_nt(jobs[j][0], k_ref[pl.ds(ks, blk), :]) * LOG2E
            if 0 <= n - 1 < len(chains):
                z, mask = zs.pop(n - 1), chains[n - 1][2]
                neg = -z
                log_keep = jnp.minimum(neg, 0.0) - jnp.log2(1.0 + jnp.exp2(jnp.minimum(z, neg)))
                log_beta = z + log_keep
                if mask is not None:
                    log_keep = jnp.where(mask, log_keep, 0.0)
                    log_beta = jnp.where(mask, log_beta, MASKED)
                hi = log_keep.astype(BF16)
                lo = (log_keep - hi.astype(F32)).astype(BF16)
                later = _dot(hi, tri) + _dot(lo, tri)
                mids[n - 1] = (log_beta, later, jnp.sum(log_keep, axis=-1, keepdims=True))
            if 0 <= n - 2 < len(chains):
                j, ks, _ = chains[n - 2]
                log_beta, later, total = mids.pop(n - 2)
                carry, acc = states[j]
                p = jnp.exp2(log_beta + later + carry)
                states[j] = (carry + total, acc + _dot(p.astype(BF16), v_ref[pl.ds(ks, blk), :]))
        return states

    def alive(state):
        return (jnp.max(state[0]) > SB_DEAD_LOG2).astype(jnp.int32)

    zero = (jnp.zeros((2 * blk, 1), F32), jnp.zeros((2 * blk, LANES), F32))
    qqs = [stacked(q_ref[g * blk:(g + 1) * blk, :]) for g in range(q_blocks)]
    states = pairs([(qq, first_q + g, True, zero) for g, qq in enumerate(qqs)])

    def store(g, state):
        o_ref[g * blk:(g + 1) * blk, :] = jnp.where(head0, state[1][:blk], state[1][blk:]).astype(o_ref.dtype)

    live = [alive(state) for state in states]
    for g, state in enumerate(states):
        store(g, state)

    for g, (qq, state) in enumerate(zip(qqs, states)):
        i = first_q + g
        n_pairs = lax.shift_right_logical(i, 1)

        @pl.when((live[g] > 0) & (n_pairs > 0))
        def _(g=g, qq=qq, state=state, i=i, n_pairs=n_pairs):
            def cond(loop):
                t, alive_now, _ = loop
                return (t < n_pairs) & (alive_now > 0)

            def body(loop):
                t, _, state = loop
                state, = pairs([(qq, i - 2 - 2 * t, False, state)])
                return t + 1, alive(state), state

            _, _, final = lax.while_loop(cond, body, (jnp.int32(0), live[g], state))
            store(g, final)


def _sb_attention(qkv, *, blk=256, q_blocks=8):
    b, t, w3 = qkv.shape
    w = w3 // 3
    pairs = w // LANES
    blk = min(blk, t)
    rows = blk * q_blocks
    assert t % rows == 0
    tri = jnp.asarray(np.tril(np.ones((blk, blk), np.float32), -1), BF16)
    return pl.pallas_call(
        functools.partial(_sb_kernel, blk=blk, q_blocks=q_blocks),
        grid=(b, pairs, t // rows),
        in_specs=[pl.BlockSpec((None, rows, LANES), lambda bi, p, i: (bi, i, p)),
                  pl.BlockSpec((None, t, LANES), lambda bi, p, i: (bi, 0, pairs + p)),
                  pl.BlockSpec((None, t, LANES), lambda bi, p, i: (bi, 0, 2 * pairs + p)),
                  _resident((blk, blk))],
        out_specs=pl.BlockSpec((None, rows, LANES), lambda bi, p, i: (bi, i, p)),
        out_shape=jax.ShapeDtypeStruct((b, t, w), BF16),
        compiler_params=_params(("parallel", "parallel", "arbitrary")),
        name="sb_attn",
    )(qkv, qkv, qkv, tri)


def _t5_bucket_np(dist):
    max_exact = NUM_BUCKETS // 2
    d = np.maximum(dist, 1).astype(np.float32)
    log_b = max_exact + (np.log(d / np.float32(max_exact)) / np.float32(math.log(MAX_DISTANCE / max_exact))
                         * np.float32(NUM_BUCKETS - max_exact)).astype(np.int32)
    log_b = np.minimum(log_b, NUM_BUCKETS - 1)
    return np.where(dist < max_exact, dist, log_b)


def _dil_tables():
    qi = np.arange(DIL_BLOCK)[:, None]
    ki = np.arange(2 * DIL_BLOCK)[None, :]
    dist = qi + DIL_BLOCK - ki
    buckets, valid = [], []
    for window, dil in DIL_CONFIGS:
        steps = window // dil
        buckets.append(_t5_bucket_np(np.maximum(dist, 0) * dil))
        valid.append((dist >= 0) & (dist <= steps))
    return np.stack(buckets).astype(np.int32), np.stack(valid).astype(np.int32)


def _dil_kernel(rb_ref, bucket_ref, valid_ref, q_ref, k_ref, v_ref, o_ref,
                bias_scr, num_scr, m_scr, l_scr, *, seq):
    p = pl.program_id(0)
    qb = DIL_BLOCK
    lane = lax.broadcasted_iota(jnp.int32, (1, LANES), 1)
    head0 = lane < HEAD_DIM_ATTN
    first_half = lax.broadcasted_iota(jnp.int32, (1, 2 * qb), 1) < qb

    @pl.when(pl.program_id(1) == 0)
    def _():
        for br in range(len(DIL_CONFIGS)):
            bucket = bucket_ref[br]
            valid = valid_ref[br] > 0
            for h in range(2):
                bias = jnp.zeros((qb, 2 * qb), F32)
                for b in range(NUM_BUCKETS):
                    bias = jnp.where(bucket == b, rb_ref[b, 2 * p + h], bias)
                bias_scr[br, h * qb:(h + 1) * qb, :] = jnp.where(valid, bias, MASKED)

    for br, (_, dil) in enumerate(DIL_CONFIGS):
        n_units = seq // qb

        n_blocks = seq // (qb * dil)
        run = min(DIL_UNROLL, n_blocks)
        runs = DIL_UNROLL // run
        assert DIL_UNROLL % run == 0 and n_blocks % run == 0 and (dil == 1 or n_blocks == run)

        def rows(ref, start, dil=dil):
            if dil == 1:
                return ref[pl.ds(start, qb), :].astype(BF16)
            return ref[pl.ds(start, qb, stride=dil), :].astype(BF16)

        def load_run(rho, dil=dil, run=run, whole=(n_blocks == run)):
            res, base = (rho, 0) if whole else (0, rho * run)
            starts = [(base + i) * (qb * dil) + res for i in range(run)]
            kb = [rows(k_ref, st) for st in starts]
            vb = [rows(v_ref, st) for st in starts]
            if whole:
                k_prev, v_prev = kb[0], vb[0]
                pen = jnp.where(first_half, MASKED, 0.0)
            else:
                st = jnp.maximum(base - 1, 0) * (qb * dil) + res
                k_prev, v_prev = rows(k_ref, st), rows(v_ref, st)
                pen = jnp.where(first_half, jnp.where(base == 0, MASKED, 0.0), 0.0)
            units = []
            for i, st in enumerate(starts):
                q2 = rows(q_ref, st)
                zeros = jnp.zeros_like(q2)
                qq = jnp.concatenate([jnp.where(head0, q2, zeros), jnp.where(head0, zeros, q2)], axis=0)
                k2 = jnp.concatenate([kb[i - 1] if i else k_prev, kb[i]], axis=0)
                v2 = jnp.concatenate([vb[i - 1] if i else v_prev, vb[i]], axis=0)
                units.append((st, pen if i == 0 else None, qq, k2, v2))
            return units

        def softmax_parts(s):
            m = jnp.max(s, axis=-1, keepdims=True)
            e = jnp.exp(s - m)
            return m, e, jnp.sum(e, axis=-1, keepdims=True)

        def group(g, _, br=br, dil=dil, runs=runs):
            units = [u for r in range(runs) for u in load_run(g * runs + r)]
            scores, parts = {}, {}
            for j in range(DIL_UNROLL + 2):
                if j < DIL_UNROLL:
                    _, pen, qq, k2, _ = units[j]
                    scores[j] = _dot_nt(qq, k2) + bias_scr[br]
                    if pen is not None:
                        scores[j] = scores[j] + pen
                if 0 <= j - 1 < DIL_UNROLL:
                    parts[j - 1] = softmax_parts(scores.pop(j - 1))
                if 0 <= j - 2 < DIL_UNROLL:
                    st, _, _, _, v2 = units[j - 2]
                    m, e, l = parts.pop(j - 2)
                    pv = _dot(e.astype(BF16), v2)
                    idx = pl.ds(st, qb) if dil == 1 else pl.ds(st, qb, stride=dil)
                    num_scr[br, idx, :] = jnp.where(head0, pv[:qb], pv[qb:])
                    m_scr[br, idx, :] = jnp.where(head0, m[:qb], m[qb:])
                    l_scr[br, idx, :] = jnp.where(head0, l[:qb], l[qb:])
            return 0

        lax.fori_loop(0, n_units // DIL_UNROLL, group, 0)

    rows_out = 256

    def finish(c, _):
        idx = pl.ds(pl.multiple_of(c * rows_out, rows_out), rows_out)
        m_all = [m_scr[br, idx, :] for br in range(len(DIL_CONFIGS))]
        m_max = functools.reduce(jnp.maximum, m_all)
        num = den = None
        for br, m_br in enumerate(m_all):
            wt = jnp.exp(m_br - m_max)
            n_br = wt * num_scr[br, idx, :]
            d_br = wt * l_scr[br, idx, :]
            num = n_br if num is None else num + n_br
            den = d_br if den is None else den + d_br
        o_ref[idx, :] = (num / den).astype(o_ref.dtype)
        return 0

    lax.fori_loop(0, seq // rows_out, finish, 0)


def _dil_attention(qkv, rel_bias):
    b, t, w3 = qkv.shape
    w = w3 // 3
    pairs = w // LANES
    assert t % (DIL_BLOCK * max(d for _, d in DIL_CONFIGS)) == 0 and t % 256 == 0
    bucket, valid = _dil_tables()
    nbr = len(DIL_CONFIGS)
    seq_spec = lambda off: pl.BlockSpec((None, t, LANES), lambda p, bi: (bi, 0, off + p))
    return pl.pallas_call(
        functools.partial(_dil_kernel, seq=t),
        grid=(pairs, b),
        in_specs=[pl.BlockSpec(memory_space=pltpu.SMEM),
                  _resident((nbr, DIL_BLOCK, 2 * DIL_BLOCK)),
                  _resident((nbr, DIL_BLOCK, 2 * DIL_BLOCK)),
                  seq_spec(0), seq_spec(pairs), seq_spec(2 * pairs)],
        out_specs=pl.BlockSpec((None, t, LANES), lambda p, bi: (bi, 0, p)),
        out_shape=jax.ShapeDtypeStruct((b, t, w), BF16),
        scratch_shapes=[pltpu.VMEM((nbr, 2 * DIL_BLOCK, 2 * DIL_BLOCK), F32),
                        pltpu.VMEM((nbr, t, LANES), F32), pltpu.VMEM((nbr, t, LANES), F32),
                        pltpu.VMEM((nbr, t, LANES), F32)],
        compiler_params=_params(("arbitrary", "arbitrary")),
        name="dil_attn",
    )(rel_bias, jnp.asarray(bucket), jnp.asarray(valid), qkv, qkv, qkv)


def _mlstm_proj_kernel(x_ref, g_ref, w_ref, wg_ref, cw_ref, bg_ref, q_ref, kt_ref, v_ref, o_ref,
                       gate_ref, *hist_scrs, tiles_per_seq, n_chunk, scale):
    i = pl.program_id(0)
    tm = x_ref.shape[0]
    width = q_ref.shape[1]
    xn = _rms(x_ref[...], g_ref[...]).astype(BF16)

    def even_rows(first, n):
        return pl.ds(2 * first, n, stride=2)

    slabs = n_chunk // LANES
    n_conv = 2 * width // n_chunk
    assert len(hist_scrs) == n_conv

    @pl.when(i % tiles_per_seq == 0)
    def _():
        for hist in hist_scrs:
            hist[:, 0:2 * CONV_PAD, :] = jnp.zeros((slabs, 2 * CONV_PAD, LANES), F32)

    def project(c):
        pre = _dot(xn, w_ref[:, c * n_chunk:(c + 1) * n_chunk])
        for s in range(slabs):
            hist_scrs[c][s, even_rows(CONV_PAD, tm), :] = pre[:, s * LANES:(s + 1) * LANES]

    def conv(c):
        sl = slice(c * n_chunk, (c + 1) * n_chunk)
        hist = hist_scrs[c]
        parts = []
        for s in range(slabs):
            y = None
            for tap in range(CONV_WIDTH):
                off = CONV_PAD - (CONV_WIDTH - 1) + tap
                w_tap = cw_ref[tap:tap + 1, (c * slabs + s) * LANES:(c * slabs + s + 1) * LANES]
                term = w_tap * hist[s, even_rows(off, tm), :]
                y = term if y is None else y + term
            parts.append(y)
            hist[s, even_rows(0, CONV_PAD), :] = hist[s, even_rows(tm, CONV_PAD), :]
        y = jnp.concatenate(parts, axis=1)
        y = y * jax.nn.sigmoid(y)
        if c * n_chunk < width:
            q_ref[:, sl] = (y * scale).astype(BF16)
        else:
            kt_ref[c * n_chunk - width:(c + 1) * n_chunk - width, :] = y.T.astype(BF16)

    def plain(c):
        sl = slice((c // 2) * n_chunk, (c // 2 + 1) * n_chunk)
        if c % 2 == 0:
            v_ref[:, sl] = _dot(xn, w_ref[:, 2 * width + sl.start:2 * width + sl.stop]).astype(BF16)
        else:
            o_ref[:, sl] = _dot(xn, w_ref[:, 3 * width + sl.start:3 * width + sl.stop]).astype(BF16)

    n_plain = 2 * width // n_chunk
    for c in range(n_conv + 1):
        if c < n_conv:
            project(c)
        if c >= 1:
            conv(c - 1)
            if c - 1 < n_plain:
                plain(c - 1)
    for c in range(n_conv, n_plain):
        plain(c)
    gate_ref[...] = _dot(xn, wg_ref[...]) + bg_ref[...]


def _mlstm_proj(x, g, w_in, w_gates, conv_w, b_gates, *, seq, tm=512, n_chunk=512):
    m, d = x.shape
    width = conv_w.shape[1] // 2
    assert m % tm == 0 and seq % tm == 0 and width % n_chunk == 0 and w_in.shape[1] >= 4 * width
    row = lambda wd: pl.BlockSpec((tm, wd), lambda i: (i, 0))
    scale = 1.0 / math.sqrt(width // N_HEADS_MLSTM)
    return pl.pallas_call(
        functools.partial(_mlstm_proj_kernel, tiles_per_seq=seq // tm, n_chunk=n_chunk, scale=scale),
        grid=(m // tm,),
        in_specs=[row(d), _resident((1, d)), _resident(w_in.shape), _resident((d, 2 * LANES)),
                  _resident(conv_w.shape), _resident((1, 2 * LANES))],
        out_specs=[row(width), pl.BlockSpec((width, tm), lambda i: (0, i)), row(width), row(width),
                   row(2 * LANES)],
        out_shape=[jax.ShapeDtypeStruct((m, width), BF16), jax.ShapeDtypeStruct((width, m), BF16),
                   jax.ShapeDtypeStruct((m, width), BF16), jax.ShapeDtypeStruct((m, width), BF16),
                   jax.ShapeDtypeStruct((m, 2 * LANES), F32)],
        scratch_shapes=[pltpu.VMEM((n_chunk // LANES, 2 * (tm + CONV_PAD), LANES), F32)
                        for _ in range(2 * width // n_chunk)],
        compiler_params=_params(("arbitrary",)),
        name="mlstm_proj",
    )(x, g.reshape(1, d), w_in, w_gates, conv_w, b_gates)


def _split3(x):
    hi = x.astype(BF16)
    r = x - hi.astype(F32)
    mid = r.astype(BF16)
    return hi, mid, (r - mid.astype(F32)).astype(BF16)


def _gate_prep_kernel(g_ref, cols_ref, rows_ref, *, heads, cl):
    tm = g_ref.shape[0]
    row = lax.broadcasted_iota(jnp.int32, (cl, cl), 0)
    col = lax.broadcasted_iota(jnp.int32, (cl, cl), 1)
    incl = jnp.where(row >= col, 1.0, 0.0).astype(BF16)
    lane = lax.broadcasted_iota(jnp.int32, (1, cl), 1)
    pad = jnp.zeros((cl - heads, cl), F32)
    spans = [slice(c * cl, (c + 1) * cl) for c in range(tm // cl)]
    bs = []
    for r in spans:
        b = None
        for part in _split3(jax.nn.log_sigmoid(g_ref[r, LANES:])):
            term = _dot(incl, part)
            b = term if b is None else b + term
        bs.append(b)
    us = [g_ref[r, :LANES] - b for r, b in zip(spans, bs)]
    u_rows = jnp.concatenate([u.T[0:heads, :] for u in us], axis=0)
    b_rows = jnp.concatenate([b.T[0:heads, :] for b in bs], axis=0)
    cmax = u_rows
    shift = 1
    while shift < cl:
        cmax = jnp.where(lane >= shift, jnp.maximum(cmax, pltpu.roll(cmax, shift, axis=1)), cmax)
        shift *= 2
    u_max = jnp.broadcast_to(jnp.max(u_rows, axis=1, keepdims=True), u_rows.shape)
    b_last = jnp.broadcast_to(jnp.min(b_rows, axis=1, keepdims=True), b_rows.shape)
    for c, (r, u, b) in enumerate(zip(spans, us, bs)):
        hs = slice(c * heads, (c + 1) * heads)
        cols_ref[r, :] = jnp.concatenate([u, b, jnp.concatenate([cmax[hs], pad], axis=0).T], axis=1)
        rows_ref[:, r] = jnp.concatenate([u_rows[hs], u_max[hs], b_last[hs]], axis=0)


def _gate_prep(gates, *, heads, cl, tm=4096):
    m = gates.shape[0]
    tm = min(tm, m)
    assert m % tm == 0 and tm % cl == 0 and cl == LANES and heads == 8
    return pl.pallas_call(
        functools.partial(_gate_prep_kernel, heads=heads, cl=cl),
        grid=(m // tm,),
        in_specs=[pl.BlockSpec((tm, 2 * LANES), lambda i: (i, 0))],
        out_specs=[pl.BlockSpec((tm, 3 * LANES), lambda i: (i, 0)),
                   pl.BlockSpec((3 * heads, tm), lambda i: (0, i))],
        out_shape=[jax.ShapeDtypeStruct((m, 3 * LANES), F32),
                   jax.ShapeDtypeStruct((3 * heads, m), F32)],
        compiler_params=_params(("parallel",)),
        name="gate_prep",
    )(gates)


def _mlstm_kernel(q_ref, kt_ref, v_ref, o_ref, gcol_ref, grow_ref, hg_ref, out_ref,
                  s_scr, m_scr, ml_scr, *, heads):
    cl = MLSTM_CHUNK
    n_chunks = q_ref.shape[0] // cl
    dh = q_ref.shape[1] // heads

    @pl.when(pl.program_id(1) == 0)
    def _():
        s_scr[...] = jnp.zeros_like(s_scr)
        m_scr[...] = jnp.zeros_like(m_scr)
        ml_scr[...] = jnp.zeros_like(ml_scr)

    row = lax.broadcasted_iota(jnp.int32, (cl, cl), 0)
    col = lax.broadcasted_iota(jnp.int32, (cl, cl), 1)
    lower = row >= col
    ones = jnp.ones((cl, dh), BF16)

    m_rows = m_scr[...]
    m_lane = ml_scr[0:1, :]
    tables = []
    for c in range(n_chunks):
        span = slice(c * cl, (c + 1) * cl)
        u_cols, b_cols, cmax_cols = (gcol_ref[span, t * LANES:(t + 1) * LANES] for t in range(3))
        u_rows, u_max_rows, b_last_rows = (grow_ref[t * heads:(t + 1) * heads, span] for t in range(3))
        mm_last_rows = jnp.maximum(m_rows, u_max_rows)
        mm_cols = jnp.maximum(cmax_cols, m_lane)
        tables.append(dict(
            span=span, u_rows=u_rows, m_prev_rows=m_rows, mm_cols=mm_cols,
            decay_rows=jnp.exp(m_rows - mm_last_rows),
            ws_rows=jnp.exp(u_rows - mm_last_rows),
            floor_cols=jnp.exp(-(b_cols + mm_cols))))
        m_rows = b_last_rows + mm_last_rows
        m_lane = b_cols[cl - 1:cl, :] + mm_cols[cl - 1:cl, :]
    m_scr[...] = m_rows
    ml_scr[...] = jnp.broadcast_to(m_lane, ml_scr.shape)

    def lanes_of(mat, c):
        return jnp.broadcast_to(mat[:, c:c + 1], (cl, LANES))

    def stage_a(tab, h):
        sl = slice(h * dh, (h + 1) * dh)
        q, kt = q_ref[tab["span"], sl], kt_ref[sl, tab["span"]]
        v1 = jnp.concatenate([v_ref[tab["span"], sl], ones], axis=1)
        s_prev = s_scr[h]
        qk = _dot(q, kt)
        qs = _dot(q, s_prev.astype(BF16))
        kw = (kt.astype(F32) * tab["ws_rows"][h:h + 1, :]).astype(BF16)
        decay = jnp.broadcast_to(tab["decay_rows"][h:h + 1, :], (dh, LANES))
        s_scr[h] = jnp.concatenate([decay, decay], axis=1) * s_prev + _dot(kw, v1)
        return tab, h, sl, v1, qk, qs

    def stage_b(tab, h, sl, v1, qk, qs):
        mm = lanes_of(tab["mm_cols"], h)
        weight = jnp.where(lower, jnp.exp(tab["u_rows"][h:h + 1, :] - mm), 0.0)
        w_inter = jnp.exp(tab["m_prev_rows"][h:h + 1, :] - mm)
        return tab, sl, v1, qs, (qk * weight).astype(BF16), w_inter, lanes_of(tab["floor_cols"], h)

    def stage_c(tab, sl, v1, qs, p, w_inter, floor):
        pv = _dot(p, v1)
        both = pv + jnp.concatenate([w_inter, w_inter], axis=1) * qs
        hid = both[:, :dh] / jnp.maximum(jnp.abs(both[:, dh:]), floor)
        hid = hid * lax.rsqrt(jnp.mean(hid * hid, axis=-1, keepdims=True) + EPS) * hg_ref[:, sl]
        gate = jax.nn.sigmoid(o_ref[tab["span"], sl].astype(F32))
        out_ref[tab["span"], sl] = (hid * gate).astype(out_ref.dtype)

    jobs = [(tab, h) for tab in tables for h in range(heads)]
    after_a, after_b = {}, {}
    for j in range(len(jobs) + 2):
        if j < len(jobs):
            after_a[j] = stage_a(*jobs[j])
        if 0 <= j - 1 < len(jobs):
            after_b[j - 1] = stage_b(*after_a.pop(j - 1))
        if 0 <= j - 2 < len(jobs):
            stage_c(*after_b.pop(j - 2))


def _mlstm(q, kt, v, o, gates, head_g, *, heads=N_HEADS_MLSTM, chunks=4):
    b, t, w = q.shape
    cl = MLSTM_CHUNK
    dh = w // heads
    rows = chunks * cl
    nc = t // rows
    assert t % rows == 0 and dh == LANES and cl == LANES
    gcols, grows = _gate_prep(gates, heads=heads, cl=cl)
    blk = lambda wd: pl.BlockSpec((None, rows, wd), lambda bi, c: (bi, c, 0))
    return pl.pallas_call(
        functools.partial(_mlstm_kernel, heads=heads),
        grid=(b, nc),
        in_specs=[blk(w), pl.BlockSpec((w, rows), lambda bi, c: (0, bi * nc + c)), blk(w), blk(w),
                  pl.BlockSpec((rows, 3 * LANES), lambda bi, c: (bi * nc + c, 0)),
                  pl.BlockSpec((3 * heads, rows), lambda bi, c: (0, bi * nc + c)),
                  _resident((1, w))],
        out_specs=blk(w),
        out_shape=jax.ShapeDtypeStruct((b, t, w), BF16),
        scratch_shapes=[pltpu.VMEM((heads, dh, 2 * dh), F32), pltpu.VMEM((heads, LANES), F32),
                        pltpu.VMEM((heads, LANES), F32)],
        compiler_params=_params(("parallel", "arbitrary")),
        name="mlstm",
    )(q, kt, v, o, gcols, grows, head_g.reshape(1, w))


def kernel(x, norm_g, ffn_w_gate, ffn_w_up, ffn_w_down, attn_w_in, attn_w_out, rel_bias,
           mlstm_w_in, mlstm_b_gates, mlstm_conv_w, mlstm_head_g, mlstm_w_out):
    bsz, t, d = x.shape
    depth = norm_g.shape[0]
    h = x.reshape(bsz * t, d)
    bf = lambda a: a.astype(BF16)

    w_gate, w_up, w_down = bf(ffn_w_gate), bf(ffn_w_up), bf(ffn_w_down)

    def ffn(h, layer, half, mix=None):
        g = norm_g[layer]
        return _ffn(h, g[2 * half * 2], g[2 * half * 2 + 1], w_gate, w_up, w_down, (layer, half), mix)

    for layer in range(depth):
        g = norm_g[layer]
        j = layer // 2
        h = ffn(h, layer, 0)
        if layer % 2 == 0:
            sb, dil = _attn_proj(h, g[2], bf(attn_w_in[j]))
            out_sb = _sb_attention(sb.reshape(bsz, t, -1))
            out_dil = _dil_attention(dil.reshape(bsz, t, -1), rel_bias.astype(F32))
            mix = ([out_sb.reshape(bsz * t, -1), out_dil.reshape(bsz * t, -1)], bf(attn_w_out[j]), g[3])
        else:
            width = mlstm_w_out.shape[1]
            n_head = mlstm_b_gates.shape[1] // 2
            tiles = lambda a: jnp.concatenate(
                [jnp.pad(part, ((0, 0), (0, LANES - n_head)))
                 for part in (a[:, :n_head], a[:, n_head:])], axis=1)
            w_in = bf(mlstm_w_in[j])
            w_gates = tiles(w_in[:, 4 * width:])
            b_gates = tiles(mlstm_b_gates[j].astype(F32).reshape(1, -1))
            q, kt, v, o, gates = _mlstm_proj(h, g[2], w_in, w_gates, mlstm_conv_w[j].astype(F32),
                                             b_gates, seq=t)
            r3 = lambda a: a.reshape(bsz, t, -1)
            hid = _mlstm(r3(q), kt, r3(v), r3(o), gates, mlstm_head_g[j].astype(F32))
            mix = ([hid.reshape(bsz * t, width)], bf(mlstm_w_out[j]), g[3])
        h = ffn(h, layer, 1, mix)
    return h.reshape(bsz, t, d)
```

```python
import functools
import math

import numpy as np
import jax
import jax.numpy as jnp
from jax import lax
from jax.experimental import pallas as pl
from jax.experimental.pallas import tpu as pltpu

EPS = 1e-6
HEAD_DIM_ATTN = 64
DIL_CONFIGS = ((128, 1), (512, 4), (2048, 16))
DIL_BLOCK = 128
DIL_UNROLL = 32
NUM_BUCKETS = 32
MAX_DISTANCE = 2048
N_HEADS_MLSTM = 8
MLSTM_CHUNK = 128
CONV_WIDTH = 4
LANES = 128
CONV_PAD = 8
MASKED = -1e30
V7X_VMEM_BYTES = 64 * 1024 * 1024
VMEM_LIMIT = V7X_VMEM_BYTES * 7 // 8

F32 = jnp.float32
BF16 = jnp.bfloat16


def _params(sem, vmem=VMEM_LIMIT):
    return pltpu.CompilerParams(dimension_semantics=sem, vmem_limit_bytes=vmem)


def _resident(shape):
    zeros = (0,) * len(shape)
    return pl.BlockSpec(shape, lambda *_: zeros, pipeline_mode=pl.Buffered(1))


def _rms(x, g):
    return x * lax.rsqrt(jnp.mean(x * x, axis=-1, keepdims=True) + EPS) * g


def _dot(a, b):
    return jnp.dot(a, b, preferred_element_type=F32)


def _dot_nt(a, b):
    return lax.dot_general(a, b, (((1,), (1,)), ((), ())), preferred_element_type=F32)


def _ffn_kernel(*refs, ff_chunk, n_mix, groups):
    mix_refs, refs = refs[:n_mix], refs[n_mix:]
    if n_mix:
        (wmix_ref, gmix_ref), refs = refs[:2], refs[2:]
    x_ref, gin_ref, gout_ref, wg_ref, wu_ref, wd_ref, o_ref = refs
    d_ff = wg_ref.shape[1]
    rows = x_ref.shape[0] // groups
    spans = [slice(r * rows, (r + 1) * rows) for r in range(groups)]

    xs = []
    for rs in spans:
        x = x_ref[rs, :]
        if n_mix:
            mixed = jnp.concatenate([r[rs, :] for r in mix_refs], axis=-1)
            x = x + _rms(_dot(mixed, wmix_ref[...]), gmix_ref[...])
        xs.append(x)
    xns = [_rms(x, gin_ref[...]).astype(BF16) for x in xs]
    accs = []
    for xn in xns:
        acc = None
        for c in range(d_ff // ff_chunk):
            sl = slice(c * ff_chunk, (c + 1) * ff_chunk)
            gate = _dot(xn, wg_ref[:, sl])
            up = _dot(xn, wu_ref[:, sl])
            h = (gate * jax.nn.sigmoid(gate) * up).astype(BF16)
            part = _dot(h, wd_ref[sl, :])
            acc = part if acc is None else acc + part
        accs.append(acc)
    for rs, x, acc in zip(spans, xs, accs):
        o_ref[rs, :] = x + 0.5 * _rms(acc, gout_ref[...])


def _ffn(x, g_in, g_out, wg, wu, wd, which=(), mix=None, *, tm=1024, groups=2, ff_chunk=256):
    m, d = x.shape
    d_ff = wg.shape[-1]
    assert m % tm == 0 and tm % groups == 0 and d_ff % ff_chunk == 0 and len(which) == wg.ndim - 2
    row = lambda width: pl.BlockSpec((tm, width), lambda i: (i, 0))
    lead = (None,) * len(which)
    weight = lambda shape: pl.BlockSpec(lead + shape, lambda i: tuple(which) + (0, 0),
                                        pipeline_mode=pl.Buffered(1))
    parts, w_mix, g_mix = mix if mix else ((), None, None)
    mix_specs = [row(a.shape[1]) for a in parts] + ([_resident(w_mix.shape), _resident((1, d))] if mix else [])
    mix_args = list(parts) + ([w_mix, g_mix.reshape(1, d)] if mix else [])
    return pl.pallas_call(
        functools.partial(_ffn_kernel, ff_chunk=ff_chunk, n_mix=len(parts), groups=groups),
        grid=(m // tm,),
        in_specs=mix_specs + [row(d), _resident((1, d)), _resident((1, d)),
                              weight((d, d_ff)), weight((d, d_ff)), weight((d_ff, d))],
        out_specs=row(d),
        out_shape=jax.ShapeDtypeStruct((m, d), F32),
        compiler_params=_params(("parallel",)),
        name="ffn",
    )(*mix_args, x, g_in.reshape(1, d), g_out.reshape(1, d), wg, wu, wd)


def _attn_proj_kernel(x_ref, g_ref, w_ref, sb_ref, dil_ref, *, n_chunk, scale, groups):
    w_sb = sb_ref.shape[1]
    w_q = w_sb // 3
    rows = x_ref.shape[0] // groups
    spans = [slice(r * rows, (r + 1) * rows) for r in range(groups)]
    xns = [_rms(x_ref[rs, :], g_ref[...]).astype(BF16) for rs in spans]
    for rs, xn in zip(spans, xns):
        for c in range(w_sb // n_chunk):
            sl = slice(c * n_chunk, (c + 1) * n_chunk)
            y = _dot(xn, w_ref[:, sl])
            if (c + 1) * n_chunk <= w_q:
                y = y * scale
            sb_ref[rs, sl] = y.astype(BF16)
        for c in range(dil_ref.shape[1] // n_chunk):
            sl = slice(c * n_chunk, (c + 1) * n_chunk)
            y = _dot(xn, w_ref[:, w_sb + c * n_chunk: w_sb + (c + 1) * n_chunk])
            if (c + 1) * n_chunk <= w_q:
                y = y * scale
            dil_ref[rs, sl] = y


def _attn_proj(x, g, w, *, tm=1024, n_chunk=512, groups=2):
    m, d = x.shape
    n = w.shape[1]
    half = n // 2
    assert m % tm == 0 and (half // 3) % n_chunk == 0
    scale = 1.0 / math.sqrt(HEAD_DIM_ATTN)
    return pl.pallas_call(
        functools.partial(_attn_proj_kernel, n_chunk=n_chunk, scale=scale, groups=groups),
        grid=(m // tm,),
        in_specs=[pl.BlockSpec((tm, d), lambda i: (i, 0)), _resident((1, d)), _resident((d, n))],
        out_specs=[pl.BlockSpec((tm, half), lambda i: (i, 0)),
                   pl.BlockSpec((tm, half), lambda i: (i, 0))],
        out_shape=[jax.ShapeDtypeStruct((m, half), BF16), jax.ShapeDtypeStruct((m, half), F32)],
        compiler_params=_params(("parallel",)),
        name="attn_proj",
    )(x, g.reshape(1, d), w)


LOG2E = math.log2(math.e)
SB_DEAD_LOG2 = -160.0


def _sb_kernel(q_ref, k_ref, v_ref, tri_ref, o_ref, *, blk, q_blocks):
    first_q = pl.program_id(2) * q_blocks
    lane = lax.broadcasted_iota(jnp.int32, (1, LANES), 1)
    row = lax.broadcasted_iota(jnp.int32, (blk, blk), 0)
    col = lax.broadcasted_iota(jnp.int32, (blk, blk), 1)
    causal = col < row
    causal2 = jnp.concatenate([causal, causal], axis=0)
    tri = tri_ref[...]
    head0 = lane < HEAD_DIM_ATTN

    def stacked(q2):
        zeros = jnp.zeros_like(q2)
        return jnp.concatenate([jnp.where(head0, q2, zeros), jnp.where(head0, zeros, q2)], axis=0)

    def pairs(jobs):
        chains = []
        for j, (qq, kb, diag, _) in enumerate(jobs):
            chains.append((j, pl.multiple_of(kb * blk, blk), causal2 if diag else None))
            chains.append((j, pl.multiple_of(jnp.maximum(kb - 1, 0) * blk, blk), kb >= 1))
        states = [job[3] for job in jobs]
        zs, mids = {}, {}
        for n in range(len(chains) + 2):
            if n < len(chains):
                j, ks, _ = chains[n]
                zs[n] = _dot_nt(jobs[j][0], k_ref[pl.ds(ks, blk), :]) * LOG2E
            if 0 <= n - 1 < len(chains):
                z, mask = zs.pop(n - 1), chains[n - 1][2]
                neg = -z
                log_keep = jnp.minimum(neg, 0.0) - jnp.log2(1.0 + jnp.exp2(jnp.minimum(z, neg)))
                log_beta = z + log_keep
                if mask is not None:
                    log_keep = jnp.where(mask, log_keep, 0.0)
                    log_beta = jnp.where(mask, log_beta, MASKED)
                hi = log_keep.astype(BF16)
                lo = (log_keep - hi.astype(F32)).astype(BF16)
                later = _dot(hi, tri) + _dot(lo, tri)
                mids[n - 1] = (log_beta, later, jnp.sum(log_keep, axis=-1, keepdims=True))
            if 0 <= n - 2 < len(chains):
                j, ks, _ = chains[n - 2]
                log_beta, later, total = mids.pop(n - 2)
                carry, acc = states[j]
                p = jnp.exp2(log_beta + later + carry)
                states[j] = (carry + total, acc + _dot(p.astype(BF16), v_ref[pl.ds(ks, blk), :]))
        return states

    def alive(state):
        return (jnp.max(state[0]) > SB_DEAD_LOG2).astype(jnp.int32)

    zero = (jnp.zeros((2 * blk, 1), F32), jnp.zeros((2 * blk, LANES), F32))
    qqs = [stacked(q_ref[g * blk:(g + 1) * blk, :]) for g in range(q_blocks)]
    states = pairs([(qq, first_q + g, True, zero) for g, qq in enumerate(qqs)])

    def store(g, state):
        o_ref[g * blk:(g + 1) * blk, :] = jnp.where(head0, state[1][:blk], state[1][blk:]).astype(o_ref.dtype)

    live = [alive(state) for state in states]
    for g, state in enumerate(states):
        store(g, state)

    for g, (qq, state) in enumerate(zip(qqs, states)):
        i = first_q + g
        n_pairs = lax.shift_right_logical(i, 1)

        @pl.when((live[g] > 0) & (n_pairs > 0))
        def _(g=g, qq=qq, state=state, i=i, n_pairs=n_pairs):
            def cond(loop):
                t, alive_now, _ = loop
                return (t < n_pairs) & (alive_now > 0)

            def body(loop):
                t, _, state = loop
                state, = pairs([(qq, i - 2 - 2 * t, False, state)])
                return t + 1, alive(state), state

            _, _, final = lax.while_loop(cond, body, (jnp.int32(0), live[g], state))
            store(g, final)


def _sb_attention(qkv, *, blk=256, q_blocks=8):
    b, t, w3 = qkv.shape
    w = w3 // 3
    pairs = w // LANES
    blk = min(blk, t)
    rows = blk * q_blocks
    assert t % rows == 0
    tri = jnp.asarray(np.tril(np.ones((blk, blk), np.float32), -1), BF16)
    return pl.pallas_call(
        functools.partial(_sb_kernel, blk=blk, q_blocks=q_blocks),
        grid=(b, pairs, t // rows),
        in_specs=[pl.BlockSpec((None, rows, LANES), lambda bi, p, i: (bi, i, p)),
                  pl.BlockSpec((None, t, LANES), lambda bi, p, i: (bi, 0, pairs + p)),
                  pl.BlockSpec((None, t, LANES), lambda bi, p, i: (bi, 0, 2 * pairs + p)),
                  _resident((blk, blk))],
        out_specs=pl.BlockSpec((None, rows, LANES), lambda bi, p, i: (bi, i, p)),
        out_shape=jax.ShapeDtypeStruct((b, t, w), BF16),
        compiler_params=_params(("parallel", "parallel", "arbitrary")),
        name="sb_attn",
    )(qkv, qkv, qkv, tri)


def _t5_bucket_np(dist):
    max_exact = NUM_BUCKETS // 2
    d = np.maximum(dist, 1).astype(np.float32)
    log_b = max_exact + (np.log(d / np.float32(max_exact)) / np.float32(math.log(MAX_DISTANCE / max_exact))
                         * np.float32(NUM_BUCKETS - max_exact)).astype(np.int32)
    log_b = np.minimum(log_b, NUM_BUCKETS - 1)
    return np.where(dist < max_exact, dist, log_b)


def _dil_tables():
    qi = np.arange(DIL_BLOCK)[:, None]
    ki = np.arange(2 * DIL_BLOCK)[None, :]
    dist = qi + DIL_BLOCK - ki
    buckets, valid = [], []
    for window, dil in DIL_CONFIGS:
        steps = window // dil
        buckets.append(_t5_bucket_np(np.maximum(dist, 0) * dil))
        valid.append((dist >= 0) & (dist <= steps))
    return np.stack(buckets).astype(np.int32), np.stack(valid).astype(np.int32)


def _dil_kernel(rb_ref, bucket_ref, valid_ref, q_ref, k_ref, v_ref, o_ref,
                bias_scr, num_scr, m_scr, l_scr, *, seq):
    p = pl.program_id(0)
    qb = DIL_BLOCK
    lane = lax.broadcasted_iota(jnp.int32, (1, LANES), 1)
    head0 = lane < HEAD_DIM_ATTN
    first_half = lax.broadcasted_iota(jnp.int32, (1, 2 * qb), 1) < qb

    @pl.when(pl.program_id(1) == 0)
    def _():
        for br in range(len(DIL_CONFIGS)):
            bucket = bucket_ref[br]
            valid = valid_ref[br] > 0
            for h in range(2):
                bias = jnp.zeros((qb, 2 * qb), F32)
                for b in range(NUM_BUCKETS):
                    bias = jnp.where(bucket == b, rb_ref[b, 2 * p + h], bias)
                bias_scr[br, h * qb:(h + 1) * qb, :] = jnp.where(valid, bias, MASKED)

    for br, (_, dil) in enumerate(DIL_CONFIGS):
        n_units = seq // qb

        n_blocks = seq // (qb * dil)
        run = min(DIL_UNROLL, n_blocks)
        runs = DIL_UNROLL // run
        assert DIL_UNROLL % run == 0 and n_blocks % run == 0 and (dil == 1 or n_blocks == run)

        def rows(ref, start, dil=dil):
            if dil == 1:
                return ref[pl.ds(start, qb), :].astype(BF16)
            return ref[pl.ds(start, qb, stride=dil), :].astype(BF16)

        def load_run(rho, dil=dil, run=run, whole=(n_blocks == run)):
            res, base = (rho, 0) if whole else (0, rho * run)
            starts = [(base + i) * (qb * dil) + res for i in range(run)]
            kb = [rows(k_ref, st) for st in starts]
            vb = [rows(v_ref, st) for st in starts]
            if whole:
                k_prev, v_prev = kb[0], vb[0]
                pen = jnp.where(first_half, MASKED, 0.0)
            else:
                st = jnp.maximum(base - 1, 0) * (qb * dil) + res
                k_prev, v_prev = rows(k_ref, st), rows(v_ref, st)
                pen = jnp.where(first_half, jnp.where(base == 0, MASKED, 0.0), 0.0)
            units = []
            for i, st in enumerate(starts):
                q2 = rows(q_ref, st)
                zeros = jnp.zeros_like(q2)
                qq = jnp.concatenate([jnp.where(head0, q2, zeros), jnp.where(head0, zeros, q2)], axis=0)
                k2 = jnp.concatenate([kb[i - 1] if i else k_prev, kb[i]], axis=0)
                v2 = jnp.concatenate([vb[i - 1] if i else v_prev, vb[i]], axis=0)
                units.append((st, pen if i == 0 else None, qq, k2, v2))
            return units

        def softmax_parts(s):
            m = jnp.max(s, axis=-1, keepdims=True)
            e = jnp.exp(s - m)
            return m, e, jnp.sum(e, axis=-1, keepdims=True)

        def group(g, _, br=br, dil=dil, runs=runs):
            units = [u for r in range(runs) for u in load_run(g * runs + r)]
            scores, parts = {}, {}
            for j in range(DIL_UNROLL + 2):
                if j < DIL_UNROLL:
                    _, pen, qq, k2, _ = units[j]
                    scores[j] = _dot_nt(qq, k2) + bias_scr[br]
                    if pen is not None:
                        scores[j] = scores[j] + pen
                if 0 <= j - 1 < DIL_UNROLL:
                    parts[j - 1] = softmax_parts(scores.pop(j - 1))
                if 0 <= j - 2 < DIL_UNROLL:
                    st, _, _, _, v2 = units[j - 2]
                    m, e, l = parts.pop(j - 2)
                    pv = _dot(e.astype(BF16), v2)
                    idx = pl.ds(st, qb) if dil == 1 else pl.ds(st, qb, stride=dil)
                    num_scr[br, idx, :] = jnp.where(head0, pv[:qb], pv[qb:])
                    m_scr[br, idx, :] = jnp.where(head0, m[:qb], m[qb:])
                    l_scr[br, idx, :] = jnp.where(head0, l[:qb], l[qb:])
            return 0

        lax.fori_loop(0, n_units // DIL_UNROLL, group, 0)

    rows_out = 256

    def finish(c, _):
        idx = pl.ds(pl.multiple_of(c * rows_out, rows_out), rows_out)
        m_all = [m_scr[br, idx, :] for br in range(len(DIL_CONFIGS))]
        m_max = functools.reduce(jnp.maximum, m_all)
        num = den = None
        for br, m_br in enumerate(m_all):
            wt = jnp.exp(m_br - m_max)
            n_br = wt * num_scr[br, idx, :]
            d_br = wt * l_scr[br, idx, :]
            num = n_br if num is None else num + n_br
            den = d_br if den is None else den + d_br
        o_ref[idx, :] = (num / den).astype(o_ref.dtype)
        return 0

    lax.fori_loop(0, seq // rows_out, finish, 0, unroll=4)


def _dil_attention(qkv, rel_bias):
    b, t, w3 = qkv.shape
    w = w3 // 3
    pairs = w // LANES
    assert t % (DIL_BLOCK * max(d for _, d in DIL_CONFIGS)) == 0 and t % 256 == 0
    bucket, valid = _dil_tables()
    nbr = len(DIL_CONFIGS)
    seq_spec = lambda off: pl.BlockSpec((None, t, LANES), lambda p, bi: (bi, 0, off + p))
    return pl.pallas_call(
        functools.partial(_dil_kernel, seq=t),
        grid=(pairs, b),
        in_specs=[pl.BlockSpec(memory_space=pltpu.SMEM),
                  _resident((nbr, DIL_BLOCK, 2 * DIL_BLOCK)),
                  _resident((nbr, DIL_BLOCK, 2 * DIL_BLOCK)),
                  seq_spec(0), seq_spec(pairs), seq_spec(2 * pairs)],
        out_specs=pl.BlockSpec((None, t, LANES), lambda p, bi: (bi, 0, p)),
        out_shape=jax.ShapeDtypeStruct((b, t, w), BF16),
        scratch_shapes=[pltpu.VMEM((nbr, 2 * DIL_BLOCK, 2 * DIL_BLOCK), F32),
                        pltpu.VMEM((nbr, t, LANES), F32), pltpu.VMEM((nbr, t, LANES), F32),
                        pltpu.VMEM((nbr, t, LANES), F32)],
        compiler_params=_params(("arbitrary", "arbitrary")),
        name="dil_attn",
    )(rel_bias, jnp.asarray(bucket), jnp.asarray(valid), qkv, qkv, qkv)


def _mlstm_proj_kernel(x_ref, g_ref, w_ref, wg_ref, cw_ref, bg_ref, q_ref, kt_ref, v_ref, o_ref,
                       gate_ref, *hist_scrs, tiles_per_seq, n_chunk, scale):
    i = pl.program_id(0)
    tm = x_ref.shape[0]
    width = q_ref.shape[1]
    xn = _rms(x_ref[...], g_ref[...]).astype(BF16)

    def even_rows(first, n):
        return pl.ds(2 * first, n, stride=2)

    slabs = n_chunk // LANES
    n_conv = 2 * width // n_chunk
    assert len(hist_scrs) == n_conv

    @pl.when(i % tiles_per_seq == 0)
    def _():
        for hist in hist_scrs:
            hist[:, 0:2 * CONV_PAD, :] = jnp.zeros((slabs, 2 * CONV_PAD, LANES), F32)

    def project(c):
        pre = _dot(xn, w_ref[:, c * n_chunk:(c + 1) * n_chunk])
        for s in range(slabs):
            hist_scrs[c][s, even_rows(CONV_PAD, tm), :] = pre[:, s * LANES:(s + 1) * LANES]

    def conv(c):
        sl = slice(c * n_chunk, (c + 1) * n_chunk)
        hist = hist_scrs[c]
        parts = []
        for s in range(slabs):
            y = None
            for tap in range(CONV_WIDTH):
                off = CONV_PAD - (CONV_WIDTH - 1) + tap
                w_tap = cw_ref[tap:tap + 1, (c * slabs + s) * LANES:(c * slabs + s + 1) * LANES]
                term = w_tap * hist[s, even_rows(off, tm), :]
                y = term if y is None else y + term
            parts.append(y)
            hist[s, even_rows(0, CONV_PAD), :] = hist[s, even_rows(tm, CONV_PAD), :]
        y = jnp.concatenate(parts, axis=1)
        y = y * jax.nn.sigmoid(y)
        if c * n_chunk < width:
            q_ref[:, sl] = (y * scale).astype(BF16)
        else:
            kt_ref[c * n_chunk - width:(c + 1) * n_chunk - width, :] = y.T.astype(BF16)

    def plain(c):
        sl = slice((c // 2) * n_chunk, (c // 2 + 1) * n_chunk)
        if c % 2 == 0:
            v_ref[:, sl] = _dot(xn, w_ref[:, 2 * width + sl.start:2 * width + sl.stop]).astype(BF16)
        else:
            o_ref[:, sl] = _dot(xn, w_ref[:, 3 * width + sl.start:3 * width + sl.stop])

    n_plain = 2 * width // n_chunk
    for c in range(n_conv + 1):
        if c < n_conv:
            project(c)
        if c >= 1:
            conv(c - 1)
            if c - 1 < n_plain:
                plain(c - 1)
    for c in range(n_conv, n_plain):
        plain(c)
    gate_ref[...] = _dot(xn, wg_ref[...]) + bg_ref[...]


def _mlstm_proj(x, g, w_in, w_gates, conv_w, b_gates, *, seq, tm=512, n_chunk=512):
    m, d = x.shape
    width = conv_w.shape[1] // 2
    assert m % tm == 0 and seq % tm == 0 and width % n_chunk == 0 and w_in.shape[1] >= 4 * width
    row = lambda wd: pl.BlockSpec((tm, wd), lambda i: (i, 0))
    scale = 1.0 / math.sqrt(width // N_HEADS_MLSTM)
    return pl.pallas_call(
        functools.partial(_mlstm_proj_kernel, tiles_per_seq=seq // tm, n_chunk=n_chunk, scale=scale),
        grid=(m // tm,),
        in_specs=[row(d), _resident((1, d)), _resident(w_in.shape), _resident((d, 2 * LANES)),
                  _resident(conv_w.shape), _resident((1, 2 * LANES))],
        out_specs=[row(width), pl.BlockSpec((width, tm), lambda i: (0, i)), row(width), row(width),
                   row(2 * LANES)],
        out_shape=[jax.ShapeDtypeStruct((m, width), BF16), jax.ShapeDtypeStruct((width, m), BF16),
                   jax.ShapeDtypeStruct((m, width), BF16), jax.ShapeDtypeStruct((m, width), F32),
                   jax.ShapeDtypeStruct((m, 2 * LANES), F32)],
        scratch_shapes=[pltpu.VMEM((n_chunk // LANES, 2 * (tm + CONV_PAD), LANES), F32)
                        for _ in range(2 * width // n_chunk)],
        compiler_params=_params(("arbitrary",)),
        name="mlstm_proj",
    )(x, g.reshape(1, d), w_in, w_gates, conv_w, b_gates)


def _split3(x):
    hi = x.astype(BF16)
    r = x - hi.astype(F32)
    mid = r.astype(BF16)
    return hi, mid, (r - mid.astype(F32)).astype(BF16)


def _gate_prep_kernel(g_ref, cols_ref, rows_ref, *, heads, cl):
    tm = g_ref.shape[0]
    row = lax.broadcasted_iota(jnp.int32, (cl, cl), 0)
    col = lax.broadcasted_iota(jnp.int32, (cl, cl), 1)
    incl = jnp.where(row >= col, 1.0, 0.0).astype(BF16)
    lane = lax.broadcasted_iota(jnp.int32, (1, cl), 1)
    pad = jnp.zeros((cl - heads, cl), F32)
    spans = [slice(c * cl, (c + 1) * cl) for c in range(tm // cl)]
    bs = []
    for r in spans:
        b = None
        for part in _split3(jax.nn.log_sigmoid(g_ref[r, LANES:])):
            term = _dot(incl, part)
            b = term if b is None else b + term
        bs.append(b)
    us = [g_ref[r, :LANES] - b for r, b in zip(spans, bs)]
    u_rows = jnp.concatenate([u.T[0:heads, :] for u in us], axis=0)
    b_rows = jnp.concatenate([b.T[0:heads, :] for b in bs], axis=0)
    cmax = u_rows
    shift = 1
    while shift < cl:
        cmax = jnp.where(lane >= shift, jnp.maximum(cmax, pltpu.roll(cmax, shift, axis=1)), cmax)
        shift *= 2
    u_max = jnp.broadcast_to(jnp.max(u_rows, axis=1, keepdims=True), u_rows.shape)
    b_last = jnp.broadcast_to(jnp.min(b_rows, axis=1, keepdims=True), b_rows.shape)
    for c, (r, u, b) in enumerate(zip(spans, us, bs)):
        hs = slice(c * heads, (c + 1) * heads)
        cols_ref[r, :] = jnp.concatenate([u, b, jnp.concatenate([cmax[hs], pad], axis=0).T], axis=1)
        rows_ref[:, r] = jnp.concatenate([u_rows[hs], u_max[hs], b_last[hs]], axis=0)


def _gate_prep(gates, *, heads, cl, tm=4096):
    m = gates.shape[0]
    tm = min(tm, m)
    assert m % tm == 0 and tm % cl == 0 and cl == LANES and heads == 8
    return pl.pallas_call(
        functools.partial(_gate_prep_kernel, heads=heads, cl=cl),
        grid=(m // tm,),
        in_specs=[pl.BlockSpec((tm, 2 * LANES), lambda i: (i, 0))],
        out_specs=[pl.BlockSpec((tm, 3 * LANES), lambda i: (i, 0)),
                   pl.BlockSpec((3 * heads, tm), lambda i: (0, i))],
        out_shape=[jax.ShapeDtypeStruct((m, 3 * LANES), F32),
                   jax.ShapeDtypeStruct((3 * heads, m), F32)],
        compiler_params=_params(("parallel",)),
        name="gate_prep",
    )(gates)


def _mlstm_kernel(q_ref, kt_ref, v_ref, o_ref, gcol_ref, grow_ref, hg_ref, out_ref,
                  s_scr, m_scr, ml_scr, *, heads):
    cl = MLSTM_CHUNK
    n_chunks = q_ref.shape[0] // cl
    dh = q_ref.shape[1] // heads

    @pl.when(pl.program_id(1) == 0)
    def _():
        s_scr[...] = jnp.zeros_like(s_scr)
        m_scr[...] = jnp.zeros_like(m_scr)
        ml_scr[...] = jnp.zeros_like(ml_scr)

    row = lax.broadcasted_iota(jnp.int32, (cl, cl), 0)
    col = lax.broadcasted_iota(jnp.int32, (cl, cl), 1)
    lower = row >= col
    ones = jnp.ones((cl, dh), BF16)

    m_rows = m_scr[...]
    m_lane = ml_scr[0:1, :]
    tables = []
    for c in range(n_chunks):
        span = slice(c * cl, (c + 1) * cl)
        u_cols, b_cols, cmax_cols = (gcol_ref[span, t * LANES:(t + 1) * LANES] for t in range(3))
        u_rows, u_max_rows, b_last_rows = (grow_ref[t * heads:(t + 1) * heads, span] for t in range(3))
        mm_last_rows = jnp.maximum(m_rows, u_max_rows)
        mm_cols = jnp.maximum(cmax_cols, m_lane)
        tables.append(dict(
            span=span, u_rows=u_rows, m_prev_rows=m_rows, mm_cols=mm_cols,
            decay_rows=jnp.exp(m_rows - mm_last_rows),
            ws_rows=jnp.exp(u_rows - mm_last_rows),
            floor_cols=jnp.exp(-(b_cols + mm_cols))))
        m_rows = b_last_rows + mm_last_rows
        m_lane = b_cols[cl - 1:cl, :] + mm_cols[cl - 1:cl, :]
    m_scr[...] = m_rows
    ml_scr[...] = jnp.broadcast_to(m_lane, ml_scr.shape)

    def lanes_of(mat, c):
        return jnp.broadcast_to(mat[:, c:c + 1], (cl, LANES))

    def stage_a(tab, h):
        sl = slice(h * dh, (h + 1) * dh)
        q, kt = q_ref[tab["span"], sl], kt_ref[sl, tab["span"]]
        v1 = jnp.concatenate([v_ref[tab["span"], sl], ones], axis=1)
        s_prev = s_scr[h]
        qk = _dot(q, kt)
        qs = _dot(q, s_prev.astype(BF16))
        kw = (kt.astype(F32) * tab["ws_rows"][h:h + 1, :]).astype(BF16)
        decay = jnp.broadcast_to(tab["decay_rows"][h:h + 1, :], (dh, LANES))
        s_scr[h] = jnp.concatenate([decay, decay], axis=1) * s_prev + _dot(kw, v1)
        return tab, h, sl, v1, qk, qs

    def stage_b(tab, h, sl, v1, qk, qs):
        mm = lanes_of(tab["mm_cols"], h)
        weight = jnp.where(lower, jnp.exp(tab["u_rows"][h:h + 1, :] - mm), 0.0)
        w_inter = jnp.exp(tab["m_prev_rows"][h:h + 1, :] - mm)
        return tab, sl, v1, qs, (qk * weight).astype(BF16), w_inter, lanes_of(tab["floor_cols"], h)

    def stage_c(tab, sl, v1, qs, p, w_inter, floor):
        pv = _dot(p, v1)
        both = pv + jnp.concatenate([w_inter, w_inter], axis=1) * qs
        hid = both[:, :dh] / jnp.maximum(jnp.abs(both[:, dh:]), floor)
        hid = hid * lax.rsqrt(jnp.mean(hid * hid, axis=-1, keepdims=True) + EPS) * hg_ref[:, sl]
        out_ref[tab["span"], sl] = (hid * jax.nn.sigmoid(o_ref[tab["span"], sl])).astype(out_ref.dtype)

    jobs = [(tab, h) for tab in tables for h in range(heads)]
    after_a, after_b = {}, {}
    for j in range(len(jobs) + 2):
        if j < len(jobs):
            after_a[j] = stage_a(*jobs[j])
        if 0 <= j - 1 < len(jobs):
            after_b[j - 1] = stage_b(*after_a.pop(j - 1))
        if 0 <= j - 2 < len(jobs):
            stage_c(*after_b.pop(j - 2))


def _mlstm(q, kt, v, o, gates, head_g, *, heads=N_HEADS_MLSTM, chunks=4):
    b, t, w = q.shape
    cl = MLSTM_CHUNK
    dh = w // heads
    rows = chunks * cl
    nc = t // rows
    assert t % rows == 0 and dh == LANES and cl == LANES
    gcols, grows = _gate_prep(gates, heads=heads, cl=cl)
    blk = lambda wd: pl.BlockSpec((None, rows, wd), lambda bi, c: (bi, c, 0))
    return pl.pallas_call(
        functools.partial(_mlstm_kernel, heads=heads),
        grid=(b, nc),
        in_specs=[blk(w), pl.BlockSpec((w, rows), lambda bi, c: (0, bi * nc + c)), blk(w), blk(w),
                  pl.BlockSpec((rows, 3 * LANES), lambda bi, c: (bi * nc + c, 0)),
                  pl.BlockSpec((3 * heads, rows), lambda bi, c: (0, bi * nc + c)),
                  _resident((1, w))],
        out_specs=blk(w),
        out_shape=jax.ShapeDtypeStruct((b, t, w), BF16),
        scratch_shapes=[pltpu.VMEM((heads, dh, 2 * dh), F32), pltpu.VMEM((heads, LANES), F32),
                        pltpu.VMEM((heads, LANES), F32)],
        compiler_params=_params(("parallel", "arbitrary")),
        name="mlstm",
    )(q, kt, v, o, gcols, grows, head_g.reshape(1, w))


def kernel(x, norm_g, ffn_w_gate, ffn_w_up, ffn_w_down, attn_w_in, attn_w_out, rel_bias,
           mlstm_w_in, mlstm_b_gates, mlstm_conv_w, mlstm_head_g, mlstm_w_out):
    bsz, t, d = x.shape
    depth = norm_g.shape[0]
    h = x.reshape(bsz * t, d)
    bf = lambda a: a.astype(BF16)

    w_gate, w_up, w_down = bf(ffn_w_gate), bf(ffn_w_up), bf(ffn_w_down)

    def ffn(h, layer, half, mix=None):
        g = norm_g[layer]
        return _ffn(h, g[2 * half * 2], g[2 * half * 2 + 1], w_gate, w_up, w_down, (layer, half), mix)

    for layer in range(depth):
        g = norm_g[layer]
        j = layer // 2
        h = ffn(h, layer, 0)
        if layer % 2 == 0:
            sb, dil = _attn_proj(h, g[2], bf(attn_w_in[j]))
            out_sb = _sb_attention(sb.reshape(bsz, t, -1))
            out_dil = _dil_attention(dil.reshape(bsz, t, -1), rel_bias.astype(F32))
            mix = ([out_sb.reshape(bsz * t, -1), out_dil.reshape(bsz * t, -1)], bf(attn_w_out[j]), g[3])
        else:
            width = mlstm_w_out.shape[1]
            n_head = mlstm_b_gates.shape[1] // 2
            tiles = lambda a: jnp.concatenate(
                [jnp.pad(part, ((0, 0), (0, LANES - n_head)))
                 for part in (a[:, :n_head], a[:, n_head:])], axis=1)
            w_in = bf(mlstm_w_in[j])
            w_gates = tiles(w_in[:, 4 * width:])
            b_gates = tiles(mlstm_b_gates[j].astype(F32).reshape(1, -1))
            q, kt, v, o, gates = _mlstm_proj(h, g[2], w_in, w_gates, mlstm_conv_w[j].astype(F32),
                                             b_gates, seq=t)
            r3 = lambda a: a.reshape(bsz, t, -1)
            hid = _mlstm(r3(q), kt, r3(v), r3(o), gates, mlstm_head_g[j].astype(F32))
            mix = ([hid.reshape(bsz * t, width)], bf(mlstm_w_out[j]), g[3])
        h = ffn(h, layer, 1, mix)
    return h.reshape(bsz, t, d)
```
